```python
import math
import jax, jax.numpy as jnp
from jax import lax
import numpy as np

D_MODEL = 4096
BATCH = 1
SEQ = 8192
DEPTH = 1
DEC_BATCH = 128
DEC_SEQ = 4
PAST_LEN = 8192
PAGE_SIZE = 128

HEAD_DIM = 128
ATTN_HEADS = (D_MODEL // 2) // HEAD_DIM
ATTN_KV_HEADS = ATTN_HEADS // 4
ATTN_GROUP = ATTN_HEADS // ATTN_KV_HEADS
ATTN_WIDTH = ATTN_HEADS * HEAD_DIM
KV_WIDTH = ATTN_KV_HEADS * HEAD_DIM
WINDOW = 128
ATTN_BLOCK = 128
REL_BUCKETS = 32
REL_MAX_DIST = 128
GDN_HEADS = (D_MODEL - ATTN_WIDTH) // HEAD_DIM
GDN_DK = HEAD_DIM
GDN_DV = HEAD_DIM
GDN_KEY_WIDTH = GDN_HEADS * GDN_DK
GDN_WIDTH = GDN_HEADS * GDN_DV
GDN_CONV = 4
GDN_CONV_CH = 2 * GDN_KEY_WIDTH + GDN_WIDTH
GDN_CHUNK = 64
MIX_WIDTH = ATTN_WIDTH + GDN_WIDTH
FF_DIM = -(-8 * D_MODEL // (3 * 256)) * 256
PROJ_SIZES = (ATTN_WIDTH, KV_WIDTH, KV_WIDTH, GDN_CONV_CH, GDN_WIDTH, GDN_HEADS, GDN_HEADS)
PROJ_WIDTH = ATTN_WIDTH + 2 * KV_WIDTH + GDN_CONV_CH + GDN_WIDTH + 2 * GDN_HEADS
NORM_EPS = 1e-6
NEG_INF = -1e30

kernel_name = 'hybrid_swa_sink_gated_deltanet_step'


def rms_norm(x, w):
    xf = x.astype(jnp.float32)
    y = xf * lax.rsqrt(jnp.mean(xf * xf, axis=-1, keepdims=True) + NORM_EPS)
    return (y * w.astype(jnp.float32)).astype(x.dtype)


def l2_norm(x):
    xf = x.astype(jnp.float32)
    return xf * lax.rsqrt(jnp.sum(xf * xf, axis=-1, keepdims=True) + NORM_EPS)


def rel_bucket(d):
    n = jnp.maximum(d, 0)
    max_exact = REL_BUCKETS // 2
    nf = jnp.maximum(n, 1).astype(jnp.float32)
    large = max_exact + (jnp.log(nf / max_exact) / math.log(REL_MAX_DIST / max_exact)
                         * (REL_BUCKETS - max_exact)).astype(jnp.int32)
    large = jnp.minimum(large, REL_BUCKETS - 1)
    return jnp.where(n < max_exact, n, large)


def swa_attend(q, k, v, q_pos, k_pos, k_valid, sinks, rel_bias):
    tq, tk = q_pos.shape[0], k_pos.shape[0]
    d = q_pos[:, None] - k_pos[None, :]
    allowed = (d >= 0) & (d <= WINDOW)
    bias = rel_bias.astype(jnp.float32)[rel_bucket(d)]
    bias = bias.reshape(tq, tk, ATTN_KV_HEADS, ATTN_GROUP).transpose(2, 3, 0, 1)
    s = jnp.einsum('...qhgd,...khd->...hgqk', q, k).astype(jnp.float32) * (HEAD_DIM ** -0.5) + bias
    mask = allowed & k_valid[..., None, None, None, :]
    s = jnp.where(mask, s, NEG_INF)
    sink = sinks.astype(jnp.float32).reshape(ATTN_KV_HEADS, ATTN_GROUP)[:, :, None, None]
    m = jnp.maximum(jnp.max(s, axis=-1, keepdims=True), sink)
    p = jnp.exp(s - m)
    denom = jnp.sum(p, axis=-1, keepdims=True) + jnp.exp(sink - m)
    return jnp.einsum('...hgqk,...khd->...qhgd', (p / denom).astype(v.dtype), v)


def swa_prompt(q, k, v, sinks, rel_bias):
    b, t = q.shape[:2]
    c = ATTN_BLOCK
    nb = t // c
    qb = q.reshape(b, nb, c, ATTN_KV_HEADS, ATTN_GROUP, HEAD_DIM)
    kb = k.reshape(b, nb, c, ATTN_KV_HEADS, HEAD_DIM)
    vb = v.reshape(b, nb, c, ATTN_KV_HEADS, HEAD_DIM)
    padw = ((0, 0), (1, 0), (0, 0), (0, 0), (0, 0))
    kk = jnp.concatenate([jnp.pad(kb, padw)[:, :-1], kb], axis=2)
    vv = jnp.concatenate([jnp.pad(vb, padw)[:, :-1], vb], axis=2)
    q_pos = c + jnp.arange(c)
    k_pos = jnp.arange(2 * c)
    k_valid = (jnp.arange(nb)[:, None] > 0) | (k_pos[None, :] >= c)
    o = swa_attend(qb, kk, vv, q_pos, k_pos, k_valid, sinks, rel_bias)
    return o.reshape(b, t, ATTN_WIDTH)


def swa_sample(q, k, v, win_k, win_v, sinks, rel_bias):
    b, t = q.shape[:2]
    l = win_k.shape[1]
    kk = jnp.concatenate([win_k, k], axis=1)
    vv = jnp.concatenate([win_v, v], axis=1)
    q_pos = l + jnp.arange(t)
    k_pos = jnp.arange(l + t)
    k_valid = jnp.ones((l + t,), dtype=bool)
    o = swa_attend(q.reshape(b, t, ATTN_KV_HEADS, ATTN_GROUP, HEAD_DIM), kk, vv,
                   q_pos, k_pos, k_valid, sinks, rel_bias)
    return o.reshape(b, t, ATTN_WIDTH), kk[:, -l:], vv[:, -l:]


def causal_conv(x, conv_state, w):
    t = x.shape[1]
    xp = jnp.concatenate([conv_state.astype(x.dtype), x], axis=1)
    y = xp[:, 0:t] * w[0]
    for i in range(1, GDN_CONV):
        y = y + xp[:, i:i + t] * w[i]
    return jax.nn.silu(y), xp[:, -(GDN_CONV - 1):]


def gated_delta_chunked(q, k, v, g, beta, s0):
    f32 = jnp.float32
    q, k, v, g, beta = (a.astype(f32) for a in (q, k, v, g, beta))
    b, t, h, dk = q.shape
    dv = v.shape[-1]
    c = min(GDN_CHUNK, t)
    pad = (-t) % c
    if pad:
        p4 = ((0, 0), (0, pad), (0, 0), (0, 0))
        p3 = ((0, 0), (0, pad), (0, 0))
        q, k, v = jnp.pad(q, p4), jnp.pad(k, p4), jnp.pad(v, p4)
        g, beta = jnp.pad(g, p3), jnp.pad(beta, p3)
    n = (t + pad) // c

    def chunks(a):
        a = a.reshape((b, n, c) + a.shape[2:])
        return jnp.moveaxis(a, (1, 2), (0, 3))

    q, k, v, g, beta = chunks(q * dk ** -0.5), chunks(k), chunks(v), chunks(g), chunks(beta)
    gc = jnp.cumsum(g, axis=-1)
    idx = jnp.arange(c)
    causal = idx[:, None] >= idx[None, :]
    strict = idx[:, None] > idx[None, :]
    decay = jnp.exp(jnp.where(causal, gc[..., :, None] - gc[..., None, :], -jnp.inf))
    kb = k * beta[..., None]
    a_mat = jnp.where(strict, jnp.einsum('...id,...jd->...ij', kb, k) * decay, 0.0)
    eye = jnp.eye(c, dtype=f32)
    t_mat = lax.linalg.triangular_solve(a_mat + eye, jnp.broadcast_to(eye, a_mat.shape),
                                        left_side=True, lower=True)
    u = t_mat @ (v * beta[..., None])
    w = t_mat @ (kb * jnp.exp(gc)[..., None])
    qk = jnp.einsum('...id,...jd->...ij', q, k) * decay

    def step(s, inp):
        qc, kc, uc, wc, gcc, qkc = inp
        v_new = uc - wc @ s
        o = (qc * jnp.exp(gcc)[..., None]) @ s + qkc @ v_new
        g_last = gcc[..., -1:]
        s = s * jnp.exp(g_last)[..., None] + jnp.einsum(
            'bhck,bhcv->bhkv', kc * jnp.exp(g_last - gcc)[..., None], v_new)
        return s, o

    s, o = lax.scan(step, s0.astype(f32), (q, k, u, w, gc, qk))
    o = jnp.moveaxis(o, (0, 3), (1, 2)).reshape(b, n * c, h, dv)[:, :t]
    return o, s


def decoder_layer(x, params, win_k, win_v, conv_state, gdn_state, rel_bias):
    (w_in, sinks, conv_w, a_log, dt_bias, gnorm_w, w_out,
     n_pre_mix, n_post_mix, n_pre_ffn, n_post_ffn, w_gate, w_up, w_down) = params
    b, t, _ = x.shape
    h = rms_norm(x, n_pre_mix)
    proj = h @ w_in
    pts, acc = [], 0
    for sz in PROJ_SIZES[:-1]:
        acc += sz
        pts.append(acc)
    q_a, k_a, v_a, qkv_g, z_g, a_g, b_g = jnp.split(proj, pts, axis=-1)

    q_a = q_a.reshape(b, t, ATTN_HEADS, HEAD_DIM)
    k_a = k_a.reshape(b, t, ATTN_KV_HEADS, HEAD_DIM)
    v_a = v_a.reshape(b, t, ATTN_KV_HEADS, HEAD_DIM)
    if win_k is None:
        attn = swa_prompt(q_a, k_a, v_a, sinks, rel_bias)
        lw = min(WINDOW, t)
        new_k, new_v = k_a[:, t - lw:], v_a[:, t - lw:]
    else:
        attn, new_k, new_v = swa_sample(q_a, k_a, v_a, win_k, win_v, sinks, rel_bias)

    if conv_state is None:
        conv_state = jnp.zeros((b, GDN_CONV - 1, GDN_CONV_CH), x.dtype)
    if gdn_state is None:
        gdn_state = jnp.zeros((b, GDN_HEADS, GDN_DK, GDN_DV), jnp.float32)
    cq, new_conv = causal_conv(qkv_g, conv_state, conv_w)
    q_g, k_g, v_g = jnp.split(cq, [GDN_KEY_WIDTH, 2 * GDN_KEY_WIDTH], axis=-1)
    q_g = l2_norm(q_g.reshape(b, t, GDN_HEADS, GDN_DK))
    k_g = l2_norm(k_g.reshape(b, t, GDN_HEADS, GDN_DK))
    v_g = v_g.reshape(b, t, GDN_HEADS, GDN_DV)
    g = -jnp.exp(a_log.astype(jnp.float32)) * jax.nn.softplus(
        a_g.astype(jnp.float32) + dt_bias.astype(jnp.float32))
    beta = jax.nn.sigmoid(b_g.astype(jnp.float32))
    o_g, new_s = gated_delta_chunked(q_g, k_g, v_g, g, beta, gdn_state)
    o_g = rms_norm(o_g, gnorm_w) * jax.nn.silu(z_g.reshape(b, t, GDN_HEADS, GDN_DV).astype(jnp.float32))
    o_g = o_g.reshape(b, t, GDN_WIDTH).astype(x.dtype)

    mix = jnp.concatenate([attn.astype(x.dtype), o_g], axis=-1) @ w_out
    x = x + rms_norm(mix, n_post_mix)
    h2 = rms_norm(x, n_pre_ffn)
    ffn = (jax.nn.silu(h2 @ w_gate) * (h2 @ w_up)) @ w_down
    x = x + rms_norm(ffn, n_post_ffn)
    return x, new_k, new_v, new_conv, new_s.astype(x.dtype)


def setup_inputs(seed: int = 0) -> dict:
    key = jax.random.key(seed)
    ks = jax.random.split(key, 24)
    f32 = jnp.float32
    win_len = min(WINDOW, PAST_LEN)
    nrm = lambda kk, shape, s: jax.random.normal(kk, shape, f32) * s
    dt = jnp.exp(jax.random.uniform(ks[10], (DEPTH, GDN_HEADS), f32)
                 * (math.log(0.1) - math.log(0.001)) + math.log(0.001))
    return {
        'x_prompt': nrm(ks[0], (BATCH, SEQ, D_MODEL), 1.0),
        'x_sample': nrm(ks[1], (DEC_BATCH, DEC_SEQ, D_MODEL), 1.0),
        'cache_win_k': nrm(ks[2], (DEPTH, DEC_BATCH, win_len, ATTN_KV_HEADS, HEAD_DIM), 1.0),
        'cache_win_v': nrm(ks[3], (DEPTH, DEC_BATCH, win_len, ATTN_KV_HEADS, HEAD_DIM), 1.0),
        'state_conv': nrm(ks[4], (DEPTH, DEC_BATCH, GDN_CONV - 1, GDN_CONV_CH), 1.0),
        'state_gdn': nrm(ks[5], (DEPTH, DEC_BATCH, GDN_HEADS, GDN_DK, GDN_DV), GDN_DK ** -0.5),
        'rel_bias': nrm(ks[6], (REL_BUCKETS, ATTN_HEADS), 0.5),
        'w_in': nrm(ks[7], (DEPTH, D_MODEL, PROJ_WIDTH), D_MODEL ** -0.5),
        'attn_sinks': nrm(ks[8], (DEPTH, ATTN_HEADS), 1.0),
        'gdn_conv_w': nrm(ks[9], (DEPTH, GDN_CONV, GDN_CONV_CH), GDN_CONV ** -0.5),
        'gdn_a_log': jnp.log(jax.random.uniform(ks[11], (DEPTH, GDN_HEADS), f32, 1.0, 16.0)),
        'gdn_dt_bias': dt + jnp.log(-jnp.expm1(-dt)),
        'gdn_norm_w': 1.0 + nrm(ks[12], (DEPTH, GDN_DV), 0.1),
        'w_out': nrm(ks[13], (DEPTH, MIX_WIDTH, D_MODEL), MIX_WIDTH ** -0.5),
        'norm_pre_mix': 1.0 + nrm(ks[14], (DEPTH, D_MODEL), 0.1),
        'norm_post_mix': 1.0 + nrm(ks[15], (DEPTH, D_MODEL), 0.1),
        'norm_pre_ffn': 1.0 + nrm(ks[16], (DEPTH, D_MODEL), 0.1),
        'norm_post_ffn': 1.0 + nrm(ks[17], (DEPTH, D_MODEL), 0.1),
        'w_gate': nrm(ks[18], (DEPTH, D_MODEL, FF_DIM), D_MODEL ** -0.5),
        'w_up': nrm(ks[19], (DEPTH, D_MODEL, FF_DIM), D_MODEL ** -0.5),
        'w_down': nrm(ks[20], (DEPTH, FF_DIM, D_MODEL), FF_DIM ** -0.5),
    }


def reference(x_prompt, x_sample, cache_win_k, cache_win_v, state_conv, state_gdn, rel_bias,
              w_in, attn_sinks, gdn_conv_w, gdn_a_log, gdn_dt_bias, gdn_norm_w, w_out,
              norm_pre_mix, norm_post_mix, norm_pre_ffn, norm_post_ffn, w_gate, w_up, w_down):
    yp, ys = x_prompt, x_sample
    pk, pv, pc, ps = [], [], [], []
    sk, sv, sc, ss = [], [], [], []
    for l in range(DEPTH):
        params = (w_in[l], attn_sinks[l], gdn_conv_w[l], gdn_a_log[l], gdn_dt_bias[l], gdn_norm_w[l],
                  w_out[l], norm_pre_mix[l], norm_post_mix[l], norm_pre_ffn[l], norm_post_ffn[l],
                  w_gate[l], w_up[l], w_down[l])
        yp, k1, v1, c1, s1 = decoder_layer(yp, params, None, None, None, None, rel_bias)
        ys, k2, v2, c2, s2 = decoder_layer(ys, params, cache_win_k[l], cache_win_v[l],
                                           state_conv[l], state_gdn[l], rel_bias)
        pk.append(k1); pv.append(v1); pc.append(c1); ps.append(s1)
        sk.append(k2); sv.append(v2); sc.append(c2); ss.append(s2)
    return (yp, ys, jnp.stack(pk), jnp.stack(pv), jnp.stack(pc), jnp.stack(ps),
            jnp.stack(sk), jnp.stack(sv), jnp.stack(sc), jnp.stack(ss))
```

```python
import functools
import math

import jax
import jax.numpy as jnp
from jax import lax
from jax.experimental import pallas as pl
from jax.experimental.pallas import tpu as pltpu

F32 = jnp.float32
BF16 = jnp.bfloat16

HEAD_DIM = 128
ATTN_HEADS = 16
ATTN_KV_HEADS = 4
ATTN_GROUP = ATTN_HEADS // ATTN_KV_HEADS
ATTN_WIDTH = ATTN_HEADS * HEAD_DIM
KV_WIDTH = ATTN_KV_HEADS * HEAD_DIM
WINDOW = 128
REL_BUCKETS = 32
REL_MAX_DIST = 128
GDN_HEADS = 16
GDN_WIDTH = GDN_HEADS * HEAD_DIM
GDN_CONV = 4
GDN_CONV_CH = 3 * GDN_WIDTH
GDN_CHUNK = 64
NORM_EPS = 1e-6
NEG_INF = -1e30

Q_OFF = 0
K_OFF = ATTN_WIDTH
V_OFF = K_OFF + KV_WIDTH
GQ_OFF = V_OFF + KV_WIDTH
GK_OFF = GQ_OFF + GDN_WIDTH
GV_OFF = GK_OFF + GDN_WIDTH
Z_OFF = GV_OFF + GDN_WIDTH
PROJ_MAIN = Z_OFF + GDN_WIDTH
AB_WIDTH = 2 * GDN_HEADS

LANES = 128
SUBLANES = 8
GDN_HB = 4
GDN_NG = GDN_HEADS // GDN_HB
SAMPLE_PAD = 8
VMEM_LIMIT = 56 * 1024 * 1024


def _cparams(n_axes, vmem=VMEM_LIMIT):
    return pltpu.CompilerParams(dimension_semantics=("arbitrary",) * n_axes, vmem_limit_bytes=vmem)


def _rms(x, w):
    return x * lax.rsqrt(jnp.mean(x * x, axis=-1, keepdims=True) + NORM_EPS) * w


def _rmsnorm_kernel(x_ref, w_ref, o_ref):
    o_ref[...] = _rms(x_ref[...].astype(F32), w_ref[...]).astype(o_ref.dtype)


def _rmsnorm(x, w, out_dtype, rows):
    m, d = x.shape
    return pl.pallas_call(
        _rmsnorm_kernel,
        grid=(m // rows,),
        in_specs=[pl.BlockSpec((rows, d), lambda i: (i, 0)), pl.BlockSpec((1, d), lambda i: (0, 0))],
        out_specs=pl.BlockSpec((rows, d), lambda i: (i, 0)),
        out_shape=jax.ShapeDtypeStruct((m, d), out_dtype),
        compiler_params=_cparams(1),
        name="rmsnorm",
    )(x, w.reshape(1, d).astype(F32))


def _post_mix_kernel(x_ref, mix_ref, wpost_ref, wpre_ref, x1_ref, h2_ref):
    x1 = x_ref[...] + _rms(mix_ref[...], wpost_ref[...])
    x1_ref[...] = x1
    h2_ref[...] = _rms(x1, wpre_ref[...]).astype(h2_ref.dtype)


def _post_mix(x, mix, w_post, w_pre, rows):
    m, d = x.shape
    row_spec = pl.BlockSpec((rows, d), lambda i: (i, 0))
    w_spec = pl.BlockSpec((1, d), lambda i: (0, 0))
    return pl.pallas_call(
        _post_mix_kernel,
        grid=(m // rows,),
        in_specs=[row_spec, row_spec, w_spec, w_spec],
        out_specs=[row_spec, row_spec],
        out_shape=[jax.ShapeDtypeStruct((m, d), F32), jax.ShapeDtypeStruct((m, d), BF16)],
        compiler_params=_cparams(1),
        name="post_mix",
    )(x, mix, w_post.reshape(1, d).astype(F32), w_pre.reshape(1, d).astype(F32))


def _residual_norm_kernel(x_ref, y_ref, w_ref, o_ref):
    o_ref[...] = x_ref[...] + _rms(y_ref[...], w_ref[...])


def _residual_norm(x, y, w, rows):
    m, d = x.shape
    row_spec = pl.BlockSpec((rows, d), lambda i: (i, 0))
    return pl.pallas_call(
        _residual_norm_kernel,
        grid=(m // rows,),
        in_specs=[row_spec, row_spec, pl.BlockSpec((1, d), lambda i: (0, 0))],
        out_specs=row_spec,
        out_shape=jax.ShapeDtypeStruct((m, d), F32),
        compiler_params=_cparams(1),
        name="residual_norm",
    )(x, y, w.reshape(1, d).astype(F32))


def _matmul_kernel(*refs, n_pairs):
    o_ref = refs[2 * n_pairs]
    acc = None
    for a_ref, b_ref in zip(refs[:n_pairs], refs[n_pairs:2 * n_pairs]):
        d = jnp.dot(a_ref[...], b_ref[...], preferred_element_type=F32)
        acc = d if acc is None else acc + d
    o_ref[...] = acc.astype(o_ref.dtype)


def _matmul(a_list, b_list, out_dtype, bm, bn, name):
    m = a_list[0].shape[0]
    n = b_list[0].shape[1]
    in_specs = [pl.BlockSpec((bm, a.shape[1]), lambda i, j: (i, 0)) for a in a_list]
    in_specs += [pl.BlockSpec((b.shape[0], bn), lambda i, j: (0, j)) for b in b_list]
    return pl.pallas_call(
        functools.partial(_matmul_kernel, n_pairs=len(a_list)),
        grid=(m // bm, n // bn),
        in_specs=in_specs,
        out_specs=pl.BlockSpec((bm, bn), lambda i, j: (i, j)),
        out_shape=jax.ShapeDtypeStruct((m, n), out_dtype),
        compiler_params=_cparams(2),
        name=name,
    )(*a_list, *b_list)


def _gate_up_kernel(h_ref, wg_ref, wu_ref, o_ref):
    h = h_ref[...]
    g = jnp.dot(h, wg_ref[...], preferred_element_type=F32)
    u = jnp.dot(h, wu_ref[...], preferred_element_type=F32)
    o_ref[...] = (g * jax.nn.sigmoid(g) * u).astype(o_ref.dtype)


def _gate_up(h, w_gate, w_up, bm, bn):
    m, d = h.shape
    n = w_gate.shape[1]
    w_spec = pl.BlockSpec((d, bn), lambda i, j: (0, j))
    return pl.pallas_call(
        _gate_up_kernel,
        grid=(m // bm, n // bn),
        in_specs=[pl.BlockSpec((bm, d), lambda i, j: (i, 0)), w_spec, w_spec],
        out_specs=pl.BlockSpec((bm, bn), lambda i, j: (i, j)),
        out_shape=jax.ShapeDtypeStruct((m, n), BF16),
        compiler_params=_cparams(2),
        name="ffn_gate_up",
    )(h, w_gate, w_up)


def _down_kernel(a_ref, b_ref, o_ref, *, bn):
    @pl.when(pl.program_id(1) == 0)
    def _():
        o_ref[...] = jnp.zeros_like(o_ref)

    a = a_ref[...]
    for n0 in range(0, o_ref.shape[1], bn):
        o_ref[:, n0:n0 + bn] += jnp.dot(a, b_ref[:, n0:n0 + bn], preferred_element_type=F32)


def _down(a, b, bm, bk):
    m, k = a.shape
    n = b.shape[1]
    return pl.pallas_call(
        functools.partial(_down_kernel, bn=min(n, 512)),
        grid=(m // bm, k // bk),
        in_specs=[pl.BlockSpec((bm, bk), lambda i, kk: (i, kk)), pl.BlockSpec((bk, n), lambda i, kk: (kk, 0))],
        out_specs=pl.BlockSpec((bm, n), lambda i, kk: (i, 0)),
        out_shape=jax.ShapeDtypeStruct((m, n), F32),
        compiler_params=_cparams(2),
        name="ffn_down",
    )(a, b)


def _rel_bucket(d):
    n = jnp.maximum(d, 0)
    max_exact = REL_BUCKETS // 2
    nf = jnp.maximum(n, 1).astype(F32)
    large = max_exact + (jnp.log(nf / max_exact) / math.log(REL_MAX_DIST / max_exact)
                         * (REL_BUCKETS - max_exact)).astype(jnp.int32)
    large = jnp.minimum(large, REL_BUCKETS - 1)
    return jnp.where(n < max_exact, n, large)


def _bias_kernel(rel_ref, bucket_ref, allowed_ref, o_ref):
    h = pl.program_id(0)
    bucket = bucket_ref[...]
    acc = jnp.zeros(bucket.shape, F32)
    for n in range(REL_BUCKETS):
        acc = jnp.where(bucket == n, rel_ref[n, h], acc)
    o_ref[0] = jnp.where(allowed_ref[...] > 0, acc, NEG_INF)


def _bias_table(rel_bias, dist):
    tq, tk = dist.shape
    allowed = ((dist >= 0) & (dist <= WINDOW)).astype(jnp.int32)
    full = pl.BlockSpec((tq, tk), lambda h: (0, 0))
    return pl.pallas_call(
        _bias_kernel,
        grid=(ATTN_HEADS,),
        in_specs=[pl.BlockSpec(memory_space=pltpu.SMEM), full, full],
        out_specs=pl.BlockSpec((1, tq, tk), lambda h: (h, 0, 0)),
        out_shape=jax.ShapeDtypeStruct((ATTN_HEADS, tq, tk), F32),
        compiler_params=_cparams(1),
        name="rel_bias_table",
    )(rel_bias.astype(F32), _rel_bucket(dist), allowed)


def _softmax_pv(s, sink, vv):
    m = jnp.maximum(jnp.max(s, axis=-1, keepdims=True), sink)
    p = jnp.exp(s - m)
    denom = jnp.sum(p, axis=-1, keepdims=True) + jnp.exp(sink - m)
    return p.astype(BF16), denom


def _attn_prompt_kernel(sink_ref, q_ref, kc_ref, kp_ref, vc_ref, vp_ref, bias_ref, o_ref):
    h = pl.program_id(0)
    i = pl.program_id(1)
    c = WINDOW
    kk = jnp.concatenate([kp_ref[...], kc_ref[...]], axis=0).astype(BF16)
    vv = jnp.concatenate([vp_ref[...], vc_ref[...]], axis=0).astype(BF16)
    col = lax.broadcasted_iota(jnp.int32, (c, 2 * c), 1)
    no_prev = jnp.logical_and(i == 0, col < c)
    outs = []
    for g in range(ATTN_GROUP):
        q = q_ref[:, g * HEAD_DIM:(g + 1) * HEAD_DIM].astype(BF16)
        s = lax.dot_general(q, kk, (((1,), (1,)), ((), ())), preferred_element_type=F32)
        s = s * (HEAD_DIM ** -0.5) + bias_ref[g]
        s = jnp.where(no_prev, NEG_INF, s)
        p, denom = _softmax_pv(s, sink_ref[h * ATTN_GROUP + g], vv)
        outs.append(jnp.dot(p, vv, preferred_element_type=F32) / denom)
    o_ref[...] = jnp.concatenate(outs, axis=1).astype(o_ref.dtype)


def _attn_prompt(proj, bias, sinks, t):
    c = WINDOW
    gw = ATTN_GROUP * HEAD_DIM
    kb, vb = K_OFF // HEAD_DIM, V_OFF // HEAD_DIM
    cur = lambda base: pl.BlockSpec((c, HEAD_DIM), lambda h, i: (i, base + h))
    prev = lambda base: pl.BlockSpec((c, HEAD_DIM), lambda h, i: (jnp.maximum(i - 1, 0), base + h))
    return pl.pallas_call(
        _attn_prompt_kernel,
        grid=(ATTN_KV_HEADS, t // c),
        in_specs=[pl.BlockSpec(memory_space=pltpu.SMEM),
                  pl.BlockSpec((c, gw), lambda h, i: (i, h)),
                  cur(kb), prev(kb), cur(vb), prev(vb),
                  pl.BlockSpec((ATTN_GROUP, c, 2 * c), lambda h, i: (h, 0, 0))],
        out_specs=pl.BlockSpec((c, gw), lambda h, i: (i, h)),
        out_shape=jax.ShapeDtypeStruct((t, ATTN_WIDTH), BF16),
        compiler_params=_cparams(2),
        name="attn_prompt",
    )(sinks.astype(F32), proj, proj, proj, proj, proj, bias)


def _attn_sample_kernel(q_ref, kn_ref, vn_ref, ck_ref, cv_ref, bias_ref, sink_ref,
                        o_ref, ok_ref, ov_ref, kf_ref, vf_ref, *, t_new):
    l = WINDOW

    @pl.when(jnp.logical_and(pl.program_id(0) == 0, pl.program_id(1) == 0))
    def _():
        kf_ref[:, l:, :] = jnp.zeros_like(kf_ref[:, l:, :])
        vf_ref[:, l:, :] = jnp.zeros_like(vf_ref[:, l:, :])

    kf_ref[:, 0:l, :] = ck_ref[...]
    vf_ref[:, 0:l, :] = cv_ref[...]
    kf_ref[:, l:l + t_new, :] = kn_ref[...]
    vf_ref[:, l:l + t_new, :] = vn_ref[...]
    ok_ref[...] = kf_ref[:, t_new:l + t_new, :]
    ov_ref[...] = vf_ref[:, t_new:l + t_new, :]

    q = q_ref[...].astype(BF16)
    s = jnp.einsum("bqd,bkd->bqk", q, kf_ref[...].astype(BF16), preferred_element_type=F32)
    s = s * (HEAD_DIM ** -0.5) + bias_ref[...]
    p, denom = _softmax_pv(s, sink_ref[...], None)
    o = jnp.einsum("bqk,bkd->bqd", p, vf_ref[...].astype(BF16), preferred_element_type=F32)
    o_ref[...] = (o / denom).astype(o_ref.dtype)


def _attn_sample(q, k_new, v_new, cache_k, cache_v, bias, sink_rows, gb):
    b, _, rows, _ = q.shape
    t_new = k_new.shape[1]
    l = WINDOW
    new_spec = pl.BlockSpec((gb, t_new, HEAD_DIM), lambda i, h: (i, 0, h))
    cache_spec = pl.BlockSpec((gb, l, HEAD_DIM), lambda i, h: (i, 0, h))
    q_spec = pl.BlockSpec((gb, None, rows, HEAD_DIM), lambda i, h: (i, h, 0, 0))
    return pl.pallas_call(
        functools.partial(_attn_sample_kernel, t_new=t_new),
        grid=(b // gb, ATTN_KV_HEADS),
        in_specs=[q_spec, new_spec, new_spec, cache_spec, cache_spec,
                  pl.BlockSpec((1, rows, 2 * l), lambda i, h: (h, 0, 0)),
                  pl.BlockSpec((1, rows, 1), lambda i, h: (h, 0, 0))],
        out_specs=[q_spec, cache_spec, cache_spec],
        out_shape=[jax.ShapeDtypeStruct(q.shape, BF16),
                   jax.ShapeDtypeStruct(cache_k.shape, F32),
                   jax.ShapeDtypeStruct(cache_v.shape, F32)],
        scratch_shapes=[pltpu.VMEM((gb, 2 * l, HEAD_DIM), F32), pltpu.VMEM((gb, 2 * l, HEAD_DIM), F32)],
        compiler_params=_cparams(2),
        name="attn_sample",
    )(q, k_new, v_new, cache_k, cache_v, bias, sink_rows)


def _split3(x):
    hi = x.astype(BF16)
    r = x - hi.astype(F32)
    mid = r.astype(BF16)
    lo = (r - mid.astype(F32)).astype(BF16)
    return hi, mid, lo


def _dot(a, b):
    return jnp.dot(a, b, preferred_element_type=F32)


def _select_dot(sel, x):
    hi, mid, lo = _split3(x)
    return _dot(sel, hi) + _dot(sel, mid) + _dot(sel, lo)


def _dot_select(x, sel):
    hi, mid, lo = _split3(x)
    return _dot(hi, sel) + _dot(mid, sel) + _dot(lo, sel)


def _mm3(a, b):
    a_hi = a.astype(BF16)
    a_lo = (a - a_hi.astype(F32)).astype(BF16)
    b_hi = b.astype(BF16)
    b_lo = (b - b_hi.astype(F32)).astype(BF16)
    return _dot(a_hi, b_hi) + _dot(a_hi, b_lo) + _dot(a_lo, b_hi)


def _softplus(x):
    return jnp.maximum(x, 0.0) + jnp.log1p(jnp.exp(-jnp.abs(x)))


def _conv_silu(xf, w):
    y = xf * w[GDN_CONV - 1:GDN_CONV]
    for s in range(1, GDN_CONV):
        y = y + pltpu.roll(xf, s, 0) * w[GDN_CONV - 1 - s:GDN_CONV - s]
    return y * jax.nn.sigmoid(y)


def _l2norm_heads(x, scale):
    outs = []
    for hh in range(GDN_HB):
        xh = x[:, hh * HEAD_DIM:(hh + 1) * HEAD_DIM]
        outs.append(xh * (lax.rsqrt(jnp.sum(xh * xh, axis=-1, keepdims=True) + NORM_EPS) * scale))
    return jnp.concatenate(outs, axis=1)


def _gdn_kernel(*refs, rows, seq, carry, live_from):
    c = GDN_CHUNK
    nc = rows // c
    nseq = c // seq
    lw = GDN_HB * HEAD_DIM
    if carry:
        (xq_ref, xk_ref, xv_ref, tq_ref, tk_ref, tv_ref, z_ref, wq_ref, wk_ref, wv_ref, gn_ref,
         abc_ref, abr_ref, pc_ref, pr_ref, og_ref, s_ref,
         kn_s, kb_s, qn_s, vb_s, kbe_s, qe_s, kd_s, egc_s, egl_s, gcr_s) = refs
        s0_ref = None
    else:
        (xq_ref, xk_ref, xv_ref, z_ref, wq_ref, wk_ref, wv_ref, gn_ref,
         abc_ref, abr_ref, pc_ref, pr_ref, s0_ref, og_ref, s_ref,
         kn_s, kb_s, qn_s, vb_s, kbe_s, qe_s, kd_s, egc_s, egl_s, gcr_s) = refs

    step = pl.program_id(1)
    row = lax.broadcasted_iota(jnp.int32, (rows, 1), 0)
    live = (row % seq) >= live_from

    def conv(x_ref, t_ref, w_ref):
        if carry:
            tail = jnp.where(step == 0, 0.0, t_ref[...])
            y = _conv_silu(jnp.concatenate([tail, x_ref[...]], axis=0), w_ref[...])[SUBLANES:]
        else:
            y = _conv_silu(x_ref[...], w_ref[...])
        return jnp.where(live, y, 0.0)

    qn = _l2norm_heads(conv(xq_ref, tq_ref if carry else None, wq_ref), HEAD_DIM ** -0.5)
    kn = _l2norm_heads(conv(xk_ref, tk_ref if carry else None, wk_ref), 1.0)
    v = conv(xv_ref, tv_ref if carry else None, wv_ref)

    ab = abc_ref[...]
    g_col = jnp.where(live, -jnp.exp(pc_ref[0:1, :]) * _softplus(ab + pc_ref[1:2, :]), 0.0)
    beta_col = jnp.where(live, jax.nn.sigmoid(ab), 0.0)
    ri = lax.broadcasted_iota(jnp.int32, (rows, rows), 0)
    ci = lax.broadcasted_iota(jnp.int32, (rows, rows), 1)
    same = (ri // seq) == (ci // seq)
    cum_sel = jnp.where(jnp.logical_and(same, ci <= ri), 1.0, 0.0).astype(BF16)
    tot_sel = jnp.where(same, 1.0, 0.0).astype(BF16)
    gc_col = _select_dot(cum_sel, g_col)
    gl_col = _select_dot(tot_sel, g_col)
    li = lax.broadcasted_iota(jnp.int32, (LANES, lw), 0)
    ni = lax.broadcasted_iota(jnp.int32, (LANES, lw), 1)
    spread_g = jnp.where(li == ni // HEAD_DIM, 1.0, 0.0).astype(BF16)
    spread_b = jnp.where(li == ni // HEAD_DIM + GDN_HB, 1.0, 0.0).astype(BF16)
    egc = _dot_select(gc_col, spread_g)
    egl = _dot_select(gl_col, spread_g)
    eb = _dot_select(beta_col, spread_b)
    decay_in = jnp.exp(egc)
    kb = kn * eb
    kn_s[...] = kn
    kb_s[...] = kb
    qn_s[...] = qn
    vb_s[...] = v * eb
    kbe_s[...] = kb * decay_in
    qe_s[...] = qn * decay_in
    kd_s[...] = kn * jnp.exp(egl - egc)
    egc_s[...] = egc
    egl_s[...] = egl

    cr = lax.broadcasted_iota(jnp.int32, (c, c), 0)
    cc = lax.broadcasted_iota(jnp.int32, (c, c), 1)
    same_c = (cr // seq) == (cc // seq)
    causal = jnp.logical_and(same_c, cr >= cc)
    strict = jnp.logical_and(same_c, cr > cc)
    cum_sel_t = jnp.where(jnp.logical_and(same_c, cr <= cc), 1.0, 0.0).astype(BF16)
    live_r = (lax.broadcasted_iota(jnp.int32, (SUBLANES, c), 1) % seq) >= live_from
    for ch in range(nc):
        g_row = jnp.where(live_r, -jnp.exp(pr_ref[0, :, 0:c]) * _softplus(abr_ref[ch] + pr_ref[1, :, 0:c]), 0.0)
        gcr_s[ch] = _dot_select(g_row, cum_sel_t)

    if carry:
        @pl.when(step == 0)
        def _():
            s_ref[...] = jnp.zeros_like(s_ref)

    eye = jnp.where(cr == cc, 1.0, 0.0)
    seq_of_row = lax.broadcasted_iota(jnp.int32, (c, HEAD_DIM), 0) // seq
    n_levels = max(1, math.ceil(math.log2(seq - live_from)))

    def chunk(ch, _):
        r0 = pl.multiple_of(ch * c, c)
        rs = pl.ds(r0, c)
        for hh in range(GDN_HB):
            hs = slice(hh * HEAD_DIM, (hh + 1) * HEAD_DIM)
            k_h = kn_s[rs, hs]
            gcol = egc_s[rs, hh * HEAD_DIM:hh * HEAD_DIM + c]
            grow = gcr_s[ch, hh:hh + 1, :]
            decay = jnp.exp(jnp.where(causal, gcol - grow, NEG_INF))
            kq = jnp.concatenate([kb_s[rs, hs], qn_s[rs, hs]], axis=0).astype(BF16)
            kk = lax.dot_general(kq, k_h.astype(BF16), (((1,), (1,)), ((), ())),
                                 preferred_element_type=F32) * jnp.concatenate([decay, decay], axis=0)
            a_mat = jnp.where(strict, kk[:c], 0.0)
            qk = kk[c:]
            t_mat = eye - a_mat
            a_pow = a_mat
            for _lvl in range(n_levels - 1):
                a_pow = _mm3(a_pow, a_pow)
                t_mat = t_mat + _mm3(t_mat, a_pow)
            uw = _dot(t_mat.astype(BF16), jnp.concatenate([vb_s[rs, hs], kbe_s[rs, hs]], axis=1).astype(BF16))
            u, w = uw[:, :HEAD_DIM], uw[:, HEAD_DIM:]
            q_e = qe_s[rs, hs]
            k_d = kd_s[rs, hs]
            if carry:
                s_old = s_ref[hh]
                wq = _dot(jnp.concatenate([w, q_e], axis=0).astype(BF16), s_old.astype(BF16))
                w_s, q_s = wq[:c], wq[c:]
            else:
                parts = []
                for b in range(nseq):
                    bs = slice(b * seq, (b + 1) * seq)
                    parts.append(_dot(jnp.concatenate([w[bs], q_e[bs]], axis=0).astype(BF16),
                                      s0_ref[ch * nseq + b, hh].astype(BF16)))
                w_s = jnp.concatenate([p[:seq] for p in parts], axis=0)
                q_s = jnp.concatenate([p[seq:] for p in parts], axis=0)
            v_new = u - w_s
            v_new_b = v_new.astype(BF16)
            o = q_s + _dot(qk.astype(BF16), v_new_b)
            if carry:
                s_ref[hh] = s_old * jnp.exp(egl_s[pl.ds(r0, 1), hs]) + lax.dot_general(
                    k_d.astype(BF16), v_new_b, (((0,), (0,)), ((), ())), preferred_element_type=F32)
            else:
                for b in range(nseq):
                    k_db = jnp.where(seq_of_row == b, k_d, 0.0).astype(BF16)
                    s_ref[ch * nseq + b, hh] = (
                        s0_ref[ch * nseq + b, hh] * jnp.exp(egl_s[pl.ds(r0 + b * seq, 1), hs])
                        + lax.dot_general(k_db, v_new_b, (((0,), (0,)), ((), ())), preferred_element_type=F32))
            z = z_ref[rs, hs]
            og_ref[rs, hs] = (_rms(o, gn_ref[...]) * (z * jax.nn.sigmoid(z))).astype(og_ref.dtype)
        return 0

    lax.fori_loop(0, nc, chunk, 0)


def _gdn_params(a_log, dt_bias):
    al = a_log.astype(F32).reshape(GDN_NG, GDN_HB)
    db = dt_bias.astype(F32).reshape(GDN_NG, GDN_HB)
    pad_l = lambda x: jnp.pad(x, ((0, 0), (0, LANES - x.shape[1])))
    col = jnp.stack([pad_l(al), pad_l(db)], axis=1)
    col = jnp.pad(col, ((0, 0), (0, SUBLANES - 2), (0, 0)))
    pad_s = lambda x: jnp.broadcast_to(jnp.pad(x, ((0, 0), (0, SUBLANES - GDN_HB)))[:, :, None],
                                       (GDN_NG, SUBLANES, LANES))
    return col, jnp.stack([pad_s(al), pad_s(db)], axis=1)


def _gdn_ab_layouts(ab_rows):
    n = ab_rows.shape[0]
    a = ab_rows[:, :GDN_HEADS].reshape(n, GDN_NG, GDN_HB)
    b = ab_rows[:, GDN_HEADS:].reshape(n, GDN_NG, GDN_HB)
    col = jnp.concatenate([a, b], axis=-1).transpose(1, 0, 2)
    col = jnp.pad(col, ((0, 0), (0, 0), (0, LANES - 2 * GDN_HB)))
    rowf = a.reshape(n // GDN_CHUNK, GDN_CHUNK, GDN_NG, GDN_HB).transpose(2, 0, 3, 1)
    rowf = jnp.pad(rowf, ((0, 0), (0, 0), (0, SUBLANES - GDN_HB), (0, 0)))
    return col, rowf


def _gdn_scratch(rows):
    lw = GDN_HB * HEAD_DIM
    return [pltpu.VMEM((rows, lw), F32) for _ in range(9)] + [
        pltpu.VMEM((rows // GDN_CHUNK, SUBLANES, GDN_CHUNK), F32)]


def _gdn_prompt(proj, ab, conv_w, gnorm_w, pcol, prow, t, rows):
    lw = GDN_HB * HEAD_DIM
    col, rowf = _gdn_ab_layouts(ab[:t])
    nc = rows // GDN_CHUNK
    x_spec = lambda off: pl.BlockSpec((rows, lw), lambda g, i: (i, off // lw + g))
    tail_spec = lambda off: pl.BlockSpec(
        (SUBLANES, lw), lambda g, i: (jnp.maximum(i * (rows // SUBLANES) - 1, 0), off // lw + g))
    w_spec = lambda off: pl.BlockSpec((GDN_CONV, lw), lambda g, i: (0, (off - GQ_OFF) // lw + g))
    return pl.pallas_call(
        functools.partial(_gdn_kernel, rows=rows, seq=GDN_CHUNK, carry=True, live_from=0),
        grid=(GDN_NG, t // rows),
        in_specs=[x_spec(GQ_OFF), x_spec(GK_OFF), x_spec(GV_OFF),
                  tail_spec(GQ_OFF), tail_spec(GK_OFF), tail_spec(GV_OFF),
                  x_spec(Z_OFF), w_spec(GQ_OFF), w_spec(GK_OFF), w_spec(GV_OFF),
                  pl.BlockSpec((1, HEAD_DIM), lambda g, i: (0, 0)),
                  pl.BlockSpec((None, rows, LANES), lambda g, i: (g, i, 0)),
                  pl.BlockSpec((None, nc, SUBLANES, GDN_CHUNK), lambda g, i: (g, i, 0, 0)),
                  pl.BlockSpec((None, SUBLANES, LANES), lambda g, i: (g, 0, 0)),
                  pl.BlockSpec((None, 2, SUBLANES, LANES), lambda g, i: (g, 0, 0, 0))],
        out_specs=[pl.BlockSpec((rows, lw), lambda g, i: (i, g)),
                   pl.BlockSpec((GDN_HB, HEAD_DIM, HEAD_DIM), lambda g, i: (g, 0, 0))],
        out_shape=[jax.ShapeDtypeStruct((t, GDN_WIDTH), BF16),
                   jax.ShapeDtypeStruct((GDN_HEADS, HEAD_DIM, HEAD_DIM), F32)],
        scratch_shapes=_gdn_scratch(rows),
        compiler_params=_cparams(2),
        name="gdn_prompt",
    )(proj, proj, proj, proj, proj, proj, proj, conv_w, conv_w, conv_w,
      gnorm_w.reshape(1, HEAD_DIM).astype(F32), col, rowf, pcol, prow)


def _gdn_sample(xq, xk, xv, z, ab_rows, state, conv_w, gnorm_w, pcol, prow, live_from):
    n = xq.shape[0]
    rows = GDN_CHUNK
    lw = GDN_HB * HEAD_DIM
    nseq = rows // SAMPLE_PAD
    col, rowf = _gdn_ab_layouts(ab_rows)
    x_spec = pl.BlockSpec((rows, lw), lambda g, i: (i, g))
    w_spec = lambda off: pl.BlockSpec((GDN_CONV, lw), lambda g, i: (0, (off - GQ_OFF) // lw + g))
    s_spec = pl.BlockSpec((nseq, GDN_HB, HEAD_DIM, HEAD_DIM), lambda g, i: (i, g, 0, 0))
    return pl.pallas_call(
        functools.partial(_gdn_kernel, rows=rows, seq=SAMPLE_PAD, carry=False, live_from=live_from),
        grid=(GDN_NG, n // rows),
        in_specs=[x_spec, x_spec, x_spec, x_spec, w_spec(GQ_OFF), w_spec(GK_OFF), w_spec(GV_OFF),
                  pl.BlockSpec((1, HEAD_DIM), lambda g, i: (0, 0)),
                  pl.BlockSpec((None, rows, LANES), lambda g, i: (g, i, 0)),
                  pl.BlockSpec((None, 1, SUBLANES, GDN_CHUNK), lambda g, i: (g, i, 0, 0)),
                  pl.BlockSpec((None, SUBLANES, LANES), lambda g, i: (g, 0, 0)),
                  pl.BlockSpec((None, 2, SUBLANES, LANES), lambda g, i: (g, 0, 0, 0)),
                  s_spec],
        out_specs=[x_spec, s_spec],
        out_shape=[jax.ShapeDtypeStruct((n, GDN_WIDTH), BF16), jax.ShapeDtypeStruct(state.shape, F32)],
        scratch_shapes=_gdn_scratch(rows),
        compiler_params=_cparams(2),
        name="gdn_sample",
    )(xq, xk, xv, z, conv_w, conv_w, conv_w, gnorm_w.reshape(1, HEAD_DIM).astype(F32),
      col, rowf, pcol, prow, state)


def _layer(x_prompt, x_sample, win_k, win_v, conv_state, gdn_state, rel_bias,
           w_in, sinks, conv_w, a_log, dt_bias, gnorm_w, w_out,
           n_pre_mix, n_post_mix, n_pre_ffn, n_post_ffn, w_gate, w_up, w_down, tiles):
    _, t, d = x_prompt.shape
    nb, ts, _ = x_sample.shape
    ms = nb * ts
    m = t + ms
    l = win_k.shape[1]
    ff = w_gate.shape[1]
    ff_pad = -(-ff // tiles["ff_bn"]) * tiles["ff_bn"]

    x_all = jnp.concatenate([x_prompt.reshape(t, d), x_sample.reshape(ms, d)], axis=0)
    w_in_b = w_in.astype(BF16)
    w_in_tail = jnp.pad(w_in_b[:, PROJ_MAIN:], ((0, 0), (0, LANES - AB_WIDTH)))
    w_out_b = w_out.astype(BF16)
    pad_ff = lambda w: jnp.pad(w.astype(BF16), ((0, 0), (0, ff_pad - ff)))
    w_gate_b, w_up_b = pad_ff(w_gate), pad_ff(w_up)
    w_down_b = jnp.pad(w_down.astype(BF16), ((0, ff_pad - ff), (0, 0)))

    h = _rmsnorm(x_all, n_pre_mix, BF16, tiles["norm_rows"])
    proj = _matmul([h], [w_in_b[:, :PROJ_MAIN]], F32, tiles["bm"], tiles["in_bn"], "in_proj")
    ab = _matmul([h], [w_in_tail], F32, tiles["bm"], LANES, "in_proj_ab")[:, :AB_WIDTH]

    qi = jnp.arange(WINDOW)[:, None]
    kj = jnp.arange(2 * WINDOW)[None, :]
    dist_p = WINDOW + qi - kj
    dist_s = l + jnp.arange(SUBLANES)[:, None] - kj
    dist_s = jnp.where(kj < l + ts, dist_s, -1)
    bias_all = _bias_table(rel_bias, jnp.concatenate([dist_p, dist_s], axis=0))
    attn_p = _attn_prompt(proj, bias_all[:, :WINDOW], sinks, t)

    samp = proj[t:]
    q_s = samp[:, Q_OFF:Q_OFF + ATTN_WIDTH].reshape(nb, ts, ATTN_KV_HEADS, ATTN_GROUP, HEAD_DIM)
    q_s = q_s.transpose(0, 2, 1, 3, 4).reshape(nb, ATTN_KV_HEADS, ts * ATTN_GROUP, HEAD_DIM)
    bias_s = bias_all[:, WINDOW:WINDOW + ts].reshape(ATTN_KV_HEADS, ATTN_GROUP, ts, 2 * WINDOW)
    bias_s = bias_s.transpose(0, 2, 1, 3).reshape(ATTN_KV_HEADS, ts * ATTN_GROUP, 2 * WINDOW)
    sink_rows = jnp.broadcast_to(sinks.astype(F32).reshape(ATTN_KV_HEADS, 1, ATTN_GROUP),
                                 (ATTN_KV_HEADS, ts, ATTN_GROUP)).reshape(ATTN_KV_HEADS, ts * ATTN_GROUP, 1)
    attn_s, new_k, new_v = _attn_sample(
        q_s, samp[:, K_OFF:K_OFF + KV_WIDTH].reshape(nb, ts, KV_WIDTH),
        samp[:, V_OFF:V_OFF + KV_WIDTH].reshape(nb, ts, KV_WIDTH),
        win_k.reshape(nb, l, KV_WIDTH), win_v.reshape(nb, l, KV_WIDTH), bias_s, sink_rows, tiles["attn_gb"])
    attn_s = attn_s.reshape(nb, ATTN_KV_HEADS, ts, ATTN_GROUP, HEAD_DIM).transpose(0, 2, 1, 3, 4)
    attn_all = jnp.concatenate([attn_p, attn_s.reshape(ms, ATTN_WIDTH)], axis=0)

    pcol, prow = _gdn_params(a_log, dt_bias)
    og_p, s_p = _gdn_prompt(proj, ab, conv_w, gnorm_w, pcol, prow, t, tiles["gdn_rows"])
    pad_front = SAMPLE_PAD - ts
    pre_s = samp[:, GQ_OFF:GQ_OFF + GDN_CONV_CH].reshape(nb, ts, GDN_CONV_CH)
    xp = jnp.concatenate([jnp.zeros((nb, pad_front - (GDN_CONV - 1), GDN_CONV_CH), F32),
                          conv_state.astype(F32), pre_s], axis=1).reshape(nb * SAMPLE_PAD, GDN_CONV_CH)
    pad_rows = lambda a: jnp.pad(a.reshape(nb, ts, -1), ((0, 0), (pad_front, 0), (0, 0))).reshape(
        nb * SAMPLE_PAD, -1)
    og_s, s_s = _gdn_sample(xp[:, :GDN_WIDTH], xp[:, GDN_WIDTH:2 * GDN_WIDTH], xp[:, 2 * GDN_WIDTH:],
                            pad_rows(samp[:, Z_OFF:Z_OFF + GDN_WIDTH]), pad_rows(ab[t:]), gdn_state,
                            conv_w, gnorm_w, pcol, prow, pad_front)
    og_s = og_s.reshape(nb, SAMPLE_PAD, GDN_WIDTH)[:, pad_front:].reshape(ms, GDN_WIDTH)
    og_all = jnp.concatenate([og_p, og_s], axis=0)

    mix = _matmul([attn_all, og_all], [w_out_b[:ATTN_WIDTH], w_out_b[ATTN_WIDTH:]], F32,
                  tiles["bm"], tiles["out_bn"], "out_proj")
    x1, h2 = _post_mix(x_all, mix, n_post_mix, n_pre_ffn, tiles["post_rows"])
    act = _gate_up(h2, w_gate_b, w_up_b, tiles["bm"], tiles["ff_bn"])
    ffn = _down(act, w_down_b, tiles["bm"], tiles["ff_bn"])
    y = _residual_norm(x1, ffn, n_post_ffn, tiles["post_rows"])

    lw = min(WINDOW, t)
    prompt_k = proj[t - lw:t, K_OFF:K_OFF + KV_WIDTH].reshape(1, lw, ATTN_KV_HEADS, HEAD_DIM)
    prompt_v = proj[t - lw:t, V_OFF:V_OFF + KV_WIDTH].reshape(1, lw, ATTN_KV_HEADS, HEAD_DIM)
    prompt_conv = proj[t - (GDN_CONV - 1):t, GQ_OFF:GQ_OFF + GDN_CONV_CH][None]
    sample_conv = pre_s[:, ts - (GDN_CONV - 1):]
    return (y[:t].reshape(1, t, d), y[t:].reshape(nb, ts, d),
            prompt_k, prompt_v, prompt_conv, s_p[None],
            new_k.reshape(nb, l, ATTN_KV_HEADS, HEAD_DIM), new_v.reshape(nb, l, ATTN_KV_HEADS, HEAD_DIM),
            sample_conv, s_s)


def _tiles(m):
    bm = m // 8 if m % (8 * 16) == 0 else m
    return dict(bm=bm, in_bn=1024, out_bn=1024, ff_bn=512, norm_rows=512, post_rows=256,
                gdn_rows=256, attn_gb=8)


def kernel(x_prompt, x_sample, cache_win_k, cache_win_v, state_conv, state_gdn, rel_bias, w_in, attn_sinks,
           gdn_conv_w, gdn_a_log, gdn_dt_bias, gdn_norm_w, w_out, norm_pre_mix, norm_post_mix, norm_pre_ffn,
           norm_post_ffn, w_gate, w_up, w_down):
    depth = w_in.shape[0]
    assert depth == 1 and x_prompt.shape[0] == 1
    m = x_prompt.shape[1] + x_sample.shape[0] * x_sample.shape[1]
    outs = _layer(x_prompt, x_sample, cache_win_k[0], cache_win_v[0], state_conv[0], state_gdn[0], rel_bias,
                  w_in[0], attn_sinks[0], gdn_conv_w[0], gdn_a_log[0], gdn_dt_bias[0], gdn_norm_w[0], w_out[0],
                  norm_pre_mix[0], norm_post_mix[0], norm_pre_ffn[0], norm_post_ffn[0],
                  w_gate[0], w_up[0], w_down[0], _tiles(m))
    yp, ys, pk, pv, pc, ps, sk, sv, sc, ss = outs
    return (yp, ys, pk[None], pv[None], pc[None], ps[None], sk[None], sv[None], sc[None], ss[None])
```

```python
import functools
import math

import jax
import jax.numpy as jnp
from jax import lax
from jax.experimental import pallas as pl
from jax.experimental.pallas import tpu as pltpu

F32 = jnp.float32
BF16 = jnp.bfloat16

HEAD_DIM = 128
ATTN_HEADS = 16
ATTN_KV_HEADS = 4
ATTN_GROUP = ATTN_HEADS // ATTN_KV_HEADS
ATTN_WIDTH = ATTN_HEADS * HEAD_DIM
KV_WIDTH = ATTN_KV_HEADS * HEAD_DIM
WINDOW = 128
REL_BUCKETS = 32
REL_MAX_DIST = 128
GDN_HEADS = 16
GDN_WIDTH = GDN_HEADS * HEAD_DIM
GDN_CONV = 4
GDN_CONV_CH = 3 * GDN_WIDTH
GDN_CHUNK = 64
NORM_EPS = 1e-6
NEG_INF = -1e30

Q_OFF = 0
K_OFF = ATTN_WIDTH
V_OFF = K_OFF + KV_WIDTH
GQ_OFF = V_OFF + KV_WIDTH
GK_OFF = GQ_OFF + GDN_WIDTH
GV_OFF = GK_OFF + GDN_WIDTH
Z_OFF = GV_OFF + GDN_WIDTH
PROJ_MAIN = Z_OFF + GDN_WIDTH
AB_WIDTH = 2 * GDN_HEADS

LANES = 128
SUBLANES = 8
GDN_HB = 4
GDN_NG = GDN_HEADS // GDN_HB
SAMPLE_PAD = 8
VMEM_LIMIT = 56 * 1024 * 1024


def _cparams(n_axes, vmem=VMEM_LIMIT):
    return pltpu.CompilerParams(dimension_semantics=("arbitrary",) * n_axes, vmem_limit_bytes=vmem)


def _rms(x, w):
    return x * lax.rsqrt(jnp.mean(x * x, axis=-1, keepdims=True) + NORM_EPS) * w


def _rmsnorm_kernel(x_ref, w_ref, o_ref):
    o_ref[...] = _rms(x_ref[...].astype(F32), w_ref[...]).astype(o_ref.dtype)


def _rmsnorm(x, w, out_dtype, rows):
    m, d = x.shape
    return pl.pallas_call(
        _rmsnorm_kernel,
        grid=(m // rows,),
        in_specs=[pl.BlockSpec((rows, d), lambda i: (i, 0)), pl.BlockSpec((1, d), lambda i: (0, 0))],
        out_specs=pl.BlockSpec((rows, d), lambda i: (i, 0)),
        out_shape=jax.ShapeDtypeStruct((m, d), out_dtype),
        compiler_params=_cparams(1),
        name="rmsnorm",
    )(x, w.reshape(1, d).astype(F32))


def _post_mix_kernel(x_ref, mix_ref, wpost_ref, wpre_ref, x1_ref, h2_ref):
    x1 = x_ref[...] + _rms(mix_ref[...], wpost_ref[...])
    x1_ref[...] = x1
    h2_ref[...] = _rms(x1, wpre_ref[...]).astype(h2_ref.dtype)


def _post_mix(x, mix, w_post, w_pre, rows):
    m, d = x.shape
    row_spec = pl.BlockSpec((rows, d), lambda i: (i, 0))
    w_spec = pl.BlockSpec((1, d), lambda i: (0, 0))
    return pl.pallas_call(
        _post_mix_kernel,
        grid=(m // rows,),
        in_specs=[row_spec, row_spec, w_spec, w_spec],
        out_specs=[row_spec, row_spec],
        out_shape=[jax.ShapeDtypeStruct((m, d), F32), jax.ShapeDtypeStruct((m, d), BF16)],
        compiler_params=_cparams(1),
        name="post_mix",
    )(x, mix, w_post.reshape(1, d).astype(F32), w_pre.reshape(1, d).astype(F32))


def _residual_norm_kernel(x_ref, y_ref, w_ref, o_ref):
    o_ref[...] = x_ref[...] + _rms(y_ref[...], w_ref[...])


def _residual_norm(x, y, w, rows):
    m, d = x.shape
    row_spec = pl.BlockSpec((rows, d), lambda i: (i, 0))
    return pl.pallas_call(
        _residual_norm_kernel,
        grid=(m // rows,),
        in_specs=[row_spec, row_spec, pl.BlockSpec((1, d), lambda i: (0, 0))],
        out_specs=row_spec,
        out_shape=jax.ShapeDtypeStruct((m, d), F32),
        compiler_params=_cparams(1),
        name="residual_norm",
    )(x, y, w.reshape(1, d).astype(F32))


def _matmul_kernel(*refs, n_pairs):
    o_ref = refs[2 * n_pairs]
    acc = None
    for a_ref, b_ref in zip(refs[:n_pairs], refs[n_pairs:2 * n_pairs]):
        d = jnp.dot(a_ref[...], b_ref[...], preferred_element_type=F32)
        acc = d if acc is None else acc + d
    o_ref[...] = acc.astype(o_ref.dtype)


def _matmul(a_list, b_list, out_dtype, bm, bn, name):
    m = a_list[0].shape[0]
    n = b_list[0].shape[1]
    in_specs = [pl.BlockSpec((bm, a.shape[1]), lambda i, j: (i, 0)) for a in a_list]
    in_specs += [pl.BlockSpec((b.shape[0], bn), lambda i, j: (0, j)) for b in b_list]
    return pl.pallas_call(
        functools.partial(_matmul_kernel, n_pairs=len(a_list)),
        grid=(m // bm, n // bn),
        in_specs=in_specs,
        out_specs=pl.BlockSpec((bm, bn), lambda i, j: (i, j)),
        out_shape=jax.ShapeDtypeStruct((m, n), out_dtype),
        compiler_params=_cparams(2),
        name=name,
    )(*a_list, *b_list)


def _gate_up_kernel(h_ref, wg_ref, wu_ref, o_ref):
    h = h_ref[...]
    g = jnp.dot(h, wg_ref[...], preferred_element_type=F32)
    u = jnp.dot(h, wu_ref[...], preferred_element_type=F32)
    o_ref[...] = (g * jax.nn.sigmoid(g) * u).astype(o_ref.dtype)


def _gate_up(h, w_gate, w_up, bm, bn):
    m, d = h.shape
    n = w_gate.shape[1]
    w_spec = pl.BlockSpec((d, bn), lambda i, j: (0, j))
    return pl.pallas_call(
        _gate_up_kernel,
        grid=(m // bm, n // bn),
        in_specs=[pl.BlockSpec((bm, d), lambda i, j: (i, 0)), w_spec, w_spec],
        out_specs=pl.BlockSpec((bm, bn), lambda i, j: (i, j)),
        out_shape=jax.ShapeDtypeStruct((m, n), BF16),
        compiler_params=_cparams(2),
        name="ffn_gate_up",
    )(h, w_gate, w_up)


def _down_kernel(a_ref, b_ref, o_ref, *, bn):
    @pl.when(pl.program_id(1) == 0)
    def _():
        o_ref[...] = jnp.zeros_like(o_ref)

    a = a_ref[...]
    for n0 in range(0, o_ref.shape[1], bn):
        o_ref[:, n0:n0 + bn] += jnp.dot(a, b_ref[:, n0:n0 + bn], preferred_element_type=F32)


def _down(a, b, bm, bk):
    m, k = a.shape
    n = b.shape[1]
    return pl.pallas_call(
        functools.partial(_down_kernel, bn=min(n, 512)),
        grid=(m // bm, k // bk),
        in_specs=[pl.BlockSpec((bm, bk), lambda i, kk: (i, kk)), pl.BlockSpec((bk, n), lambda i, kk: (kk, 0))],
        out_specs=pl.BlockSpec((bm, n), lambda i, kk: (i, 0)),
        out_shape=jax.ShapeDtypeStruct((m, n), F32),
        compiler_params=_cparams(2),
        name="ffn_down",
    )(a, b)


def _rel_bucket(d):
    n = jnp.maximum(d, 0)
    max_exact = REL_BUCKETS // 2
    nf = jnp.maximum(n, 1).astype(F32)
    large = max_exact + (jnp.log(nf / max_exact) / math.log(REL_MAX_DIST / max_exact)
                         * (REL_BUCKETS - max_exact)).astype(jnp.int32)
    large = jnp.minimum(large, REL_BUCKETS - 1)
    return jnp.where(n < max_exact, n, large)


def _bias_kernel(rel_ref, bucket_ref, allowed_ref, o_ref):
    h = pl.program_id(0)
    bucket = bucket_ref[...]
    acc = jnp.zeros(bucket.shape, F32)
    for n in range(REL_BUCKETS):
        acc = jnp.where(bucket == n, rel_ref[n, h], acc)
    o_ref[0] = jnp.where(allowed_ref[...] > 0, acc, NEG_INF)


def _bias_table(rel_bias, dist):
    tq, tk = dist.shape
    allowed = ((dist >= 0) & (dist <= WINDOW)).astype(jnp.int32)
    full = pl.BlockSpec((tq, tk), lambda h: (0, 0))
    return pl.pallas_call(
        _bias_kernel,
        grid=(ATTN_HEADS,),
        in_specs=[pl.BlockSpec(memory_space=pltpu.SMEM), full, full],
        out_specs=pl.BlockSpec((1, tq, tk), lambda h: (h, 0, 0)),
        out_shape=jax.ShapeDtypeStruct((ATTN_HEADS, tq, tk), F32),
        compiler_params=_cparams(1),
        name="rel_bias_table",
    )(rel_bias.astype(F32), _rel_bucket(dist), allowed)


def _softmax_pv(s, sink, vv):
    m = jnp.maximum(jnp.max(s, axis=-1, keepdims=True), sink)
    p = jnp.exp(s - m)
    denom = jnp.sum(p, axis=-1, keepdims=True) + jnp.exp(sink - m)
    return p.astype(BF16), denom


def _attn_prompt_kernel(sink_ref, q_ref, kc_ref, kp_ref, vc_ref, vp_ref, bias_ref, o_ref):
    h = pl.program_id(0)
    i = pl.program_id(1)
    c = WINDOW
    kk = jnp.concatenate([kp_ref[...], kc_ref[...]], axis=0).astype(BF16)
    vv = jnp.concatenate([vp_ref[...], vc_ref[...]], axis=0).astype(BF16)
    col = lax.broadcasted_iota(jnp.int32, (c, 2 * c), 1)
    no_prev = jnp.logical_and(i == 0, col < c)
    outs = []
    for g in range(ATTN_GROUP):
        q = q_ref[:, g * HEAD_DIM:(g + 1) * HEAD_DIM].astype(BF16)
        s = lax.dot_general(q, kk, (((1,), (1,)), ((), ())), preferred_element_type=F32)
        s = s * (HEAD_DIM ** -0.5) + bias_ref[g]
        s = jnp.where(no_prev, NEG_INF, s)
        p, denom = _softmax_pv(s, sink_ref[h * ATTN_GROUP + g], vv)
        outs.append(jnp.dot(p, vv, preferred_element_type=F32) / denom)
    o_ref[...] = jnp.concatenate(outs, axis=1).astype(o_ref.dtype)


def _attn_prompt(proj, bias, sinks, t):
    c = WINDOW
    gw = ATTN_GROUP * HEAD_DIM
    kb, vb = K_OFF // HEAD_DIM, V_OFF // HEAD_DIM
    cur = lambda base: pl.BlockSpec((c, HEAD_DIM), lambda h, i: (i, base + h))
    prev = lambda base: pl.BlockSpec((c, HEAD_DIM), lambda h, i: (jnp.maximum(i - 1, 0), base + h))
    return pl.pallas_call(
        _attn_prompt_kernel,
        grid=(ATTN_KV_HEADS, t // c),
        in_specs=[pl.BlockSpec(memory_space=pltpu.SMEM),
                  pl.BlockSpec((c, gw), lambda h, i: (i, h)),
                  cur(kb), prev(kb), cur(vb), prev(vb),
                  pl.BlockSpec((ATTN_GROUP, c, 2 * c), lambda h, i: (h, 0, 0))],
        out_specs=pl.BlockSpec((c, gw), lambda h, i: (i, h)),
        out_shape=jax.ShapeDtypeStruct((t, ATTN_WIDTH), BF16),
        compiler_params=_cparams(2),
        name="attn_prompt",
    )(sinks.astype(F32), proj, proj, proj, proj, proj, bias)


def _attn_sample_kernel(q_ref, kn_ref, vn_ref, ck_ref, cv_ref, bias_ref, sink_ref,
                        o_ref, ok_ref, ov_ref, kf_ref, vf_ref, *, t_new):
    l = WINDOW

    @pl.when(jnp.logical_and(pl.program_id(0) == 0, pl.program_id(1) == 0))
    def _():
        kf_ref[:, l:, :] = jnp.zeros_like(kf_ref[:, l:, :])
        vf_ref[:, l:, :] = jnp.zeros_like(vf_ref[:, l:, :])

    kf_ref[:, 0:l, :] = ck_ref[...]
    vf_ref[:, 0:l, :] = cv_ref[...]
    kf_ref[:, l:l + t_new, :] = kn_ref[...]
    vf_ref[:, l:l + t_new, :] = vn_ref[...]
    ok_ref[...] = kf_ref[:, t_new:l + t_new, :]
    ov_ref[...] = vf_ref[:, t_new:l + t_new, :]

    q = q_ref[...].astype(BF16)
    s = jnp.einsum("bqd,bkd->bqk", q, kf_ref[...].astype(BF16), preferred_element_type=F32)
    s = s * (HEAD_DIM ** -0.5) + bias_ref[...]
    p, denom = _softmax_pv(s, sink_ref[...], None)
    o = jnp.einsum("bqk,bkd->bqd", p, vf_ref[...].astype(BF16), preferred_element_type=F32)
    o_ref[...] = (o / denom).astype(o_ref.dtype)


def _attn_sample(q, k_new, v_new, cache_k, cache_v, bias, sink_rows, gb):
    b, _, rows, _ = q.shape
    t_new = k_new.shape[1]
    l = WINDOW
    new_spec = pl.BlockSpec((gb, t_new, HEAD_DIM), lambda i, h: (i, 0, h))
    cache_spec = pl.BlockSpec((gb, l, HEAD_DIM), lambda i, h: (i, 0, h))
    q_spec = pl.BlockSpec((gb, None, rows, HEAD_DIM), lambda i, h: (i, h, 0, 0))
    return pl.pallas_call(
        functools.partial(_attn_sample_kernel, t_new=t_new),
        grid=(b // gb, ATTN_KV_HEADS),
        in_specs=[q_spec, new_spec, new_spec, cache_spec, cache_spec,
                  pl.BlockSpec((1, rows, 2 * l), lambda i, h: (h, 0, 0)),
                  pl.BlockSpec((1, rows, 1), lambda i, h: (h, 0, 0))],
        out_specs=[q_spec, cache_spec, cache_spec],
        out_shape=[jax.ShapeDtypeStruct(q.shape, BF16),
                   jax.ShapeDtypeStruct(cache_k.shape, F32),
                   jax.ShapeDtypeStruct(cache_v.shape, F32)],
        scratch_shapes=[pltpu.VMEM((gb, 2 * l, HEAD_DIM), F32), pltpu.VMEM((gb, 2 * l, HEAD_DIM), F32)],
        compiler_params=_cparams(2),
        name="attn_sample",
    )(q, k_new, v_new, cache_k, cache_v, bias, sink_rows)


def _split3(x):
    hi = x.astype(BF16)
    r = x - hi.astype(F32)
    mid = r.astype(BF16)
    lo = (r - mid.astype(F32)).astype(BF16)
    return hi, mid, lo


def _dot(a, b):
    return jnp.dot(a, b, preferred_element_type=F32)


def _select_dot(sel, x):
    hi, mid, lo = _split3(x)
    return _dot(sel, hi) + _dot(sel, mid) + _dot(sel, lo)


def _dot_select(x, sel):
    hi, mid, lo = _split3(x)
    return _dot(hi, sel) + _dot(mid, sel) + _dot(lo, sel)


def _mm3(a, b):
    a_hi = a.astype(BF16)
    a_lo = (a - a_hi.astype(F32)).astype(BF16)
    b_hi = b.astype(BF16)
    b_lo = (b - b_hi.astype(F32)).astype(BF16)
    return _dot(jnp.concatenate([a_hi, a_lo, a_hi], axis=1), jnp.concatenate([b_hi, b_hi, b_lo], axis=0))


def _softplus(x):
    return jnp.maximum(x, 0.0) + jnp.log1p(jnp.exp(-jnp.abs(x)))


def _conv_silu(xf, w):
    y = xf * w[GDN_CONV - 1:GDN_CONV]
    for s in range(1, GDN_CONV):
        y = y + pltpu.roll(xf, s, 0) * w[GDN_CONV - 1 - s:GDN_CONV - s]
    return y * jax.nn.sigmoid(y)


def _l2norm_heads(x, scale):
    outs = []
    for hh in range(GDN_HB):
        xh = x[:, hh * HEAD_DIM:(hh + 1) * HEAD_DIM]
        outs.append(xh * (lax.rsqrt(jnp.sum(xh * xh, axis=-1, keepdims=True) + NORM_EPS) * scale))
    return jnp.concatenate(outs, axis=1)


def _gdn_kernel(*refs, rows, seq, carry, live_from):
    c = GDN_CHUNK
    nc = rows // c
    nseq = c // seq
    lw = GDN_HB * HEAD_DIM
    if carry:
        (xq_ref, xk_ref, xv_ref, tq_ref, tk_ref, tv_ref, z_ref, wq_ref, wk_ref, wv_ref, gn_ref,
         abc_ref, abr_ref, pc_ref, pr_ref, og_ref, s_ref,
         kn_s, kb_s, qn_s, vb_s, kbe_s, qe_s, kd_s, egc_s, egl_s, gcr_s) = refs
        s0_ref = None
    else:
        (xq_ref, xk_ref, xv_ref, z_ref, wq_ref, wk_ref, wv_ref, gn_ref,
         abc_ref, abr_ref, pc_ref, pr_ref, s0_ref, og_ref, s_ref,
         kn_s, kb_s, qn_s, vb_s, kbe_s, qe_s, kd_s, egc_s, egl_s, gcr_s) = refs

    step = pl.program_id(1)
    row = lax.broadcasted_iota(jnp.int32, (rows, 1), 0)
    live = (row % seq) >= live_from

    def conv(x_ref, t_ref, w_ref):
        if carry:
            tail = jnp.where(step == 0, 0.0, t_ref[...])
            y = _conv_silu(jnp.concatenate([tail, x_ref[...]], axis=0), w_ref[...])[SUBLANES:]
        else:
            y = _conv_silu(x_ref[...], w_ref[...])
        return jnp.where(live, y, 0.0)

    qn = _l2norm_heads(conv(xq_ref, tq_ref if carry else None, wq_ref), HEAD_DIM ** -0.5)
    kn = _l2norm_heads(conv(xk_ref, tk_ref if carry else None, wk_ref), 1.0)
    v = conv(xv_ref, tv_ref if carry else None, wv_ref)

    ab = abc_ref[...]
    g_col = jnp.where(live, -jnp.exp(pc_ref[0:1, :]) * _softplus(ab + pc_ref[1:2, :]), 0.0)
    beta_col = jnp.where(live, jax.nn.sigmoid(ab), 0.0)
    ri = lax.broadcasted_iota(jnp.int32, (rows, rows), 0)
    ci = lax.broadcasted_iota(jnp.int32, (rows, rows), 1)
    same = (ri // seq) == (ci // seq)
    cum_sel = jnp.where(jnp.logical_and(same, ci <= ri), 1.0, 0.0).astype(BF16)
    tot_sel = jnp.where(same, 1.0, 0.0).astype(BF16)
    gc_col = _select_dot(cum_sel, g_col)
    gl_col = _select_dot(tot_sel, g_col)
    li = lax.broadcasted_iota(jnp.int32, (LANES, lw), 0)
    ni = lax.broadcasted_iota(jnp.int32, (LANES, lw), 1)
    spread_g = jnp.where(li == ni // HEAD_DIM, 1.0, 0.0).astype(BF16)
    spread_b = jnp.where(li == ni // HEAD_DIM + GDN_HB, 1.0, 0.0).astype(BF16)
    egc = _dot_select(gc_col, spread_g)
    egl = _dot_select(gl_col, spread_g)
    eb = _dot_select(beta_col, spread_b)
    decay_in = jnp.exp(egc)
    kb = kn * eb
    kn_s[...] = kn
    kb_s[...] = kb
    qn_s[...] = qn
    vb_s[...] = v * eb
    kbe_s[...] = kb * decay_in
    qe_s[...] = qn * decay_in
    kd_s[...] = kn * jnp.exp(egl - egc)
    egc_s[...] = egc
    egl_s[...] = egl

    cr = lax.broadcasted_iota(jnp.int32, (c, c), 0)
    cc = lax.broadcasted_iota(jnp.int32, (c, c), 1)
    same_c = (cr // seq) == (cc // seq)
    causal = jnp.logical_and(same_c, cr >= cc)
    strict = jnp.logical_and(same_c, cr > cc)
    cum_sel_t = jnp.where(jnp.logical_and(same_c, cr <= cc), 1.0, 0.0).astype(BF16)
    live_r = (lax.broadcasted_iota(jnp.int32, (SUBLANES, c), 1) % seq) >= live_from
    for ch in range(nc):
        g_row = jnp.where(live_r, -jnp.exp(pr_ref[0, :, 0:c]) * _softplus(abr_ref[ch] + pr_ref[1, :, 0:c]), 0.0)
        gcr_s[ch] = _dot_select(g_row, cum_sel_t)

    if carry:
        @pl.when(step == 0)
        def _():
            s_ref[...] = jnp.zeros_like(s_ref)

    eye = jnp.where(cr == cc, 1.0, 0.0)
    seq_of_row = lax.broadcasted_iota(jnp.int32, (c, HEAD_DIM), 0) // seq
    n_levels = max(1, math.ceil(math.log2(seq - live_from)))

    pairs = [(ch, hh) for ch in range(nc) for hh in range(GDN_HB)]
    rows_of = lambda ch: slice(ch * c, (ch + 1) * c)
    lanes_of = lambda hh: slice(hh * HEAD_DIM, (hh + 1) * HEAD_DIM)
    uw_all, qk_all, a_pow, t_mat = {}, {}, {}, {}
    for ch, hh in pairs:
        rs, hs = rows_of(ch), lanes_of(hh)
        gcol = egc_s[rs, hh * HEAD_DIM:hh * HEAD_DIM + c]
        grow = gcr_s[ch, hh:hh + 1, :]
        decay = jnp.exp(jnp.where(causal, gcol - grow, NEG_INF))
        kq = jnp.concatenate([kb_s[rs, hs], qn_s[rs, hs]], axis=0).astype(BF16)
        kk = lax.dot_general(kq, kn_s[rs, hs].astype(BF16), (((1,), (1,)), ((), ())),
                             preferred_element_type=F32) * jnp.concatenate([decay, decay], axis=0)
        a_pow[ch, hh] = jnp.where(strict, kk[:c], 0.0)
        qk_all[ch, hh] = kk[c:].astype(BF16)
        t_mat[ch, hh] = eye - a_pow[ch, hh]
    for _lvl in range(n_levels - 1):
        for p in pairs:
            a_pow[p] = _mm3(a_pow[p], a_pow[p])
        for p in pairs:
            t_mat[p] = t_mat[p] + _mm3(t_mat[p], a_pow[p])
    for ch, hh in pairs:
        rs, hs = rows_of(ch), lanes_of(hh)
        uw_all[ch, hh] = _dot(t_mat[ch, hh].astype(BF16),
                              jnp.concatenate([vb_s[rs, hs], kbe_s[rs, hs]], axis=1).astype(BF16))

    if carry:
        state = [s_ref[hh] for hh in range(GDN_HB)]
    new_states = {}
    heads = range(GDN_HB)
    tn_dot = lambda a, b: lax.dot_general(a, b, (((0,), (0,)), ((), ())), preferred_element_type=F32)
    for ch in range(nc):
        rs = rows_of(ch)
        w_s, q_s, v_new_b, o = {}, {}, {}, {}
        for hh in heads:
            w = uw_all[ch, hh][:, HEAD_DIM:]
            q_e = qe_s[rs, lanes_of(hh)]
            if carry:
                wq = _dot(jnp.concatenate([w, q_e], axis=0).astype(BF16), state[hh].astype(BF16))
                w_s[hh], q_s[hh] = wq[:c], wq[c:]
            else:
                parts = []
                for b in range(nseq):
                    bs = slice(b * seq, (b + 1) * seq)
                    parts.append(_dot(jnp.concatenate([w[bs], q_e[bs]], axis=0).astype(BF16),
                                      s0_ref[ch * nseq + b, hh].astype(BF16)))
                w_s[hh] = jnp.concatenate([p[:seq] for p in parts], axis=0)
                q_s[hh] = jnp.concatenate([p[seq:] for p in parts], axis=0)
        for hh in heads:
            v_new_b[hh] = (uw_all[ch, hh][:, :HEAD_DIM] - w_s[hh]).astype(BF16)
            o[hh] = q_s[hh] + _dot(qk_all[ch, hh], v_new_b[hh])
        for hh in heads:
            hs = lanes_of(hh)
            k_d = kd_s[rs, hs]
            if carry:
                state[hh] = (state[hh] * jnp.exp(egl_s[ch * c:ch * c + 1, hs])
                             + tn_dot(k_d.astype(BF16), v_new_b[hh]))
            else:
                for b in range(nseq):
                    k_db = jnp.where(seq_of_row == b, k_d, 0.0).astype(BF16)
                    r1 = ch * c + b * seq
                    new_states[ch * nseq + b, hh] = (
                        s0_ref[ch * nseq + b, hh] * jnp.exp(egl_s[r1:r1 + 1, hs]) + tn_dot(k_db, v_new_b[hh]))
        outs = []
        for hh in heads:
            z = z_ref[rs, lanes_of(hh)]
            outs.append((_rms(o[hh], gn_ref[...]) * (z * jax.nn.sigmoid(z))).astype(og_ref.dtype))
        og_ref[rs, :] = jnp.concatenate(outs, axis=1)
    if carry:
        s_ref[...] = jnp.stack(state, axis=0)
    else:
        for (b, hh), s_new in new_states.items():
            s_ref[b, hh] = s_new


def _gdn_params(a_log, dt_bias):
    al = a_log.astype(F32).reshape(GDN_NG, GDN_HB)
    db = dt_bias.astype(F32).reshape(GDN_NG, GDN_HB)
    pad_l = lambda x: jnp.pad(x, ((0, 0), (0, LANES - x.shape[1])))
    col = jnp.stack([pad_l(al), pad_l(db)], axis=1)
    col = jnp.pad(col, ((0, 0), (0, SUBLANES - 2), (0, 0)))
    pad_s = lambda x: jnp.broadcast_to(jnp.pad(x, ((0, 0), (0, SUBLANES - GDN_HB)))[:, :, None],
                                       (GDN_NG, SUBLANES, LANES))
    return col, jnp.stack([pad_s(al), pad_s(db)], axis=1)


def _gdn_ab_layouts(ab_rows):
    n = ab_rows.shape[0]
    a = ab_rows[:, :GDN_HEADS].reshape(n, GDN_NG, GDN_HB)
    b = ab_rows[:, GDN_HEADS:].reshape(n, GDN_NG, GDN_HB)
    col = jnp.concatenate([a, b], axis=-1).transpose(1, 0, 2)
    col = jnp.pad(col, ((0, 0), (0, 0), (0, LANES - 2 * GDN_HB)))
    rowf = a.reshape(n // GDN_CHUNK, GDN_CHUNK, GDN_NG, GDN_HB).transpose(2, 0, 3, 1)
    rowf = jnp.pad(rowf, ((0, 0), (0, 0), (0, SUBLANES - GDN_HB), (0, 0)))
    return col, rowf


def _gdn_scratch(rows):
    lw = GDN_HB * HEAD_DIM
    return [pltpu.VMEM((rows, lw), F32) for _ in range(9)] + [
        pltpu.VMEM((rows // GDN_CHUNK, SUBLANES, GDN_CHUNK), F32)]


def _gdn_prompt(proj, ab, conv_w, gnorm_w, pcol, prow, t, rows):
    lw = GDN_HB * HEAD_DIM
    col, rowf = _gdn_ab_layouts(ab[:t])
    nc = rows // GDN_CHUNK
    x_spec = lambda off: pl.BlockSpec((rows, lw), lambda g, i: (i, off // lw + g))
    tail_spec = lambda off: pl.BlockSpec(
        (SUBLANES, lw), lambda g, i: (jnp.maximum(i * (rows // SUBLANES) - 1, 0), off // lw + g))
    w_spec = lambda off: pl.BlockSpec((GDN_CONV, lw), lambda g, i: (0, (off - GQ_OFF) // lw + g))
    return pl.pallas_call(
        functools.partial(_gdn_kernel, rows=rows, seq=GDN_CHUNK, carry=True, live_from=0),
        grid=(GDN_NG, t // rows),
        in_specs=[x_spec(GQ_OFF), x_spec(GK_OFF), x_spec(GV_OFF),
                  tail_spec(GQ_OFF), tail_spec(GK_OFF), tail_spec(GV_OFF),
                  x_spec(Z_OFF), w_spec(GQ_OFF), w_spec(GK_OFF), w_spec(GV_OFF),
                  pl.BlockSpec((1, HEAD_DIM), lambda g, i: (0, 0)),
                  pl.BlockSpec((None, rows, LANES), lambda g, i: (g, i, 0)),
                  pl.BlockSpec((None, nc, SUBLANES, GDN_CHUNK), lambda g, i: (g, i, 0, 0)),
                  pl.BlockSpec((None, SUBLANES, LANES), lambda g, i: (g, 0, 0)),
                  pl.BlockSpec((None, 2, SUBLANES, LANES), lambda g, i: (g, 0, 0, 0))],
        out_specs=[pl.BlockSpec((rows, lw), lambda g, i: (i, g)),
                   pl.BlockSpec((GDN_HB, HEAD_DIM, HEAD_DIM), lambda g, i: (g, 0, 0))],
        out_shape=[jax.ShapeDtypeStruct((t, GDN_WIDTH), BF16),
                   jax.ShapeDtypeStruct((GDN_HEADS, HEAD_DIM, HEAD_DIM), F32)],
        scratch_shapes=_gdn_scratch(rows),
        compiler_params=_cparams(2),
        name="gdn_prompt",
    )(proj, proj, proj, proj, proj, proj, proj, conv_w, conv_w, conv_w,
      gnorm_w.reshape(1, HEAD_DIM).astype(F32), col, rowf, pcol, prow)


def _gdn_sample(xq, xk, xv, z, ab_rows, state, conv_w, gnorm_w, pcol, prow, live_from):
    n = xq.shape[0]
    rows = GDN_CHUNK
    lw = GDN_HB * HEAD_DIM
    nseq = rows // SAMPLE_PAD
    col, rowf = _gdn_ab_layouts(ab_rows)
    x_spec = pl.BlockSpec((rows, lw), lambda g, i: (i, g))
    w_spec = lambda off: pl.BlockSpec((GDN_CONV, lw), lambda g, i: (0, (off - GQ_OFF) // lw + g))
    s_spec = pl.BlockSpec((nseq, GDN_HB, HEAD_DIM, HEAD_DIM), lambda g, i: (i, g, 0, 0))
    return pl.pallas_call(
        functools.partial(_gdn_kernel, rows=rows, seq=SAMPLE_PAD, carry=False, live_from=live_from),
        grid=(GDN_NG, n // rows),
        in_specs=[x_spec, x_spec, x_spec, x_spec, w_spec(GQ_OFF), w_spec(GK_OFF), w_spec(GV_OFF),
                  pl.BlockSpec((1, HEAD_DIM), lambda g, i: (0, 0)),
                  pl.BlockSpec((None, rows, LANES), lambda g, i: (g, i, 0)),
                  pl.BlockSpec((None, 1, SUBLANES, GDN_CHUNK), lambda g, i: (g, i, 0, 0)),
                  pl.BlockSpec((None, SUBLANES, LANES), lambda g, i: (g, 0, 0)),
                  pl.BlockSpec((None, 2, SUBLANES, LANES), lambda g, i: (g, 0, 0, 0)),
                  s_spec],
        out_specs=[x_spec, s_spec],
        out_shape=[jax.ShapeDtypeStruct((n, GDN_WIDTH), BF16), jax.ShapeDtypeStruct(state.shape, F32)],
        scratch_shapes=_gdn_scratch(rows),
        compiler_params=_cparams(2),
        name="gdn_sample",
    )(xq, xk, xv, z, conv_w, conv_w, conv_w, gnorm_w.reshape(1, HEAD_DIM).astype(F32),
      col, rowf, pcol, prow, state)


def _layer(x_prompt, x_sample, win_k, win_v, conv_state, gdn_state, rel_bias,
           w_in, sinks, conv_w, a_log, dt_bias, gnorm_w, w_out,
           n_pre_mix, n_post_mix, n_pre_ffn, n_post_ffn, w_gate, w_up, w_down, tiles):
    _, t, d = x_prompt.shape
    nb, ts, _ = x_sample.shape
    ms = nb * ts
    m = t + ms
    l = win_k.shape[1]
    ff = w_gate.shape[1]
    ff_pad = -(-ff // tiles["ff_bn"]) * tiles["ff_bn"]

    x_all = jnp.concatenate([x_prompt.reshape(t, d), x_sample.reshape(ms, d)], axis=0)
    w_in_b = w_in.astype(BF16)
    w_in_tail = jnp.pad(w_in_b[:, PROJ_MAIN:], ((0, 0), (0, LANES - AB_WIDTH)))
    w_out_b = w_out.astype(BF16)
    pad_ff = lambda w: jnp.pad(w.astype(BF16), ((0, 0), (0, ff_pad - ff)))
    w_gate_b, w_up_b = pad_ff(w_gate), pad_ff(w_up)
    w_down_b = jnp.pad(w_down.astype(BF16), ((0, ff_pad - ff), (0, 0)))

    h = _rmsnorm(x_all, n_pre_mix, BF16, tiles["norm_rows"])
    proj = _matmul([h], [w_in_b[:, :PROJ_MAIN]], F32, tiles["bm"], tiles["in_bn"], "in_proj")
    ab = _matmul([h], [w_in_tail], F32, tiles["bm"], LANES, "in_proj_ab")[:, :AB_WIDTH]

    qi = jnp.arange(WINDOW)[:, None]
    kj = jnp.arange(2 * WINDOW)[None, :]
    dist_p = WINDOW + qi - kj
    dist_s = l + jnp.arange(SUBLANES)[:, None] - kj
    dist_s = jnp.where(kj < l + ts, dist_s, -1)
    bias_all = _bias_table(rel_bias, jnp.concatenate([dist_p, dist_s], axis=0))
    attn_p = _attn_prompt(proj, bias_all[:, :WINDOW], sinks, t)

    samp = proj[t:]
    q_s = samp[:, Q_OFF:Q_OFF + ATTN_WIDTH].reshape(nb, ts, ATTN_KV_HEADS, ATTN_GROUP, HEAD_DIM)
    q_s = q_s.transpose(0, 2, 1, 3, 4).reshape(nb, ATTN_KV_HEADS, ts * ATTN_GROUP, HEAD_DIM)
    bias_s = bias_all[:, WINDOW:WINDOW + ts].reshape(ATTN_KV_HEADS, ATTN_GROUP, ts, 2 * WINDOW)
    bias_s = bias_s.transpose(0, 2, 1, 3).reshape(ATTN_KV_HEADS, ts * ATTN_GROUP, 2 * WINDOW)
    sink_rows = jnp.broadcast_to(sinks.astype(F32).reshape(ATTN_KV_HEADS, 1, ATTN_GROUP),
                                 (ATTN_KV_HEADS, ts, ATTN_GROUP)).reshape(ATTN_KV_HEADS, ts * ATTN_GROUP, 1)
    attn_s, new_k, new_v = _attn_sample(
        q_s, samp[:, K_OFF:K_OFF + KV_WIDTH].reshape(nb, ts, KV_WIDTH),
        samp[:, V_OFF:V_OFF + KV_WIDTH].reshape(nb, ts, KV_WIDTH),
        win_k.reshape(nb, l, KV_WIDTH), win_v.reshape(nb, l, KV_WIDTH), bias_s, sink_rows, tiles["attn_gb"])
    attn_s = attn_s.reshape(nb, ATTN_KV_HEADS, ts, ATTN_GROUP, HEAD_DIM).transpose(0, 2, 1, 3, 4)
    attn_all = jnp.concatenate([attn_p, attn_s.reshape(ms, ATTN_WIDTH)], axis=0)

    pcol, prow = _gdn_params(a_log, dt_bias)
    og_p, s_p = _gdn_prompt(proj, ab, conv_w, gnorm_w, pcol, prow, t, tiles["gdn_rows"])
    pad_front = SAMPLE_PAD - ts
    pre_s = samp[:, GQ_OFF:GQ_OFF + GDN_CONV_CH].reshape(nb, ts, GDN_CONV_CH)
    xp = jnp.concatenate([jnp.zeros((nb, pad_front - (GDN_CONV - 1), GDN_CONV_CH), F32),
                          conv_state.astype(F32), pre_s], axis=1).reshape(nb * SAMPLE_PAD, GDN_CONV_CH)
    pad_rows = lambda a: jnp.pad(a.reshape(nb, ts, -1), ((0, 0), (pad_front, 0), (0, 0))).reshape(
        nb * SAMPLE_PAD, -1)
    og_s, s_s = _gdn_sample(xp[:, :GDN_WIDTH], xp[:, GDN_WIDTH:2 * GDN_WIDTH], xp[:, 2 * GDN_WIDTH:],
                            pad_rows(samp[:, Z_OFF:Z_OFF + GDN_WIDTH]), pad_rows(ab[t:]), gdn_state,
                            conv_w, gnorm_w, pcol, prow, pad_front)
    og_s = og_s.reshape(nb, SAMPLE_PAD, GDN_WIDTH)[:, pad_front:].reshape(ms, GDN_WIDTH)
    og_all = jnp.concatenate([og_p, og_s], axis=0)

    mix = _matmul([attn_all, og_all], [w_out_b[:ATTN_WIDTH], w_out_b[ATTN_WIDTH:]], F32,
                  tiles["bm"], tiles["out_bn"], "out_proj")
    x1, h2 = _post_mix(x_all, mix, n_post_mix, n_pre_ffn, tiles["post_rows"])
    act = _gate_up(h2, w_gate_b, w_up_b, tiles["bm"], tiles["ff_bn"])
    ffn = _down(act, w_down_b, tiles["bm"], tiles["ff_bn"])
    y = _residual_norm(x1, ffn, n_post_ffn, tiles["post_rows"])

    lw = min(WINDOW, t)
    prompt_k = proj[t - lw:t, K_OFF:K_OFF + KV_WIDTH].reshape(1, lw, ATTN_KV_HEADS, HEAD_DIM)
    prompt_v = proj[t - lw:t, V_OFF:V_OFF + KV_WIDTH].reshape(1, lw, ATTN_KV_HEADS, HEAD_DIM)
    prompt_conv = proj[t - (GDN_CONV - 1):t, GQ_OFF:GQ_OFF + GDN_CONV_CH][None]
    sample_conv = pre_s[:, ts - (GDN_CONV - 1):]
    return (y[:t].reshape(1, t, d), y[t:].reshape(nb, ts, d),
            prompt_k, prompt_v, prompt_conv, s_p[None],
            new_k.reshape(nb, l, ATTN_KV_HEADS, HEAD_DIM), new_v.reshape(nb, l, ATTN_KV_HEADS, HEAD_DIM),
            sample_conv, s_s)


def _tiles(m):
    bm = m // 8 if m % (8 * 16) == 0 else m
    return dict(bm=bm, in_bn=1024, out_bn=1024, ff_bn=512, norm_rows=512, post_rows=256,
                gdn_rows=256, attn_gb=8)


def kernel(x_prompt, x_sample, cache_win_k, cache_win_v, state_conv, state_gdn, rel_bias, w_in, attn_sinks,
           gdn_conv_w, gdn_a_log, gdn_dt_bias, gdn_norm_w, w_out, norm_pre_mix, norm_post_mix, norm_pre_ffn,
           norm_post_ffn, w_gate, w_up, w_down):
    depth = w_in.shape[0]
    assert depth == 1 and x_prompt.shape[0] == 1
    m = x_prompt.shape[1] + x_sample.shape[0] * x_sample.shape[1]
    outs = _layer(x_prompt, x_sample, cache_win_k[0], cache_win_v[0], state_conv[0], state_gdn[0], rel_bias,
                  w_in[0], attn_sinks[0], gdn_conv_w[0], gdn_a_log[0], gdn_dt_bias[0], gdn_norm_w[0], w_out[0],
                  norm_pre_mix[0], norm_post_mix[0], norm_pre_ffn[0], norm_post_ffn[0],
                  w_gate[0], w_up[0], w_down[0], _tiles(m))
    yp, ys, pk, pv, pc, ps, sk, sv, sc, ss = outs
    return (yp, ys, pk[None], pv[None], pc[None], ps[None], sk[None], sv[None], sc[None], ss[None])
```

```python
import functools
import math

import jax
import jax.numpy as jnp
from jax import lax
from jax.experimental import pallas as pl
from jax.experimental.pallas import tpu as pltpu

F32 = jnp.float32
BF16 = jnp.bfloat16

HEAD_DIM = 128
ATTN_HEADS = 16
ATTN_KV_HEADS = 4
ATTN_GROUP = ATTN_HEADS // ATTN_KV_HEADS
ATTN_WIDTH = ATTN_HEADS * HEAD_DIM
KV_WIDTH = ATTN_KV_HEADS * HEAD_DIM
WINDOW = 128
REL_BUCKETS = 32
REL_MAX_DIST = 128
GDN_HEADS = 16
GDN_WIDTH = GDN_HEADS * HEAD_DIM
GDN_CONV = 4
GDN_CONV_CH = 3 * GDN_WIDTH
GDN_CHUNK = 64
NORM_EPS = 1e-6
NEG_INF = -1e30

Q_OFF = 0
K_OFF = ATTN_WIDTH
V_OFF = K_OFF + KV_WIDTH
GQ_OFF = V_OFF + KV_WIDTH
GK_OFF = GQ_OFF + GDN_WIDTH
GV_OFF = GK_OFF + GDN_WIDTH
Z_OFF = GV_OFF + GDN_WIDTH
PROJ_MAIN = Z_OFF + GDN_WIDTH
AB_WIDTH = 2 * GDN_HEADS

LANES = 128
SUBLANES = 8
GDN_HB = 4
GDN_NG = GDN_HEADS // GDN_HB
SAMPLE_PAD = 8
VMEM_LIMIT = 56 * 1024 * 1024


def _cparams(n_axes, vmem=VMEM_LIMIT):
    return pltpu.CompilerParams(dimension_semantics=("arbitrary",) * n_axes, vmem_limit_bytes=vmem)


def _dot(a, b):
    return jnp.dot(a, b, preferred_element_type=F32)


def _rms(x, w):
    return x * lax.rsqrt(jnp.mean(x * x, axis=-1, keepdims=True) + NORM_EPS) * w


def _rmsnorm_kernel(x_ref, w_ref, o_ref):
    o_ref[...] = _rms(x_ref[...].astype(F32), w_ref[...]).astype(o_ref.dtype)


def _rmsnorm(x, w, out_dtype, rows):
    m, d = x.shape
    return pl.pallas_call(
        _rmsnorm_kernel,
        grid=(m // rows,),
        in_specs=[pl.BlockSpec((rows, d), lambda i: (i, 0)), pl.BlockSpec((1, d), lambda i: (0, 0))],
        out_specs=pl.BlockSpec((rows, d), lambda i: (i, 0)),
        out_shape=jax.ShapeDtypeStruct((m, d), out_dtype),
        compiler_params=_cparams(1),
        name="rmsnorm",
    )(x, w.reshape(1, d).astype(F32))


def _post_mix_kernel(x_ref, mix_ref, wpost_ref, wpre_ref, x1_ref, h2_ref):
    x1 = x_ref[...] + _rms(mix_ref[...], wpost_ref[...])
    x1_ref[...] = x1
    h2_ref[...] = _rms(x1, wpre_ref[...]).astype(h2_ref.dtype)


def _post_mix(x, mix, w_post, w_pre, rows):
    m, d = x.shape
    row_spec = pl.BlockSpec((rows, d), lambda i: (i, 0))
    w_spec = pl.BlockSpec((1, d), lambda i: (0, 0))
    return pl.pallas_call(
        _post_mix_kernel,
        grid=(m // rows,),
        in_specs=[row_spec, row_spec, w_spec, w_spec],
        out_specs=[row_spec, row_spec],
        out_shape=[jax.ShapeDtypeStruct((m, d), F32), jax.ShapeDtypeStruct((m, d), BF16)],
        compiler_params=_cparams(1),
        name="post_mix",
    )(x, mix, w_post.reshape(1, d).astype(F32), w_pre.reshape(1, d).astype(F32))


def _residual_norm_kernel(x_ref, y_ref, w_ref, o_ref):
    o_ref[...] = x_ref[...] + _rms(y_ref[...], w_ref[...])


def _residual_norm(x, y, w, rows):
    m, d = x.shape
    row_spec = pl.BlockSpec((rows, d), lambda i: (i, 0))
    return pl.pallas_call(
        _residual_norm_kernel,
        grid=(m // rows,),
        in_specs=[row_spec, row_spec, pl.BlockSpec((1, d), lambda i: (0, 0))],
        out_specs=row_spec,
        out_shape=jax.ShapeDtypeStruct((m, d), F32),
        compiler_params=_cparams(1),
        name="residual_norm",
    )(x, y, w.reshape(1, d).astype(F32))


def _stack_rows(p_ref, s_ref, cols=slice(None)):
    return jnp.concatenate([p_ref[:, cols].astype(BF16), s_ref[:, cols].astype(BF16)], axis=0)


def _proj_kernel(*refs, n_a, n_w, swiglu):
    a_refs = refs[:2 * n_a]
    w_refs = refs[2 * n_a:2 * n_a + n_w]
    op_ref, os_ref = refs[2 * n_a + n_w:2 * n_a + n_w + 2]
    wb_refs = refs[2 * n_a + n_w + 2:]

    @pl.when(pl.program_id(1) == 0)
    def _():
        for w_ref, wb_ref in zip(w_refs, wb_refs):
            wb_ref[...] = w_ref[...].astype(BF16)

    lhs = [_stack_rows(a_refs[2 * p], a_refs[2 * p + 1]) for p in range(n_a)]
    if swiglu:
        g = _dot(lhs[0], wb_refs[0][...])
        o = g * jax.nn.sigmoid(g) * _dot(lhs[0], wb_refs[1][...])
    else:
        o = _dot(lhs[0], wb_refs[0][...])
        for l, wb_ref in zip(lhs[1:], wb_refs[1:]):
            o = o + _dot(l, wb_ref[...])
    bmp = op_ref.shape[0]
    op_ref[...] = o[:bmp].astype(op_ref.dtype)
    os_ref[...] = o[bmp:].astype(os_ref.dtype)


def _proj(a_pairs, w_blocks, n, bn, out_dtype, n_mt, name, swiglu=False):
    tp, ts = a_pairs[0][0].shape[0], a_pairs[0][1].shape[0]
    bmp, bms = tp // n_mt, ts // n_mt
    in_specs = []
    for a_p, a_s in a_pairs:
        in_specs.append(pl.BlockSpec((bmp, a_p.shape[1]), lambda j, i: (i, 0)))
        in_specs.append(pl.BlockSpec((bms, a_s.shape[1]), lambda j, i: (i, 0)))
    for _, rb, k in w_blocks:
        in_specs.append(pl.BlockSpec((k, bn), lambda j, i, rb=rb: (rb, j)))
    operands = [a for pair in a_pairs for a in pair] + [w for w, _, _ in w_blocks]
    return pl.pallas_call(
        functools.partial(_proj_kernel, n_a=len(a_pairs), n_w=len(w_blocks), swiglu=swiglu),
        grid=(n // bn, n_mt),
        in_specs=in_specs,
        out_specs=[pl.BlockSpec((bmp, bn), lambda j, i: (i, j)), pl.BlockSpec((bms, bn), lambda j, i: (i, j))],
        out_shape=[jax.ShapeDtypeStruct((tp, n), out_dtype), jax.ShapeDtypeStruct((ts, n), out_dtype)],
        scratch_shapes=[pltpu.VMEM((k, bn), BF16) for _, _, k in w_blocks],
        compiler_params=_cparams(2),
        name=name,
    )(*operands)


def _down_kernel(ap_ref, as_ref, b_ref, op_ref, os_ref, *, bn, k_total):
    k = pl.program_id(1)
    nk = pl.num_programs(1)
    bk = b_ref.shape[0]
    bmp = ap_ref.shape[0]
    last_rows = k_total - (k_total // bk) * bk

    @pl.when(k == 0)
    def _():
        op_ref[...] = jnp.zeros_like(op_ref)
        os_ref[...] = jnp.zeros_like(os_ref)

    def accumulate(kk):
        a = _stack_rows(ap_ref, as_ref, slice(0, kk))
        for n0 in range(0, op_ref.shape[1], bn):
            d = _dot(a, b_ref[:kk, n0:n0 + bn].astype(BF16))
            op_ref[:, n0:n0 + bn] += d[:bmp]
            os_ref[:, n0:n0 + bn] += d[bmp:]

    if last_rows == 0:
        accumulate(bk)
    else:
        pl.when(k < nk - 1)(lambda: accumulate(bk))
        pl.when(k == nk - 1)(lambda: accumulate(last_rows))


def _down(a_p, a_s, b, bk, n_mt):
    (tp, k_total), ts = a_p.shape, a_s.shape[0]
    n = b.shape[1]
    bmp, bms = tp // n_mt, ts // n_mt
    return pl.pallas_call(
        functools.partial(_down_kernel, bn=min(n, 512), k_total=k_total),
        grid=(n_mt, pl.cdiv(k_total, bk)),
        in_specs=[pl.BlockSpec((bmp, bk), lambda i, k: (i, k)), pl.BlockSpec((bms, bk), lambda i, k: (i, k)),
                  pl.BlockSpec((bk, n), lambda i, k: (k, 0))],
        out_specs=[pl.BlockSpec((bmp, n), lambda i, k: (i, 0), pipeline_mode=pl.Buffered(1)),
                   pl.BlockSpec((bms, n), lambda i, k: (i, 0))],
        out_shape=[jax.ShapeDtypeStruct((tp, n), F32), jax.ShapeDtypeStruct((ts, n), F32)],
        compiler_params=_cparams(2),
        name="ffn_down",
    )(a_p, a_s, b)


def _rel_bucket(d):
    n = jnp.maximum(d, 0)
    max_exact = REL_BUCKETS // 2
    nf = jnp.maximum(n, 1).astype(F32)
    large = max_exact + (jnp.log(nf / max_exact) / math.log(REL_MAX_DIST / max_exact)
                         * (REL_BUCKETS - max_exact)).astype(jnp.int32)
    large = jnp.minimum(large, REL_BUCKETS - 1)
    return jnp.where(n < max_exact, n, large)


def _bias_kernel(rel_ref, bucket_ref, allowed_ref, o_ref):
    h = pl.program_id(0)
    bucket = bucket_ref[...]
    acc = jnp.zeros(bucket.shape, F32)
    for n in range(REL_BUCKETS):
        acc = jnp.where(bucket == n, rel_ref[n, h], acc)
    o_ref[0] = jnp.where(allowed_ref[...] > 0, acc, NEG_INF)


def _bias_table(rel_bias, dist):
    tq, tk = dist.shape
    allowed = ((dist >= 0) & (dist <= WINDOW)).astype(jnp.int32)
    full = pl.BlockSpec((tq, tk), lambda h: (0, 0))
    return pl.pallas_call(
        _bias_kernel,
        grid=(ATTN_HEADS,),
        in_specs=[pl.BlockSpec(memory_space=pltpu.SMEM), full, full],
        out_specs=pl.BlockSpec((1, tq, tk), lambda h: (h, 0, 0)),
        out_shape=jax.ShapeDtypeStruct((ATTN_HEADS, tq, tk), F32),
        compiler_params=_cparams(1),
        name="rel_bias_table",
    )(rel_bias.astype(F32), _rel_bucket(dist), allowed)


def _softmax_pv(s, sink):
    m = jnp.maximum(jnp.max(s, axis=-1, keepdims=True), sink)
    p = jnp.exp(s - m)
    denom = jnp.sum(p, axis=-1, keepdims=True) + jnp.exp(sink - m)
    return p.astype(BF16), denom


def _attn_prompt_kernel(sink_ref, q_ref, kc_ref, kp_ref, vc_ref, vp_ref, bias_ref, o_ref):
    h = pl.program_id(0)
    i = pl.program_id(1)
    c = WINDOW
    kk = jnp.concatenate([kp_ref[...], kc_ref[...]], axis=0).astype(BF16)
    vv = jnp.concatenate([vp_ref[...], vc_ref[...]], axis=0).astype(BF16)
    col = lax.broadcasted_iota(jnp.int32, (c, 2 * c), 1)
    no_prev = jnp.logical_and(i == 0, col < c)
    outs = []
    for g in range(ATTN_GROUP):
        q = q_ref[:, g * HEAD_DIM:(g + 1) * HEAD_DIM].astype(BF16)
        s = lax.dot_general(q, kk, (((1,), (1,)), ((), ())), preferred_element_type=F32)
        s = s * (HEAD_DIM ** -0.5) + bias_ref[g]
        s = jnp.where(no_prev, NEG_INF, s)
        p, denom = _softmax_pv(s, sink_ref[h * ATTN_GROUP + g])
        outs.append(_dot(p, vv) / denom)
    o_ref[...] = jnp.concatenate(outs, axis=1).astype(o_ref.dtype)


def _attn_prompt(proj, bias, sinks):
    t = proj.shape[0]
    c = WINDOW
    gw = ATTN_GROUP * HEAD_DIM
    kb, vb = K_OFF // HEAD_DIM, V_OFF // HEAD_DIM
    cur = lambda base: pl.BlockSpec((c, HEAD_DIM), lambda h, i: (i, base + h))
    prev = lambda base: pl.BlockSpec((c, HEAD_DIM), lambda h, i: (jnp.maximum(i - 1, 0), base + h))
    return pl.pallas_call(
        _attn_prompt_kernel,
        grid=(ATTN_KV_HEADS, t // c),
        in_specs=[pl.BlockSpec(memory_space=pltpu.SMEM),
                  pl.BlockSpec((c, gw), lambda h, i: (i, h)),
                  cur(kb), prev(kb), cur(vb), prev(vb),
                  pl.BlockSpec((ATTN_GROUP, c, 2 * c), lambda h, i: (h, 0, 0))],
        out_specs=pl.BlockSpec((c, gw), lambda h, i: (i, h)),
        out_shape=jax.ShapeDtypeStruct((t, ATTN_WIDTH), BF16),
        compiler_params=_cparams(2),
        name="attn_prompt",
    )(sinks.astype(F32), proj, proj, proj, proj, proj, bias)


def _attn_sample_kernel(q_ref, kn_ref, vn_ref, ck_ref, cv_ref, bias_ref, sink_ref,
                        o_ref, ok_ref, ov_ref, kf_ref, vf_ref, *, t_new):
    l = WINDOW

    @pl.when(jnp.logical_and(pl.program_id(0) == 0, pl.program_id(1) == 0))
    def _():
        kf_ref[:, l:, :] = jnp.zeros_like(kf_ref[:, l:, :])
        vf_ref[:, l:, :] = jnp.zeros_like(vf_ref[:, l:, :])

    kf_ref[:, 0:l, :] = ck_ref[...]
    vf_ref[:, 0:l, :] = cv_ref[...]
    kf_ref[:, l:l + t_new, :] = kn_ref[...]
    vf_ref[:, l:l + t_new, :] = vn_ref[...]
    ok_ref[...] = kf_ref[:, t_new:l + t_new, :]
    ov_ref[...] = vf_ref[:, t_new:l + t_new, :]

    q = q_ref[...].astype(BF16)
    s = jnp.einsum("bqd,bkd->bqk", q, kf_ref[...].astype(BF16), preferred_element_type=F32)
    s = s * (HEAD_DIM ** -0.5) + bias_ref[...]
    p, denom = _softmax_pv(s, sink_ref[...])
    o = jnp.einsum("bqk,bkd->bqd", p, vf_ref[...].astype(BF16), preferred_element_type=F32)
    o_ref[...] = (o / denom).astype(o_ref.dtype)


def _attn_sample(q, k_new, v_new, cache_k, cache_v, bias, sink_rows, gb):
    b, _, rows, _ = q.shape
    t_new = k_new.shape[1]
    l = WINDOW
    new_spec = pl.BlockSpec((gb, t_new, HEAD_DIM), lambda i, h: (i, 0, h))
    cache_spec = pl.BlockSpec((gb, l, HEAD_DIM), lambda i, h: (i, 0, h))
    q_spec = pl.BlockSpec((gb, None, rows, HEAD_DIM), lambda i, h: (i, h, 0, 0))
    return pl.pallas_call(
        functools.partial(_attn_sample_kernel, t_new=t_new),
        grid=(b // gb, ATTN_KV_HEADS),
        in_specs=[q_spec, new_spec, new_spec, cache_spec, cache_spec,
                  pl.BlockSpec((1, rows, 2 * l), lambda i, h: (h, 0, 0)),
                  pl.BlockSpec((1, rows, 1), lambda i, h: (h, 0, 0))],
        out_specs=[q_spec, cache_spec, cache_spec],
        out_shape=[jax.ShapeDtypeStruct(q.shape, F32),
                   jax.ShapeDtypeStruct(cache_k.shape, F32),
                   jax.ShapeDtypeStruct(cache_v.shape, F32)],
        scratch_shapes=[pltpu.VMEM((gb, 2 * l, HEAD_DIM), F32), pltpu.VMEM((gb, 2 * l, HEAD_DIM), F32)],
        compiler_params=_cparams(2),
        name="attn_sample",
    )(q, k_new, v_new, cache_k, cache_v, bias, sink_rows)


def _split3(x):
    hi = x.astype(BF16)
    r = x - hi.astype(F32)
    mid = r.astype(BF16)
    lo = (r - mid.astype(F32)).astype(BF16)
    return hi, mid, lo


def _select_dot(sel, x):
    hi, mid, lo = _split3(x)
    return _dot(sel, hi) + _dot(sel, mid) + _dot(sel, lo)


def _dot_select(x, sel):
    hi, mid, lo = _split3(x)
    return _dot(hi, sel) + _dot(mid, sel) + _dot(lo, sel)


def _mm3(a, b):
    a_hi = a.astype(BF16)
    a_lo = (a - a_hi.astype(F32)).astype(BF16)
    b_hi = b.astype(BF16)
    b_lo = (b - b_hi.astype(F32)).astype(BF16)
    return _dot(jnp.concatenate([a_hi, a_lo, a_hi], axis=1), jnp.concatenate([b_hi, b_hi, b_lo], axis=0))


def _softplus(x):
    return jnp.maximum(x, 0.0) + jnp.log1p(jnp.exp(-jnp.abs(x)))


def _conv_silu(xf, w):
    y = xf * w[GDN_CONV - 1:GDN_CONV]
    for s in range(1, GDN_CONV):
        y = y + pltpu.roll(xf, s, 0) * w[GDN_CONV - 1 - s:GDN_CONV - s]
    return y * jax.nn.sigmoid(y)


def _l2norm_heads(x, scale):
    outs = []
    for hh in range(GDN_HB):
        xh = x[:, hh * HEAD_DIM:(hh + 1) * HEAD_DIM]
        outs.append(xh * (lax.rsqrt(jnp.sum(xh * xh, axis=-1, keepdims=True) + NORM_EPS) * scale))
    return jnp.concatenate(outs, axis=1)


def _gdn_kernel(*refs, rows, seq, carry, live_from):
    c = GDN_CHUNK
    nc = rows // c
    nseq = c // seq
    n_live = seq - live_from
    lw = GDN_HB * HEAD_DIM
    n_scratch = 10
    if carry:
        (xq_ref, xk_ref, xv_ref, tq_ref, tk_ref, tv_ref, z_ref, wq_ref, wk_ref, wv_ref, gn_ref,
         ab_ref, pc_ref, og_ref, s_ref) = refs[:-n_scratch]
        s0_ref = None
    else:
        (xq_in, xk_in, xv_in, cq_ref, ck_ref, cv_ref, z_in, wq_ref, wk_ref, wv_ref, gn_ref,
         ab_in, pc_ref, s0_ref, og_ref, s_ref, xq_ref, xk_ref, xv_ref, z_ref, ab_ref) = refs[:-n_scratch]
    kn_s, kb_s, qn_s, vb_s, kbe_s, qe_s, kd_s, egc_s, egl_s, gcr_s = refs[-n_scratch:]

    group = pl.program_id(0)
    step = pl.program_id(1)
    row = lax.broadcasted_iota(jnp.int32, (rows, 1), 0)
    live = (row % seq) >= live_from

    if not carry:
        def gather(dst_ref, x_ref, hist_ref):
            dst_ref[...] = jnp.zeros_like(dst_ref)
            for b in range(rows // seq):
                r0 = b * seq + live_from
                if hist_ref is not None:
                    dst_ref[r0 - (GDN_CONV - 1):r0, :] = hist_ref[b]
                dst_ref[r0:r0 + n_live, :] = x_ref[b * n_live:(b + 1) * n_live, :]

        gather(xq_ref, xq_in, cq_ref)
        gather(xk_ref, xk_in, ck_ref)
        gather(xv_ref, xv_in, cv_ref)
        gather(z_ref, z_in, None)
        gather(ab_ref, ab_in, None)

    def conv(x_ref, t_ref, w_ref):
        if carry:
            tail = jnp.where(step == 0, 0.0, t_ref[...])
            y = _conv_silu(jnp.concatenate([tail, x_ref[...]], axis=0), w_ref[...])[SUBLANES:]
        else:
            y = _conv_silu(x_ref[...], w_ref[...])
        return jnp.where(live, y, 0.0)

    qn = _l2norm_heads(conv(xq_ref, tq_ref if carry else None, wq_ref), HEAD_DIM ** -0.5)
    kn = _l2norm_heads(conv(xk_ref, tk_ref if carry else None, wk_ref), 1.0)
    v = conv(xv_ref, tv_ref if carry else None, wv_ref)

    ab = ab_ref[...]
    g_col = jnp.where(live, -jnp.exp(pc_ref[0:1, :]) * _softplus(ab + pc_ref[1:2, :]), 0.0)
    beta_col = jnp.where(live, jax.nn.sigmoid(ab), 0.0)
    ri = lax.broadcasted_iota(jnp.int32, (rows, rows), 0)
    ci = lax.broadcasted_iota(jnp.int32, (rows, rows), 1)
    same = (ri // seq) == (ci // seq)
    cum_sel = jnp.where(jnp.logical_and(same, ci <= ri), 1.0, 0.0).astype(BF16)
    tot_sel = jnp.where(same, 1.0, 0.0).astype(BF16)
    gc_col = _select_dot(cum_sel, g_col)
    gl_col = _select_dot(tot_sel, g_col)
    first = group * GDN_HB
    li = lax.broadcasted_iota(jnp.int32, (LANES, lw), 0)
    ni = lax.broadcasted_iota(jnp.int32, (LANES, lw), 1)
    spread_g = jnp.where(li == ni // HEAD_DIM + first, 1.0, 0.0).astype(BF16)
    spread_b = jnp.where(li == ni // HEAD_DIM + first + GDN_HEADS, 1.0, 0.0).astype(BF16)
    egc = _dot_select(gc_col, spread_g)
    egl = _dot_select(gl_col, spread_g)
    eb = _dot_select(beta_col, spread_b)
    decay_in = jnp.exp(egc)
    kb = kn * eb
    kn_s[...] = kn
    kb_s[...] = kb
    qn_s[...] = qn
    vb_s[...] = v * eb
    kbe_s[...] = kb * decay_in
    qe_s[...] = qn * decay_in
    kd_s[...] = kn * jnp.exp(egl - egc)
    egc_s[...] = egc
    egl_s[...] = egl
    pi = lax.broadcasted_iota(jnp.int32, (LANES, LANES), 0)
    pj = lax.broadcasted_iota(jnp.int32, (LANES, LANES), 1)
    pick = jnp.where(jnp.logical_and(pi == pj + first, pj < GDN_HB), 1.0, 0.0).astype(BF16)
    gc_heads = _dot_select(gc_col, pick)
    if rows % LANES:
        gc_heads = jnp.concatenate([gc_heads, jnp.zeros((LANES - rows % LANES, LANES), F32)], axis=0)
    gcr_s[...] = gc_heads.T[:SUBLANES, :rows]

    cr = lax.broadcasted_iota(jnp.int32, (c, c), 0)
    cc = lax.broadcasted_iota(jnp.int32, (c, c), 1)
    same_c = (cr // seq) == (cc // seq)
    causal = jnp.logical_and(same_c, cr >= cc)
    strict = jnp.logical_and(same_c, cr > cc)

    if carry:
        @pl.when(step == 0)
        def _():
            s_ref[...] = jnp.zeros_like(s_ref)

    eye = jnp.where(cr == cc, 1.0, 0.0)
    seq_of_row = lax.broadcasted_iota(jnp.int32, (c, HEAD_DIM), 0) // seq
    n_levels = max(1, math.ceil(math.log2(n_live)))

    pairs = [(ch, hh) for ch in range(nc) for hh in range(GDN_HB)]
    rows_of = lambda ch: slice(ch * c, (ch + 1) * c)
    lanes_of = lambda hh: slice(hh * HEAD_DIM, (hh + 1) * HEAD_DIM)
    uw_all, qk_all, a_pow, t_mat = {}, {}, {}, {}
    for ch, hh in pairs:
        rs, hs = rows_of(ch), lanes_of(hh)
        gcol = egc_s[rs, hh * HEAD_DIM:hh * HEAD_DIM + c]
        grow = gcr_s[hh:hh + 1, rs]
        decay = jnp.exp(jnp.where(causal, gcol - grow, NEG_INF))
        kq = jnp.concatenate([kb_s[rs, hs], qn_s[rs, hs]], axis=0).astype(BF16)
        kk = lax.dot_general(kq, kn_s[rs, hs].astype(BF16), (((1,), (1,)), ((), ())),
                             preferred_element_type=F32) * jnp.concatenate([decay, decay], axis=0)
        a_pow[ch, hh] = jnp.where(strict, kk[:c], 0.0)
        qk_all[ch, hh] = kk[c:].astype(BF16)
        t_mat[ch, hh] = eye - a_pow[ch, hh]
    for _lvl in range(n_levels - 1):
        for p in pairs:
            a_pow[p] = _mm3(a_pow[p], a_pow[p])
        for p in pairs:
            t_mat[p] = t_mat[p] + _mm3(t_mat[p], a_pow[p])
    for ch, hh in pairs:
        rs, hs = rows_of(ch), lanes_of(hh)
        uw_all[ch, hh] = _dot(t_mat[ch, hh].astype(BF16),
                              jnp.concatenate([vb_s[rs, hs], kbe_s[rs, hs]], axis=1).astype(BF16))

    if carry:
        state = [s_ref[hh] for hh in range(GDN_HB)]
    new_states = {}
    heads = range(GDN_HB)
    tn_dot = lambda a, b: lax.dot_general(a, b, (((0,), (0,)), ((), ())), preferred_element_type=F32)
    for ch in range(nc):
        rs = rows_of(ch)
        w_s, q_s, v_new_b, o = {}, {}, {}, {}
        for hh in heads:
            w = uw_all[ch, hh][:, HEAD_DIM:]
            q_e = qe_s[rs, lanes_of(hh)]
            if carry:
                wq = _dot(jnp.concatenate([w, q_e], axis=0).astype(BF16), state[hh].astype(BF16))
                w_s[hh], q_s[hh] = wq[:c], wq[c:]
            else:
                parts = []
                for b in range(nseq):
                    bs = slice(b * seq, (b + 1) * seq)
                    parts.append(_dot(jnp.concatenate([w[bs], q_e[bs]], axis=0).astype(BF16),
                                      s0_ref[ch * nseq + b, hh].astype(BF16)))
                w_s[hh] = jnp.concatenate([p[:seq] for p in parts], axis=0)
                q_s[hh] = jnp.concatenate([p[seq:] for p in parts], axis=0)
        for hh in heads:
            v_new_b[hh] = (uw_all[ch, hh][:, :HEAD_DIM] - w_s[hh]).astype(BF16)
            o[hh] = q_s[hh] + _dot(qk_all[ch, hh], v_new_b[hh])
        for hh in heads:
            hs = lanes_of(hh)
            k_d = kd_s[rs, hs]
            if carry:
                state[hh] = (state[hh] * jnp.exp(egl_s[ch * c:ch * c + 1, hs])
                             + tn_dot(k_d.astype(BF16), v_new_b[hh]))
            else:
                for b in range(nseq):
                    k_db = jnp.where(seq_of_row == b, k_d, 0.0).astype(BF16)
                    r1 = ch * c + b * seq
                    new_states[ch * nseq + b, hh] = (
                        s0_ref[ch * nseq + b, hh] * jnp.exp(egl_s[r1:r1 + 1, hs]) + tn_dot(k_db, v_new_b[hh]))
        outs = []
        for hh in heads:
            z = z_ref[rs, lanes_of(hh)]
            outs.append(_rms(o[hh], gn_ref[...]) * (z * jax.nn.sigmoid(z)))
        og = jnp.concatenate(outs, axis=1)
        if carry:
            og_ref[rs, :] = og.astype(og_ref.dtype)
        else:
            for b in range(nseq):
                r0 = b * seq + live_from
                og_ref[(ch * nseq + b) * n_live:(ch * nseq + b + 1) * n_live, :] = (
                    og[r0:r0 + n_live].astype(og_ref.dtype))
    if carry:
        s_ref[...] = jnp.stack(state, axis=0)
    else:
        for (b, hh), s_new in new_states.items():
            s_ref[b, hh] = s_new


def _gdn_params(a_log, dt_bias):
    rows = jnp.stack([a_log.astype(F32), dt_bias.astype(F32)], axis=0)
    return jnp.pad(rows, ((0, SUBLANES - 2), (0, LANES - GDN_HEADS)))


def _gdn_scratch(rows):
    lw = GDN_HB * HEAD_DIM
    return [pltpu.VMEM((rows, lw), F32) for _ in range(9)] + [pltpu.VMEM((SUBLANES, rows), F32)]


def _gdn_prompt(proj, ab, conv_w, gnorm_w, params, rows):
    t = proj.shape[0]
    lw = GDN_HB * HEAD_DIM
    x_spec = lambda off: pl.BlockSpec((rows, lw), lambda g, i: (i, off // lw + g))
    tail_spec = lambda off: pl.BlockSpec(
        (SUBLANES, lw), lambda g, i: (jnp.maximum(i * (rows // SUBLANES) - 1, 0), off // lw + g))
    w_spec = lambda off: pl.BlockSpec((GDN_CONV, lw), lambda g, i: (0, (off - GQ_OFF) // lw + g))
    return pl.pallas_call(
        functools.partial(_gdn_kernel, rows=rows, seq=GDN_CHUNK, carry=True, live_from=0),
        grid=(GDN_NG, t // rows),
        in_specs=[x_spec(GQ_OFF), x_spec(GK_OFF), x_spec(GV_OFF),
                  tail_spec(GQ_OFF), tail_spec(GK_OFF), tail_spec(GV_OFF),
                  x_spec(Z_OFF), w_spec(GQ_OFF), w_spec(GK_OFF), w_spec(GV_OFF),
                  pl.BlockSpec((1, HEAD_DIM), lambda g, i: (0, 0)),
                  pl.BlockSpec((rows, LANES), lambda g, i: (i, 0)),
                  pl.BlockSpec((SUBLANES, LANES), lambda g, i: (0, 0))],
        out_specs=[pl.BlockSpec((rows, lw), lambda g, i: (i, g)),
                   pl.BlockSpec((GDN_HB, HEAD_DIM, HEAD_DIM), lambda g, i: (g, 0, 0))],
        out_shape=[jax.ShapeDtypeStruct((t, GDN_WIDTH), BF16),
                   jax.ShapeDtypeStruct((GDN_HEADS, HEAD_DIM, HEAD_DIM), F32)],
        scratch_shapes=_gdn_scratch(rows),
        compiler_params=_cparams(2),
        name="gdn_prompt",
    )(proj, proj, proj, proj, proj, proj, proj, conv_w, conv_w, conv_w,
      gnorm_w.reshape(1, HEAD_DIM).astype(F32), ab, params)


def _gdn_sample(proj, ab, conv_state, state, conv_w, gnorm_w, params, n_live):
    rows = GDN_CHUNK
    lw = GDN_HB * HEAD_DIM
    nseq = rows // SAMPLE_PAD
    n_in = nseq * n_live
    x_spec = lambda off: pl.BlockSpec((n_in, lw), lambda g, i: (i, off // lw + g))
    hist_spec = lambda off: pl.BlockSpec((nseq, GDN_CONV - 1, lw), lambda g, i: (i, 0, (off - GQ_OFF) // lw + g))
    w_spec = lambda off: pl.BlockSpec((GDN_CONV, lw), lambda g, i: (0, (off - GQ_OFF) // lw + g))
    s_spec = pl.BlockSpec((nseq, GDN_HB, HEAD_DIM, HEAD_DIM), lambda g, i: (i, g, 0, 0))
    return pl.pallas_call(
        functools.partial(_gdn_kernel, rows=rows, seq=SAMPLE_PAD, carry=False, live_from=SAMPLE_PAD - n_live),
        grid=(GDN_NG, proj.shape[0] // n_in),
        in_specs=[x_spec(GQ_OFF), x_spec(GK_OFF), x_spec(GV_OFF),
                  hist_spec(GQ_OFF), hist_spec(GK_OFF), hist_spec(GV_OFF),
                  x_spec(Z_OFF), w_spec(GQ_OFF), w_spec(GK_OFF), w_spec(GV_OFF),
                  pl.BlockSpec((1, HEAD_DIM), lambda g, i: (0, 0)),
                  pl.BlockSpec((n_in, LANES), lambda g, i: (i, 0)),
                  pl.BlockSpec((SUBLANES, LANES), lambda g, i: (0, 0)),
                  s_spec],
        out_specs=[pl.BlockSpec((n_in, lw), lambda g, i: (i, g)), s_spec],
        out_shape=[jax.ShapeDtypeStruct((proj.shape[0], GDN_WIDTH), F32),
                   jax.ShapeDtypeStruct(state.shape, F32)],
        scratch_shapes=[pltpu.VMEM((rows, lw), F32) for _ in range(4)] + [pltpu.VMEM((rows, LANES), F32)]
        + _gdn_scratch(rows),
        compiler_params=_cparams(2),
        name="gdn_sample",
    )(proj, proj, proj, conv_state, conv_state, conv_state, proj, conv_w, conv_w, conv_w,
      gnorm_w.reshape(1, HEAD_DIM).astype(F32), ab, params, state)


def _layer(x_prompt, x_sample, win_k, win_v, conv_state, gdn_state, rel_bias,
           w_in, sinks, conv_w, a_log, dt_bias, gnorm_w, w_out,
           n_pre_mix, n_post_mix, n_pre_ffn, n_post_ffn, w_gate, w_up, w_down, tiles):
    _, t, d = x_prompt.shape
    nb, ts, _ = x_sample.shape
    ms = nb * ts
    l = win_k.shape[1]
    ff = w_gate.shape[1]
    n_mt = tiles["n_mt"]
    xp, xs = x_prompt.reshape(t, d), x_sample.reshape(ms, d)

    hp = _rmsnorm(xp, n_pre_mix, BF16, tiles["norm_rows"])
    hs = _rmsnorm(xs, n_pre_mix, BF16, tiles["sample_rows"])
    proj_p, proj_s = _proj([(hp, hs)], [(w_in, 0, d)], PROJ_MAIN, tiles["in_bn"], F32, n_mt, "in_proj")
    w_tail = jnp.pad(w_in[:, PROJ_MAIN:], ((0, 0), (0, LANES - AB_WIDTH)))
    ab_p, ab_s = _proj([(hp, hs)], [(w_tail, 0, d)], LANES, LANES, F32, n_mt, "in_proj_ab")

    qi = jnp.arange(WINDOW)[:, None]
    kj = jnp.arange(2 * WINDOW)[None, :]
    dist_p = WINDOW + qi - kj
    dist_s = l + jnp.arange(SUBLANES)[:, None] - kj
    dist_s = jnp.where(kj < l + ts, dist_s, -1)
    bias_all = _bias_table(rel_bias, jnp.concatenate([dist_p, dist_s], axis=0))
    attn_p = _attn_prompt(proj_p, bias_all[:, :WINDOW], sinks)

    q_s = proj_s[:, Q_OFF:Q_OFF + ATTN_WIDTH].reshape(nb, ts, ATTN_KV_HEADS, ATTN_GROUP, HEAD_DIM)
    q_s = q_s.transpose(0, 2, 1, 3, 4).reshape(nb, ATTN_KV_HEADS, ts * ATTN_GROUP, HEAD_DIM)
    bias_s = bias_all[:, WINDOW:WINDOW + ts].reshape(ATTN_KV_HEADS, ATTN_GROUP, ts, 2 * WINDOW)
    bias_s = bias_s.transpose(0, 2, 1, 3).reshape(ATTN_KV_HEADS, ts * ATTN_GROUP, 2 * WINDOW)
    sink_rows = jnp.broadcast_to(sinks.astype(F32).reshape(ATTN_KV_HEADS, 1, ATTN_GROUP),
                                 (ATTN_KV_HEADS, ts, ATTN_GROUP)).reshape(ATTN_KV_HEADS, ts * ATTN_GROUP, 1)
    attn_s, new_k, new_v = _attn_sample(
        q_s, proj_s[:, K_OFF:K_OFF + KV_WIDTH].reshape(nb, ts, KV_WIDTH),
        proj_s[:, V_OFF:V_OFF + KV_WIDTH].reshape(nb, ts, KV_WIDTH),
        win_k.reshape(nb, l, KV_WIDTH), win_v.reshape(nb, l, KV_WIDTH), bias_s, sink_rows, tiles["attn_gb"])
    attn_s = attn_s.reshape(nb, ATTN_KV_HEADS, ts, ATTN_GROUP, HEAD_DIM).transpose(0, 2, 1, 3, 4)
    attn_s = attn_s.reshape(ms, ATTN_WIDTH)

    params = _gdn_params(a_log, dt_bias)
    og_p, s_p = _gdn_prompt(proj_p, ab_p, conv_w, gnorm_w, params, tiles["gdn_rows"])
    og_s, s_s = _gdn_sample(proj_s, ab_s, conv_state.astype(F32), gdn_state, conv_w, gnorm_w, params, ts)

    mix_p, mix_s = _proj([(attn_p, attn_s), (og_p, og_s)], [(w_out, 0, ATTN_WIDTH), (w_out, 1, GDN_WIDTH)],
                         d, tiles["out_bn"], F32, n_mt, "out_proj")
    x1_p, h2_p = _post_mix(xp, mix_p, n_post_mix, n_pre_ffn, tiles["post_rows"])
    x1_s, h2_s = _post_mix(xs, mix_s, n_post_mix, n_pre_ffn, tiles["sample_rows"])
    act_p, act_s = _proj([(h2_p, h2_s)], [(w_gate, 0, d), (w_up, 0, d)], ff, tiles["ff_bn"], BF16, n_mt,
                         "ffn_gate_up", swiglu=True)
    ffn_p, ffn_s = _down(act_p, act_s, w_down, tiles["ff_bk"], n_mt)
    y_p = _residual_norm(x1_p, ffn_p, n_post_ffn, tiles["post_rows"])
    y_s = _residual_norm(x1_s, ffn_s, n_post_ffn, tiles["sample_rows"])

    lw = min(WINDOW, t)
    prompt_k = proj_p[t - lw:, K_OFF:K_OFF + KV_WIDTH].reshape(1, lw, ATTN_KV_HEADS, HEAD_DIM)
    prompt_v = proj_p[t - lw:, V_OFF:V_OFF + KV_WIDTH].reshape(1, lw, ATTN_KV_HEADS, HEAD_DIM)
    prompt_conv = proj_p[t - (GDN_CONV - 1):, GQ_OFF:GQ_OFF + GDN_CONV_CH][None]
    sample_conv = proj_s[:, GQ_OFF:GQ_OFF + GDN_CONV_CH].reshape(nb, ts, GDN_CONV_CH)[:, ts - (GDN_CONV - 1):]
    return (y_p.reshape(1, t, d), y_s.reshape(nb, ts, d),
            prompt_k, prompt_v, prompt_conv, s_p[None],
            new_k.reshape(nb, l, ATTN_KV_HEADS, HEAD_DIM), new_v.reshape(nb, l, ATTN_KV_HEADS, HEAD_DIM),
            sample_conv, s_s)


def _tiles(t, ms):
    n_mt = 8 if (t % (8 * 16) == 0 and ms % (8 * 16) == 0) else 1
    return dict(n_mt=n_mt, in_bn=512, out_bn=512, ff_bn=256, ff_bk=512, norm_rows=512, post_rows=256,
                sample_rows=ms, gdn_rows=256, attn_gb=8)


def kernel(x_prompt, x_sample, cache_win_k, cache_win_v, state_conv, state_gdn, rel_bias, w_in, attn_sinks,
           gdn_conv_w, gdn_a_log, gdn_dt_bias, gdn_norm_w, w_out, norm_pre_mix, norm_post_mix, norm_pre_ffn,
           norm_post_ffn, w_gate, w_up, w_down):
    depth = w_in.shape[0]
    assert depth == 1 and x_prompt.shape[0] == 1
    tiles = _tiles(x_prompt.shape[1], x_sample.shape[0] * x_sample.shape[1])
    outs = _layer(x_prompt, x_sample, cache_win_k[0], cache_win_v[0], state_conv[0], state_gdn[0], rel_bias,
                  w_in[0], attn_sinks[0], gdn_conv_w[0], gdn_a_log[0], gdn_dt_bias[0], gdn_norm_w[0], w_out[0],
                  norm_pre_mix[0], norm_post_mix[0], norm_pre_ffn[0], norm_post_ffn[0],
                  w_gate[0], w_up[0], w_down[0], tiles)
    yp, ys, pk, pv, pc, ps, sk, sv, sc, ss = outs
    return (yp, ys, pk[None], pv[None], pc[None], ps[None], sk[None], sv[None], sc[None], ss[None])
```

```python
import functools
import math

import jax
import jax.numpy as jnp
from jax import lax
from jax.experimental import pallas as pl
from jax.experimental.pallas import tpu as pltpu

F32 = jnp.float32
BF16 = jnp.bfloat16

HEAD_DIM = 128
ATTN_HEADS = 16
ATTN_KV_HEADS = 4
ATTN_GROUP = ATTN_HEADS // ATTN_KV_HEADS
ATTN_WIDTH = ATTN_HEADS * HEAD_DIM
KV_WIDTH = ATTN_KV_HEADS * HEAD_DIM
WINDOW = 128
REL_BUCKETS = 32
REL_MAX_DIST = 128
GDN_HEADS = 16
GDN_WIDTH = GDN_HEADS * HEAD_DIM
GDN_CONV = 4
GDN_CONV_CH = 3 * GDN_WIDTH
GDN_CHUNK = 64
NORM_EPS = 1e-6
NEG_INF = -1e30

Q_OFF = 0
K_OFF = ATTN_WIDTH
V_OFF = K_OFF + KV_WIDTH
GQ_OFF = V_OFF + KV_WIDTH
GK_OFF = GQ_OFF + GDN_WIDTH
GV_OFF = GK_OFF + GDN_WIDTH
Z_OFF = GV_OFF + GDN_WIDTH
PROJ_MAIN = Z_OFF + GDN_WIDTH
AB_WIDTH = 2 * GDN_HEADS

LANES = 128
SUBLANES = 8
GDN_HB = 4
GDN_NG = GDN_HEADS // GDN_HB
SAMPLE_PAD = 8
VMEM_LIMIT = 56 * 1024 * 1024


def _cparams(n_axes, vmem=VMEM_LIMIT):
    return pltpu.CompilerParams(dimension_semantics=("arbitrary",) * n_axes, vmem_limit_bytes=vmem)


def _dot(a, b):
    return jnp.dot(a, b, preferred_element_type=F32)


def _rms(x, w):
    return x * lax.rsqrt(jnp.mean(x * x, axis=-1, keepdims=True) + NORM_EPS) * w


def _rmsnorm_kernel(x_ref, w_ref, o_ref):
    o_ref[...] = _rms(x_ref[...].astype(F32), w_ref[...]).astype(o_ref.dtype)


def _rmsnorm(x, w, out_dtype, rows):
    m, d = x.shape
    return pl.pallas_call(
        _rmsnorm_kernel,
        grid=(m // rows,),
        in_specs=[pl.BlockSpec((rows, d), lambda i: (i, 0)), pl.BlockSpec((1, d), lambda i: (0, 0))],
        out_specs=pl.BlockSpec((rows, d), lambda i: (i, 0)),
        out_shape=jax.ShapeDtypeStruct((m, d), out_dtype),
        compiler_params=_cparams(1),
        name="rmsnorm",
    )(x, w.reshape(1, d).astype(F32))


def _post_mix_kernel(x_ref, mix_ref, wpost_ref, wpre_ref, x1_ref, h2_ref):
    x1 = x_ref[...] + _rms(mix_ref[...], wpost_ref[...])
    x1_ref[...] = x1
    h2_ref[...] = _rms(x1, wpre_ref[...]).astype(h2_ref.dtype)


def _post_mix(x, mix, w_post, w_pre, rows):
    m, d = x.shape
    row_spec = pl.BlockSpec((rows, d), lambda i: (i, 0))
    w_spec = pl.BlockSpec((1, d), lambda i: (0, 0))
    return pl.pallas_call(
        _post_mix_kernel,
        grid=(m // rows,),
        in_specs=[row_spec, row_spec, w_spec, w_spec],
        out_specs=[row_spec, row_spec],
        out_shape=[jax.ShapeDtypeStruct((m, d), F32), jax.ShapeDtypeStruct((m, d), BF16)],
        compiler_params=_cparams(1),
        name="post_mix",
    )(x, mix, w_post.reshape(1, d).astype(F32), w_pre.reshape(1, d).astype(F32))


def _residual_norm_kernel(x_ref, y_ref, w_ref, o_ref):
    o_ref[...] = x_ref[...] + _rms(y_ref[...], w_ref[...])


def _residual_norm(x, y, w, rows):
    m, d = x.shape
    row_spec = pl.BlockSpec((rows, d), lambda i: (i, 0))
    return pl.pallas_call(
        _residual_norm_kernel,
        grid=(m // rows,),
        in_specs=[row_spec, row_spec, pl.BlockSpec((1, d), lambda i: (0, 0))],
        out_specs=row_spec,
        out_shape=jax.ShapeDtypeStruct((m, d), F32),
        compiler_params=_cparams(1),
        name="residual_norm",
    )(x, y, w.reshape(1, d).astype(F32))


def _stack_rows(p_ref, s_ref, cols=slice(None)):
    return jnp.concatenate([p_ref[:, cols].astype(BF16), s_ref[:, cols].astype(BF16)], axis=0)


def _dot_nt(a, b):
    return lax.dot_general(a, b, (((1,), (1,)), ((), ())), preferred_element_type=F32)


def _proj_kernel(*refs, n_a, n_w, swiglu, w_transposed):
    a_refs = refs[:2 * n_a]
    w_refs = refs[2 * n_a:2 * n_a + n_w]
    op_ref, os_ref = refs[2 * n_a + n_w:2 * n_a + n_w + 2]
    wb_refs = refs[2 * n_a + n_w + 2:]
    mm = _dot_nt if w_transposed else _dot

    @pl.when(pl.program_id(1) == 0)
    def _():
        for w_ref, wb_ref in zip(w_refs, wb_refs):
            wb_ref[...] = w_ref[...].astype(BF16)

    lhs = [_stack_rows(a_refs[2 * p], a_refs[2 * p + 1]) for p in range(n_a)]
    if swiglu:
        g = mm(lhs[0], wb_refs[0][...])
        o = g * jax.nn.sigmoid(g) * mm(lhs[0], wb_refs[1][...])
    else:
        o = mm(lhs[0], wb_refs[0][...])
        for l, wb_ref in zip(lhs[1:], wb_refs[1:]):
            o = o + mm(l, wb_ref[...])
    bmp = op_ref.shape[0]
    op_ref[...] = o[:bmp].astype(op_ref.dtype)
    os_ref[...] = o[bmp:].astype(os_ref.dtype)


def _proj(a_pairs, w_blocks, n, bn, out_dtype, n_mt, name, swiglu=False, w_transposed=False):
    tp, ts = a_pairs[0][0].shape[0], a_pairs[0][1].shape[0]
    bmp, bms = tp // n_mt, ts // n_mt
    in_specs = []
    for a_p, a_s in a_pairs:
        in_specs.append(pl.BlockSpec((bmp, a_p.shape[1]), lambda j, i: (i, 0)))
        in_specs.append(pl.BlockSpec((bms, a_s.shape[1]), lambda j, i: (i, 0)))
    for _, rb, k in w_blocks:
        if w_transposed:
            in_specs.append(pl.BlockSpec((bn, k), lambda j, i, rb=rb: (j, rb)))
        else:
            in_specs.append(pl.BlockSpec((k, bn), lambda j, i, rb=rb: (rb, j)))
    operands = [a for pair in a_pairs for a in pair] + [w for w, _, _ in w_blocks]
    return pl.pallas_call(
        functools.partial(_proj_kernel, n_a=len(a_pairs), n_w=len(w_blocks), swiglu=swiglu,
                          w_transposed=w_transposed),
        grid=(n // bn, n_mt),
        in_specs=in_specs,
        out_specs=[pl.BlockSpec((bmp, bn), lambda j, i: (i, j)), pl.BlockSpec((bms, bn), lambda j, i: (i, j))],
        out_shape=[jax.ShapeDtypeStruct((tp, n), out_dtype), jax.ShapeDtypeStruct((ts, n), out_dtype)],
        scratch_shapes=[pltpu.VMEM((bn, k) if w_transposed else (k, bn), BF16) for _, _, k in w_blocks],
        compiler_params=_cparams(2),
        name=name,
    )(*operands)


def _down_kernel(ap_ref, as_ref, b_ref, op_ref, os_ref, *, bn, k_total):
    k = pl.program_id(1)
    nk = pl.num_programs(1)
    bk = b_ref.shape[0]
    bmp = ap_ref.shape[0]
    last_rows = k_total - (k_total // bk) * bk

    @pl.when(k == 0)
    def _():
        op_ref[...] = jnp.zeros_like(op_ref)
        os_ref[...] = jnp.zeros_like(os_ref)

    def accumulate(kk):
        a = _stack_rows(ap_ref, as_ref, slice(0, kk))
        for n0 in range(0, op_ref.shape[1], bn):
            d = _dot(a, b_ref[:kk, n0:n0 + bn].astype(BF16))
            op_ref[:, n0:n0 + bn] += d[:bmp]
            os_ref[:, n0:n0 + bn] += d[bmp:]

    if last_rows == 0:
        accumulate(bk)
    else:
        pl.when(k < nk - 1)(lambda: accumulate(bk))
        pl.when(k == nk - 1)(lambda: accumulate(last_rows))


def _down(a_p, a_s, b, bk, n_mt):
    (tp, k_total), ts = a_p.shape, a_s.shape[0]
    n = b.shape[1]
    bmp, bms = tp // n_mt, ts // n_mt
    return pl.pallas_call(
        functools.partial(_down_kernel, bn=min(n, 512), k_total=k_total),
        grid=(n_mt, pl.cdiv(k_total, bk)),
        in_specs=[pl.BlockSpec((bmp, bk), lambda i, k: (i, k)), pl.BlockSpec((bms, bk), lambda i, k: (i, k)),
                  pl.BlockSpec((bk, n), lambda i, k: (k, 0))],
        out_specs=[pl.BlockSpec((bmp, n), lambda i, k: (i, 0), pipeline_mode=pl.Buffered(1)),
                   pl.BlockSpec((bms, n), lambda i, k: (i, 0))],
        out_shape=[jax.ShapeDtypeStruct((tp, n), F32), jax.ShapeDtypeStruct((ts, n), F32)],
        compiler_params=_cparams(2),
        name="ffn_down",
    )(a_p, a_s, b)


def _rel_bucket(d):
    n = jnp.maximum(d, 0)
    max_exact = REL_BUCKETS // 2
    nf = jnp.maximum(n, 1).astype(F32)
    large = max_exact + (jnp.log(nf / max_exact) / math.log(REL_MAX_DIST / max_exact)
                         * (REL_BUCKETS - max_exact)).astype(jnp.int32)
    large = jnp.minimum(large, REL_BUCKETS - 1)
    return jnp.where(n < max_exact, n, large)


def _bias_kernel(rel_ref, bucket_ref, allowed_ref, o_ref):
    h = pl.program_id(0)
    bucket = bucket_ref[...]
    acc = jnp.zeros(bucket.shape, F32)
    for n in range(REL_BUCKETS):
        acc = jnp.where(bucket == n, rel_ref[n, h], acc)
    o_ref[0] = jnp.where(allowed_ref[...] > 0, acc, NEG_INF)


def _bias_table(rel_bias, dist):
    tq, tk = dist.shape
    allowed = ((dist >= 0) & (dist <= WINDOW)).astype(jnp.int32)
    full = pl.BlockSpec((tq, tk), lambda h: (0, 0))
    return pl.pallas_call(
        _bias_kernel,
        grid=(ATTN_HEADS,),
        in_specs=[pl.BlockSpec(memory_space=pltpu.SMEM), full, full],
        out_specs=pl.BlockSpec((1, tq, tk), lambda h: (h, 0, 0)),
        out_shape=jax.ShapeDtypeStruct((ATTN_HEADS, tq, tk), F32),
        compiler_params=_cparams(1),
        name="rel_bias_table",
    )(rel_bias.astype(F32), _rel_bucket(dist), allowed)


def _softmax_pv(s, sink):
    m = jnp.maximum(jnp.max(s, axis=-1, keepdims=True), sink)
    p = jnp.exp(s - m)
    denom = jnp.sum(p, axis=-1, keepdims=True) + jnp.exp(sink - m)
    return p.astype(BF16), denom


def _attn_prompt_kernel(sink_ref, q_ref, kc_ref, kp_ref, vc_ref, vp_ref, bias_ref, o_ref):
    h = pl.program_id(0)
    i = pl.program_id(1)
    c = WINDOW
    kk = jnp.concatenate([kp_ref[...], kc_ref[...]], axis=0).astype(BF16)
    vv = jnp.concatenate([vp_ref[...], vc_ref[...]], axis=0).astype(BF16)
    col = lax.broadcasted_iota(jnp.int32, (c, 2 * c), 1)
    no_prev = jnp.logical_and(i == 0, col < c)
    outs = []
    for g in range(ATTN_GROUP):
        q = q_ref[:, g * HEAD_DIM:(g + 1) * HEAD_DIM].astype(BF16)
        s = lax.dot_general(q, kk, (((1,), (1,)), ((), ())), preferred_element_type=F32)
        s = s * (HEAD_DIM ** -0.5) + bias_ref[g]
        s = jnp.where(no_prev, NEG_INF, s)
        p, denom = _softmax_pv(s, sink_ref[h * ATTN_GROUP + g])
        outs.append(_dot(p, vv) / denom)
    o_ref[...] = jnp.concatenate(outs, axis=1).astype(o_ref.dtype)


def _attn_prompt(proj, bias, sinks):
    t = proj.shape[0]
    c = WINDOW
    gw = ATTN_GROUP * HEAD_DIM
    kb, vb = K_OFF // HEAD_DIM, V_OFF // HEAD_DIM
    cur = lambda base: pl.BlockSpec((c, HEAD_DIM), lambda h, i: (i, base + h))
    prev = lambda base: pl.BlockSpec((c, HEAD_DIM), lambda h, i: (jnp.maximum(i - 1, 0), base + h))
    return pl.pallas_call(
        _attn_prompt_kernel,
        grid=(ATTN_KV_HEADS, t // c),
        in_specs=[pl.BlockSpec(memory_space=pltpu.SMEM),
                  pl.BlockSpec((c, gw), lambda h, i: (i, h)),
                  cur(kb), prev(kb), cur(vb), prev(vb),
                  pl.BlockSpec((ATTN_GROUP, c, 2 * c), lambda h, i: (h, 0, 0))],
        out_specs=pl.BlockSpec((c, gw), lambda h, i: (i, h)),
        out_shape=jax.ShapeDtypeStruct((t, ATTN_WIDTH), BF16),
        compiler_params=_cparams(2),
        name="attn_prompt",
    )(sinks.astype(F32), proj, proj, proj, proj, proj, bias)


def _attn_sample_kernel(x_ref, ck_ref, cv_ref, bias_ref, sink_ref, o_ref, ok_ref, ov_ref,
                        q_s, kf_ref, vf_ref, *, t_new):
    l = WINDOW
    gb = ck_ref.shape[0]

    @pl.when(pl.program_id(0) == 0)
    def _():
        kf_ref[:, l + t_new:, :] = jnp.zeros_like(kf_ref[:, l + t_new:, :])
        vf_ref[:, l + t_new:, :] = jnp.zeros_like(vf_ref[:, l + t_new:, :])

    for h in range(ATTN_KV_HEADS):
        kf_ref[:, 0:l, :] = ck_ref[:, :, h, :]
        vf_ref[:, 0:l, :] = cv_ref[:, :, h, :]
        for b in range(gb):
            rows = slice(b * t_new, (b + 1) * t_new)
            kf_ref[b, l:l + t_new, :] = x_ref[rows, K_OFF + h * HEAD_DIM:K_OFF + (h + 1) * HEAD_DIM]
            vf_ref[b, l:l + t_new, :] = x_ref[rows, V_OFF + h * HEAD_DIM:V_OFF + (h + 1) * HEAD_DIM]
            for g in range(ATTN_GROUP):
                col = (h * ATTN_GROUP + g) * HEAD_DIM
                q_s[b, g * t_new:(g + 1) * t_new, :] = x_ref[rows, col:col + HEAD_DIM]
        ok_ref[:, :, h, :] = kf_ref[:, t_new:l + t_new, :]
        ov_ref[:, :, h, :] = vf_ref[:, t_new:l + t_new, :]

        s = jnp.einsum("bqd,bkd->bqk", q_s[...].astype(BF16), kf_ref[...].astype(BF16),
                       preferred_element_type=F32)
        s = s * (HEAD_DIM ** -0.5) + bias_ref[h]
        p, denom = _softmax_pv(s, sink_ref[h])
        o = jnp.einsum("bqk,bkd->bqd", p, vf_ref[...].astype(BF16), preferred_element_type=F32) / denom
        for b in range(gb):
            for g in range(ATTN_GROUP):
                col = (h * ATTN_GROUP + g) * HEAD_DIM
                o_ref[b * t_new:(b + 1) * t_new, col:col + HEAD_DIM] = o[b, g * t_new:(g + 1) * t_new, :]


def _attn_sample(proj, cache_k, cache_v, bias, sink_rows, t_new, gb):
    b, l = cache_k.shape[:2]
    rows = ATTN_GROUP * t_new
    qkv = V_OFF + KV_WIDTH
    cache_spec = pl.BlockSpec((gb, l, ATTN_KV_HEADS, HEAD_DIM), lambda i: (i, 0, 0, 0))
    return pl.pallas_call(
        functools.partial(_attn_sample_kernel, t_new=t_new),
        grid=(b // gb,),
        in_specs=[pl.BlockSpec((gb * t_new, qkv), lambda i: (i, 0)), cache_spec, cache_spec,
                  pl.BlockSpec((ATTN_KV_HEADS, rows, 2 * l), lambda i: (0, 0, 0)),
                  pl.BlockSpec((ATTN_KV_HEADS, rows, 1), lambda i: (0, 0, 0))],
        out_specs=[pl.BlockSpec((gb * t_new, ATTN_WIDTH), lambda i: (i, 0)), cache_spec, cache_spec],
        out_shape=[jax.ShapeDtypeStruct((b * t_new, ATTN_WIDTH), F32),
                   jax.ShapeDtypeStruct(cache_k.shape, F32),
                   jax.ShapeDtypeStruct(cache_v.shape, F32)],
        scratch_shapes=[pltpu.VMEM((gb, rows, HEAD_DIM), F32),
                        pltpu.VMEM((gb, 2 * l, HEAD_DIM), F32), pltpu.VMEM((gb, 2 * l, HEAD_DIM), F32)],
        compiler_params=_cparams(1),
        name="attn_sample",
    )(proj, cache_k, cache_v, bias, sink_rows)


def _split3(x):
    hi = x.astype(BF16)
    r = x - hi.astype(F32)
    mid = r.astype(BF16)
    lo = (r - mid.astype(F32)).astype(BF16)
    return hi, mid, lo


def _select_dot(sel, x):
    hi, mid, lo = _split3(x)
    return _dot(sel, hi) + _dot(sel, mid) + _dot(sel, lo)


def _dot_select(x, sel):
    hi, mid, lo = _split3(x)
    return _dot(hi, sel) + _dot(mid, sel) + _dot(lo, sel)


def _softplus(x):
    return jnp.maximum(x, 0.0) + jnp.log1p(jnp.exp(-jnp.abs(x)))


def _conv_silu(pad_ref, w, rows):
    y = pad_ref[SUBLANES:SUBLANES + rows, :] * w[GDN_CONV - 1:GDN_CONV]
    for s in range(1, GDN_CONV):
        y = y + pad_ref[SUBLANES - s:SUBLANES - s + rows, :] * w[GDN_CONV - 1 - s:GDN_CONV - s]
    return y * jax.nn.sigmoid(y)


def _l2norm_heads(x, scale):
    outs = []
    for hh in range(GDN_HB):
        xh = x[:, hh * HEAD_DIM:(hh + 1) * HEAD_DIM]
        outs.append(xh * (lax.rsqrt(jnp.sum(xh * xh, axis=-1, keepdims=True) + NORM_EPS) * scale))
    return jnp.concatenate(outs, axis=1)


def _gdn_kernel(*refs, rows, seq, carry, live_from):
    c = GDN_CHUNK
    nc = rows // c
    nseq = c // seq
    n_live = seq - live_from
    lw = GDN_HB * HEAD_DIM
    n_scratch = 13
    if carry:
        (xq_ref, xk_ref, xv_ref, tq_ref, tk_ref, tv_ref, z_ref, wq_ref, wk_ref, wv_ref, gn_ref,
         ab_ref, pc_ref, og_ref, s_ref) = refs[:-n_scratch]
        s0_ref = None
    else:
        (xq_ref, xk_ref, xv_ref, cq_ref, ck_ref, cv_ref, z_in, wq_ref, wk_ref, wv_ref, gn_ref,
         ab_in, pc_ref, s0_ref, og_ref, s_ref, z_ref, ab_ref) = refs[:-n_scratch]
    (padq_s, padk_s, padv_s, kn_s, kb_s, qn_s, vb_s, kbe_s, qe_s, kd_s, egc_s, egl_s, gcr_s) = refs[-n_scratch:]

    group = pl.program_id(0)
    step = pl.program_id(1)
    row = lax.broadcasted_iota(jnp.int32, (rows, 1), 0)
    live = (row % seq) >= live_from
    keep_live = (lambda x: jnp.where(live, x, 0.0)) if live_from else (lambda x: x)

    if carry:
        for pad_ref, x_ref, t_ref in ((padq_s, xq_ref, tq_ref), (padk_s, xk_ref, tk_ref), (padv_s, xv_ref, tv_ref)):
            pad_ref[0:SUBLANES, :] = jnp.where(step == 0, 0.0, t_ref[...])
            pad_ref[SUBLANES:, :] = x_ref[...]
    else:
        def gather(dst_ref, x_ref, hist_ref, top):
            dst_ref[...] = jnp.zeros_like(dst_ref)
            for b in range(rows // seq):
                r0 = top + b * seq + live_from
                if hist_ref is not None:
                    dst_ref[r0 - (GDN_CONV - 1):r0, :] = hist_ref[b]
                dst_ref[r0:r0 + n_live, :] = x_ref[b * n_live:(b + 1) * n_live, :]

        gather(padq_s, xq_ref, cq_ref, SUBLANES)
        gather(padk_s, xk_ref, ck_ref, SUBLANES)
        gather(padv_s, xv_ref, cv_ref, SUBLANES)
        gather(z_ref, z_in, None, 0)
        gather(ab_ref, ab_in, None, 0)

    conv = lambda pad_ref, w_ref: keep_live(_conv_silu(pad_ref, w_ref[...], rows))
    qn = _l2norm_heads(conv(padq_s, wq_ref), HEAD_DIM ** -0.5)
    kn = _l2norm_heads(conv(padk_s, wk_ref), 1.0)
    v = conv(padv_s, wv_ref)

    ab = ab_ref[...]
    g_col = keep_live(-jnp.exp(pc_ref[0:1, :]) * _softplus(ab + pc_ref[1:2, :]))
    beta_col = keep_live(jax.nn.sigmoid(ab))
    ri = lax.broadcasted_iota(jnp.int32, (rows, rows), 0)
    ci = lax.broadcasted_iota(jnp.int32, (rows, rows), 1)
    same = (ri // seq) == (ci // seq)
    cum_sel = jnp.where(jnp.logical_and(same, ci <= ri), 1.0, 0.0).astype(BF16)
    tot_sel = jnp.where(same, 1.0, 0.0).astype(BF16)
    gc_col = _select_dot(cum_sel, g_col)
    gl_col = _select_dot(tot_sel, g_col)
    first = group * GDN_HB
    li = lax.broadcasted_iota(jnp.int32, (LANES, lw), 0)
    ni = lax.broadcasted_iota(jnp.int32, (LANES, lw), 1)
    spread_g = jnp.where(li == ni // HEAD_DIM + first, 1.0, 0.0).astype(BF16)
    spread_b = jnp.where(li == ni // HEAD_DIM + first + GDN_HEADS, 1.0, 0.0).astype(BF16)
    egc = _dot_select(gc_col, spread_g)
    egl = _dot_select(gl_col, spread_g)
    eb = _dot_select(beta_col, spread_b)
    decay_in = jnp.exp(egc)
    kb = kn * eb
    kn_s[...] = kn
    kb_s[...] = kb
    qn_s[...] = qn
    vb_s[...] = v * eb
    kbe_s[...] = kb * decay_in
    qe_s[...] = qn * decay_in
    kd_s[...] = kn * jnp.exp(egl - egc)
    egc_s[...] = egc
    egl_s[...] = egl
    pi = lax.broadcasted_iota(jnp.int32, (LANES, LANES), 0)
    pj = lax.broadcasted_iota(jnp.int32, (LANES, LANES), 1)
    pick = jnp.where(jnp.logical_and(pi == pj + first, pj < GDN_HB), 1.0, 0.0).astype(BF16)
    gc_heads = _dot_select(gc_col, pick)
    if rows % LANES:
        gc_heads = jnp.concatenate([gc_heads, jnp.zeros((LANES - rows % LANES, LANES), F32)], axis=0)
    gcr_s[...] = gc_heads.T[:SUBLANES, :rows]

    cr = lax.broadcasted_iota(jnp.int32, (c, c), 0)
    cc = lax.broadcasted_iota(jnp.int32, (c, c), 1)
    same_c = (cr // seq) == (cc // seq)
    causal = jnp.logical_and(same_c, cr >= cc)
    strict = jnp.logical_and(same_c, cr > cc)

    if carry:
        @pl.when(step == 0)
        def _():
            s_ref[...] = jnp.zeros_like(s_ref)

    eye = jnp.where(cr == cc, 1.0, 0.0)
    seq_of_row = lax.broadcasted_iota(jnp.int32, (c, HEAD_DIM), 0) // seq

    def lower_left(s):
        return jnp.logical_and(jnp.logical_and(cr // (2 * s) == cc // (2 * s), (cr // s) % 2 == 1),
                               (cc // s) % 2 == 0)

    merge_sizes = [2 ** e for e in range(1, max(1, math.ceil(math.log2(n_live))))]
    merge_masks = [lower_left(s) for s in merge_sizes]

    pairs = [(ch, hh) for ch in range(nc) for hh in range(GDN_HB)]
    rows_of = lambda ch: slice(ch * c, (ch + 1) * c)
    lanes_of = lambda hh: slice(hh * HEAD_DIM, (hh + 1) * HEAD_DIM)
    uw_all, qk_all, a_mat, t_mat, t_a = {}, {}, {}, {}, {}
    for ch, hh in pairs:
        rs, hs = rows_of(ch), lanes_of(hh)
        gcol = egc_s[rs, hh * HEAD_DIM:hh * HEAD_DIM + c]
        grow = gcr_s[hh:hh + 1, rs]
        decay = jnp.exp(jnp.where(causal, gcol - grow, NEG_INF))
        kq = jnp.concatenate([kb_s[rs, hs], qn_s[rs, hs]], axis=0).astype(BF16)
        kk = _dot_nt(kq, kn_s[rs, hs].astype(BF16)) * jnp.concatenate([decay, decay], axis=0)
        a_mat[ch, hh] = jnp.where(strict, kk[:c], 0.0)
        qk_all[ch, hh] = kk[c:].astype(BF16)
        t_mat[ch, hh] = eye - jnp.where(lower_left(1), a_mat[ch, hh], 0.0)
    for mask in merge_masks:
        for p in pairs:
            t_a[p] = _dot(t_mat[p].astype(BF16), jnp.where(mask, a_mat[p], 0.0).astype(BF16))
        for p in pairs:
            t_mat[p] = t_mat[p] - _dot(t_a[p].astype(BF16), t_mat[p].astype(BF16))
    for ch, hh in pairs:
        rs, hs = rows_of(ch), lanes_of(hh)
        uw_all[ch, hh] = _dot(t_mat[ch, hh].astype(BF16),
                              jnp.concatenate([vb_s[rs, hs], kbe_s[rs, hs]], axis=1).astype(BF16))

    if carry:
        state = [s_ref[hh] for hh in range(GDN_HB)]
    new_states = {}
    heads = range(GDN_HB)
    tn_dot = lambda a, b: lax.dot_general(a, b, (((0,), (0,)), ((), ())), preferred_element_type=F32)
    for ch in range(nc):
        rs = rows_of(ch)
        w_s, q_s, v_new_b, o = {}, {}, {}, {}
        for hh in heads:
            w = uw_all[ch, hh][:, HEAD_DIM:]
            q_e = qe_s[rs, lanes_of(hh)]
            if carry:
                wq = _dot(jnp.concatenate([w, q_e], axis=0).astype(BF16), state[hh].astype(BF16))
                w_s[hh], q_s[hh] = wq[:c], wq[c:]
            else:
                parts = []
                for b in range(nseq):
                    bs = slice(b * seq, (b + 1) * seq)
                    parts.append(_dot(jnp.concatenate([w[bs], q_e[bs]], axis=0).astype(BF16),
                                      s0_ref[ch * nseq + b, hh].astype(BF16)))
                w_s[hh] = jnp.concatenate([p[:seq] for p in parts], axis=0)
                q_s[hh] = jnp.concatenate([p[seq:] for p in parts], axis=0)
        for hh in heads:
            v_new_b[hh] = (uw_all[ch, hh][:, :HEAD_DIM] - w_s[hh]).astype(BF16)
            o[hh] = q_s[hh] + _dot(qk_all[ch, hh], v_new_b[hh])
        for hh in heads:
            hs = lanes_of(hh)
            k_d = kd_s[rs, hs]
            if carry:
                state[hh] = (state[hh] * jnp.exp(egl_s[ch * c:ch * c + 1, hs])
                             + tn_dot(k_d.astype(BF16), v_new_b[hh]))
            else:
                for b in range(nseq):
                    k_db = jnp.where(seq_of_row == b, k_d, 0.0).astype(BF16)
                    r1 = ch * c + b * seq
                    new_states[ch * nseq + b, hh] = (
                        s0_ref[ch * nseq + b, hh] * jnp.exp(egl_s[r1:r1 + 1, hs]) + tn_dot(k_db, v_new_b[hh]))
        outs = []
        for hh in heads:
            z = z_ref[rs, lanes_of(hh)]
            outs.append(_rms(o[hh], gn_ref[...]) * (z * jax.nn.sigmoid(z)))
        og = jnp.concatenate(outs, axis=1)
        if carry:
            og_ref[rs, :] = og.astype(og_ref.dtype)
        else:
            for b in range(nseq):
                r0 = b * seq + live_from
                og_ref[(ch * nseq + b) * n_live:(ch * nseq + b + 1) * n_live, :] = (
                    og[r0:r0 + n_live].astype(og_ref.dtype))
    if carry:
        s_ref[...] = jnp.stack(state, axis=0)
    else:
        for (b, hh), s_new in new_states.items():
            s_ref[b, hh] = s_new


def _gdn_params(a_log, dt_bias):
    rows = jnp.stack([a_log.astype(F32), dt_bias.astype(F32)], axis=0)
    return jnp.pad(rows, ((0, SUBLANES - 2), (0, LANES - GDN_HEADS)))


def _gdn_scratch(rows):
    lw = GDN_HB * HEAD_DIM
    return ([pltpu.VMEM((SUBLANES + rows, lw), F32) for _ in range(3)]
            + [pltpu.VMEM((rows, lw), F32) for _ in range(9)] + [pltpu.VMEM((SUBLANES, rows), F32)])


def _gdn_prompt(proj, ab, conv_w, gnorm_w, params, rows):
    t = proj.shape[0]
    lw = GDN_HB * HEAD_DIM
    x_spec = lambda off: pl.BlockSpec((rows, lw), lambda g, i: (i, off // lw + g))
    tail_spec = lambda off: pl.BlockSpec(
        (SUBLANES, lw), lambda g, i: (jnp.maximum(i * (rows // SUBLANES) - 1, 0), off // lw + g))
    w_spec = lambda off: pl.BlockSpec((GDN_CONV, lw), lambda g, i: (0, (off - GQ_OFF) // lw + g))
    return pl.pallas_call(
        functools.partial(_gdn_kernel, rows=rows, seq=GDN_CHUNK, carry=True, live_from=0),
        grid=(GDN_NG, t // rows),
        in_specs=[x_spec(GQ_OFF), x_spec(GK_OFF), x_spec(GV_OFF),
                  tail_spec(GQ_OFF), tail_spec(GK_OFF), tail_spec(GV_OFF),
                  x_spec(Z_OFF), w_spec(GQ_OFF), w_spec(GK_OFF), w_spec(GV_OFF),
                  pl.BlockSpec((1, HEAD_DIM), lambda g, i: (0, 0)),
                  pl.BlockSpec((rows, LANES), lambda g, i: (i, 0)),
                  pl.BlockSpec((SUBLANES, LANES), lambda g, i: (0, 0))],
        out_specs=[pl.BlockSpec((rows, lw), lambda g, i: (i, g)),
                   pl.BlockSpec((GDN_HB, HEAD_DIM, HEAD_DIM), lambda g, i: (g, 0, 0))],
        out_shape=[jax.ShapeDtypeStruct((t, GDN_WIDTH), BF16),
                   jax.ShapeDtypeStruct((GDN_HEADS, HEAD_DIM, HEAD_DIM), F32)],
        scratch_shapes=_gdn_scratch(rows),
        compiler_params=_cparams(2),
        name="gdn_prompt",
    )(proj, proj, proj, proj, proj, proj, proj, conv_w, conv_w, conv_w,
      gnorm_w.reshape(1, HEAD_DIM).astype(F32), ab, params)


def _gdn_sample(proj, ab, conv_state, state, conv_w, gnorm_w, params, n_live):
    rows = GDN_CHUNK
    lw = GDN_HB * HEAD_DIM
    nseq = rows // SAMPLE_PAD
    n_in = nseq * n_live
    x_spec = lambda off: pl.BlockSpec((n_in, lw), lambda g, i: (i, off // lw + g))
    hist_spec = lambda off: pl.BlockSpec((nseq, GDN_CONV - 1, lw), lambda g, i: (i, 0, (off - GQ_OFF) // lw + g))
    w_spec = lambda off: pl.BlockSpec((GDN_CONV, lw), lambda g, i: (0, (off - GQ_OFF) // lw + g))
    s_spec = pl.BlockSpec((nseq, GDN_HB, HEAD_DIM, HEAD_DIM), lambda g, i: (i, g, 0, 0))
    return pl.pallas_call(
        functools.partial(_gdn_kernel, rows=rows, seq=SAMPLE_PAD, carry=False, live_from=SAMPLE_PAD - n_live),
        grid=(GDN_NG, proj.shape[0] // n_in),
        in_specs=[x_spec(GQ_OFF), x_spec(GK_OFF), x_spec(GV_OFF),
                  hist_spec(GQ_OFF), hist_spec(GK_OFF), hist_spec(GV_OFF),
                  x_spec(Z_OFF), w_spec(GQ_OFF), w_spec(GK_OFF), w_spec(GV_OFF),
                  pl.BlockSpec((1, HEAD_DIM), lambda g, i: (0, 0)),
                  pl.BlockSpec((n_in, LANES), lambda g, i: (i, 0)),
                  pl.BlockSpec((SUBLANES, LANES), lambda g, i: (0, 0)),
                  s_spec],
        out_specs=[pl.BlockSpec((n_in, lw), lambda g, i: (i, g)), s_spec],
        out_shape=[jax.ShapeDtypeStruct((proj.shape[0], GDN_WIDTH), F32),
                   jax.ShapeDtypeStruct(state.shape, F32)],
        scratch_shapes=[pltpu.VMEM((rows, lw), F32), pltpu.VMEM((rows, LANES), F32)] + _gdn_scratch(rows),
        compiler_params=_cparams(2),
        name="gdn_sample",
    )(proj, proj, proj, conv_state, conv_state, conv_state, proj, conv_w, conv_w, conv_w,
      gnorm_w.reshape(1, HEAD_DIM).astype(F32), ab, params, state)


def _layer(x_prompt, x_sample, win_k, win_v, conv_state, gdn_state, rel_bias,
           w_in, sinks, conv_w, a_log, dt_bias, gnorm_w, w_out,
           n_pre_mix, n_post_mix, n_pre_ffn, n_post_ffn, w_gate, w_up, w_down, tiles):
    _, t, d = x_prompt.shape
    nb, ts, _ = x_sample.shape
    ms = nb * ts
    l = win_k.shape[1]
    ff = w_gate.shape[1]
    n_mt = tiles["n_mt"]
    xp, xs = x_prompt.reshape(t, d), x_sample.reshape(ms, d)

    hp = _rmsnorm(xp, n_pre_mix, BF16, tiles["norm_rows"])
    hs = _rmsnorm(xs, n_pre_mix, BF16, tiles["sample_rows"])
    w_in_t = w_in.T
    proj_p, proj_s = _proj([(hp, hs)], [(w_in_t, 0, d)], PROJ_MAIN, tiles["in_bn"], F32, n_mt, "in_proj",
                           w_transposed=True)
    w_tail = jnp.pad(w_in_t[PROJ_MAIN:], ((0, LANES - AB_WIDTH), (0, 0)))
    ab_p, ab_s = _proj([(hp, hs)], [(w_tail, 0, d)], LANES, LANES, F32, n_mt, "in_proj_ab", w_transposed=True)

    qi = jnp.arange(WINDOW)[:, None]
    kj = jnp.arange(2 * WINDOW)[None, :]
    dist_p = WINDOW + qi - kj
    dist_s = l + jnp.arange(SUBLANES)[:, None] - kj
    dist_s = jnp.where(kj < l + ts, dist_s, -1)
    bias_all = _bias_table(rel_bias, jnp.concatenate([dist_p, dist_s], axis=0))
    attn_p = _attn_prompt(proj_p, bias_all[:, :WINDOW], sinks)

    bias_s = bias_all[:, WINDOW:WINDOW + ts].reshape(ATTN_KV_HEADS, ATTN_GROUP * ts, 2 * WINDOW)
    sink_rows = jnp.repeat(sinks.astype(F32), ts).reshape(ATTN_KV_HEADS, ATTN_GROUP * ts, 1)
    attn_s, new_k, new_v = _attn_sample(proj_s, win_k, win_v, bias_s, sink_rows, ts, tiles["attn_gb"])

    params = _gdn_params(a_log, dt_bias)
    og_p, s_p = _gdn_prompt(proj_p, ab_p, conv_w, gnorm_w, params, tiles["gdn_rows"])
    og_s, s_s = _gdn_sample(proj_s, ab_s, conv_state.astype(F32), gdn_state, conv_w, gnorm_w, params, ts)

    mix_p, mix_s = _proj([(attn_p, attn_s), (og_p, og_s)], [(w_out, 0, ATTN_WIDTH), (w_out, 1, GDN_WIDTH)],
                         d, tiles["out_bn"], F32, n_mt, "out_proj")
    x1_p, h2_p = _post_mix(xp, mix_p, n_post_mix, n_pre_ffn, tiles["post_rows"])
    x1_s, h2_s = _post_mix(xs, mix_s, n_post_mix, n_pre_ffn, tiles["sample_rows"])
    act_p, act_s = _proj([(h2_p, h2_s)], [(w_gate, 0, d), (w_up, 0, d)], ff, tiles["ff_bn"], BF16, n_mt,
                         "ffn_gate_up", swiglu=True)
    ffn_p, ffn_s = _down(act_p, act_s, w_down, tiles["ff_bk"], n_mt)
    y_p = _residual_norm(x1_p, ffn_p, n_post_ffn, tiles["post_rows"])
    y_s = _residual_norm(x1_s, ffn_s, n_post_ffn, tiles["sample_rows"])

    lw = min(WINDOW, t)
    prompt_k = proj_p[t - lw:, K_OFF:K_OFF + KV_WIDTH].reshape(1, lw, ATTN_KV_HEADS, HEAD_DIM)
    prompt_v = proj_p[t - lw:, V_OFF:V_OFF + KV_WIDTH].reshape(1, lw, ATTN_KV_HEADS, HEAD_DIM)
    prompt_conv = proj_p[t - (GDN_CONV - 1):, GQ_OFF:GQ_OFF + GDN_CONV_CH][None]
    sample_conv = proj_s[:, GQ_OFF:GQ_OFF + GDN_CONV_CH].reshape(nb, ts, GDN_CONV_CH)[:, ts - (GDN_CONV - 1):]
    return (y_p.reshape(1, t, d), y_s.reshape(nb, ts, d),
            prompt_k, prompt_v, prompt_conv, s_p[None],
            new_k.reshape(nb, l, ATTN_KV_HEADS, HEAD_DIM), new_v.reshape(nb, l, ATTN_KV_HEADS, HEAD_DIM),
            sample_conv, s_s)


def _tiles(t, ms):
    n_mt = 8 if (t % (8 * 16) == 0 and ms % (8 * 16) == 0) else 1
    return dict(n_mt=n_mt, in_bn=512, out_bn=512, ff_bn=256, ff_bk=512, norm_rows=512, post_rows=256,
                sample_rows=ms, gdn_rows=256, attn_gb=8)


def kernel(x_prompt, x_sample, cache_win_k, cache_win_v, state_conv, state_gdn, rel_bias, w_in, attn_sinks,
           gdn_conv_w, gdn_a_log, gdn_dt_bias, gdn_norm_w, w_out, norm_pre_mix, norm_post_mix, norm_pre_ffn,
           norm_post_ffn, w_gate, w_up, w_down):
    depth = w_in.shape[0]
    assert depth == 1 and x_prompt.shape[0] == 1
    tiles = _tiles(x_prompt.shape[1], x_sample.shape[0] * x_sample.shape[1])
    outs = _layer(x_prompt, x_sample, cache_win_k[0], cache_win_v[0], state_conv[0], state_gdn[0], rel_bias,
                  w_in[0], attn_sinks[0], gdn_conv_w[0], gdn_a_log[0], gdn_dt_bias[0], gdn_norm_w[0], w_out[0],
                  norm_pre_mix[0], norm_post_mix[0], norm_pre_ffn[0], norm_post_ffn[0],
                  w_gate[0], w_up[0], w_down[0], tiles)
    yp, ys, pk, pv, pc, ps, sk, sv, sc, ss = outs
    return (yp, ys, pk[None], pv[None], pc[None], ps[None], sk[None], sv[None], sc[None], ss[None])
```

```python
import functools
import math

import jax
import jax.numpy as jnp
from jax import lax
from jax.experimental import pallas as pl
from jax.experimental.pallas import tpu as pltpu

F32 = jnp.float32
BF16 = jnp.bfloat16

HEAD_DIM = 128
ATTN_HEADS = 16
ATTN_KV_HEADS = 4
ATTN_GROUP = ATTN_HEADS // ATTN_KV_HEADS
ATTN_WIDTH = ATTN_HEADS * HEAD_DIM
KV_WIDTH = ATTN_KV_HEADS * HEAD_DIM
WINDOW = 128
REL_BUCKETS = 32
REL_MAX_DIST = 128
GDN_HEADS = 16
GDN_WIDTH = GDN_HEADS * HEAD_DIM
GDN_CONV = 4
GDN_CONV_CH = 3 * GDN_WIDTH
GDN_CHUNK = 64
NORM_EPS = 1e-6
NEG_INF = -1e30

Q_OFF = 0
K_OFF = ATTN_WIDTH
V_OFF = K_OFF + KV_WIDTH
GQ_OFF = V_OFF + KV_WIDTH
GK_OFF = GQ_OFF + GDN_WIDTH
GV_OFF = GK_OFF + GDN_WIDTH
Z_OFF = GV_OFF + GDN_WIDTH
PROJ_MAIN = Z_OFF + GDN_WIDTH
AB_WIDTH = 2 * GDN_HEADS

LANES = 128
SUBLANES = 8
GDN_HB = 4
GDN_NG = GDN_HEADS // GDN_HB
SAMPLE_PAD = 8
VMEM_LIMIT = 56 * 1024 * 1024


def _cparams(n_axes, vmem=VMEM_LIMIT):
    return pltpu.CompilerParams(dimension_semantics=("arbitrary",) * n_axes, vmem_limit_bytes=vmem)


def _dot(a, b):
    return jnp.dot(a, b, preferred_element_type=F32)


def _rms(x, w):
    return x * lax.rsqrt(jnp.mean(x * x, axis=-1, keepdims=True) + NORM_EPS) * w


def _rmsnorm_kernel(x_ref, w_ref, o_ref):
    o_ref[...] = _rms(x_ref[...].astype(F32), w_ref[...]).astype(o_ref.dtype)


def _rmsnorm(x, w, out_dtype, rows):
    m, d = x.shape
    return pl.pallas_call(
        _rmsnorm_kernel,
        grid=(m // rows,),
        in_specs=[pl.BlockSpec((rows, d), lambda i: (i, 0)), pl.BlockSpec((1, d), lambda i: (0, 0))],
        out_specs=pl.BlockSpec((rows, d), lambda i: (i, 0)),
        out_shape=jax.ShapeDtypeStruct((m, d), out_dtype),
        compiler_params=_cparams(1),
        name="rmsnorm",
    )(x, w.reshape(1, d).astype(F32))


def _post_mix_kernel(x_ref, mix_ref, wpost_ref, wpre_ref, x1_ref, h2_ref):
    x1 = x_ref[...] + _rms(mix_ref[...], wpost_ref[...])
    x1_ref[...] = x1
    h2_ref[...] = _rms(x1, wpre_ref[...]).astype(h2_ref.dtype)


def _post_mix(x, mix, w_post, w_pre, rows):
    m, d = x.shape
    row_spec = pl.BlockSpec((rows, d), lambda i: (i, 0))
    w_spec = pl.BlockSpec((1, d), lambda i: (0, 0))
    return pl.pallas_call(
        _post_mix_kernel,
        grid=(m // rows,),
        in_specs=[row_spec, row_spec, w_spec, w_spec],
        out_specs=[row_spec, row_spec],
        out_shape=[jax.ShapeDtypeStruct((m, d), F32), jax.ShapeDtypeStruct((m, d), BF16)],
        compiler_params=_cparams(1),
        name="post_mix",
    )(x, mix, w_post.reshape(1, d).astype(F32), w_pre.reshape(1, d).astype(F32))


def _residual_norm_kernel(x_ref, y_ref, w_ref, o_ref):
    o_ref[...] = x_ref[...] + _rms(y_ref[...], w_ref[...])


def _residual_norm(x, y, w, rows):
    m, d = x.shape
    row_spec = pl.BlockSpec((rows, d), lambda i: (i, 0))
    return pl.pallas_call(
        _residual_norm_kernel,
        grid=(m // rows,),
        in_specs=[row_spec, row_spec, pl.BlockSpec((1, d), lambda i: (0, 0))],
        out_specs=row_spec,
        out_shape=jax.ShapeDtypeStruct((m, d), F32),
        compiler_params=_cparams(1),
        name="residual_norm",
    )(x, y, w.reshape(1, d).astype(F32))


def _stack_rows(p_ref, s_ref, cols=slice(None)):
    return jnp.concatenate([p_ref[:, cols].astype(BF16), s_ref[:, cols].astype(BF16)], axis=0)


def _dot_nt(a, b):
    return lax.dot_general(a, b, (((1,), (1,)), ((), ())), preferred_element_type=F32)


def _proj_kernel(*refs, n_a, n_w, swiglu, w_transposed, row_blocks, n_mt):
    a_refs = refs[:2 * n_a]
    w_refs = refs[2 * n_a:2 * n_a + n_w]
    op_ref, os_ref = refs[2 * n_a + n_w:2 * n_a + n_w + 2]
    scratch = refs[2 * n_a + n_w + 2:]
    stage_refs, wb_refs, sem = scratch[:n_w], scratch[n_w:2 * n_w], scratch[2 * n_w]
    mm = _dot_nt if w_transposed else _dot
    j, i = pl.program_id(0), pl.program_id(1)
    nj = pl.num_programs(0)

    def slice_copy(w, tile, s, slot):
        rows, cols = stage_refs[w].shape[1:]
        if w_transposed:
            src = w_refs[w].at[pl.ds(tile * (rows * n_mt) + s * rows, rows), pl.ds(row_blocks[w] * cols, cols)]
        else:
            src = w_refs[w].at[pl.ds((row_blocks[w] * n_mt + s) * rows, rows), pl.ds(tile * cols, cols)]
        return pltpu.make_async_copy(src, stage_refs[w].at[slot], sem.at[w, slot])

    def cast_slice(w, half, s, slot):
        rows = stage_refs[w].shape[1]
        wb_refs[w][half, pl.ds(pl.multiple_of(s * rows, rows), rows), :] = stage_refs[w][slot].astype(BF16)

    @pl.when(jnp.logical_and(j == 0, i == 0))
    def _():
        for w in range(n_w):
            for s in range(n_mt):
                copy = slice_copy(w, 0, s, s % 2)
                copy.start()
                copy.wait()
                cast_slice(w, 0, s, s % 2)

        @pl.when(nj > 1)
        def _():
            for w in range(n_w):
                slice_copy(w, 1, 0, 0).start()

    @pl.when(j + 1 < nj)
    def _():
        slot = (j * n_mt + i) % 2
        for w in range(n_w):
            slice_copy(w, j + 1, i, slot).wait()
            cast_slice(w, (j + 1) % 2, i, slot)
        last = i + 1 == n_mt
        next_tile = jnp.where(last, j + 2, j + 1)
        next_s = jnp.where(last, 0, i + 1)

        @pl.when(next_tile < nj)
        def _():
            for w in range(n_w):
                slice_copy(w, next_tile, next_s, 1 - slot).start()

    half = j % 2
    lhs = [_stack_rows(a_refs[2 * p], a_refs[2 * p + 1]) for p in range(n_a)]
    if swiglu:
        g = mm(lhs[0], wb_refs[0][half])
        o = g * jax.nn.sigmoid(g) * mm(lhs[0], wb_refs[1][half])
    else:
        o = mm(lhs[0], wb_refs[0][half])
        for l, wb_ref in zip(lhs[1:], wb_refs[1:]):
            o = o + mm(l, wb_ref[half])
    bmp = op_ref.shape[0]
    op_ref[...] = o[:bmp].astype(op_ref.dtype)
    os_ref[...] = o[bmp:].astype(os_ref.dtype)


def _proj(a_pairs, w_blocks, n, bn, out_dtype, n_mt, name, swiglu=False, w_transposed=False):
    tp, ts = a_pairs[0][0].shape[0], a_pairs[0][1].shape[0]
    bmp, bms = tp // n_mt, ts // n_mt
    in_specs = []
    for a_p, a_s in a_pairs:
        in_specs.append(pl.BlockSpec((bmp, a_p.shape[1]), lambda j, i: (i, 0)))
        in_specs.append(pl.BlockSpec((bms, a_s.shape[1]), lambda j, i: (i, 0)))
    in_specs += [pl.BlockSpec(memory_space=pl.ANY) for _ in w_blocks]
    tile = lambda k: (bn, k) if w_transposed else (k, bn)
    stage = lambda k: (2, bn // n_mt, k) if w_transposed else (2, k // n_mt, bn)
    operands = [a for pair in a_pairs for a in pair] + [w for w, _, _ in w_blocks]
    return pl.pallas_call(
        functools.partial(_proj_kernel, n_a=len(a_pairs), n_w=len(w_blocks), swiglu=swiglu,
                          w_transposed=w_transposed, row_blocks=tuple(rb for _, rb, _ in w_blocks), n_mt=n_mt),
        grid=(n // bn, n_mt),
        in_specs=in_specs,
        out_specs=[pl.BlockSpec((bmp, bn), lambda j, i: (i, j)), pl.BlockSpec((bms, bn), lambda j, i: (i, j))],
        out_shape=[jax.ShapeDtypeStruct((tp, n), out_dtype), jax.ShapeDtypeStruct((ts, n), out_dtype)],
        scratch_shapes=([pltpu.VMEM(stage(k), F32) for _, _, k in w_blocks]
                        + [pltpu.VMEM((2,) + tile(k), BF16) for _, _, k in w_blocks]
                        + [pltpu.SemaphoreType.DMA((len(w_blocks), 2))]),
        compiler_params=_cparams(2),
        name=name,
    )(*operands)


def _down_kernel(ap_ref, as_ref, b_ref, op_ref, os_ref, *, bn, k_total):
    k = pl.program_id(1)
    nk = pl.num_programs(1)
    bk = b_ref.shape[0]
    bmp = ap_ref.shape[0]
    last_rows = k_total - (k_total // bk) * bk

    @pl.when(k == 0)
    def _():
        op_ref[...] = jnp.zeros_like(op_ref)
        os_ref[...] = jnp.zeros_like(os_ref)

    def accumulate(kk):
        a = _stack_rows(ap_ref, as_ref, slice(0, kk))
        for n0 in range(0, op_ref.shape[1], bn):
            d = _dot(a, b_ref[:kk, n0:n0 + bn].astype(BF16))
            op_ref[:, n0:n0 + bn] += d[:bmp]
            os_ref[:, n0:n0 + bn] += d[bmp:]

    if last_rows == 0:
        accumulate(bk)
    else:
        pl.when(k < nk - 1)(lambda: accumulate(bk))
        pl.when(k == nk - 1)(lambda: accumulate(last_rows))


def _down(a_p, a_s, b, bk, n_mt):
    (tp, k_total), ts = a_p.shape, a_s.shape[0]
    n = b.shape[1]
    bmp, bms = tp // n_mt, ts // n_mt
    return pl.pallas_call(
        functools.partial(_down_kernel, bn=min(n, 512), k_total=k_total),
        grid=(n_mt, pl.cdiv(k_total, bk)),
        in_specs=[pl.BlockSpec((bmp, bk), lambda i, k: (i, k)), pl.BlockSpec((bms, bk), lambda i, k: (i, k)),
                  pl.BlockSpec((bk, n), lambda i, k: (k, 0))],
        out_specs=[pl.BlockSpec((bmp, n), lambda i, k: (i, 0), pipeline_mode=pl.Buffered(1)),
                   pl.BlockSpec((bms, n), lambda i, k: (i, 0))],
        out_shape=[jax.ShapeDtypeStruct((tp, n), F32), jax.ShapeDtypeStruct((ts, n), F32)],
        compiler_params=_cparams(2),
        name="ffn_down",
    )(a_p, a_s, b)


def _rel_bucket(d):
    n = jnp.maximum(d, 0)
    max_exact = REL_BUCKETS // 2
    nf = jnp.maximum(n, 1).astype(F32)
    large = max_exact + (jnp.log(nf / max_exact) / math.log(REL_MAX_DIST / max_exact)
                         * (REL_BUCKETS - max_exact)).astype(jnp.int32)
    large = jnp.minimum(large, REL_BUCKETS - 1)
    return jnp.where(n < max_exact, n, large)


def _bias_kernel(rel_ref, bucket_ref, allowed_ref, o_ref):
    h = pl.program_id(0)
    bucket = bucket_ref[...]
    acc = jnp.zeros(bucket.shape, F32)
    for n in range(REL_BUCKETS):
        acc = jnp.where(bucket == n, rel_ref[n, h], acc)
    o_ref[0] = jnp.where(allowed_ref[...] > 0, acc, NEG_INF)


def _bias_table(rel_bias, dist):
    tq, tk = dist.shape
    allowed = ((dist >= 0) & (dist <= WINDOW)).astype(jnp.int32)
    full = pl.BlockSpec((tq, tk), lambda h: (0, 0))
    return pl.pallas_call(
        _bias_kernel,
        grid=(ATTN_HEADS,),
        in_specs=[pl.BlockSpec(memory_space=pltpu.SMEM), full, full],
        out_specs=pl.BlockSpec((1, tq, tk), lambda h: (h, 0, 0)),
        out_shape=jax.ShapeDtypeStruct((ATTN_HEADS, tq, tk), F32),
        compiler_params=_cparams(1),
        name="rel_bias_table",
    )(rel_bias.astype(F32), _rel_bucket(dist), allowed)


def _softmax_pv(s, sink):
    m = jnp.maximum(jnp.max(s, axis=-1, keepdims=True), sink)
    p = jnp.exp(s - m)
    denom = jnp.sum(p, axis=-1, keepdims=True) + jnp.exp(sink - m)
    return p.astype(BF16), denom


def _attn_prompt_kernel(sink_ref, cur_ref, prev_ref, bias_ref, o_ref):
    i = pl.program_id(0)
    c = WINDOW
    col = lax.broadcasted_iota(jnp.int32, (c, 2 * c), 1)
    no_prev = jnp.logical_and(i == 0, col < c)
    head_cols = lambda base, h: slice(base + h * HEAD_DIM, base + (h + 1) * HEAD_DIM)
    kk, vv = [], []
    for h in range(ATTN_KV_HEADS):
        kk.append(jnp.concatenate([prev_ref[:, head_cols(0, h)], cur_ref[:, head_cols(K_OFF, h)]],
                                  axis=0).astype(BF16))
        vv.append(jnp.concatenate([prev_ref[:, head_cols(KV_WIDTH, h)], cur_ref[:, head_cols(V_OFF, h)]],
                                  axis=0).astype(BF16))
    heads = range(ATTN_HEADS)
    s = [_dot_nt(cur_ref[:, head_cols(Q_OFF, hd)].astype(BF16), kk[hd // ATTN_GROUP]) for hd in heads]
    pd = []
    for hd in heads:
        sc = jnp.where(no_prev, NEG_INF, s[hd] * (HEAD_DIM ** -0.5) + bias_ref[hd])
        pd.append(_softmax_pv(sc, sink_ref[hd]))
    outs = [_dot(pd[hd][0], vv[hd // ATTN_GROUP]) / pd[hd][1] for hd in heads]
    o_ref[...] = jnp.concatenate(outs, axis=1).astype(o_ref.dtype)


def _attn_prompt(proj, bias, sinks):
    t = proj.shape[0]
    c = WINDOW
    qkv = V_OFF + KV_WIDTH
    return pl.pallas_call(
        _attn_prompt_kernel,
        grid=(t // c,),
        in_specs=[pl.BlockSpec(memory_space=pltpu.SMEM),
                  pl.BlockSpec((c, qkv), lambda i: (i, 0)),
                  pl.BlockSpec((c, 2 * KV_WIDTH), lambda i: (jnp.maximum(i - 1, 0), K_OFF // (2 * KV_WIDTH))),
                  pl.BlockSpec((ATTN_HEADS, c, 2 * c), lambda i: (0, 0, 0))],
        out_specs=pl.BlockSpec((c, ATTN_WIDTH), lambda i: (i, 0)),
        out_shape=jax.ShapeDtypeStruct((t, ATTN_WIDTH), BF16),
        compiler_params=_cparams(1),
        name="attn_prompt",
    )(sinks.astype(F32), proj, proj, bias)


def _attn_sample_kernel(x_ref, ck_ref, cv_ref, bias_ref, sink_ref, o_ref, ok_ref, ov_ref,
                        q_s, kf_ref, vf_ref, *, t_new):
    l = WINDOW
    gb = ck_ref.shape[0]

    @pl.when(pl.program_id(0) == 0)
    def _():
        kf_ref[:, l + t_new:, :] = jnp.zeros_like(kf_ref[:, l + t_new:, :])
        vf_ref[:, l + t_new:, :] = jnp.zeros_like(vf_ref[:, l + t_new:, :])

    for h in range(ATTN_KV_HEADS):
        kf_ref[:, 0:l, :] = ck_ref[:, :, h, :]
        vf_ref[:, 0:l, :] = cv_ref[:, :, h, :]
        for b in range(gb):
            rows = slice(b * t_new, (b + 1) * t_new)
            kf_ref[b, l:l + t_new, :] = x_ref[rows, K_OFF + h * HEAD_DIM:K_OFF + (h + 1) * HEAD_DIM]
            vf_ref[b, l:l + t_new, :] = x_ref[rows, V_OFF + h * HEAD_DIM:V_OFF + (h + 1) * HEAD_DIM]
            for g in range(ATTN_GROUP):
                col = (h * ATTN_GROUP + g) * HEAD_DIM
                q_s[b, g * t_new:(g + 1) * t_new, :] = x_ref[rows, col:col + HEAD_DIM]
        ok_ref[:, :, h, :] = kf_ref[:, t_new:l + t_new, :]
        ov_ref[:, :, h, :] = vf_ref[:, t_new:l + t_new, :]

        s = jnp.einsum("bqd,bkd->bqk", q_s[...].astype(BF16), kf_ref[...].astype(BF16),
                       preferred_element_type=F32)
        s = s * (HEAD_DIM ** -0.5) + bias_ref[h]
        p, denom = _softmax_pv(s, sink_ref[h])
        o = jnp.einsum("bqk,bkd->bqd", p, vf_ref[...].astype(BF16), preferred_element_type=F32) / denom
        for b in range(gb):
            for g in range(ATTN_GROUP):
                col = (h * ATTN_GROUP + g) * HEAD_DIM
                o_ref[b * t_new:(b + 1) * t_new, col:col + HEAD_DIM] = o[b, g * t_new:(g + 1) * t_new, :]


def _attn_sample(proj, cache_k, cache_v, bias, sink_rows, t_new, gb):
    b, l = cache_k.shape[:2]
    rows = ATTN_GROUP * t_new
    qkv = V_OFF + KV_WIDTH
    cache_spec = pl.BlockSpec((gb, l, ATTN_KV_HEADS, HEAD_DIM), lambda i: (i, 0, 0, 0))
    return pl.pallas_call(
        functools.partial(_attn_sample_kernel, t_new=t_new),
        grid=(b // gb,),
        in_specs=[pl.BlockSpec((gb * t_new, qkv), lambda i: (i, 0)), cache_spec, cache_spec,
                  pl.BlockSpec((ATTN_KV_HEADS, rows, 2 * l), lambda i: (0, 0, 0)),
                  pl.BlockSpec((ATTN_KV_HEADS, rows, 1), lambda i: (0, 0, 0))],
        out_specs=[pl.BlockSpec((gb * t_new, ATTN_WIDTH), lambda i: (i, 0)), cache_spec, cache_spec],
        out_shape=[jax.ShapeDtypeStruct((b * t_new, ATTN_WIDTH), F32),
                   jax.ShapeDtypeStruct(cache_k.shape, F32),
                   jax.ShapeDtypeStruct(cache_v.shape, F32)],
        scratch_shapes=[pltpu.VMEM((gb, rows, HEAD_DIM), F32),
                        pltpu.VMEM((gb, 2 * l, HEAD_DIM), F32), pltpu.VMEM((gb, 2 * l, HEAD_DIM), F32)],
        compiler_params=_cparams(1),
        name="attn_sample",
    )(proj, cache_k, cache_v, bias, sink_rows)


def _split3(x):
    hi = x.astype(BF16)
    r = x - hi.astype(F32)
    mid = r.astype(BF16)
    lo = (r - mid.astype(F32)).astype(BF16)
    return hi, mid, lo


def _select_dot(sel, x):
    hi, mid, lo = _split3(x)
    return _dot(sel, hi) + _dot(sel, mid) + _dot(sel, lo)


def _dot_select(x, sel):
    hi, mid, lo = _split3(x)
    return _dot(hi, sel) + _dot(mid, sel) + _dot(lo, sel)


def _softplus(x):
    return jnp.maximum(x, 0.0) + jnp.log1p(jnp.exp(-jnp.abs(x)))


def _conv_silu(pad_ref, w, rows):
    y = pad_ref[SUBLANES:SUBLANES + rows, :] * w[GDN_CONV - 1:GDN_CONV]
    for s in range(1, GDN_CONV):
        y = y + pad_ref[SUBLANES - s:SUBLANES - s + rows, :] * w[GDN_CONV - 1 - s:GDN_CONV - s]
    return y * jax.nn.sigmoid(y)


def _l2norm_heads(x, scale):
    outs = []
    for hh in range(GDN_HB):
        xh = x[:, hh * HEAD_DIM:(hh + 1) * HEAD_DIM]
        outs.append(xh * (lax.rsqrt(jnp.sum(xh * xh, axis=-1, keepdims=True) + NORM_EPS) * scale))
    return jnp.concatenate(outs, axis=1)


def _gdn_kernel(*refs, rows, seq, carry, live_from):
    c = GDN_CHUNK
    nc = rows // c
    nseq = c // seq
    n_live = seq - live_from
    lw = GDN_HB * HEAD_DIM
    n_scratch = 13
    if carry:
        (xq_ref, xk_ref, xv_ref, tq_ref, tk_ref, tv_ref, z_ref, wq_ref, wk_ref, wv_ref, gn_ref,
         ab_ref, pc_ref, og_ref, s_ref) = refs[:-n_scratch]
        s0_ref = None
    else:
        (xq_ref, xk_ref, xv_ref, cq_ref, ck_ref, cv_ref, z_in, wq_ref, wk_ref, wv_ref, gn_ref,
         ab_in, pc_ref, s0_ref, og_ref, s_ref, z_ref, ab_ref) = refs[:-n_scratch]
    (padq_s, padk_s, padv_s, kn_s, kb_s, qn_s, vb_s, kbe_s, qe_s, kd_s, egc_s, egl_s, gcr_s) = refs[-n_scratch:]

    group = pl.program_id(0)
    step = pl.program_id(1)
    row = lax.broadcasted_iota(jnp.int32, (rows, 1), 0)
    live = (row % seq) >= live_from
    keep_live = (lambda x: jnp.where(live, x, 0.0)) if live_from else (lambda x: x)

    if carry:
        for pad_ref, x_ref, t_ref in ((padq_s, xq_ref, tq_ref), (padk_s, xk_ref, tk_ref), (padv_s, xv_ref, tv_ref)):
            pad_ref[0:SUBLANES, :] = jnp.where(step == 0, 0.0, t_ref[...])
            pad_ref[SUBLANES:, :] = x_ref[...]
    else:
        def gather(dst_ref, x_ref, hist_ref, top):
            dst_ref[...] = jnp.zeros_like(dst_ref)
            for b in range(rows // seq):
                r0 = top + b * seq + live_from
                if hist_ref is not None:
                    dst_ref[r0 - (GDN_CONV - 1):r0, :] = hist_ref[b]
                dst_ref[r0:r0 + n_live, :] = x_ref[b * n_live:(b + 1) * n_live, :]

        gather(padq_s, xq_ref, cq_ref, SUBLANES)
        gather(padk_s, xk_ref, ck_ref, SUBLANES)
        gather(padv_s, xv_ref, cv_ref, SUBLANES)
        gather(z_ref, z_in, None, 0)
        gather(ab_ref, ab_in, None, 0)

    conv = lambda pad_ref, w_ref: keep_live(_conv_silu(pad_ref, w_ref[...], rows))
    qn = _l2norm_heads(conv(padq_s, wq_ref), HEAD_DIM ** -0.5)
    kn = _l2norm_heads(conv(padk_s, wk_ref), 1.0)
    v = conv(padv_s, wv_ref)

    ab = ab_ref[...]
    g_col = keep_live(-jnp.exp(pc_ref[0:1, :]) * _softplus(ab + pc_ref[1:2, :]))
    beta_col = keep_live(jax.nn.sigmoid(ab))
    ri = lax.broadcasted_iota(jnp.int32, (rows, rows), 0)
    ci = lax.broadcasted_iota(jnp.int32, (rows, rows), 1)
    same = (ri // seq) == (ci // seq)
    cum_sel = jnp.where(jnp.logical_and(same, ci <= ri), 1.0, 0.0).astype(BF16)
    tot_sel = jnp.where(same, 1.0, 0.0).astype(BF16)
    gc_col = _select_dot(cum_sel, g_col)
    gl_col = _select_dot(tot_sel, g_col)
    first = group * GDN_HB
    li = lax.broadcasted_iota(jnp.int32, (LANES, lw), 0)
    ni = lax.broadcasted_iota(jnp.int32, (LANES, lw), 1)
    spread_g = jnp.where(li == ni // HEAD_DIM + first, 1.0, 0.0).astype(BF16)
    spread_b = jnp.where(li == ni // HEAD_DIM + first + GDN_HEADS, 1.0, 0.0).astype(BF16)
    egc = _dot_select(gc_col, spread_g)
    egl = _dot_select(gl_col, spread_g)
    eb = _dot_select(beta_col, spread_b)
    decay_in = jnp.exp(egc)
    kb = kn * eb
    kn_s[...] = kn
    kb_s[...] = kb
    qn_s[...] = qn
    vb_s[...] = v * eb
    kbe_s[...] = kb * decay_in
    qe_s[...] = qn * decay_in
    kd_s[...] = kn * jnp.exp(egl - egc)
    egc_s[...] = egc
    egl_s[...] = egl
    pi = lax.broadcasted_iota(jnp.int32, (LANES, LANES), 0)
    pj = lax.broadcasted_iota(jnp.int32, (LANES, LANES), 1)
    pick = jnp.where(jnp.logical_and(pi == pj + first, pj < GDN_HB), 1.0, 0.0).astype(BF16)
    gc_heads = _dot_select(gc_col, pick)
    if rows % LANES:
        gc_heads = jnp.concatenate([gc_heads, jnp.zeros((LANES - rows % LANES, LANES), F32)], axis=0)
    gcr_s[...] = gc_heads.T[:SUBLANES, :rows]

    cr = lax.broadcasted_iota(jnp.int32, (c, c), 0)
    cc = lax.broadcasted_iota(jnp.int32, (c, c), 1)
    same_c = (cr // seq) == (cc // seq)
    causal = jnp.logical_and(same_c, cr >= cc)
    strict = jnp.logical_and(same_c, cr > cc)

    if carry:
        @pl.when(step == 0)
        def _():
            s_ref[...] = jnp.zeros_like(s_ref)

    eye = jnp.where(cr == cc, 1.0, 0.0)
    seq_of_row = lax.broadcasted_iota(jnp.int32, (c, HEAD_DIM), 0) // seq

    def lower_left(s):
        return jnp.logical_and(jnp.logical_and(cr // (2 * s) == cc // (2 * s), (cr // s) % 2 == 1),
                               (cc // s) % 2 == 0)

    merge_sizes = [2 ** e for e in range(1, max(1, math.ceil(math.log2(n_live))))]
    merge_masks = [lower_left(s) for s in merge_sizes]

    pairs = [(ch, hh) for ch in range(nc) for hh in range(GDN_HB)]
    rows_of = lambda ch: slice(ch * c, (ch + 1) * c)
    lanes_of = lambda hh: slice(hh * HEAD_DIM, (hh + 1) * HEAD_DIM)
    uw_all, qk_all, a_mat, t_mat, t_a = {}, {}, {}, {}, {}
    for ch, hh in pairs:
        rs, hs = rows_of(ch), lanes_of(hh)
        gcol = egc_s[rs, hh * HEAD_DIM:hh * HEAD_DIM + c]
        grow = gcr_s[hh:hh + 1, rs]
        decay = jnp.exp(jnp.where(causal, gcol - grow, NEG_INF))
        kq = jnp.concatenate([kb_s[rs, hs], qn_s[rs, hs]], axis=0).astype(BF16)
        kk = _dot_nt(kq, kn_s[rs, hs].astype(BF16)) * jnp.concatenate([decay, decay], axis=0)
        a_mat[ch, hh] = jnp.where(strict, kk[:c], 0.0)
        qk_all[ch, hh] = kk[c:].astype(BF16)
        t_mat[ch, hh] = eye - jnp.where(lower_left(1), a_mat[ch, hh], 0.0)
    for mask in merge_masks:
        for p in pairs:
            t_a[p] = _dot(t_mat[p].astype(BF16), jnp.where(mask, a_mat[p], 0.0).astype(BF16))
        for p in pairs:
            t_mat[p] = t_mat[p] - _dot(t_a[p].astype(BF16), t_mat[p].astype(BF16))
    for ch, hh in pairs:
        rs, hs = rows_of(ch), lanes_of(hh)
        uw_all[ch, hh] = _dot(t_mat[ch, hh].astype(BF16),
                              jnp.concatenate([vb_s[rs, hs], kbe_s[rs, hs]], axis=1).astype(BF16))

    if carry:
        state = [s_ref[hh] for hh in range(GDN_HB)]
    new_states = {}
    heads = range(GDN_HB)
    tn_dot = lambda a, b: lax.dot_general(a, b, (((0,), (0,)), ((), ())), preferred_element_type=F32)
    for ch in range(nc):
        rs = rows_of(ch)
        w_s, q_s, v_new_b, o = {}, {}, {}, {}
        for hh in heads:
            w = uw_all[ch, hh][:, HEAD_DIM:]
            q_e = qe_s[rs, lanes_of(hh)]
            if carry:
                wq = _dot(jnp.concatenate([w, q_e], axis=0).astype(BF16), state[hh].astype(BF16))
                w_s[hh], q_s[hh] = wq[:c], wq[c:]
            else:
                parts = []
                for b in range(nseq):
                    bs = slice(b * seq, (b + 1) * seq)
                    parts.append(_dot(jnp.concatenate([w[bs], q_e[bs]], axis=0).astype(BF16),
                                      s0_ref[ch * nseq + b, hh].astype(BF16)))
                w_s[hh] = jnp.concatenate([p[:seq] for p in parts], axis=0)
                q_s[hh] = jnp.concatenate([p[seq:] for p in parts], axis=0)
        for hh in heads:
            v_new_b[hh] = (uw_all[ch, hh][:, :HEAD_DIM] - w_s[hh]).astype(BF16)
            o[hh] = q_s[hh] + _dot(qk_all[ch, hh], v_new_b[hh])
        for hh in heads:
            hs = lanes_of(hh)
            k_d = kd_s[rs, hs]
            if carry:
                state[hh] = (state[hh] * jnp.exp(egl_s[ch * c:ch * c + 1, hs])
                             + tn_dot(k_d.astype(BF16), v_new_b[hh]))
            else:
                for b in range(nseq):
                    k_db = jnp.where(seq_of_row == b, k_d, 0.0).astype(BF16)
                    r1 = ch * c + b * seq
                    new_states[ch * nseq + b, hh] = (
                        s0_ref[ch * nseq + b, hh] * jnp.exp(egl_s[r1:r1 + 1, hs]) + tn_dot(k_db, v_new_b[hh]))
        outs = []
        for hh in heads:
            z = z_ref[rs, lanes_of(hh)]
            outs.append(_rms(o[hh], gn_ref[...]) * (z * jax.nn.sigmoid(z)))
        og = jnp.concatenate(outs, axis=1)
        if carry:
            og_ref[rs, :] = og.astype(og_ref.dtype)
        else:
            for b in range(nseq):
                r0 = b * seq + live_from
                og_ref[(ch * nseq + b) * n_live:(ch * nseq + b + 1) * n_live, :] = (
                    og[r0:r0 + n_live].astype(og_ref.dtype))
    if carry:
        s_ref[...] = jnp.stack(state, axis=0)
    else:
        for (b, hh), s_new in new_states.items():
            s_ref[b, hh] = s_new


def _gdn_params(a_log, dt_bias):
    rows = jnp.stack([a_log.astype(F32), dt_bias.astype(F32)], axis=0)
    return jnp.pad(rows, ((0, SUBLANES - 2), (0, LANES - GDN_HEADS)))


def _gdn_scratch(rows):
    lw = GDN_HB * HEAD_DIM
    return ([pltpu.VMEM((SUBLANES + rows, lw), F32) for _ in range(3)]
            + [pltpu.VMEM((rows, lw), F32) for _ in range(9)] + [pltpu.VMEM((SUBLANES, rows), F32)])


def _gdn_prompt(proj, ab, conv_w, gnorm_w, params, rows):
    t = proj.shape[0]
    lw = GDN_HB * HEAD_DIM
    x_spec = lambda off: pl.BlockSpec((rows, lw), lambda g, i: (i, off // lw + g))
    tail_spec = lambda off: pl.BlockSpec(
        (SUBLANES, lw), lambda g, i: (jnp.maximum(i * (rows // SUBLANES) - 1, 0), off // lw + g))
    w_spec = lambda off: pl.BlockSpec((GDN_CONV, lw), lambda g, i: (0, (off - GQ_OFF) // lw + g))
    return pl.pallas_call(
        functools.partial(_gdn_kernel, rows=rows, seq=GDN_CHUNK, carry=True, live_from=0),
        grid=(GDN_NG, t // rows),
        in_specs=[x_spec(GQ_OFF), x_spec(GK_OFF), x_spec(GV_OFF),
                  tail_spec(GQ_OFF), tail_spec(GK_OFF), tail_spec(GV_OFF),
                  x_spec(Z_OFF), w_spec(GQ_OFF), w_spec(GK_OFF), w_spec(GV_OFF),
                  pl.BlockSpec((1, HEAD_DIM), lambda g, i: (0, 0)),
                  pl.BlockSpec((rows, LANES), lambda g, i: (i, 0)),
                  pl.BlockSpec((SUBLANES, LANES), lambda g, i: (0, 0))],
        out_specs=[pl.BlockSpec((rows, lw), lambda g, i: (i, g)),
                   pl.BlockSpec((GDN_HB, HEAD_DIM, HEAD_DIM), lambda g, i: (g, 0, 0))],
        out_shape=[jax.ShapeDtypeStruct((t, GDN_WIDTH), BF16),
                   jax.ShapeDtypeStruct((GDN_HEADS, HEAD_DIM, HEAD_DIM), F32)],
        scratch_shapes=_gdn_scratch(rows),
        compiler_params=_cparams(2),
        name="gdn_prompt",
    )(proj, proj, proj, proj, proj, proj, proj, conv_w, conv_w, conv_w,
      gnorm_w.reshape(1, HEAD_DIM).astype(F32), ab, params)


def _gdn_sample(proj, ab, conv_state, state, conv_w, gnorm_w, params, n_live):
    rows = GDN_CHUNK
    lw = GDN_HB * HEAD_DIM
    nseq = rows // SAMPLE_PAD
    n_in = nseq * n_live
    x_spec = lambda off: pl.BlockSpec((n_in, lw), lambda g, i: (i, off // lw + g))
    hist_spec = lambda off: pl.BlockSpec((nseq, GDN_CONV - 1, lw), lambda g, i: (i, 0, (off - GQ_OFF) // lw + g))
    w_spec = lambda off: pl.BlockSpec((GDN_CONV, lw), lambda g, i: (0, (off - GQ_OFF) // lw + g))
    s_spec = pl.BlockSpec((nseq, GDN_HB, HEAD_DIM, HEAD_DIM), lambda g, i: (i, g, 0, 0))
    return pl.pallas_call(
        functools.partial(_gdn_kernel, rows=rows, seq=SAMPLE_PAD, carry=False, live_from=SAMPLE_PAD - n_live),
        grid=(GDN_NG, proj.shape[0] // n_in),
        in_specs=[x_spec(GQ_OFF), x_spec(GK_OFF), x_spec(GV_OFF),
                  hist_spec(GQ_OFF), hist_spec(GK_OFF), hist_spec(GV_OFF),
                  x_spec(Z_OFF), w_spec(GQ_OFF), w_spec(GK_OFF), w_spec(GV_OFF),
                  pl.BlockSpec((1, HEAD_DIM), lambda g, i: (0, 0)),
                  pl.BlockSpec((n_in, LANES), lambda g, i: (i, 0)),
                  pl.BlockSpec((SUBLANES, LANES), lambda g, i: (0, 0)),
                  s_spec],
        out_specs=[pl.BlockSpec((n_in, lw), lambda g, i: (i, g)), s_spec],
        out_shape=[jax.ShapeDtypeStruct((proj.shape[0], GDN_WIDTH), F32),
                   jax.ShapeDtypeStruct(state.shape, F32)],
        scratch_shapes=[pltpu.VMEM((rows, lw), F32), pltpu.VMEM((rows, LANES), F32)] + _gdn_scratch(rows),
        compiler_params=_cparams(2),
        name="gdn_sample",
    )(proj, proj, proj, conv_state, conv_state, conv_state, proj, conv_w, conv_w, conv_w,
      gnorm_w.reshape(1, HEAD_DIM).astype(F32), ab, params, state)


def _layer(x_prompt, x_sample, win_k, win_v, conv_state, gdn_state, rel_bias,
           w_in, sinks, conv_w, a_log, dt_bias, gnorm_w, w_out,
           n_pre_mix, n_post_mix, n_pre_ffn, n_post_ffn, w_gate, w_up, w_down, tiles):
    _, t, d = x_prompt.shape
    nb, ts, _ = x_sample.shape
    ms = nb * ts
    l = win_k.shape[1]
    ff = w_gate.shape[1]
    n_mt = tiles["n_mt"]
    xp, xs = x_prompt.reshape(t, d), x_sample.reshape(ms, d)

    hp = _rmsnorm(xp, n_pre_mix, BF16, tiles["norm_rows"])
    hs = _rmsnorm(xs, n_pre_mix, BF16, tiles["sample_rows"])
    w_in_t = w_in.T
    proj_p, proj_s = _proj([(hp, hs)], [(w_in_t, 0, d)], PROJ_MAIN, tiles["in_bn"], F32, n_mt, "in_proj",
                           w_transposed=True)
    w_tail = jnp.pad(w_in_t[PROJ_MAIN:], ((0, LANES - AB_WIDTH), (0, 0)))
    ab_p, ab_s = _proj([(hp, hs)], [(w_tail, 0, d)], LANES, LANES, F32, n_mt, "in_proj_ab", w_transposed=True)

    qi = jnp.arange(WINDOW)[:, None]
    kj = jnp.arange(2 * WINDOW)[None, :]
    dist_p = WINDOW + qi - kj
    dist_s = l + jnp.arange(SUBLANES)[:, None] - kj
    dist_s = jnp.where(kj < l + ts, dist_s, -1)
    bias_all = _bias_table(rel_bias, jnp.concatenate([dist_p, dist_s], axis=0))
    attn_p = _attn_prompt(proj_p, bias_all[:, :WINDOW], sinks)

    bias_s = bias_all[:, WINDOW:WINDOW + ts].reshape(ATTN_KV_HEADS, ATTN_GROUP * ts, 2 * WINDOW)
    sink_rows = jnp.repeat(sinks.astype(F32), ts).reshape(ATTN_KV_HEADS, ATTN_GROUP * ts, 1)
    attn_s, new_k, new_v = _attn_sample(proj_s, win_k, win_v, bias_s, sink_rows, ts, tiles["attn_gb"])

    params = _gdn_params(a_log, dt_bias)
    og_p, s_p = _gdn_prompt(proj_p, ab_p, conv_w, gnorm_w, params, tiles["gdn_rows"])
    og_s, s_s = _gdn_sample(proj_s, ab_s, conv_state.astype(F32), gdn_state, conv_w, gnorm_w, params, ts)

    mix_p, mix_s = _proj([(attn_p, attn_s), (og_p, og_s)], [(w_out, 0, ATTN_WIDTH), (w_out, 1, GDN_WIDTH)],
                         d, tiles["out_bn"], F32, n_mt, "out_proj")
    x1_p, h2_p = _post_mix(xp, mix_p, n_post_mix, n_pre_ffn, tiles["post_rows"])
    x1_s, h2_s = _post_mix(xs, mix_s, n_post_mix, n_pre_ffn, tiles["sample_rows"])
    act_p, act_s = _proj([(h2_p, h2_s)], [(w_gate, 0, d), (w_up, 0, d)], ff, tiles["ff_bn"], BF16, n_mt,
                         "ffn_gate_up", swiglu=True)
    ffn_p, ffn_s = _down(act_p, act_s, w_down, tiles["ff_bk"], n_mt)
    y_p = _residual_norm(x1_p, ffn_p, n_post_ffn, tiles["post_rows"])
    y_s = _residual_norm(x1_s, ffn_s, n_post_ffn, tiles["sample_rows"])

    lw = min(WINDOW, t)
    prompt_k = proj_p[t - lw:, K_OFF:K_OFF + KV_WIDTH].reshape(1, lw, ATTN_KV_HEADS, HEAD_DIM)
    prompt_v = proj_p[t - lw:, V_OFF:V_OFF + KV_WIDTH].reshape(1, lw, ATTN_KV_HEADS, HEAD_DIM)
    prompt_conv = proj_p[t - (GDN_CONV - 1):, GQ_OFF:GQ_OFF + GDN_CONV_CH][None]
    sample_conv = proj_s[:, GQ_OFF:GQ_OFF + GDN_CONV_CH].reshape(nb, ts, GDN_CONV_CH)[:, ts - (GDN_CONV - 1):]
    return (y_p.reshape(1, t, d), y_s.reshape(nb, ts, d),
            prompt_k, prompt_v, prompt_conv, s_p[None],
            new_k.reshape(nb, l, ATTN_KV_HEADS, HEAD_DIM), new_v.reshape(nb, l, ATTN_KV_HEADS, HEAD_DIM),
            sample_conv, s_s)


def _tiles(t, ms):
    n_mt = 8 if (t % (8 * 16) == 0 and ms % (8 * 16) == 0) else 1
    return dict(n_mt=n_mt, in_bn=512, out_bn=512, ff_bn=256, ff_bk=512, norm_rows=512, post_rows=256,
                sample_rows=ms, gdn_rows=256, attn_gb=8)


def kernel(x_prompt, x_sample, cache_win_k, cache_win_v, state_conv, state_gdn, rel_bias, w_in, attn_sinks,
           gdn_conv_w, gdn_a_log, gdn_dt_bias, gdn_norm_w, w_out, norm_pre_mix, norm_post_mix, norm_pre_ffn,
           norm_post_ffn, w_gate, w_up, w_down):
    depth = w_in.shape[0]
    assert depth == 1 and x_prompt.shape[0] == 1
    tiles = _tiles(x_prompt.shape[1], x_sample.shape[0] * x_sample.shape[1])
    outs = _layer(x_prompt, x_sample, cache_win_k[0], cache_win_v[0], state_conv[0], state_gdn[0], rel_bias,
                  w_in[0], attn_sinks[0], gdn_conv_w[0], gdn_a_log[0], gdn_dt_bias[0], gdn_norm_w[0], w_out[0],
                  norm_pre_mix[0], norm_post_mix[0], norm_pre_ffn[0], norm_post_ffn[0],
                  w_gate[0], w_up[0], w_down[0], tiles)
    yp, ys, pk, pv, pc, ps, sk, sv, sc, ss = outs
    return (yp, ys, pk[None], pv[None], pc[None], ps[None], sk[None], sv[None], sc[None], ss[None])
```

```python
import functools
import math

import jax
import jax.numpy as jnp
from jax import lax
from jax.experimental import pallas as pl
from jax.experimental.pallas import tpu as pltpu

F32 = jnp.float32
BF16 = jnp.bfloat16

HEAD_DIM = 128
ATTN_HEADS = 16
ATTN_KV_HEADS = 4
ATTN_GROUP = ATTN_HEADS // ATTN_KV_HEADS
ATTN_WIDTH = ATTN_HEADS * HEAD_DIM
KV_WIDTH = ATTN_KV_HEADS * HEAD_DIM
WINDOW = 128
REL_BUCKETS = 32
REL_MAX_DIST = 128
GDN_HEADS = 16
GDN_WIDTH = GDN_HEADS * HEAD_DIM
GDN_CONV = 4
GDN_CONV_CH = 3 * GDN_WIDTH
GDN_CHUNK = 64
NORM_EPS = 1e-6
NEG_INF = -1e30

Q_OFF = 0
K_OFF = ATTN_WIDTH
V_OFF = K_OFF + KV_WIDTH
GQ_OFF = V_OFF + KV_WIDTH
GK_OFF = GQ_OFF + GDN_WIDTH
GV_OFF = GK_OFF + GDN_WIDTH
Z_OFF = GV_OFF + GDN_WIDTH
PROJ_MAIN = Z_OFF + GDN_WIDTH
AB_WIDTH = 2 * GDN_HEADS

LANES = 128
SUBLANES = 8
GDN_HB = 4
GDN_NG = GDN_HEADS // GDN_HB
SAMPLE_PAD = 8
VMEM_LIMIT = 56 * 1024 * 1024


def _cparams(n_axes, vmem=VMEM_LIMIT):
    return pltpu.CompilerParams(dimension_semantics=("arbitrary",) * n_axes, vmem_limit_bytes=vmem)


def _dot(a, b):
    return jnp.dot(a, b, preferred_element_type=F32)


def _rms(x, w):
    return x * lax.rsqrt(jnp.mean(x * x, axis=-1, keepdims=True) + NORM_EPS) * w


def _rmsnorm_kernel(x_ref, w_ref, wt_ref, o_ref, tail_ref):
    h = _rms(x_ref[...].astype(F32), w_ref[...]).astype(o_ref.dtype)
    o_ref[...] = h
    tail_ref[...] = _dot_nt(h, wt_ref[...])


def _rmsnorm_proj_tail(x, w, w_tail):
    m, d = x.shape
    rows = min(m, 512)
    n = w_tail.shape[0]
    return pl.pallas_call(
        _rmsnorm_kernel,
        grid=(m // rows,),
        in_specs=[pl.BlockSpec((rows, d), lambda i: (i, 0)), pl.BlockSpec((1, d), lambda i: (0, 0)),
                  pl.BlockSpec((n, d), lambda i: (0, 0))],
        out_specs=[pl.BlockSpec((rows, d), lambda i: (i, 0)), pl.BlockSpec((rows, n), lambda i: (i, 0))],
        out_shape=[jax.ShapeDtypeStruct((m, d), BF16), jax.ShapeDtypeStruct((m, n), F32)],
        compiler_params=_cparams(1),
        name="rmsnorm",
    )(x, w.reshape(1, d).astype(F32), w_tail)


def _post_mix_kernel(x_ref, mix_ref, wpost_ref, wpre_ref, x1_ref, h2_ref):
    x1 = x_ref[...] + _rms(mix_ref[...].astype(F32), wpost_ref[...])
    x1_ref[...] = x1
    h2_ref[...] = _rms(x1, wpre_ref[...]).astype(h2_ref.dtype)


def _post_mix(x, mix, w_post, w_pre, rows):
    m, d = x.shape
    row_spec = pl.BlockSpec((rows, d), lambda i: (i, 0))
    w_spec = pl.BlockSpec((1, d), lambda i: (0, 0))
    return pl.pallas_call(
        _post_mix_kernel,
        grid=(m // rows,),
        in_specs=[row_spec, row_spec, w_spec, w_spec],
        out_specs=[row_spec, row_spec],
        out_shape=[jax.ShapeDtypeStruct((m, d), F32), jax.ShapeDtypeStruct((m, d), BF16)],
        compiler_params=_cparams(1),
        name="post_mix",
    )(x, mix, w_post.reshape(1, d).astype(F32), w_pre.reshape(1, d).astype(F32))


def _residual_norm_kernel(x_ref, y_ref, w_ref, o_ref):
    o_ref[...] = x_ref[...] + _rms(y_ref[...].astype(F32), w_ref[...])


def _residual_norm(x, y, w, rows):
    m, d = x.shape
    row_spec = pl.BlockSpec((rows, d), lambda i: (i, 0))
    return pl.pallas_call(
        _residual_norm_kernel,
        grid=(m // rows,),
        in_specs=[row_spec, row_spec, pl.BlockSpec((1, d), lambda i: (0, 0))],
        out_specs=row_spec,
        out_shape=jax.ShapeDtypeStruct((m, d), F32),
        compiler_params=_cparams(1),
        name="residual_norm",
    )(x, y, w.reshape(1, d).astype(F32))


def _stack_rows(p_ref, s_ref, cols=slice(None)):
    return jnp.concatenate([p_ref[:, cols].astype(BF16), s_ref[:, cols].astype(BF16)], axis=0)


def _dot_nt(a, b):
    return lax.dot_general(a, b, (((1,), (1,)), ((), ())), preferred_element_type=F32)


def _proj_kernel(*refs, n_a, n_w, swiglu, w_transposed, row_blocks, n_mt):
    a_refs = refs[:2 * n_a]
    w_refs = refs[2 * n_a:2 * n_a + n_w]
    op_ref, os_ref = refs[2 * n_a + n_w:2 * n_a + n_w + 2]
    scratch = refs[2 * n_a + n_w + 2:]
    stage_refs, wb_refs, sem = scratch[:n_w], scratch[n_w:2 * n_w], scratch[2 * n_w]
    mm = _dot_nt if w_transposed else _dot
    j, i = pl.program_id(0), pl.program_id(1)
    nj = pl.num_programs(0)

    def slice_copy(w, tile, s, slot):
        rows, cols = stage_refs[w].shape[1:]
        if w_transposed:
            src = w_refs[w].at[pl.ds(tile * (rows * n_mt) + s * rows, rows), pl.ds(row_blocks[w] * cols, cols)]
        else:
            src = w_refs[w].at[pl.ds((row_blocks[w] * n_mt + s) * rows, rows), pl.ds(tile * cols, cols)]
        return pltpu.make_async_copy(src, stage_refs[w].at[slot], sem.at[w, slot])

    def cast_slice(w, half, s, slot):
        rows = stage_refs[w].shape[1]
        wb_refs[w][half, pl.ds(pl.multiple_of(s * rows, rows), rows), :] = stage_refs[w][slot].astype(BF16)

    @pl.when(jnp.logical_and(j == 0, i == 0))
    def _():
        for w in range(n_w):
            for s in range(n_mt):
                copy = slice_copy(w, 0, s, s % 2)
                copy.start()
                copy.wait()
                cast_slice(w, 0, s, s % 2)

        @pl.when(nj > 1)
        def _():
            for w in range(n_w):
                slice_copy(w, 1, 0, 0).start()

    @pl.when(j + 1 < nj)
    def _():
        slot = (j * n_mt + i) % 2
        for w in range(n_w):
            slice_copy(w, j + 1, i, slot).wait()
            cast_slice(w, (j + 1) % 2, i, slot)
        last = i + 1 == n_mt
        next_tile = jnp.where(last, j + 2, j + 1)
        next_s = jnp.where(last, 0, i + 1)

        @pl.when(next_tile < nj)
        def _():
            for w in range(n_w):
                slice_copy(w, next_tile, next_s, 1 - slot).start()

    half = j % 2
    lhs = [_stack_rows(a_refs[2 * p], a_refs[2 * p + 1]) for p in range(n_a)]
    if swiglu:
        g = mm(lhs[0], wb_refs[0][half])
        o = g * jax.nn.sigmoid(g) * mm(lhs[0], wb_refs[1][half])
    else:
        o = mm(lhs[0], wb_refs[0][half])
        for l, wb_ref in zip(lhs[1:], wb_refs[1:]):
            o = o + mm(l, wb_ref[half])
    bmp = op_ref.shape[0]
    op_ref[...] = o[:bmp].astype(op_ref.dtype)
    os_ref[...] = o[bmp:].astype(os_ref.dtype)


def _proj(a_pairs, w_blocks, n, bn, out_dtype, n_mt, name, swiglu=False, w_transposed=False):
    tp, ts = a_pairs[0][0].shape[0], a_pairs[0][1].shape[0]
    bmp, bms = tp // n_mt, ts // n_mt
    in_specs = []
    for a_p, a_s in a_pairs:
        in_specs.append(pl.BlockSpec((bmp, a_p.shape[1]), lambda j, i: (i, 0)))
        in_specs.append(pl.BlockSpec((bms, a_s.shape[1]), lambda j, i: (i, 0)))
    in_specs += [pl.BlockSpec(memory_space=pl.ANY) for _ in w_blocks]
    tile = lambda k: (bn, k) if w_transposed else (k, bn)
    stage = lambda k: (2, bn // n_mt, k) if w_transposed else (2, k // n_mt, bn)
    operands = [a for pair in a_pairs for a in pair] + [w for w, _, _ in w_blocks]
    return pl.pallas_call(
        functools.partial(_proj_kernel, n_a=len(a_pairs), n_w=len(w_blocks), swiglu=swiglu,
                          w_transposed=w_transposed, row_blocks=tuple(rb for _, rb, _ in w_blocks), n_mt=n_mt),
        grid=(n // bn, n_mt),
        in_specs=in_specs,
        out_specs=[pl.BlockSpec((bmp, bn), lambda j, i: (i, j)), pl.BlockSpec((bms, bn), lambda j, i: (i, j))],
        out_shape=[jax.ShapeDtypeStruct((tp, n), out_dtype), jax.ShapeDtypeStruct((ts, n), out_dtype)],
        scratch_shapes=([pltpu.VMEM(stage(k), F32) for _, _, k in w_blocks]
                        + [pltpu.VMEM((2,) + tile(k), BF16) for _, _, k in w_blocks]
                        + [pltpu.SemaphoreType.DMA((len(w_blocks), 2))]),
        compiler_params=_cparams(2),
        name=name,
    )(*operands)


def _down_kernel(ap_ref, as_ref, b_ref, op_ref, os_ref, accp_ref, accs_ref, *, bn, k_total):
    k = pl.program_id(1)
    nk = pl.num_programs(1)
    bk = b_ref.shape[0]
    bmp = ap_ref.shape[0]
    last_rows = k_total - (k_total // bk) * bk

    @pl.when(k == 0)
    def _():
        accp_ref[...] = jnp.zeros_like(accp_ref)
        accs_ref[...] = jnp.zeros_like(accs_ref)

    def accumulate(kk):
        a = _stack_rows(ap_ref, as_ref, slice(0, kk))
        for n0 in range(0, accp_ref.shape[1], bn):
            d = _dot(a, b_ref[:kk, n0:n0 + bn].astype(BF16))
            accp_ref[:, n0:n0 + bn] += d[:bmp]
            accs_ref[:, n0:n0 + bn] += d[bmp:]

    if last_rows == 0:
        accumulate(bk)
    else:
        pl.when(k < nk - 1)(lambda: accumulate(bk))
        pl.when(k == nk - 1)(lambda: accumulate(last_rows))

    @pl.when(k == nk - 1)
    def _():
        op_ref[...] = accp_ref[...].astype(op_ref.dtype)
        os_ref[...] = accs_ref[...].astype(os_ref.dtype)


def _down(a_p, a_s, b, bk, n_mt):
    (tp, k_total), ts = a_p.shape, a_s.shape[0]
    n = b.shape[1]
    bmp, bms = tp // n_mt, ts // n_mt
    return pl.pallas_call(
        functools.partial(_down_kernel, bn=min(n, 512), k_total=k_total),
        grid=(n_mt, pl.cdiv(k_total, bk)),
        in_specs=[pl.BlockSpec((bmp, bk), lambda i, k: (i, k)), pl.BlockSpec((bms, bk), lambda i, k: (i, k)),
                  pl.BlockSpec((bk, n), lambda i, k: (k, 0))],
        out_specs=[pl.BlockSpec((bmp, n), lambda i, k: (i, 0), pipeline_mode=pl.Buffered(1)),
                   pl.BlockSpec((bms, n), lambda i, k: (i, 0))],
        out_shape=[jax.ShapeDtypeStruct((tp, n), BF16), jax.ShapeDtypeStruct((ts, n), BF16)],
        scratch_shapes=[pltpu.VMEM((bmp, n), F32), pltpu.VMEM((bms, n), F32)],
        compiler_params=_cparams(2),
        name="ffn_down",
    )(a_p, a_s, b)


def _rel_bucket(d):
    n = jnp.maximum(d, 0)
    max_exact = REL_BUCKETS // 2
    nf = jnp.maximum(n, 1).astype(F32)
    large = max_exact + (jnp.log(nf / max_exact) / math.log(REL_MAX_DIST / max_exact)
                         * (REL_BUCKETS - max_exact)).astype(jnp.int32)
    large = jnp.minimum(large, REL_BUCKETS - 1)
    return jnp.where(n < max_exact, n, large)


def _bias_kernel(rel_ref, bucket_ref, allowed_ref, o_ref):
    h = pl.program_id(0)
    bucket = bucket_ref[...]
    acc = jnp.zeros(bucket.shape, F32)
    for n in range(REL_BUCKETS):
        acc = jnp.where(bucket == n, rel_ref[n, h], acc)
    o_ref[0] = jnp.where(allowed_ref[...] > 0, acc, NEG_INF)


def _bias_table(rel_bias, dist):
    tq, tk = dist.shape
    allowed = ((dist >= 0) & (dist <= WINDOW)).astype(jnp.int32)
    full = pl.BlockSpec((tq, tk), lambda h: (0, 0))
    return pl.pallas_call(
        _bias_kernel,
        grid=(ATTN_HEADS,),
        in_specs=[pl.BlockSpec(memory_space=pltpu.SMEM), full, full],
        out_specs=pl.BlockSpec((1, tq, tk), lambda h: (h, 0, 0)),
        out_shape=jax.ShapeDtypeStruct((ATTN_HEADS, tq, tk), F32),
        compiler_params=_cparams(1),
        name="rel_bias_table",
    )(rel_bias.astype(F32), _rel_bucket(dist), allowed)


def _softmax_pv(s, sink):
    m = jnp.maximum(jnp.max(s, axis=-1, keepdims=True), sink)
    p = jnp.exp(s - m)
    denom = jnp.sum(p, axis=-1, keepdims=True) + jnp.exp(sink - m)
    return p.astype(BF16), denom


def _attn_prompt_kernel(sink_ref, cur_ref, prev_ref, bias_ref, o_ref):
    i = pl.program_id(0)
    c = WINDOW
    col = lax.broadcasted_iota(jnp.int32, (c, 2 * c), 1)
    no_prev = jnp.logical_and(i == 0, col < c)
    head_cols = lambda base, h: slice(base + h * HEAD_DIM, base + (h + 1) * HEAD_DIM)
    kk, vv = [], []
    for h in range(ATTN_KV_HEADS):
        kk.append(jnp.concatenate([prev_ref[:, head_cols(0, h)], cur_ref[:, head_cols(K_OFF, h)]],
                                  axis=0).astype(BF16))
        vv.append(jnp.concatenate([prev_ref[:, head_cols(KV_WIDTH, h)], cur_ref[:, head_cols(V_OFF, h)]],
                                  axis=0).astype(BF16))
    heads = range(ATTN_HEADS)
    s = [_dot_nt(cur_ref[:, head_cols(Q_OFF, hd)].astype(BF16), kk[hd // ATTN_GROUP]) for hd in heads]
    pd = []
    for hd in heads:
        sc = jnp.where(no_prev, NEG_INF, s[hd] * (HEAD_DIM ** -0.5) + bias_ref[hd])
        pd.append(_softmax_pv(sc, sink_ref[hd]))
    outs = [_dot(pd[hd][0], vv[hd // ATTN_GROUP]) / pd[hd][1] for hd in heads]
    o_ref[...] = jnp.concatenate(outs, axis=1).astype(o_ref.dtype)


def _attn_prompt(proj, bias, sinks):
    t = proj.shape[0]
    c = WINDOW
    qkv = V_OFF + KV_WIDTH
    return pl.pallas_call(
        _attn_prompt_kernel,
        grid=(t // c,),
        in_specs=[pl.BlockSpec(memory_space=pltpu.SMEM),
                  pl.BlockSpec((c, qkv), lambda i: (i, 0)),
                  pl.BlockSpec((c, 2 * KV_WIDTH), lambda i: (jnp.maximum(i - 1, 0), K_OFF // (2 * KV_WIDTH))),
                  pl.BlockSpec((ATTN_HEADS, c, 2 * c), lambda i: (0, 0, 0))],
        out_specs=pl.BlockSpec((c, ATTN_WIDTH), lambda i: (i, 0)),
        out_shape=jax.ShapeDtypeStruct((t, ATTN_WIDTH), BF16),
        compiler_params=_cparams(1),
        name="attn_prompt",
    )(sinks.astype(F32), proj, proj, bias)


def _attn_sample_kernel(x_ref, ck_ref, cv_ref, bias_ref, sink_ref, o_ref, ok_ref, ov_ref,
                        q_s, kf_ref, vf_ref, *, t_new):
    l = WINDOW
    gb = ck_ref.shape[0]

    @pl.when(pl.program_id(0) == 0)
    def _():
        kf_ref[:, l + t_new:, :] = jnp.zeros_like(kf_ref[:, l + t_new:, :])
        vf_ref[:, l + t_new:, :] = jnp.zeros_like(vf_ref[:, l + t_new:, :])

    for h in range(ATTN_KV_HEADS):
        kf_ref[:, 0:l, :] = ck_ref[:, :, h, :]
        vf_ref[:, 0:l, :] = cv_ref[:, :, h, :]
        for b in range(gb):
            rows = slice(b * t_new, (b + 1) * t_new)
            kf_ref[b, l:l + t_new, :] = x_ref[rows, K_OFF + h * HEAD_DIM:K_OFF + (h + 1) * HEAD_DIM]
            vf_ref[b, l:l + t_new, :] = x_ref[rows, V_OFF + h * HEAD_DIM:V_OFF + (h + 1) * HEAD_DIM]
            for g in range(ATTN_GROUP):
                col = (h * ATTN_GROUP + g) * HEAD_DIM
                q_s[b, g * t_new:(g + 1) * t_new, :] = x_ref[rows, col:col + HEAD_DIM]
        ok_ref[:, :, h, :] = kf_ref[:, t_new:l + t_new, :]
        ov_ref[:, :, h, :] = vf_ref[:, t_new:l + t_new, :]

        s = jnp.einsum("bqd,bkd->bqk", q_s[...].astype(BF16), kf_ref[...].astype(BF16),
                       preferred_element_type=F32)
        s = s * (HEAD_DIM ** -0.5) + bias_ref[h]
        p, denom = _softmax_pv(s, sink_ref[h])
        o = jnp.einsum("bqk,bkd->bqd", p, vf_ref[...].astype(BF16), preferred_element_type=F32) / denom
        for b in range(gb):
            for g in range(ATTN_GROUP):
                col = (h * ATTN_GROUP + g) * HEAD_DIM
                o_ref[b * t_new:(b + 1) * t_new, col:col + HEAD_DIM] = o[b, g * t_new:(g + 1) * t_new, :]


def _attn_sample(proj, cache_k, cache_v, bias, sink_rows, t_new, gb):
    b, l = cache_k.shape[:2]
    rows = ATTN_GROUP * t_new
    qkv = V_OFF + KV_WIDTH
    cache_spec = pl.BlockSpec((gb, l, ATTN_KV_HEADS, HEAD_DIM), lambda i: (i, 0, 0, 0))
    return pl.pallas_call(
        functools.partial(_attn_sample_kernel, t_new=t_new),
        grid=(b // gb,),
        in_specs=[pl.BlockSpec((gb * t_new, qkv), lambda i: (i, 0)), cache_spec, cache_spec,
                  pl.BlockSpec((ATTN_KV_HEADS, rows, 2 * l), lambda i: (0, 0, 0)),
                  pl.BlockSpec((ATTN_KV_HEADS, rows, 1), lambda i: (0, 0, 0))],
        out_specs=[pl.BlockSpec((gb * t_new, ATTN_WIDTH), lambda i: (i, 0)), cache_spec, cache_spec],
        out_shape=[jax.ShapeDtypeStruct((b * t_new, ATTN_WIDTH), F32),
                   jax.ShapeDtypeStruct(cache_k.shape, F32),
                   jax.ShapeDtypeStruct(cache_v.shape, F32)],
        scratch_shapes=[pltpu.VMEM((gb, rows, HEAD_DIM), F32),
                        pltpu.VMEM((gb, 2 * l, HEAD_DIM), F32), pltpu.VMEM((gb, 2 * l, HEAD_DIM), F32)],
        compiler_params=_cparams(1),
        name="attn_sample",
    )(proj, cache_k, cache_v, bias, sink_rows)


def _split3(x):
    hi = x.astype(BF16)
    r = x - hi.astype(F32)
    mid = r.astype(BF16)
    lo = (r - mid.astype(F32)).astype(BF16)
    return hi, mid, lo


def _select_dot(sel, x):
    hi, mid, lo = _split3(x)
    return _dot(sel, hi) + _dot(sel, mid) + _dot(sel, lo)


def _dot_select(x, sel):
    hi, mid, lo = _split3(x)
    return _dot(hi, sel) + _dot(mid, sel) + _dot(lo, sel)


def _softplus(x):
    return jnp.maximum(x, 0.0) + jnp.log1p(jnp.exp(-jnp.abs(x)))


def _conv_silu(pad_ref, w, rows):
    y = pad_ref[SUBLANES:SUBLANES + rows, :] * w[GDN_CONV - 1:GDN_CONV]
    for s in range(1, GDN_CONV):
        y = y + pad_ref[SUBLANES - s:SUBLANES - s + rows, :] * w[GDN_CONV - 1 - s:GDN_CONV - s]
    return y * jax.nn.sigmoid(y)


def _l2norm_heads(x, scale):
    outs = []
    for hh in range(GDN_HB):
        xh = x[:, hh * HEAD_DIM:(hh + 1) * HEAD_DIM]
        outs.append(xh * (lax.rsqrt(jnp.sum(xh * xh, axis=-1, keepdims=True) + NORM_EPS) * scale))
    return jnp.concatenate(outs, axis=1)


def _gdn_kernel(*refs, rows, seq, carry, live_from):
    c = GDN_CHUNK
    nc = rows // c
    nseq = c // seq
    n_live = seq - live_from
    lw = GDN_HB * HEAD_DIM
    n_scratch = 13
    if carry:
        (xq_ref, xk_ref, xv_ref, tq_ref, tk_ref, tv_ref, z_ref, wq_ref, wk_ref, wv_ref, gn_ref,
         ab_ref, pc_ref, og_ref, s_ref) = refs[:-n_scratch]
        s0_ref = None
    else:
        (xq_ref, xk_ref, xv_ref, cq_ref, ck_ref, cv_ref, z_in, wq_ref, wk_ref, wv_ref, gn_ref,
         ab_in, pc_ref, s0_ref, og_ref, s_ref, z_ref, ab_ref) = refs[:-n_scratch]
    (padq_s, padk_s, padv_s, kn_s, kb_s, qn_s, vb_s, kbe_s, qe_s, kd_s, egc_s, egl_s, gcr_s) = refs[-n_scratch:]

    group = pl.program_id(0)
    step = pl.program_id(1)
    row = lax.broadcasted_iota(jnp.int32, (rows, 1), 0)
    live = (row % seq) >= live_from
    keep_live = (lambda x: jnp.where(live, x, 0.0)) if live_from else (lambda x: x)

    if carry:
        for pad_ref, x_ref, t_ref in ((padq_s, xq_ref, tq_ref), (padk_s, xk_ref, tk_ref), (padv_s, xv_ref, tv_ref)):
            pad_ref[0:SUBLANES, :] = jnp.where(step == 0, 0.0, t_ref[...])
            pad_ref[SUBLANES:, :] = x_ref[...]
    else:
        def gather(dst_ref, x_ref, hist_ref, top):
            dst_ref[...] = jnp.zeros_like(dst_ref)
            for b in range(rows // seq):
                r0 = top + b * seq + live_from
                if hist_ref is not None:
                    dst_ref[r0 - (GDN_CONV - 1):r0, :] = hist_ref[b]
                dst_ref[r0:r0 + n_live, :] = x_ref[b * n_live:(b + 1) * n_live, :]

        gather(padq_s, xq_ref, cq_ref, SUBLANES)
        gather(padk_s, xk_ref, ck_ref, SUBLANES)
        gather(padv_s, xv_ref, cv_ref, SUBLANES)
        gather(z_ref, z_in, None, 0)
        gather(ab_ref, ab_in, None, 0)

    conv = lambda pad_ref, w_ref: keep_live(_conv_silu(pad_ref, w_ref[...], rows))
    qn = _l2norm_heads(conv(padq_s, wq_ref), HEAD_DIM ** -0.5)
    kn = _l2norm_heads(conv(padk_s, wk_ref), 1.0)
    v = conv(padv_s, wv_ref)

    ab = ab_ref[...]
    g_col = keep_live(-jnp.exp(pc_ref[0:1, :]) * _softplus(ab + pc_ref[1:2, :]))
    beta_col = keep_live(jax.nn.sigmoid(ab))
    ri = lax.broadcasted_iota(jnp.int32, (rows, rows), 0)
    ci = lax.broadcasted_iota(jnp.int32, (rows, rows), 1)
    same = (ri // seq) == (ci // seq)
    cum_sel = jnp.where(jnp.logical_and(same, ci <= ri), 1.0, 0.0).astype(BF16)
    tot_sel = jnp.where(same, 1.0, 0.0).astype(BF16)
    gc_col = _select_dot(cum_sel, g_col)
    gl_col = _select_dot(tot_sel, g_col)
    first = group * GDN_HB
    li = lax.broadcasted_iota(jnp.int32, (LANES, lw), 0)
    ni = lax.broadcasted_iota(jnp.int32, (LANES, lw), 1)
    spread_g = jnp.where(li == ni // HEAD_DIM + first, 1.0, 0.0).astype(BF16)
    spread_b = jnp.where(li == ni // HEAD_DIM + first + GDN_HEADS, 1.0, 0.0).astype(BF16)
    egc = _dot_select(gc_col, spread_g)
    egl = _dot_select(gl_col, spread_g)
    eb = _dot_select(beta_col, spread_b)
    decay_in = jnp.exp(egc)
    kb = kn * eb
    kn_s[...] = kn
    kb_s[...] = kb
    qn_s[...] = qn
    vb_s[...] = v * eb
    kbe_s[...] = kb * decay_in
    qe_s[...] = qn * decay_in
    kd_s[...] = kn * jnp.exp(egl - egc)
    egc_s[...] = egc
    egl_s[...] = egl
    pi = lax.broadcasted_iota(jnp.int32, (LANES, LANES), 0)
    pj = lax.broadcasted_iota(jnp.int32, (LANES, LANES), 1)
    pick = jnp.where(jnp.logical_and(pi == pj + first, pj < GDN_HB), 1.0, 0.0).astype(BF16)
    gc_heads = _dot_select(gc_col, pick)
    if rows % LANES:
        gc_heads = jnp.concatenate([gc_heads, jnp.zeros((LANES - rows % LANES, LANES), F32)], axis=0)
    gcr_s[...] = gc_heads.T[:SUBLANES, :rows]

    cr = lax.broadcasted_iota(jnp.int32, (c, c), 0)
    cc = lax.broadcasted_iota(jnp.int32, (c, c), 1)
    same_c = (cr // seq) == (cc // seq)
    causal_bias = jnp.where(jnp.logical_and(same_c, cr >= cc), 0.0, NEG_INF)
    strict = jnp.where(jnp.logical_and(same_c, cr > cc), 1.0, 0.0)

    if carry:
        @pl.when(step == 0)
        def _():
            s_ref[...] = jnp.zeros_like(s_ref)

    eye = jnp.where(cr == cc, 1.0, 0.0)
    seq_of_row = lax.broadcasted_iota(jnp.int32, (c, HEAD_DIM), 0) // seq

    def lower_left(s):
        return jnp.logical_and(jnp.logical_and(cr // (2 * s) == cc // (2 * s), (cr // s) % 2 == 1),
                               (cc // s) % 2 == 0)

    merge_sizes = [2 ** e for e in range(1, max(1, math.ceil(math.log2(n_live))))]
    pair_mask = jnp.where(lower_left(1), 1.0, 0.0)
    merge_masks = [jnp.where(lower_left(s), 1.0, 0.0) for s in merge_sizes]

    pairs = [(ch, hh) for ch in range(nc) for hh in range(GDN_HB)]
    rows_of = lambda ch: slice(ch * c, (ch + 1) * c)
    lanes_of = lambda hh: slice(hh * HEAD_DIM, (hh + 1) * HEAD_DIM)
    uw_all, qk_all, a_mat, t_mat, t_a = {}, {}, {}, {}, {}
    for ch, hh in pairs:
        rs, hs = rows_of(ch), lanes_of(hh)
        gcol = egc_s[rs, hh * HEAD_DIM:hh * HEAD_DIM + c]
        grow = gcr_s[hh:hh + 1, rs]
        decay = jnp.exp(gcol - grow + causal_bias)
        kq = jnp.concatenate([kb_s[rs, hs], qn_s[rs, hs]], axis=0).astype(BF16)
        kk = _dot_nt(kq, kn_s[rs, hs].astype(BF16)) * jnp.concatenate([decay, decay], axis=0)
        a_mat[ch, hh] = kk[:c] * strict
        qk_all[ch, hh] = kk[c:].astype(BF16)
        t_mat[ch, hh] = eye - a_mat[ch, hh] * pair_mask
    for mask in merge_masks:
        for p in pairs:
            t_a[p] = _dot(t_mat[p].astype(BF16), (a_mat[p] * mask).astype(BF16))
        for p in pairs:
            t_mat[p] = t_mat[p] - _dot(t_a[p].astype(BF16), t_mat[p].astype(BF16))
    for ch, hh in pairs:
        rs, hs = rows_of(ch), lanes_of(hh)
        uw_all[ch, hh] = _dot(t_mat[ch, hh].astype(BF16),
                              jnp.concatenate([vb_s[rs, hs], kbe_s[rs, hs]], axis=1).astype(BF16))

    if carry:
        state = [s_ref[hh] for hh in range(GDN_HB)]
    new_states = {}
    heads = range(GDN_HB)
    tn_dot = lambda a, b: lax.dot_general(a, b, (((0,), (0,)), ((), ())), preferred_element_type=F32)
    for ch in range(nc):
        rs = rows_of(ch)
        w_s, q_s, v_new_b, o = {}, {}, {}, {}
        for hh in heads:
            w = uw_all[ch, hh][:, HEAD_DIM:]
            q_e = qe_s[rs, lanes_of(hh)]
            if carry:
                wq = _dot(jnp.concatenate([w, q_e], axis=0).astype(BF16), state[hh].astype(BF16))
                w_s[hh], q_s[hh] = wq[:c], wq[c:]
            else:
                parts = []
                for b in range(nseq):
                    bs = slice(b * seq, (b + 1) * seq)
                    parts.append(_dot(jnp.concatenate([w[bs], q_e[bs]], axis=0).astype(BF16),
                                      s0_ref[ch * nseq + b, hh].astype(BF16)))
                w_s[hh] = jnp.concatenate([p[:seq] for p in parts], axis=0)
                q_s[hh] = jnp.concatenate([p[seq:] for p in parts], axis=0)
        for hh in heads:
            v_new_b[hh] = (uw_all[ch, hh][:, :HEAD_DIM] - w_s[hh]).astype(BF16)
            o[hh] = q_s[hh] + _dot(qk_all[ch, hh], v_new_b[hh])
        for hh in heads:
            hs = lanes_of(hh)
            k_d = kd_s[rs, hs]
            if carry:
                state[hh] = (state[hh] * jnp.exp(egl_s[ch * c:ch * c + 1, hs])
                             + tn_dot(k_d.astype(BF16), v_new_b[hh]))
            else:
                for b in range(nseq):
                    k_db = jnp.where(seq_of_row == b, k_d, 0.0).astype(BF16)
                    r1 = ch * c + b * seq
                    new_states[ch * nseq + b, hh] = (
                        s0_ref[ch * nseq + b, hh] * jnp.exp(egl_s[r1:r1 + 1, hs]) + tn_dot(k_db, v_new_b[hh]))
        outs = []
        for hh in heads:
            z = z_ref[rs, lanes_of(hh)]
            outs.append(_rms(o[hh], gn_ref[...]) * (z * jax.nn.sigmoid(z)))
        og = jnp.concatenate(outs, axis=1)
        if carry:
            og_ref[rs, :] = og.astype(og_ref.dtype)
        else:
            for b in range(nseq):
                r0 = b * seq + live_from
                og_ref[(ch * nseq + b) * n_live:(ch * nseq + b + 1) * n_live, :] = (
                    og[r0:r0 + n_live].astype(og_ref.dtype))
    if carry:
        s_ref[...] = jnp.stack(state, axis=0)
    else:
        for (b, hh), s_new in new_states.items():
            s_ref[b, hh] = s_new


def _gdn_params(a_log, dt_bias):
    rows = jnp.stack([a_log.astype(F32), dt_bias.astype(F32)], axis=0)
    return jnp.pad(rows, ((0, SUBLANES - 2), (0, LANES - GDN_HEADS)))


def _gdn_scratch(rows):
    lw = GDN_HB * HEAD_DIM
    return ([pltpu.VMEM((SUBLANES + rows, lw), F32) for _ in range(3)]
            + [pltpu.VMEM((rows, lw), F32) for _ in range(9)] + [pltpu.VMEM((SUBLANES, rows), F32)])


def _gdn_prompt(proj, ab, conv_w, gnorm_w, params, rows):
    t = proj.shape[0]
    lw = GDN_HB * HEAD_DIM
    x_spec = lambda off: pl.BlockSpec((rows, lw), lambda g, i: (i, off // lw + g))
    tail_spec = lambda off: pl.BlockSpec(
        (SUBLANES, lw), lambda g, i: (jnp.maximum(i * (rows // SUBLANES) - 1, 0), off // lw + g))
    w_spec = lambda off: pl.BlockSpec((GDN_CONV, lw), lambda g, i: (0, (off - GQ_OFF) // lw + g))
    return pl.pallas_call(
        functools.partial(_gdn_kernel, rows=rows, seq=GDN_CHUNK, carry=True, live_from=0),
        grid=(GDN_NG, t // rows),
        in_specs=[x_spec(GQ_OFF), x_spec(GK_OFF), x_spec(GV_OFF),
                  tail_spec(GQ_OFF), tail_spec(GK_OFF), tail_spec(GV_OFF),
                  x_spec(Z_OFF), w_spec(GQ_OFF), w_spec(GK_OFF), w_spec(GV_OFF),
                  pl.BlockSpec((1, HEAD_DIM), lambda g, i: (0, 0)),
                  pl.BlockSpec((rows, LANES), lambda g, i: (i, 0)),
                  pl.BlockSpec((SUBLANES, LANES), lambda g, i: (0, 0))],
        out_specs=[pl.BlockSpec((rows, lw), lambda g, i: (i, g)),
                   pl.BlockSpec((GDN_HB, HEAD_DIM, HEAD_DIM), lambda g, i: (g, 0, 0))],
        out_shape=[jax.ShapeDtypeStruct((t, GDN_WIDTH), BF16),
                   jax.ShapeDtypeStruct((GDN_HEADS, HEAD_DIM, HEAD_DIM), F32)],
        scratch_shapes=_gdn_scratch(rows),
        compiler_params=_cparams(2),
        name="gdn_prompt",
    )(proj, proj, proj, proj, proj, proj, proj, conv_w, conv_w, conv_w,
      gnorm_w.reshape(1, HEAD_DIM).astype(F32), ab, params)


def _gdn_sample(proj, ab, conv_state, state, conv_w, gnorm_w, params, n_live):
    rows = GDN_CHUNK
    lw = GDN_HB * HEAD_DIM
    nseq = rows // SAMPLE_PAD
    n_in = nseq * n_live
    x_spec = lambda off: pl.BlockSpec((n_in, lw), lambda g, i: (i, off // lw + g))
    hist_spec = lambda off: pl.BlockSpec((nseq, GDN_CONV - 1, lw), lambda g, i: (i, 0, (off - GQ_OFF) // lw + g))
    w_spec = lambda off: pl.BlockSpec((GDN_CONV, lw), lambda g, i: (0, (off - GQ_OFF) // lw + g))
    s_spec = pl.BlockSpec((nseq, GDN_HB, HEAD_DIM, HEAD_DIM), lambda g, i: (i, g, 0, 0))
    return pl.pallas_call(
        functools.partial(_gdn_kernel, rows=rows, seq=SAMPLE_PAD, carry=False, live_from=SAMPLE_PAD - n_live),
        grid=(GDN_NG, proj.shape[0] // n_in),
        in_specs=[x_spec(GQ_OFF), x_spec(GK_OFF), x_spec(GV_OFF),
                  hist_spec(GQ_OFF), hist_spec(GK_OFF), hist_spec(GV_OFF),
                  x_spec(Z_OFF), w_spec(GQ_OFF), w_spec(GK_OFF), w_spec(GV_OFF),
                  pl.BlockSpec((1, HEAD_DIM), lambda g, i: (0, 0)),
                  pl.BlockSpec((n_in, LANES), lambda g, i: (i, 0)),
                  pl.BlockSpec((SUBLANES, LANES), lambda g, i: (0, 0)),
                  s_spec],
        out_specs=[pl.BlockSpec((n_in, lw), lambda g, i: (i, g)), s_spec],
        out_shape=[jax.ShapeDtypeStruct((proj.shape[0], GDN_WIDTH), F32),
                   jax.ShapeDtypeStruct(state.shape, F32)],
        scratch_shapes=[pltpu.VMEM((rows, lw), F32), pltpu.VMEM((rows, LANES), F32)] + _gdn_scratch(rows),
        compiler_params=_cparams(2),
        name="gdn_sample",
    )(proj, proj, proj, conv_state, conv_state, conv_state, proj, conv_w, conv_w, conv_w,
      gnorm_w.reshape(1, HEAD_DIM).astype(F32), ab, params, state)


def _layer(x_prompt, x_sample, win_k, win_v, conv_state, gdn_state, rel_bias,
           w_in, sinks, conv_w, a_log, dt_bias, gnorm_w, w_out,
           n_pre_mix, n_post_mix, n_pre_ffn, n_post_ffn, w_gate, w_up, w_down, tiles):
    _, t, d = x_prompt.shape
    nb, ts, _ = x_sample.shape
    ms = nb * ts
    l = win_k.shape[1]
    ff = w_gate.shape[1]
    n_mt = tiles["n_mt"]
    xp, xs = x_prompt.reshape(t, d), x_sample.reshape(ms, d)

    w_in_t = w_in.T
    w_tail = jnp.pad(w_in_t[PROJ_MAIN:], ((0, LANES - AB_WIDTH), (0, 0))).astype(BF16)
    hp, ab_p = _rmsnorm_proj_tail(xp, n_pre_mix, w_tail)
    hs, ab_s = _rmsnorm_proj_tail(xs, n_pre_mix, w_tail)
    proj_p, proj_s = _proj([(hp, hs)], [(w_in_t, 0, d)], PROJ_MAIN, tiles["in_bn"], F32, n_mt, "in_proj",
                           w_transposed=True)

    qi = jnp.arange(WINDOW)[:, None]
    kj = jnp.arange(2 * WINDOW)[None, :]
    dist_p = WINDOW + qi - kj
    dist_s = l + jnp.arange(SUBLANES)[:, None] - kj
    dist_s = jnp.where(kj < l + ts, dist_s, -1)
    bias_all = _bias_table(rel_bias, jnp.concatenate([dist_p, dist_s], axis=0))
    attn_p = _attn_prompt(proj_p, bias_all[:, :WINDOW], sinks)

    bias_s = bias_all[:, WINDOW:WINDOW + ts].reshape(ATTN_KV_HEADS, ATTN_GROUP * ts, 2 * WINDOW)
    sink_rows = jnp.repeat(sinks.astype(F32), ts).reshape(ATTN_KV_HEADS, ATTN_GROUP * ts, 1)
    attn_s, new_k, new_v = _attn_sample(proj_s, win_k, win_v, bias_s, sink_rows, ts, tiles["attn_gb"])

    params = _gdn_params(a_log, dt_bias)
    og_p, s_p = _gdn_prompt(proj_p, ab_p, conv_w, gnorm_w, params, tiles["gdn_rows"])
    og_s, s_s = _gdn_sample(proj_s, ab_s, conv_state.astype(F32), gdn_state, conv_w, gnorm_w, params, ts)

    mix_p, mix_s = _proj([(attn_p, attn_s), (og_p, og_s)], [(w_out, 0, ATTN_WIDTH), (w_out, 1, GDN_WIDTH)],
                         d, tiles["out_bn"], BF16, n_mt, "out_proj")
    x1_p, h2_p = _post_mix(xp, mix_p, n_post_mix, n_pre_ffn, tiles["post_rows"])
    x1_s, h2_s = _post_mix(xs, mix_s, n_post_mix, n_pre_ffn, tiles["sample_rows"])
    act_p, act_s = _proj([(h2_p, h2_s)], [(w_gate, 0, d), (w_up, 0, d)], ff, tiles["ff_bn"], BF16, n_mt,
                         "ffn_gate_up", swiglu=True)
    ffn_p, ffn_s = _down(act_p, act_s, w_down, tiles["ff_bk"], n_mt)
    y_p = _residual_norm(x1_p, ffn_p, n_post_ffn, tiles["post_rows"])
    y_s = _residual_norm(x1_s, ffn_s, n_post_ffn, tiles["sample_rows"])

    lw = min(WINDOW, t)
    prompt_k = proj_p[t - lw:, K_OFF:K_OFF + KV_WIDTH].reshape(1, lw, ATTN_KV_HEADS, HEAD_DIM)
    prompt_v = proj_p[t - lw:, V_OFF:V_OFF + KV_WIDTH].reshape(1, lw, ATTN_KV_HEADS, HEAD_DIM)
    prompt_conv = proj_p[t - (GDN_CONV - 1):, GQ_OFF:GQ_OFF + GDN_CONV_CH][None]
    sample_conv = proj_s[:, GQ_OFF:GQ_OFF + GDN_CONV_CH].reshape(nb, ts, GDN_CONV_CH)[:, ts - (GDN_CONV - 1):]
    return (y_p.reshape(1, t, d), y_s.reshape(nb, ts, d),
            prompt_k, prompt_v, prompt_conv, s_p[None],
            new_k.reshape(nb, l, ATTN_KV_HEADS, HEAD_DIM), new_v.reshape(nb, l, ATTN_KV_HEADS, HEAD_DIM),
            sample_conv, s_s)


def _tiles(t, ms):
    n_mt = 8 if (t % (8 * 16) == 0 and ms % (8 * 16) == 0) else 1
    return dict(n_mt=n_mt, in_bn=512, out_bn=512, ff_bn=256, ff_bk=512, norm_rows=512, post_rows=256,
                sample_rows=ms, gdn_rows=256, attn_gb=8)


def kernel(x_prompt, x_sample, cache_win_k, cache_win_v, state_conv, state_gdn, rel_bias, w_in, attn_sinks,
           gdn_conv_w, gdn_a_log, gdn_dt_bias, gdn_norm_w, w_out, norm_pre_mix, norm_post_mix, norm_pre_ffn,
           norm_post_ffn, w_gate, w_up, w_down):
    depth = w_in.shape[0]
    assert depth == 1 and x_prompt.shape[0] == 1
    tiles = _tiles(x_prompt.shape[1], x_sample.shape[0] * x_sample.shape[1])
    outs = _layer(x_prompt, x_sample, cache_win_k[0], cache_win_v[0], state_conv[0], state_gdn[0], rel_bias,
                  w_in[0], attn_sinks[0], gdn_conv_w[0], gdn_a_log[0], gdn_dt_bias[0], gdn_norm_w[0], w_out[0],
                  norm_pre_mix[0], norm_post_mix[0], norm_pre_ffn[0], norm_post_ffn[0],
                  w_gate[0], w_up[0], w_down[0], tiles)
    yp, ys, pk, pv, pc, ps, sk, sv, sc, ss = outs
    return (yp, ys, pk[None], pv[None], pc[None], ps[None], sk[None], sv[None], sc[None], ss[None])
```

```python
import functools
import math

import jax
import jax.numpy as jnp
from jax import lax
from jax.experimental import pallas as pl
from jax.experimental.pallas import tpu as pltpu

F32 = jnp.float32
BF16 = jnp.bfloat16

HEAD_DIM = 128
ATTN_HEADS = 16
ATTN_KV_HEADS = 4
ATTN_GROUP = ATTN_HEADS // ATTN_KV_HEADS
ATTN_WIDTH = ATTN_HEADS * HEAD_DIM
KV_WIDTH = ATTN_KV_HEADS * HEAD_DIM
WINDOW = 128
REL_BUCKETS = 32
REL_MAX_DIST = 128
GDN_HEADS = 16
GDN_WIDTH = GDN_HEADS * HEAD_DIM
GDN_CONV = 4
GDN_CONV_CH = 3 * GDN_WIDTH
GDN_CHUNK = 64
NORM_EPS = 1e-6
NEG_INF = -1e30

Q_OFF = 0
K_OFF = ATTN_WIDTH
V_OFF = K_OFF + KV_WIDTH
GQ_OFF = V_OFF + KV_WIDTH
GK_OFF = GQ_OFF + GDN_WIDTH
GV_OFF = GK_OFF + GDN_WIDTH
Z_OFF = GV_OFF + GDN_WIDTH
PROJ_MAIN = Z_OFF + GDN_WIDTH
AB_WIDTH = 2 * GDN_HEADS

LANES = 128
SUBLANES = 8
GDN_HB = 4
GDN_NG = GDN_HEADS // GDN_HB
SAMPLE_PAD = 8
VMEM_LIMIT = 56 * 1024 * 1024


def _cparams(n_axes, vmem=VMEM_LIMIT):
    return pltpu.CompilerParams(dimension_semantics=("arbitrary",) * n_axes, vmem_limit_bytes=vmem)


def _dot(a, b):
    return jnp.dot(a, b, preferred_element_type=F32)


def _rms(x, w):
    return x * lax.rsqrt(jnp.mean(x * x, axis=-1, keepdims=True) + NORM_EPS) * w


def _rmsnorm_kernel(x_ref, w_ref, wt_ref, o_ref, tail_ref):
    h = _rms(x_ref[...].astype(F32), w_ref[...]).astype(o_ref.dtype)
    o_ref[...] = h
    tail_ref[...] = _dot_nt(h, wt_ref[...])


def _rmsnorm_proj_tail(x, w, w_tail):
    m, d = x.shape
    rows = min(m, 512)
    n = w_tail.shape[0]
    return pl.pallas_call(
        _rmsnorm_kernel,
        grid=(m // rows,),
        in_specs=[pl.BlockSpec((rows, d), lambda i: (i, 0)), pl.BlockSpec((1, d), lambda i: (0, 0)),
                  pl.BlockSpec((n, d), lambda i: (0, 0))],
        out_specs=[pl.BlockSpec((rows, d), lambda i: (i, 0)), pl.BlockSpec((rows, n), lambda i: (i, 0))],
        out_shape=[jax.ShapeDtypeStruct((m, d), BF16), jax.ShapeDtypeStruct((m, n), F32)],
        compiler_params=_cparams(1),
        name="rmsnorm",
    )(x, w.reshape(1, d).astype(F32), w_tail)


def _post_mix_kernel(x_ref, mix_ref, wpost_ref, wpre_ref, x1_ref, h2_ref):
    x1 = x_ref[...] + _rms(mix_ref[...].astype(F32), wpost_ref[...])
    x1_ref[...] = x1
    h2_ref[...] = _rms(x1, wpre_ref[...]).astype(h2_ref.dtype)


def _post_mix(x, mix, w_post, w_pre, rows):
    m, d = x.shape
    row_spec = pl.BlockSpec((rows, d), lambda i: (i, 0))
    w_spec = pl.BlockSpec((1, d), lambda i: (0, 0))
    return pl.pallas_call(
        _post_mix_kernel,
        grid=(m // rows,),
        in_specs=[row_spec, row_spec, w_spec, w_spec],
        out_specs=[row_spec, row_spec],
        out_shape=[jax.ShapeDtypeStruct((m, d), F32), jax.ShapeDtypeStruct((m, d), BF16)],
        compiler_params=_cparams(1),
        name="post_mix",
    )(x, mix, w_post.reshape(1, d).astype(F32), w_pre.reshape(1, d).astype(F32))


def _residual_norm_kernel(x_ref, y_ref, w_ref, o_ref):
    o_ref[...] = x_ref[...] + _rms(y_ref[...].astype(F32), w_ref[...])


def _residual_norm(x, y, w, rows):
    m, d = x.shape
    row_spec = pl.BlockSpec((rows, d), lambda i: (i, 0))
    return pl.pallas_call(
        _residual_norm_kernel,
        grid=(m // rows,),
        in_specs=[row_spec, row_spec, pl.BlockSpec((1, d), lambda i: (0, 0))],
        out_specs=row_spec,
        out_shape=jax.ShapeDtypeStruct((m, d), F32),
        compiler_params=_cparams(1),
        name="residual_norm",
    )(x, y, w.reshape(1, d).astype(F32))


def _stack_rows(p_ref, s_ref, cols=slice(None)):
    return jnp.concatenate([p_ref[:, cols].astype(BF16), s_ref[:, cols].astype(BF16)], axis=0)


def _dot_nt(a, b):
    return lax.dot_general(a, b, (((1,), (1,)), ((), ())), preferred_element_type=F32)


def _proj_kernel(*refs, n_a, n_w, swiglu, w_transposed, row_blocks, n_mt):
    a_refs = refs[:2 * n_a]
    w_refs = refs[2 * n_a:2 * n_a + n_w]
    op_ref, os_ref = refs[2 * n_a + n_w:2 * n_a + n_w + 2]
    scratch = refs[2 * n_a + n_w + 2:]
    stage_refs, wb_refs, sem = scratch[:n_w], scratch[n_w:2 * n_w], scratch[2 * n_w]
    mm = _dot_nt if w_transposed else _dot
    j, i = pl.program_id(0), pl.program_id(1)
    nj = pl.num_programs(0)

    def slice_copy(w, tile, s, slot):
        rows, cols = stage_refs[w].shape[1:]
        if w_transposed:
            src = w_refs[w].at[pl.ds(tile * (rows * n_mt) + s * rows, rows), pl.ds(row_blocks[w] * cols, cols)]
        else:
            src = w_refs[w].at[pl.ds((row_blocks[w] * n_mt + s) * rows, rows), pl.ds(tile * cols, cols)]
        return pltpu.make_async_copy(src, stage_refs[w].at[slot], sem.at[w, slot])

    def cast_slice(w, half, s, slot):
        rows = stage_refs[w].shape[1]
        wb_refs[w][half, pl.ds(pl.multiple_of(s * rows, rows), rows), :] = stage_refs[w][slot].astype(BF16)

    @pl.when(jnp.logical_and(j == 0, i == 0))
    def _():
        for w in range(n_w):
            for s in range(n_mt):
                copy = slice_copy(w, 0, s, s % 2)
                copy.start()
                copy.wait()
                cast_slice(w, 0, s, s % 2)

        @pl.when(nj > 1)
        def _():
            for w in range(n_w):
                slice_copy(w, 1, 0, 0).start()

    @pl.when(j + 1 < nj)
    def _():
        slot = (j * n_mt + i) % 2
        for w in range(n_w):
            slice_copy(w, j + 1, i, slot).wait()
            cast_slice(w, (j + 1) % 2, i, slot)
        last = i + 1 == n_mt
        next_tile = jnp.where(last, j + 2, j + 1)
        next_s = jnp.where(last, 0, i + 1)

        @pl.when(next_tile < nj)
        def _():
            for w in range(n_w):
                slice_copy(w, next_tile, next_s, 1 - slot).start()

    half = j % 2
    lhs = [_stack_rows(a_refs[2 * p], a_refs[2 * p + 1]) for p in range(n_a)]
    if swiglu:
        g = mm(lhs[0], wb_refs[0][half])
        o = g * jax.nn.sigmoid(g) * mm(lhs[0], wb_refs[1][half])
    else:
        o = mm(lhs[0], wb_refs[0][half])
        for l, wb_ref in zip(lhs[1:], wb_refs[1:]):
            o = o + mm(l, wb_ref[half])
    bmp = op_ref.shape[0]
    op_ref[...] = o[:bmp].astype(op_ref.dtype)
    os_ref[...] = o[bmp:].astype(os_ref.dtype)


def _proj(a_pairs, w_blocks, n, bn, out_dtype, n_mt, name, swiglu=False, w_transposed=False):
    tp, ts = a_pairs[0][0].shape[0], a_pairs[0][1].shape[0]
    bmp, bms = tp // n_mt, ts // n_mt
    in_specs = []
    for a_p, a_s in a_pairs:
        in_specs.append(pl.BlockSpec((bmp, a_p.shape[1]), lambda j, i: (i, 0)))
        in_specs.append(pl.BlockSpec((bms, a_s.shape[1]), lambda j, i: (i, 0)))
    in_specs += [pl.BlockSpec(memory_space=pl.ANY) for _ in w_blocks]
    tile = lambda k: (bn, k) if w_transposed else (k, bn)
    stage = lambda k: (2, bn // n_mt, k) if w_transposed else (2, k // n_mt, bn)
    operands = [a for pair in a_pairs for a in pair] + [w for w, _, _ in w_blocks]
    return pl.pallas_call(
        functools.partial(_proj_kernel, n_a=len(a_pairs), n_w=len(w_blocks), swiglu=swiglu,
                          w_transposed=w_transposed, row_blocks=tuple(rb for _, rb, _ in w_blocks), n_mt=n_mt),
        grid=(n // bn, n_mt),
        in_specs=in_specs,
        out_specs=[pl.BlockSpec((bmp, bn), lambda j, i: (i, j)), pl.BlockSpec((bms, bn), lambda j, i: (i, j))],
        out_shape=[jax.ShapeDtypeStruct((tp, n), out_dtype), jax.ShapeDtypeStruct((ts, n), out_dtype)],
        scratch_shapes=([pltpu.VMEM(stage(k), F32) for _, _, k in w_blocks]
                        + [pltpu.VMEM((2,) + tile(k), BF16) for _, _, k in w_blocks]
                        + [pltpu.SemaphoreType.DMA((len(w_blocks), 2))]),
        compiler_params=_cparams(2),
        name=name,
    )(*operands)


def _down_kernel(ap_ref, as_ref, b_ref, op_ref, os_ref, accp_ref, accs_ref, *, bn, k_total):
    k = pl.program_id(1)
    nk = pl.num_programs(1)
    bk = b_ref.shape[0]
    bmp = ap_ref.shape[0]
    last_rows = k_total - (k_total // bk) * bk

    @pl.when(k == 0)
    def _():
        accp_ref[...] = jnp.zeros_like(accp_ref)
        accs_ref[...] = jnp.zeros_like(accs_ref)

    def accumulate(kk):
        a = _stack_rows(ap_ref, as_ref, slice(0, kk))
        for n0 in range(0, accp_ref.shape[1], bn):
            d = _dot(a, b_ref[:kk, n0:n0 + bn].astype(BF16))
            accp_ref[:, n0:n0 + bn] += d[:bmp]
            accs_ref[:, n0:n0 + bn] += d[bmp:]

    if last_rows == 0:
        accumulate(bk)
    else:
        pl.when(k < nk - 1)(lambda: accumulate(bk))
        pl.when(k == nk - 1)(lambda: accumulate(last_rows))

    @pl.when(k == nk - 1)
    def _():
        op_ref[...] = accp_ref[...].astype(op_ref.dtype)
        os_ref[...] = accs_ref[...].astype(os_ref.dtype)


def _down(a_p, a_s, b, bk, n_mt):
    (tp, k_total), ts = a_p.shape, a_s.shape[0]
    n = b.shape[1]
    bmp, bms = tp // n_mt, ts // n_mt
    return pl.pallas_call(
        functools.partial(_down_kernel, bn=min(n, 512), k_total=k_total),
        grid=(n_mt, pl.cdiv(k_total, bk)),
        in_specs=[pl.BlockSpec((bmp, bk), lambda i, k: (i, k)), pl.BlockSpec((bms, bk), lambda i, k: (i, k)),
                  pl.BlockSpec((bk, n), lambda i, k: (k, 0))],
        out_specs=[pl.BlockSpec((bmp, n), lambda i, k: (i, 0), pipeline_mode=pl.Buffered(1)),
                   pl.BlockSpec((bms, n), lambda i, k: (i, 0))],
        out_shape=[jax.ShapeDtypeStruct((tp, n), BF16), jax.ShapeDtypeStruct((ts, n), BF16)],
        scratch_shapes=[pltpu.VMEM((bmp, n), F32), pltpu.VMEM((bms, n), F32)],
        compiler_params=_cparams(2),
        name="ffn_down",
    )(a_p, a_s, b)


def _rel_bucket(d):
    n = jnp.maximum(d, 0)
    max_exact = REL_BUCKETS // 2
    nf = jnp.maximum(n, 1).astype(F32)
    large = max_exact + (jnp.log(nf / max_exact) / math.log(REL_MAX_DIST / max_exact)
                         * (REL_BUCKETS - max_exact)).astype(jnp.int32)
    large = jnp.minimum(large, REL_BUCKETS - 1)
    return jnp.where(n < max_exact, n, large)


def _bias_kernel(rel_ref, bucket_ref, allowed_ref, o_ref):
    h = pl.program_id(0)
    bucket = bucket_ref[...]
    acc = jnp.zeros(bucket.shape, F32)
    for n in range(REL_BUCKETS):
        acc = jnp.where(bucket == n, rel_ref[n, h], acc)
    o_ref[0] = jnp.where(allowed_ref[...] > 0, acc, NEG_INF)


def _bias_table(rel_bias, dist):
    tq, tk = dist.shape
    allowed = ((dist >= 0) & (dist <= WINDOW)).astype(jnp.int32)
    full = pl.BlockSpec((tq, tk), lambda h: (0, 0))
    return pl.pallas_call(
        _bias_kernel,
        grid=(ATTN_HEADS,),
        in_specs=[pl.BlockSpec(memory_space=pltpu.SMEM), full, full],
        out_specs=pl.BlockSpec((1, tq, tk), lambda h: (h, 0, 0)),
        out_shape=jax.ShapeDtypeStruct((ATTN_HEADS, tq, tk), F32),
        compiler_params=_cparams(1),
        name="rel_bias_table",
    )(rel_bias.astype(F32), _rel_bucket(dist), allowed)


def _softmax_pv(s, sink):
    m = jnp.maximum(jnp.max(s, axis=-1, keepdims=True), sink)
    p = jnp.exp(s - m)
    denom = jnp.sum(p, axis=-1, keepdims=True) + jnp.exp(sink - m)
    return p.astype(BF16), denom


def _attn_prompt_kernel(sink_ref, cur_ref, prev_ref, bias_ref, o_ref):
    i = pl.program_id(0)
    c = WINDOW
    col = lax.broadcasted_iota(jnp.int32, (c, 2 * c), 1)
    no_prev = jnp.logical_and(i == 0, col < c)
    head_cols = lambda base, h: slice(base + h * HEAD_DIM, base + (h + 1) * HEAD_DIM)
    kk, vv = [], []
    for h in range(ATTN_KV_HEADS):
        kk.append(jnp.concatenate([prev_ref[:, head_cols(0, h)], cur_ref[:, head_cols(K_OFF, h)]],
                                  axis=0).astype(BF16))
        vv.append(jnp.concatenate([prev_ref[:, head_cols(KV_WIDTH, h)], cur_ref[:, head_cols(V_OFF, h)]],
                                  axis=0).astype(BF16))
    heads = range(ATTN_HEADS)
    s = [_dot_nt(cur_ref[:, head_cols(Q_OFF, hd)].astype(BF16), kk[hd // ATTN_GROUP]) for hd in heads]
    pd = []
    for hd in heads:
        sc = jnp.where(no_prev, NEG_INF, s[hd] * (HEAD_DIM ** -0.5) + bias_ref[hd])
        pd.append(_softmax_pv(sc, sink_ref[hd]))
    outs = [_dot(pd[hd][0], vv[hd // ATTN_GROUP]) / pd[hd][1] for hd in heads]
    o_ref[...] = jnp.concatenate(outs, axis=1).astype(o_ref.dtype)


def _attn_prompt(proj, bias, sinks):
    t = proj.shape[0]
    c = WINDOW
    qkv = V_OFF + KV_WIDTH
    return pl.pallas_call(
        _attn_prompt_kernel,
        grid=(t // c,),
        in_specs=[pl.BlockSpec(memory_space=pltpu.SMEM),
                  pl.BlockSpec((c, qkv), lambda i: (i, 0)),
                  pl.BlockSpec((c, 2 * KV_WIDTH), lambda i: (jnp.maximum(i - 1, 0), K_OFF // (2 * KV_WIDTH))),
                  pl.BlockSpec((ATTN_HEADS, c, 2 * c), lambda i: (0, 0, 0))],
        out_specs=pl.BlockSpec((c, ATTN_WIDTH), lambda i: (i, 0)),
        out_shape=jax.ShapeDtypeStruct((t, ATTN_WIDTH), BF16),
        compiler_params=_cparams(1),
        name="attn_prompt",
    )(sinks.astype(F32), proj, proj, bias)


def _attn_sample_kernel(x_ref, ck_ref, cv_ref, bias_ref, sink_ref, o_ref, ok_ref, ov_ref,
                        q_s, kf_ref, vf_ref, *, t_new):
    l = WINDOW
    gb = ck_ref.shape[0]

    @pl.when(pl.program_id(0) == 0)
    def _():
        kf_ref[:, l + t_new:, :] = jnp.zeros_like(kf_ref[:, l + t_new:, :])
        vf_ref[:, l + t_new:, :] = jnp.zeros_like(vf_ref[:, l + t_new:, :])

    for h in range(ATTN_KV_HEADS):
        kf_ref[:, 0:l, :] = ck_ref[:, :, h, :]
        vf_ref[:, 0:l, :] = cv_ref[:, :, h, :]
        for b in range(gb):
            rows = slice(b * t_new, (b + 1) * t_new)
            kf_ref[b, l:l + t_new, :] = x_ref[rows, K_OFF + h * HEAD_DIM:K_OFF + (h + 1) * HEAD_DIM]
            vf_ref[b, l:l + t_new, :] = x_ref[rows, V_OFF + h * HEAD_DIM:V_OFF + (h + 1) * HEAD_DIM]
            for g in range(ATTN_GROUP):
                col = (h * ATTN_GROUP + g) * HEAD_DIM
                q_s[b, g * t_new:(g + 1) * t_new, :] = x_ref[rows, col:col + HEAD_DIM]
        ok_ref[:, :, h, :] = kf_ref[:, t_new:l + t_new, :]
        ov_ref[:, :, h, :] = vf_ref[:, t_new:l + t_new, :]

        s = jnp.einsum("bqd,bkd->bqk", q_s[...].astype(BF16), kf_ref[...].astype(BF16),
                       preferred_element_type=F32)
        s = s * (HEAD_DIM ** -0.5) + bias_ref[h]
        p, denom = _softmax_pv(s, sink_ref[h])
        o = jnp.einsum("bqk,bkd->bqd", p, vf_ref[...].astype(BF16), preferred_element_type=F32) / denom
        for b in range(gb):
            for g in range(ATTN_GROUP):
                col = (h * ATTN_GROUP + g) * HEAD_DIM
                o_ref[b * t_new:(b + 1) * t_new, col:col + HEAD_DIM] = o[b, g * t_new:(g + 1) * t_new, :]


def _attn_sample(proj, cache_k, cache_v, bias, sink_rows, t_new, gb):
    b, l = cache_k.shape[:2]
    rows = ATTN_GROUP * t_new
    qkv = V_OFF + KV_WIDTH
    cache_spec = pl.BlockSpec((gb, l, ATTN_KV_HEADS, HEAD_DIM), lambda i: (i, 0, 0, 0))
    return pl.pallas_call(
        functools.partial(_attn_sample_kernel, t_new=t_new),
        grid=(b // gb,),
        in_specs=[pl.BlockSpec((gb * t_new, qkv), lambda i: (i, 0)), cache_spec, cache_spec,
                  pl.BlockSpec((ATTN_KV_HEADS, rows, 2 * l), lambda i: (0, 0, 0)),
                  pl.BlockSpec((ATTN_KV_HEADS, rows, 1), lambda i: (0, 0, 0))],
        out_specs=[pl.BlockSpec((gb * t_new, ATTN_WIDTH), lambda i: (i, 0)), cache_spec, cache_spec],
        out_shape=[jax.ShapeDtypeStruct((b * t_new, ATTN_WIDTH), F32),
                   jax.ShapeDtypeStruct(cache_k.shape, F32),
                   jax.ShapeDtypeStruct(cache_v.shape, F32)],
        scratch_shapes=[pltpu.VMEM((gb, rows, HEAD_DIM), F32),
                        pltpu.VMEM((gb, 2 * l, HEAD_DIM), F32), pltpu.VMEM((gb, 2 * l, HEAD_DIM), F32)],
        compiler_params=_cparams(1),
        name="attn_sample",
    )(proj, cache_k, cache_v, bias, sink_rows)


def _split3(x):
    hi = x.astype(BF16)
    r = x - hi.astype(F32)
    mid = r.astype(BF16)
    lo = (r - mid.astype(F32)).astype(BF16)
    return hi, mid, lo


def _select_dot(sel, x):
    hi, mid, lo = _split3(x)
    return _dot(sel, hi) + _dot(sel, mid) + _dot(sel, lo)


def _dot_select(x, sel):
    hi, mid, lo = _split3(x)
    return _dot(hi, sel) + _dot(mid, sel) + _dot(lo, sel)


def _softplus(x):
    return jnp.maximum(x, 0.0) + jnp.log1p(jnp.exp(-jnp.abs(x)))


def _conv_silu(pad_ref, w, r0, n):
    top = SUBLANES + r0
    y = pad_ref[top:top + n, :] * w[GDN_CONV - 1:GDN_CONV]
    for s in range(1, GDN_CONV):
        y = y + pad_ref[top - s:top - s + n, :] * w[GDN_CONV - 1 - s:GDN_CONV - s]
    return y * jax.nn.sigmoid(y)


def _l2norm_heads(x, scale):
    outs = []
    for hh in range(GDN_HB):
        xh = x[:, hh * HEAD_DIM:(hh + 1) * HEAD_DIM]
        outs.append(xh * (lax.rsqrt(jnp.sum(xh * xh, axis=-1, keepdims=True) + NORM_EPS) * scale))
    return jnp.concatenate(outs, axis=1)


GDN_N_PADS = 4
STAGE_ROWS = 32
PROBE_BOUND = 1e30
GDN_N_STAGED = 10


def _gdn_kernel(*refs, rows, seq, carry, live_from):
    kw = dict(rows=rows, seq=seq, carry=carry, live_from=live_from)
    if not carry:
        io, pads, staged = refs[:-GDN_N_PADS - GDN_N_STAGED], refs[-GDN_N_PADS - GDN_N_STAGED:-GDN_N_STAGED], \
            refs[-GDN_N_STAGED:]
        _gdn_body(io, pads, staged, staged, **kw)
        return
    n_scratch = GDN_N_PADS + 2 * GDN_N_STAGED
    io, pads = refs[:-n_scratch], refs[-n_scratch:-2 * GDN_N_STAGED]
    set_a, set_b = refs[-2 * GDN_N_STAGED:-GDN_N_STAGED], refs[-GDN_N_STAGED:]
    s_ref = io[-1]
    step = pl.program_id(1)

    @pl.when(step == 0)
    def _():
        s_ref[...] = jnp.zeros_like(s_ref)
        for ref in set_b:
            ref[...] = jnp.zeros_like(ref)

    pl.when(step % 2 == 0)(lambda: _gdn_body(io, pads, set_a, set_b, **kw))
    pl.when(step % 2 == 1)(lambda: _gdn_body(io, pads, set_b, set_a, **kw))


def _gdn_body(refs, pads, staged_w, staged_r, *, rows, seq, carry, live_from):
    c = GDN_CHUNK
    nc = rows // c
    nseq = c // seq
    n_live = seq - live_from
    lw = GDN_HB * HEAD_DIM
    if carry:
        (xq_ref, xk_ref, xv_ref, tq_ref, tk_ref, tv_ref, z_ref, wq_ref, wk_ref, wv_ref, gn_ref,
         ab_ref, pc_ref, og_ref, s_ref) = refs
        s0_ref = None
    else:
        (xq_ref, xk_ref, xv_ref, cq_ref, ck_ref, cv_ref, z_in, wq_ref, wk_ref, wv_ref, gn_ref,
         ab_in, pc_ref, s0_ref, og_ref, s_ref, z_ref, ab_ref) = refs
    padq_s, padk_s, padv_s, eb_s = pads
    kn_w, kb_w, qn_w, vb_w, kbe_w, qe_w, kd_w, egc_w, egl_w, gcr_w = staged_w
    kn_s, kb_s, qn_s, vb_s, kbe_s, qe_s, kd_s, egc_s, egl_s, gcr_s = staged_r

    group = pl.program_id(0)
    step = pl.program_id(1)
    row = lax.broadcasted_iota(jnp.int32, (rows, 1), 0)
    live = (row % seq) >= live_from
    keep_live = (lambda x: jnp.where(live, x, 0.0)) if live_from else (lambda x: x)

    if carry:
        for pad_ref, x_ref, t_ref in ((padq_s, xq_ref, tq_ref), (padk_s, xk_ref, tk_ref), (padv_s, xv_ref, tv_ref)):
            pad_ref[0:SUBLANES, :] = jnp.where(step == 0, 0.0, t_ref[...])
            pad_ref[SUBLANES:, :] = x_ref[...]
    else:
        def gather(dst_ref, x_ref, hist_ref, top):
            dst_ref[...] = jnp.zeros_like(dst_ref)
            for b in range(rows // seq):
                r0 = top + b * seq + live_from
                if hist_ref is not None:
                    dst_ref[r0 - (GDN_CONV - 1):r0, :] = hist_ref[b]
                dst_ref[r0:r0 + n_live, :] = x_ref[b * n_live:(b + 1) * n_live, :]

        gather(padq_s, xq_ref, cq_ref, SUBLANES)
        gather(padk_s, xk_ref, ck_ref, SUBLANES)
        gather(padv_s, xv_ref, cv_ref, SUBLANES)
        gather(z_ref, z_in, None, 0)
        gather(ab_ref, ab_in, None, 0)

    def stage_decays():
        ab = ab_ref[...]
        g_col = keep_live(-jnp.exp(pc_ref[0:1, :]) * _softplus(ab + pc_ref[1:2, :]))
        beta_col = keep_live(jax.nn.sigmoid(ab))
        ri = lax.broadcasted_iota(jnp.int32, (rows, rows), 0)
        ci = lax.broadcasted_iota(jnp.int32, (rows, rows), 1)
        same = (ri // seq) == (ci // seq)
        cum_sel = jnp.where(jnp.logical_and(same, ci <= ri), 1.0, 0.0).astype(BF16)
        tot_sel = jnp.where(same, 1.0, 0.0).astype(BF16)
        gc_col = _select_dot(cum_sel, g_col)
        gl_col = _select_dot(tot_sel, g_col)
        first = group * GDN_HB
        li = lax.broadcasted_iota(jnp.int32, (LANES, lw), 0)
        ni = lax.broadcasted_iota(jnp.int32, (LANES, lw), 1)
        spread_g = jnp.where(li == ni // HEAD_DIM + first, 1.0, 0.0).astype(BF16)
        spread_b = jnp.where(li == ni // HEAD_DIM + first + GDN_HEADS, 1.0, 0.0).astype(BF16)
        egc_w[...] = _dot_select(gc_col, spread_g)
        egl_w[...] = _dot_select(gl_col, spread_g)
        eb_s[...] = _dot_select(beta_col, spread_b)
        pi = lax.broadcasted_iota(jnp.int32, (LANES, LANES), 0)
        pj = lax.broadcasted_iota(jnp.int32, (LANES, LANES), 1)
        pick = jnp.where(jnp.logical_and(pi == pj + first, pj < GDN_HB), 1.0, 0.0).astype(BF16)
        gc_heads = _dot_select(gc_col, pick)
        if rows % LANES:
            gc_heads = jnp.concatenate([gc_heads, jnp.zeros((LANES - rows % LANES, LANES), F32)], axis=0)
        gcr_w[...] = gc_heads.T[:SUBLANES, :rows]

    def stage_rows(r0, n, after=None):
        sl = slice(r0, r0 + n)
        keep = (lambda x: jnp.where(live[sl], x, 0.0)) if live_from else (lambda x: x)
        gate = 1.0 if after is None else jnp.where(after > PROBE_BOUND, 0.0, 1.0)
        conv = lambda pad_ref, w_ref: keep(_conv_silu(pad_ref, w_ref[...] * gate, r0, n))
        qn = _l2norm_heads(conv(padq_s, wq_ref), HEAD_DIM ** -0.5)
        kn = _l2norm_heads(conv(padk_s, wk_ref), 1.0)
        v = conv(padv_s, wv_ref)
        egc, egl, eb = egc_w[sl, :], egl_w[sl, :], eb_s[sl, :]
        decay_in = jnp.exp(egc)
        kb = kn * eb
        kn_w[sl, :] = kn
        kb_w[sl, :] = kb
        qn_w[sl, :] = qn
        vb_w[sl, :] = v * eb
        kbe_w[sl, :] = kb * decay_in
        qe_w[sl, :] = qn * decay_in
        kd_w[sl, :] = kn * jnp.exp(egl - egc)

    n_slices = rows // STAGE_ROWS if carry else 1
    pending = [functools.partial(stage_rows, r * (rows // n_slices), rows // n_slices) for r in range(n_slices)]
    if not carry:
        stage_decays()
        pending.pop(0)()

    cr = lax.broadcasted_iota(jnp.int32, (c, c), 0)
    cc = lax.broadcasted_iota(jnp.int32, (c, c), 1)
    same_c = (cr // seq) == (cc // seq)
    causal_bias = jnp.where(jnp.logical_and(same_c, cr >= cc), 0.0, NEG_INF)
    strict = jnp.where(jnp.logical_and(same_c, cr > cc), 1.0, 0.0)

    eye = jnp.where(cr == cc, 1.0, 0.0)
    seq_of_row = lax.broadcasted_iota(jnp.int32, (c, HEAD_DIM), 0) // seq

    def lower_left(s):
        return jnp.logical_and(jnp.logical_and(cr // (2 * s) == cc // (2 * s), (cr // s) % 2 == 1),
                               (cc // s) % 2 == 0)

    merge_sizes = [2 ** e for e in range(1, max(1, math.ceil(math.log2(n_live))))]
    pair_mask = jnp.where(lower_left(1), 1.0, 0.0)
    merge_masks = [jnp.where(lower_left(s), 1.0, 0.0) for s in merge_sizes]

    pairs = [(ch, hh) for ch in range(nc) for hh in range(GDN_HB)]
    rows_of = lambda ch: slice(ch * c, (ch + 1) * c)
    lanes_of = lambda hh: slice(hh * HEAD_DIM, (hh + 1) * HEAD_DIM)
    uw_all, qk_all, a_mat, t_mat, t_a = {}, {}, {}, {}, {}
    for ch, hh in pairs:
        rs, hs = rows_of(ch), lanes_of(hh)
        gcol = egc_s[rs, hh * HEAD_DIM:hh * HEAD_DIM + c]
        grow = gcr_s[hh:hh + 1, rs]
        decay = jnp.exp(gcol - grow + causal_bias)
        kq = jnp.concatenate([kb_s[rs, hs], qn_s[rs, hs]], axis=0).astype(BF16)
        kk = _dot_nt(kq, kn_s[rs, hs].astype(BF16)) * jnp.concatenate([decay, decay], axis=0)
        a_mat[ch, hh] = kk[:c] * strict
        qk_all[ch, hh] = kk[c:].astype(BF16)
        t_mat[ch, hh] = eye - a_mat[ch, hh] * pair_mask
    probes = [a_mat[pairs[-1]][0:1, 0:1]]
    if carry:
        stage_decays()
    for mask in merge_masks:
        for p in pairs:
            t_a[p] = _dot(t_mat[p].astype(BF16), (a_mat[p] * mask).astype(BF16))
        for p in pairs:
            t_mat[p] = t_mat[p] - _dot(t_a[p].astype(BF16), t_mat[p].astype(BF16))
        probes.append(t_mat[pairs[-1]][0:1, 0:1])
    for ch, hh in pairs:
        rs, hs = rows_of(ch), lanes_of(hh)
        uw_all[ch, hh] = _dot(t_mat[ch, hh].astype(BF16),
                              jnp.concatenate([vb_s[rs, hs], kbe_s[rs, hs]], axis=1).astype(BF16))

    if carry:
        state = [s_ref[hh] for hh in range(GDN_HB)]
    new_states = {}
    heads = range(GDN_HB)
    tn_dot = lambda a, b: lax.dot_general(a, b, (((0,), (0,)), ((), ())), preferred_element_type=F32)
    for ch in range(nc):
        rs = rows_of(ch)
        w_s, q_s, v_new_b, o = {}, {}, {}, {}
        for hh in heads:
            w = uw_all[ch, hh][:, HEAD_DIM:]
            q_e = qe_s[rs, lanes_of(hh)]
            if carry:
                wq = _dot(jnp.concatenate([w, q_e], axis=0).astype(BF16), state[hh].astype(BF16))
                w_s[hh], q_s[hh] = wq[:c], wq[c:]
            else:
                parts = []
                for b in range(nseq):
                    bs = slice(b * seq, (b + 1) * seq)
                    parts.append(_dot(jnp.concatenate([w[bs], q_e[bs]], axis=0).astype(BF16),
                                      s0_ref[ch * nseq + b, hh].astype(BF16)))
                w_s[hh] = jnp.concatenate([p[:seq] for p in parts], axis=0)
                q_s[hh] = jnp.concatenate([p[seq:] for p in parts], axis=0)
        for hh in heads:
            v_new_b[hh] = (uw_all[ch, hh][:, :HEAD_DIM] - w_s[hh]).astype(BF16)
            o[hh] = q_s[hh] + _dot(qk_all[ch, hh], v_new_b[hh])
        for hh in heads:
            hs = lanes_of(hh)
            k_d = kd_s[rs, hs]
            if carry:
                state[hh] = (state[hh] * jnp.exp(egl_s[ch * c:ch * c + 1, hs])
                             + tn_dot(k_d.astype(BF16), v_new_b[hh]))
            else:
                for b in range(nseq):
                    k_db = jnp.where(seq_of_row == b, k_d, 0.0).astype(BF16)
                    r1 = ch * c + b * seq
                    new_states[ch * nseq + b, hh] = (
                        s0_ref[ch * nseq + b, hh] * jnp.exp(egl_s[r1:r1 + 1, hs]) + tn_dot(k_db, v_new_b[hh]))
        if carry:
            probes.append(_dot(qk_all[ch, 0], jnp.ones((c, HEAD_DIM), BF16))[0:1, 0:1])
        outs = []
        for hh in heads:
            z = z_ref[rs, lanes_of(hh)]
            outs.append(_rms(o[hh], gn_ref[...]) * (z * jax.nn.sigmoid(z)))
        og = jnp.concatenate(outs, axis=1)
        if carry:
            og_ref[rs, :] = og.astype(og_ref.dtype)
        else:
            for b in range(nseq):
                r0 = b * seq + live_from
                og_ref[(ch * nseq + b) * n_live:(ch * nseq + b + 1) * n_live, :] = (
                    og[r0:r0 + n_live].astype(og_ref.dtype))
    for n, stage in enumerate(pending):
        stage(after=probes[min(n, len(probes) - 1)])
    if carry:
        s_ref[...] = jnp.stack(state, axis=0)
    else:
        for (b, hh), s_new in new_states.items():
            s_ref[b, hh] = s_new


def _gdn_params(a_log, dt_bias):
    rows = jnp.stack([a_log.astype(F32), dt_bias.astype(F32)], axis=0)
    return jnp.pad(rows, ((0, SUBLANES - 2), (0, LANES - GDN_HEADS)))


def _gdn_scratch(rows, n_sets):
    lw = GDN_HB * HEAD_DIM
    staged = [(rows, lw)] * (GDN_N_STAGED - 1) + [(SUBLANES, rows)]
    return ([pltpu.VMEM((SUBLANES + rows, lw), F32) for _ in range(GDN_N_PADS - 1)] + [pltpu.VMEM((rows, lw), F32)]
            + [pltpu.VMEM(shape, F32) for _ in range(n_sets) for shape in staged])


def _gdn_prompt(proj, ab, conv_w, gnorm_w, params, rows):
    t = proj.shape[0]
    nb = t // rows
    lw = GDN_HB * HEAD_DIM
    head = lambda i: jnp.minimum(i, nb - 1)
    done = lambda i: jnp.maximum(i - 1, 0)
    x_spec = lambda off: pl.BlockSpec((rows, lw), lambda g, i: (head(i), off // lw + g))
    tail_spec = lambda off: pl.BlockSpec(
        (SUBLANES, lw), lambda g, i: (jnp.maximum(head(i) * (rows // SUBLANES) - 1, 0), off // lw + g))
    w_spec = lambda off: pl.BlockSpec((GDN_CONV, lw), lambda g, i: (0, (off - GQ_OFF) // lw + g))
    return pl.pallas_call(
        functools.partial(_gdn_kernel, rows=rows, seq=GDN_CHUNK, carry=True, live_from=0),
        grid=(GDN_NG, nb + 1),
        in_specs=[x_spec(GQ_OFF), x_spec(GK_OFF), x_spec(GV_OFF),
                  tail_spec(GQ_OFF), tail_spec(GK_OFF), tail_spec(GV_OFF),
                  pl.BlockSpec((rows, lw), lambda g, i: (done(i), Z_OFF // lw + g)),
                  w_spec(GQ_OFF), w_spec(GK_OFF), w_spec(GV_OFF),
                  pl.BlockSpec((1, HEAD_DIM), lambda g, i: (0, 0)),
                  pl.BlockSpec((rows, LANES), lambda g, i: (head(i), 0)),
                  pl.BlockSpec((SUBLANES, LANES), lambda g, i: (0, 0))],
        out_specs=[pl.BlockSpec((rows, lw), lambda g, i: (done(i), g)),
                   pl.BlockSpec((GDN_HB, HEAD_DIM, HEAD_DIM), lambda g, i: (g, 0, 0))],
        out_shape=[jax.ShapeDtypeStruct((t, GDN_WIDTH), BF16),
                   jax.ShapeDtypeStruct((GDN_HEADS, HEAD_DIM, HEAD_DIM), F32)],
        scratch_shapes=_gdn_scratch(rows, 2),
        compiler_params=_cparams(2),
        name="gdn_prompt",
    )(proj, proj, proj, proj, proj, proj, proj, conv_w, conv_w, conv_w,
      gnorm_w.reshape(1, HEAD_DIM).astype(F32), ab, params)


def _gdn_sample(proj, ab, conv_state, state, conv_w, gnorm_w, params, n_live):
    rows = GDN_CHUNK
    lw = GDN_HB * HEAD_DIM
    nseq = rows // SAMPLE_PAD
    n_in = nseq * n_live
    x_spec = lambda off: pl.BlockSpec((n_in, lw), lambda g, i: (i, off // lw + g))
    hist_spec = lambda off: pl.BlockSpec((nseq, GDN_CONV - 1, lw), lambda g, i: (i, 0, (off - GQ_OFF) // lw + g))
    w_spec = lambda off: pl.BlockSpec((GDN_CONV, lw), lambda g, i: (0, (off - GQ_OFF) // lw + g))
    s_spec = pl.BlockSpec((nseq, GDN_HB, HEAD_DIM, HEAD_DIM), lambda g, i: (i, g, 0, 0))
    return pl.pallas_call(
        functools.partial(_gdn_kernel, rows=rows, seq=SAMPLE_PAD, carry=False, live_from=SAMPLE_PAD - n_live),
        grid=(GDN_NG, proj.shape[0] // n_in),
        in_specs=[x_spec(GQ_OFF), x_spec(GK_OFF), x_spec(GV_OFF),
                  hist_spec(GQ_OFF), hist_spec(GK_OFF), hist_spec(GV_OFF),
                  x_spec(Z_OFF), w_spec(GQ_OFF), w_spec(GK_OFF), w_spec(GV_OFF),
                  pl.BlockSpec((1, HEAD_DIM), lambda g, i: (0, 0)),
                  pl.BlockSpec((n_in, LANES), lambda g, i: (i, 0)),
                  pl.BlockSpec((SUBLANES, LANES), lambda g, i: (0, 0)),
                  s_spec],
        out_specs=[pl.BlockSpec((n_in, lw), lambda g, i: (i, g)), s_spec],
        out_shape=[jax.ShapeDtypeStruct((proj.shape[0], GDN_WIDTH), F32),
                   jax.ShapeDtypeStruct(state.shape, F32)],
        scratch_shapes=[pltpu.VMEM((rows, lw), F32), pltpu.VMEM((rows, LANES), F32)] + _gdn_scratch(rows, 1),
        compiler_params=_cparams(2),
        name="gdn_sample",
    )(proj, proj, proj, conv_state, conv_state, conv_state, proj, conv_w, conv_w, conv_w,
      gnorm_w.reshape(1, HEAD_DIM).astype(F32), ab, params, state)


def _layer(x_prompt, x_sample, win_k, win_v, conv_state, gdn_state, rel_bias,
           w_in, sinks, conv_w, a_log, dt_bias, gnorm_w, w_out,
           n_pre_mix, n_post_mix, n_pre_ffn, n_post_ffn, w_gate, w_up, w_down, tiles):
    _, t, d = x_prompt.shape
    nb, ts, _ = x_sample.shape
    ms = nb * ts
    l = win_k.shape[1]
    ff = w_gate.shape[1]
    n_mt = tiles["n_mt"]
    xp, xs = x_prompt.reshape(t, d), x_sample.reshape(ms, d)

    w_in_t = w_in.T
    w_tail = jnp.pad(w_in_t[PROJ_MAIN:], ((0, LANES - AB_WIDTH), (0, 0))).astype(BF16)
    hp, ab_p = _rmsnorm_proj_tail(xp, n_pre_mix, w_tail)
    hs, ab_s = _rmsnorm_proj_tail(xs, n_pre_mix, w_tail)
    proj_p, proj_s = _proj([(hp, hs)], [(w_in_t, 0, d)], PROJ_MAIN, tiles["in_bn"], F32, n_mt, "in_proj",
                           w_transposed=True)

    qi = jnp.arange(WINDOW)[:, None]
    kj = jnp.arange(2 * WINDOW)[None, :]
    dist_p = WINDOW + qi - kj
    dist_s = l + jnp.arange(SUBLANES)[:, None] - kj
    dist_s = jnp.where(kj < l + ts, dist_s, -1)
    bias_all = _bias_table(rel_bias, jnp.concatenate([dist_p, dist_s], axis=0))
    attn_p = _attn_prompt(proj_p, bias_all[:, :WINDOW], sinks)

    bias_s = bias_all[:, WINDOW:WINDOW + ts].reshape(ATTN_KV_HEADS, ATTN_GROUP * ts, 2 * WINDOW)
    sink_rows = jnp.repeat(sinks.astype(F32), ts).reshape(ATTN_KV_HEADS, ATTN_GROUP * ts, 1)
    attn_s, new_k, new_v = _attn_sample(proj_s, win_k, win_v, bias_s, sink_rows, ts, tiles["attn_gb"])

    params = _gdn_params(a_log, dt_bias)
    og_p, s_p = _gdn_prompt(proj_p, ab_p, conv_w, gnorm_w, params, tiles["gdn_rows"])
    og_s, s_s = _gdn_sample(proj_s, ab_s, conv_state.astype(F32), gdn_state, conv_w, gnorm_w, params, ts)

    mix_p, mix_s = _proj([(attn_p, attn_s), (og_p, og_s)], [(w_out, 0, ATTN_WIDTH), (w_out, 1, GDN_WIDTH)],
                         d, tiles["out_bn"], BF16, n_mt, "out_proj")
    x1_p, h2_p = _post_mix(xp, mix_p, n_post_mix, n_pre_ffn, tiles["post_rows"])
    x1_s, h2_s = _post_mix(xs, mix_s, n_post_mix, n_pre_ffn, tiles["sample_rows"])
    act_p, act_s = _proj([(h2_p, h2_s)], [(w_gate, 0, d), (w_up, 0, d)], ff, tiles["ff_bn"], BF16, n_mt,
                         "ffn_gate_up", swiglu=True)
    ffn_p, ffn_s = _down(act_p, act_s, w_down, tiles["ff_bk"], n_mt)
    y_p = _residual_norm(x1_p, ffn_p, n_post_ffn, tiles["post_rows"])
    y_s = _residual_norm(x1_s, ffn_s, n_post_ffn, tiles["sample_rows"])

    lw = min(WINDOW, t)
    prompt_k = proj_p[t - lw:, K_OFF:K_OFF + KV_WIDTH].reshape(1, lw, ATTN_KV_HEADS, HEAD_DIM)
    prompt_v = proj_p[t - lw:, V_OFF:V_OFF + KV_WIDTH].reshape(1, lw, ATTN_KV_HEADS, HEAD_DIM)
    prompt_conv = proj_p[t - (GDN_CONV - 1):, GQ_OFF:GQ_OFF + GDN_CONV_CH][None]
    sample_conv = proj_s[:, GQ_OFF:GQ_OFF + GDN_CONV_CH].reshape(nb, ts, GDN_CONV_CH)[:, ts - (GDN_CONV - 1):]
    return (y_p.reshape(1, t, d), y_s.reshape(nb, ts, d),
            prompt_k, prompt_v, prompt_conv, s_p[None],
            new_k.reshape(nb, l, ATTN_KV_HEADS, HEAD_DIM), new_v.reshape(nb, l, ATTN_KV_HEADS, HEAD_DIM),
            sample_conv, s_s)


def _tiles(t, ms):
    n_mt = 8 if (t % (8 * 16) == 0 and ms % (8 * 16) == 0) else 1
    return dict(n_mt=n_mt, in_bn=512, out_bn=512, ff_bn=256, ff_bk=512, norm_rows=512, post_rows=256,
                sample_rows=ms, gdn_rows=256, attn_gb=8)


def kernel(x_prompt, x_sample, cache_win_k, cache_win_v, state_conv, state_gdn, rel_bias, w_in, attn_sinks,
           gdn_conv_w, gdn_a_log, gdn_dt_bias, gdn_norm_w, w_out, norm_pre_mix, norm_post_mix, norm_pre_ffn,
           norm_post_ffn, w_gate, w_up, w_down):
    depth = w_in.shape[0]
    assert depth == 1 and x_prompt.shape[0] == 1
    tiles = _tiles(x_prompt.shape[1], x_sample.shape[0] * x_sample.shape[1])
    outs = _layer(x_prompt, x_sample, cache_win_k[0], cache_win_v[0], state_conv[0], state_gdn[0], rel_bias,
                  w_in[0], attn_sinks[0], gdn_conv_w[0], gdn_a_log[0], gdn_dt_bias[0], gdn_norm_w[0], w_out[0],
                  norm_pre_mix[0], norm_post_mix[0], norm_pre_ffn[0], norm_post_ffn[0],
                  w_gate[0], w_up[0], w_down[0], tiles)
    yp, ys, pk, pv, pc, ps, sk, sv, sc, ss = outs
    return (yp, ys, pk[None], pv[None], pc[None], ps[None], sk[None], sv[None], sc[None], ss[None])
```

```python
import functools
import math

import jax
import jax.numpy as jnp
from jax import lax
from jax.experimental import pallas as pl
from jax.experimental.pallas import tpu as pltpu

F32 = jnp.float32
BF16 = jnp.bfloat16

HEAD_DIM = 128
ATTN_HEADS = 16
ATTN_KV_HEADS = 4
ATTN_GROUP = ATTN_HEADS // ATTN_KV_HEADS
ATTN_WIDTH = ATTN_HEADS * HEAD_DIM
KV_WIDTH = ATTN_KV_HEADS * HEAD_DIM
WINDOW = 128
REL_BUCKETS = 32
REL_MAX_DIST = 128
GDN_HEADS = 16
GDN_WIDTH = GDN_HEADS * HEAD_DIM
GDN_CONV = 4
GDN_CONV_CH = 3 * GDN_WIDTH
GDN_CHUNK = 64
NORM_EPS = 1e-6
NEG_INF = -1e30

Q_OFF = 0
K_OFF = ATTN_WIDTH
V_OFF = K_OFF + KV_WIDTH
GQ_OFF = V_OFF + KV_WIDTH
GK_OFF = GQ_OFF + GDN_WIDTH
GV_OFF = GK_OFF + GDN_WIDTH
Z_OFF = GV_OFF + GDN_WIDTH
PROJ_MAIN = Z_OFF + GDN_WIDTH
AB_WIDTH = 2 * GDN_HEADS

LANES = 128
SUBLANES = 8
GDN_HB = 4
GDN_NG = GDN_HEADS // GDN_HB
SAMPLE_PAD = 8
VMEM_LIMIT = 56 * 1024 * 1024


def _cparams(n_axes, vmem=VMEM_LIMIT):
    return pltpu.CompilerParams(dimension_semantics=("arbitrary",) * n_axes, vmem_limit_bytes=vmem)


def _dot(a, b):
    return jnp.dot(a, b, preferred_element_type=F32)


def _rms(x, w):
    return x * lax.rsqrt(jnp.mean(x * x, axis=-1, keepdims=True) + NORM_EPS) * w


def _rmsnorm_kernel(x_ref, w_ref, wt_ref, o_ref, tail_ref):
    h = _rms(x_ref[...].astype(F32), w_ref[...]).astype(o_ref.dtype)
    o_ref[...] = h
    tail_ref[...] = _dot_nt(h, wt_ref[...])


def _rmsnorm_proj_tail(x, w, w_tail):
    m, d = x.shape
    rows = min(m, 512)
    n = w_tail.shape[0]
    return pl.pallas_call(
        _rmsnorm_kernel,
        grid=(m // rows,),
        in_specs=[pl.BlockSpec((rows, d), lambda i: (i, 0)), pl.BlockSpec((1, d), lambda i: (0, 0)),
                  pl.BlockSpec((n, d), lambda i: (0, 0))],
        out_specs=[pl.BlockSpec((rows, d), lambda i: (i, 0)), pl.BlockSpec((rows, n), lambda i: (i, 0))],
        out_shape=[jax.ShapeDtypeStruct((m, d), BF16), jax.ShapeDtypeStruct((m, n), F32)],
        compiler_params=_cparams(1),
        name="rmsnorm",
    )(x, w.reshape(1, d).astype(F32), w_tail)


def _post_mix_kernel(x_ref, mix_ref, wpost_ref, wpre_ref, x1_ref, h2_ref):
    x1 = x_ref[...] + _rms(mix_ref[...].astype(F32), wpost_ref[...])
    x1_ref[...] = x1
    h2_ref[...] = _rms(x1, wpre_ref[...]).astype(h2_ref.dtype)


def _post_mix(x, mix, w_post, w_pre, rows):
    m, d = x.shape
    row_spec = pl.BlockSpec((rows, d), lambda i: (i, 0))
    w_spec = pl.BlockSpec((1, d), lambda i: (0, 0))
    return pl.pallas_call(
        _post_mix_kernel,
        grid=(m // rows,),
        in_specs=[row_spec, row_spec, w_spec, w_spec],
        out_specs=[row_spec, row_spec],
        out_shape=[jax.ShapeDtypeStruct((m, d), F32), jax.ShapeDtypeStruct((m, d), BF16)],
        compiler_params=_cparams(1),
        name="post_mix",
    )(x, mix, w_post.reshape(1, d).astype(F32), w_pre.reshape(1, d).astype(F32))


def _residual_norm_kernel(x_ref, y_ref, w_ref, o_ref):
    o_ref[...] = x_ref[...] + _rms(y_ref[...].astype(F32), w_ref[...])


def _residual_norm(x, y, w, rows):
    m, d = x.shape
    row_spec = pl.BlockSpec((rows, d), lambda i: (i, 0))
    return pl.pallas_call(
        _residual_norm_kernel,
        grid=(m // rows,),
        in_specs=[row_spec, row_spec, pl.BlockSpec((1, d), lambda i: (0, 0))],
        out_specs=row_spec,
        out_shape=jax.ShapeDtypeStruct((m, d), F32),
        compiler_params=_cparams(1),
        name="residual_norm",
    )(x, y, w.reshape(1, d).astype(F32))


def _stack_rows(p_ref, s_ref, cols=slice(None)):
    return jnp.concatenate([p_ref[:, cols].astype(BF16), s_ref[:, cols].astype(BF16)], axis=0)


def _dot_nt(a, b):
    return lax.dot_general(a, b, (((1,), (1,)), ((), ())), preferred_element_type=F32)


def _proj_kernel(*refs, n_a, n_w, swiglu, w_transposed, row_blocks, n_mt, n_tiles, last_cols):
    a_refs = refs[:2 * n_a]
    w_refs = refs[2 * n_a:2 * n_a + n_w]
    op_ref, os_ref = refs[2 * n_a + n_w:2 * n_a + n_w + 2]
    scratch = refs[2 * n_a + n_w + 2:]
    stage_refs, wb_refs, sem = scratch[:n_w], scratch[n_w:2 * n_w], scratch[2 * n_w]
    mm = _dot_nt if w_transposed else _dot
    j, i = pl.program_id(0), pl.program_id(1)
    nj = pl.num_programs(0)

    def slice_copy(w, tile, s, slot, narrow):
        rows, cols = stage_refs[w].shape[1:]
        if w_transposed:
            src = w_refs[w].at[pl.ds(tile * (rows * n_mt) + s * rows, rows), pl.ds(row_blocks[w] * cols, cols)]
            dst = stage_refs[w].at[slot]
        else:
            width = last_cols if narrow else cols
            src = w_refs[w].at[pl.ds((row_blocks[w] * n_mt + s) * rows, rows), pl.ds(tile * cols, width)]
            dst = stage_refs[w].at[slot, :, pl.ds(0, width)]
        return pltpu.make_async_copy(src, dst, sem.at[w, slot])

    def with_tile(tile, fn):
        if last_cols == stage_refs[0].shape[2] or w_transposed:
            fn(False)
        elif isinstance(tile, int):
            fn(tile == n_tiles - 1)
        else:
            pl.when(tile == n_tiles - 1)(lambda: fn(True))
            pl.when(tile != n_tiles - 1)(lambda: fn(False))

    def cast_slice(w, half, s, slot):
        rows = stage_refs[w].shape[1]
        wb_refs[w][half, pl.ds(pl.multiple_of(s * rows, rows), rows), :] = stage_refs[w][slot].astype(BF16)

    @pl.when(jnp.logical_and(j == 0, i == 0))
    def _():
        if last_cols != stage_refs[0].shape[2]:
            for w in range(n_w):
                stage_refs[w][...] = jnp.zeros_like(stage_refs[w])
        for w in range(n_w):
            for s in range(n_mt):
                def fetch_now(narrow, w=w, s=s):
                    copy = slice_copy(w, 0, s, s % 2, narrow)
                    copy.start()
                    copy.wait()
                with_tile(0, fetch_now)
                cast_slice(w, 0, s, s % 2)
        if n_tiles > 1:
            for w in range(n_w):
                with_tile(1, lambda narrow, w=w: slice_copy(w, 1, 0, 0, narrow).start())

    @pl.when(j + 1 < nj)
    def _():
        slot = (j * n_mt + i) % 2
        for w in range(n_w):
            with_tile(j + 1, lambda narrow, w=w: slice_copy(w, j + 1, i, slot, narrow).wait())
            cast_slice(w, (j + 1) % 2, i, slot)
        last = i + 1 == n_mt
        next_tile = jnp.where(last, j + 2, j + 1)
        next_s = jnp.where(last, 0, i + 1)

        @pl.when(next_tile < nj)
        def _():
            for w in range(n_w):
                with_tile(next_tile, lambda narrow, w=w: slice_copy(w, next_tile, next_s, 1 - slot, narrow).start())

    half = j % 2
    lhs = [_stack_rows(a_refs[2 * p], a_refs[2 * p + 1]) for p in range(n_a)]
    if swiglu:
        g = mm(lhs[0], wb_refs[0][half])
        o = g * jax.nn.sigmoid(g) * mm(lhs[0], wb_refs[1][half])
    else:
        o = mm(lhs[0], wb_refs[0][half])
        for l, wb_ref in zip(lhs[1:], wb_refs[1:]):
            o = o + mm(l, wb_ref[half])
    bmp = op_ref.shape[0]
    op_ref[...] = o[:bmp].astype(op_ref.dtype)
    os_ref[...] = o[bmp:].astype(os_ref.dtype)


def _proj(a_pairs, w_blocks, n, bn, out_dtype, n_mt, name, swiglu=False, w_transposed=False):
    tp, ts = a_pairs[0][0].shape[0], a_pairs[0][1].shape[0]
    bmp, bms = tp // n_mt, ts // n_mt
    in_specs = []
    for a_p, a_s in a_pairs:
        in_specs.append(pl.BlockSpec((bmp, a_p.shape[1]), lambda j, i: (i, 0)))
        in_specs.append(pl.BlockSpec((bms, a_s.shape[1]), lambda j, i: (i, 0)))
    in_specs += [pl.BlockSpec(memory_space=pl.ANY) for _ in w_blocks]
    n_tiles = pl.cdiv(n, bn)
    tile = lambda k: (bn, k) if w_transposed else (k, bn)
    stage = lambda k: (2, bn // n_mt, k) if w_transposed else (2, k // n_mt, bn)
    operands = [a for pair in a_pairs for a in pair] + [w for w, _, _ in w_blocks]
    return pl.pallas_call(
        functools.partial(_proj_kernel, n_a=len(a_pairs), n_w=len(w_blocks), swiglu=swiglu,
                          w_transposed=w_transposed, row_blocks=tuple(rb for _, rb, _ in w_blocks), n_mt=n_mt,
                          n_tiles=n_tiles, last_cols=n - (n_tiles - 1) * bn),
        grid=(n_tiles, n_mt),
        in_specs=in_specs,
        out_specs=[pl.BlockSpec((bmp, bn), lambda j, i: (i, j)), pl.BlockSpec((bms, bn), lambda j, i: (i, j))],
        out_shape=[jax.ShapeDtypeStruct((tp, n), out_dtype), jax.ShapeDtypeStruct((ts, n), out_dtype)],
        scratch_shapes=([pltpu.VMEM(stage(k), F32) for _, _, k in w_blocks]
                        + [pltpu.VMEM((2,) + tile(k), BF16) for _, _, k in w_blocks]
                        + [pltpu.SemaphoreType.DMA((len(w_blocks), 2))]),
        compiler_params=_cparams(2),
        name=name,
    )(*operands)


def _down_kernel(ap_ref, as_ref, b_ref, op_ref, os_ref, accp_ref, accs_ref, *, bn, k_total):
    k = pl.program_id(1)
    nk = pl.num_programs(1)
    bk = b_ref.shape[0]
    bmp = ap_ref.shape[0]
    last_rows = k_total - (k_total // bk) * bk

    @pl.when(k == 0)
    def _():
        accp_ref[...] = jnp.zeros_like(accp_ref)
        accs_ref[...] = jnp.zeros_like(accs_ref)

    def accumulate(kk):
        a = _stack_rows(ap_ref, as_ref, slice(0, kk))
        for n0 in range(0, accp_ref.shape[1], bn):
            d = _dot(a, b_ref[:kk, n0:n0 + bn].astype(BF16))
            accp_ref[:, n0:n0 + bn] += d[:bmp]
            accs_ref[:, n0:n0 + bn] += d[bmp:]

    if last_rows == 0:
        accumulate(bk)
    else:
        pl.when(k < nk - 1)(lambda: accumulate(bk))
        pl.when(k == nk - 1)(lambda: accumulate(last_rows))

    @pl.when(k == nk - 1)
    def _():
        op_ref[...] = accp_ref[...].astype(op_ref.dtype)
        os_ref[...] = accs_ref[...].astype(os_ref.dtype)


def _down(a_p, a_s, b, bk, n_mt):
    (tp, k_total), ts = a_p.shape, a_s.shape[0]
    n = b.shape[1]
    bmp, bms = tp // n_mt, ts // n_mt
    return pl.pallas_call(
        functools.partial(_down_kernel, bn=min(n, 512), k_total=k_total),
        grid=(n_mt, pl.cdiv(k_total, bk)),
        in_specs=[pl.BlockSpec((bmp, bk), lambda i, k: (i, k)), pl.BlockSpec((bms, bk), lambda i, k: (i, k)),
                  pl.BlockSpec((bk, n), lambda i, k: (k, 0))],
        out_specs=[pl.BlockSpec((bmp, n), lambda i, k: (i, 0), pipeline_mode=pl.Buffered(1)),
                   pl.BlockSpec((bms, n), lambda i, k: (i, 0))],
        out_shape=[jax.ShapeDtypeStruct((tp, n), BF16), jax.ShapeDtypeStruct((ts, n), BF16)],
        scratch_shapes=[pltpu.VMEM((bmp, n), F32), pltpu.VMEM((bms, n), F32)],
        compiler_params=_cparams(2),
        name="ffn_down",
    )(a_p, a_s, b)


def _rel_bucket(d):
    n = jnp.maximum(d, 0)
    max_exact = REL_BUCKETS // 2
    nf = jnp.maximum(n, 1).astype(F32)
    large = max_exact + (jnp.log(nf / max_exact) / math.log(REL_MAX_DIST / max_exact)
                         * (REL_BUCKETS - max_exact)).astype(jnp.int32)
    large = jnp.minimum(large, REL_BUCKETS - 1)
    return jnp.where(n < max_exact, n, large)


def _bias_kernel(rel_ref, bucket_ref, allowed_ref, o_ref):
    h = pl.program_id(0)
    bucket = bucket_ref[...]
    acc = jnp.zeros(bucket.shape, F32)
    for n in range(REL_BUCKETS):
        acc = jnp.where(bucket == n, rel_ref[n, h], acc)
    o_ref[0] = jnp.where(allowed_ref[...] > 0, acc, NEG_INF)


def _bias_table(rel_bias, dist):
    tq, tk = dist.shape
    allowed = ((dist >= 0) & (dist <= WINDOW)).astype(jnp.int32)
    full = pl.BlockSpec((tq, tk), lambda h: (0, 0))
    return pl.pallas_call(
        _bias_kernel,
        grid=(ATTN_HEADS,),
        in_specs=[pl.BlockSpec(memory_space=pltpu.SMEM), full, full],
        out_specs=pl.BlockSpec((1, tq, tk), lambda h: (h, 0, 0)),
        out_shape=jax.ShapeDtypeStruct((ATTN_HEADS, tq, tk), F32),
        compiler_params=_cparams(1),
        name="rel_bias_table",
    )(rel_bias.astype(F32), _rel_bucket(dist), allowed)


def _softmax_pv(s, sink):
    m = jnp.maximum(jnp.max(s, axis=-1, keepdims=True), sink)
    p = jnp.exp(s - m)
    denom = jnp.sum(p, axis=-1, keepdims=True) + jnp.exp(sink - m)
    return p.astype(BF16), denom


def _attn_prompt_kernel(sink_ref, cur_ref, prev_ref, bias_ref, o_ref):
    i = pl.program_id(0)
    c = WINDOW
    col = lax.broadcasted_iota(jnp.int32, (c, 2 * c), 1)
    no_prev = jnp.logical_and(i == 0, col < c)
    head_cols = lambda base, h: slice(base + h * HEAD_DIM, base + (h + 1) * HEAD_DIM)
    kk, vv = [], []
    for h in range(ATTN_KV_HEADS):
        kk.append(jnp.concatenate([prev_ref[:, head_cols(0, h)], cur_ref[:, head_cols(K_OFF, h)]],
                                  axis=0).astype(BF16))
        vv.append(jnp.concatenate([prev_ref[:, head_cols(KV_WIDTH, h)], cur_ref[:, head_cols(V_OFF, h)]],
                                  axis=0).astype(BF16))
    heads = range(ATTN_HEADS)
    s = [_dot_nt(cur_ref[:, head_cols(Q_OFF, hd)].astype(BF16), kk[hd // ATTN_GROUP]) for hd in heads]
    pd = []
    for hd in heads:
        sc = jnp.where(no_prev, NEG_INF, s[hd] * (HEAD_DIM ** -0.5) + bias_ref[hd])
        pd.append(_softmax_pv(sc, sink_ref[hd]))
    outs = [_dot(pd[hd][0], vv[hd // ATTN_GROUP]) / pd[hd][1] for hd in heads]
    o_ref[...] = jnp.concatenate(outs, axis=1).astype(o_ref.dtype)


def _attn_prompt(proj, bias, sinks):
    t = proj.shape[0]
    c = WINDOW
    qkv = V_OFF + KV_WIDTH
    return pl.pallas_call(
        _attn_prompt_kernel,
        grid=(t // c,),
        in_specs=[pl.BlockSpec(memory_space=pltpu.SMEM),
                  pl.BlockSpec((c, qkv), lambda i: (i, 0)),
                  pl.BlockSpec((c, 2 * KV_WIDTH), lambda i: (jnp.maximum(i - 1, 0), K_OFF // (2 * KV_WIDTH))),
                  pl.BlockSpec((ATTN_HEADS, c, 2 * c), lambda i: (0, 0, 0))],
        out_specs=pl.BlockSpec((c, ATTN_WIDTH), lambda i: (i, 0)),
        out_shape=jax.ShapeDtypeStruct((t, ATTN_WIDTH), BF16),
        compiler_params=_cparams(1),
        name="attn_prompt",
    )(sinks.astype(F32), proj, proj, bias)


def _attn_sample_kernel(x_ref, ck_ref, cv_ref, bias_ref, sink_ref, o_ref, ok_ref, ov_ref,
                        q_s, kf_ref, vf_ref, *, t_new):
    l = WINDOW
    gb = ck_ref.shape[0]

    @pl.when(pl.program_id(0) == 0)
    def _():
        kf_ref[:, l + t_new:, :] = jnp.zeros_like(kf_ref[:, l + t_new:, :])
        vf_ref[:, l + t_new:, :] = jnp.zeros_like(vf_ref[:, l + t_new:, :])

    for h in range(ATTN_KV_HEADS):
        kf_ref[:, 0:l, :] = ck_ref[:, :, h, :]
        vf_ref[:, 0:l, :] = cv_ref[:, :, h, :]
        for b in range(gb):
            rows = slice(b * t_new, (b + 1) * t_new)
            kf_ref[b, l:l + t_new, :] = x_ref[rows, K_OFF + h * HEAD_DIM:K_OFF + (h + 1) * HEAD_DIM]
            vf_ref[b, l:l + t_new, :] = x_ref[rows, V_OFF + h * HEAD_DIM:V_OFF + (h + 1) * HEAD_DIM]
            for g in range(ATTN_GROUP):
                col = (h * ATTN_GROUP + g) * HEAD_DIM
                q_s[b, g * t_new:(g + 1) * t_new, :] = x_ref[rows, col:col + HEAD_DIM]
        ok_ref[:, :, h, :] = kf_ref[:, t_new:l + t_new, :]
        ov_ref[:, :, h, :] = vf_ref[:, t_new:l + t_new, :]

        s = jnp.einsum("bqd,bkd->bqk", q_s[...].astype(BF16), kf_ref[...].astype(BF16),
                       preferred_element_type=F32)
        s = s * (HEAD_DIM ** -0.5) + bias_ref[h]
        p, denom = _softmax_pv(s, sink_ref[h])
        o = jnp.einsum("bqk,bkd->bqd", p, vf_ref[...].astype(BF16), preferred_element_type=F32) / denom
        for b in range(gb):
            for g in range(ATTN_GROUP):
                col = (h * ATTN_GROUP + g) * HEAD_DIM
                o_ref[b * t_new:(b + 1) * t_new, col:col + HEAD_DIM] = o[b, g * t_new:(g + 1) * t_new, :]


def _attn_sample(proj, cache_k, cache_v, bias, sink_rows, t_new, gb):
    b, l = cache_k.shape[:2]
    rows = ATTN_GROUP * t_new
    qkv = V_OFF + KV_WIDTH
    cache_spec = pl.BlockSpec((gb, l, ATTN_KV_HEADS, HEAD_DIM), lambda i: (i, 0, 0, 0))
    return pl.pallas_call(
        functools.partial(_attn_sample_kernel, t_new=t_new),
        grid=(b // gb,),
        in_specs=[pl.BlockSpec((gb * t_new, qkv), lambda i: (i, 0)), cache_spec, cache_spec,
                  pl.BlockSpec((ATTN_KV_HEADS, rows, 2 * l), lambda i: (0, 0, 0)),
                  pl.BlockSpec((ATTN_KV_HEADS, rows, 1), lambda i: (0, 0, 0))],
        out_specs=[pl.BlockSpec((gb * t_new, ATTN_WIDTH), lambda i: (i, 0)), cache_spec, cache_spec],
        out_shape=[jax.ShapeDtypeStruct((b * t_new, ATTN_WIDTH), F32),
                   jax.ShapeDtypeStruct(cache_k.shape, F32),
                   jax.ShapeDtypeStruct(cache_v.shape, F32)],
        scratch_shapes=[pltpu.VMEM((gb, rows, HEAD_DIM), F32),
                        pltpu.VMEM((gb, 2 * l, HEAD_DIM), F32), pltpu.VMEM((gb, 2 * l, HEAD_DIM), F32)],
        compiler_params=_cparams(1),
        name="attn_sample",
    )(proj, cache_k, cache_v, bias, sink_rows)


def _split3(x):
    hi = x.astype(BF16)
    r = x - hi.astype(F32)
    mid = r.astype(BF16)
    lo = (r - mid.astype(F32)).astype(BF16)
    return hi, mid, lo


def _select_dot(sel, x):
    hi, mid, lo = _split3(x)
    return _dot(sel, hi) + _dot(sel, mid) + _dot(sel, lo)


def _dot_select(x, sel):
    hi, mid, lo = _split3(x)
    return _dot(hi, sel) + _dot(mid, sel) + _dot(lo, sel)


def _softplus(x):
    return jnp.maximum(x, 0.0) + jnp.log1p(jnp.exp(-jnp.abs(x)))


def _conv_silu(pad_ref, w, r0, n):
    top = SUBLANES + r0
    y = pad_ref[top:top + n, :] * w[GDN_CONV - 1:GDN_CONV]
    for s in range(1, GDN_CONV):
        y = y + pad_ref[top - s:top - s + n, :] * w[GDN_CONV - 1 - s:GDN_CONV - s]
    return y * jax.nn.sigmoid(y)


def _l2norm_heads(x, scale):
    outs = []
    for hh in range(GDN_HB):
        xh = x[:, hh * HEAD_DIM:(hh + 1) * HEAD_DIM]
        outs.append(xh * (lax.rsqrt(jnp.sum(xh * xh, axis=-1, keepdims=True) + NORM_EPS) * scale))
    return jnp.concatenate(outs, axis=1)


GDN_N_PADS = 4
STAGE_ROWS = 32
PROBE_BOUND = 1e30
GDN_N_STAGED = 10


def _gdn_kernel(*refs, rows, seq, carry, live_from):
    kw = dict(rows=rows, seq=seq, carry=carry, live_from=live_from)
    if not carry:
        io, pads, staged = refs[:-GDN_N_PADS - GDN_N_STAGED], refs[-GDN_N_PADS - GDN_N_STAGED:-GDN_N_STAGED], \
            refs[-GDN_N_STAGED:]
        _gdn_body(io, pads, staged, staged, **kw)
        return
    n_scratch = GDN_N_PADS + 2 * GDN_N_STAGED
    io, pads = refs[:-n_scratch], refs[-n_scratch:-2 * GDN_N_STAGED]
    set_a, set_b = refs[-2 * GDN_N_STAGED:-GDN_N_STAGED], refs[-GDN_N_STAGED:]
    s_ref = io[-1]
    step = pl.program_id(1)

    @pl.when(step == 0)
    def _():
        s_ref[...] = jnp.zeros_like(s_ref)
        for ref in set_b:
            ref[...] = jnp.zeros_like(ref)

    pl.when(step % 2 == 0)(lambda: _gdn_body(io, pads, set_a, set_b, **kw))
    pl.when(step % 2 == 1)(lambda: _gdn_body(io, pads, set_b, set_a, **kw))


def _gdn_body(refs, pads, staged_w, staged_r, *, rows, seq, carry, live_from):
    c = GDN_CHUNK
    nc = rows // c
    nseq = c // seq
    n_live = seq - live_from
    lw = GDN_HB * HEAD_DIM
    if carry:
        (xq_ref, xk_ref, xv_ref, tq_ref, tk_ref, tv_ref, z_ref, wq_ref, wk_ref, wv_ref, gn_ref,
         ab_ref, pc_ref, og_ref, s_ref) = refs
        s0_ref = None
    else:
        (xq_ref, xk_ref, xv_ref, cq_ref, ck_ref, cv_ref, z_in, wq_ref, wk_ref, wv_ref, gn_ref,
         ab_in, pc_ref, s0_ref, og_ref, s_ref, z_ref, ab_ref) = refs
    padq_s, padk_s, padv_s, eb_s = pads
    kn_w, kb_w, qn_w, vb_w, kbe_w, qe_w, kd_w, egc_w, egl_w, gcr_w = staged_w
    kn_s, kb_s, qn_s, vb_s, kbe_s, qe_s, kd_s, egc_s, egl_s, gcr_s = staged_r

    group = pl.program_id(0)
    step = pl.program_id(1)
    row = lax.broadcasted_iota(jnp.int32, (rows, 1), 0)
    live = (row % seq) >= live_from
    keep_live = (lambda x: jnp.where(live, x, 0.0)) if live_from else (lambda x: x)

    if carry:
        for pad_ref, x_ref, t_ref in ((padq_s, xq_ref, tq_ref), (padk_s, xk_ref, tk_ref), (padv_s, xv_ref, tv_ref)):
            pad_ref[0:SUBLANES, :] = jnp.where(step == 0, 0.0, t_ref[...])
            pad_ref[SUBLANES:, :] = x_ref[...]
    else:
        def gather(dst_ref, x_ref, hist_ref, top):
            dst_ref[...] = jnp.zeros_like(dst_ref)
            for b in range(rows // seq):
                r0 = top + b * seq + live_from
                if hist_ref is not None:
                    dst_ref[r0 - (GDN_CONV - 1):r0, :] = hist_ref[b]
                dst_ref[r0:r0 + n_live, :] = x_ref[b * n_live:(b + 1) * n_live, :]

        gather(padq_s, xq_ref, cq_ref, SUBLANES)
        gather(padk_s, xk_ref, ck_ref, SUBLANES)
        gather(padv_s, xv_ref, cv_ref, SUBLANES)
        gather(z_ref, z_in, None, 0)
        gather(ab_ref, ab_in, None, 0)

    def stage_decays():
        ab = ab_ref[...]
        g_col = keep_live(-jnp.exp(pc_ref[0:1, :]) * _softplus(ab + pc_ref[1:2, :]))
        beta_col = keep_live(jax.nn.sigmoid(ab))
        ri = lax.broadcasted_iota(jnp.int32, (rows, rows), 0)
        ci = lax.broadcasted_iota(jnp.int32, (rows, rows), 1)
        same = (ri // seq) == (ci // seq)
        cum_sel = jnp.where(jnp.logical_and(same, ci <= ri), 1.0, 0.0).astype(BF16)
        tot_sel = jnp.where(same, 1.0, 0.0).astype(BF16)
        gc_col = _select_dot(cum_sel, g_col)
        gl_col = _select_dot(tot_sel, g_col)
        first = group * GDN_HB
        li = lax.broadcasted_iota(jnp.int32, (LANES, lw), 0)
        ni = lax.broadcasted_iota(jnp.int32, (LANES, lw), 1)
        spread_g = jnp.where(li == ni // HEAD_DIM + first, 1.0, 0.0).astype(BF16)
        spread_b = jnp.where(li == ni // HEAD_DIM + first + GDN_HEADS, 1.0, 0.0).astype(BF16)
        egc_w[...] = _dot_select(gc_col, spread_g)
        egl_w[...] = _dot_select(gl_col, spread_g)
        eb_s[...] = _dot_select(beta_col, spread_b)
        pi = lax.broadcasted_iota(jnp.int32, (LANES, LANES), 0)
        pj = lax.broadcasted_iota(jnp.int32, (LANES, LANES), 1)
        pick = jnp.where(jnp.logical_and(pi == pj + first, pj < GDN_HB), 1.0, 0.0).astype(BF16)
        gc_heads = _dot_select(gc_col, pick)
        if rows % LANES:
            gc_heads = jnp.concatenate([gc_heads, jnp.zeros((LANES - rows % LANES, LANES), F32)], axis=0)
        gcr_w[...] = gc_heads.T[:SUBLANES, :rows]

    def stage_rows(r0, n, after=None):
        sl = slice(r0, r0 + n)
        keep = (lambda x: jnp.where(live[sl], x, 0.0)) if live_from else (lambda x: x)
        gate = 1.0 if after is None else jnp.where(after > PROBE_BOUND, 0.0, 1.0)
        conv = lambda pad_ref, w_ref: keep(_conv_silu(pad_ref, w_ref[...] * gate, r0, n))
        qn = _l2norm_heads(conv(padq_s, wq_ref), HEAD_DIM ** -0.5)
        kn = _l2norm_heads(conv(padk_s, wk_ref), 1.0)
        v = conv(padv_s, wv_ref)
        egc, egl, eb = egc_w[sl, :], egl_w[sl, :], eb_s[sl, :]
        decay_in = jnp.exp(egc)
        kb = kn * eb
        kn_w[sl, :] = kn
        kb_w[sl, :] = kb
        qn_w[sl, :] = qn
        vb_w[sl, :] = v * eb
        kbe_w[sl, :] = kb * decay_in
        qe_w[sl, :] = qn * decay_in
        kd_w[sl, :] = kn * jnp.exp(egl - egc)

    n_slices = rows // STAGE_ROWS if carry else 1
    pending = [functools.partial(stage_rows, r * (rows // n_slices), rows // n_slices) for r in range(n_slices)]
    if not carry:
        stage_decays()
        pending.pop(0)()

    cr = lax.broadcasted_iota(jnp.int32, (c, c), 0)
    cc = lax.broadcasted_iota(jnp.int32, (c, c), 1)
    same_c = (cr // seq) == (cc // seq)
    causal_bias = jnp.where(jnp.logical_and(same_c, cr >= cc), 0.0, NEG_INF)
    strict = jnp.where(jnp.logical_and(same_c, cr > cc), 1.0, 0.0)

    eye = jnp.where(cr == cc, 1.0, 0.0)
    seq_of_row = lax.broadcasted_iota(jnp.int32, (c, HEAD_DIM), 0) // seq

    def lower_left(s):
        return jnp.logical_and(jnp.logical_and(cr // (2 * s) == cc // (2 * s), (cr // s) % 2 == 1),
                               (cc // s) % 2 == 0)

    merge_sizes = [2 ** e for e in range(1, max(1, math.ceil(math.log2(n_live))))]
    pair_mask = jnp.where(lower_left(1), 1.0, 0.0)
    merge_masks = [jnp.where(lower_left(s), 1.0, 0.0) for s in merge_sizes]

    pairs = [(ch, hh) for ch in range(nc) for hh in range(GDN_HB)]
    rows_of = lambda ch: slice(ch * c, (ch + 1) * c)
    lanes_of = lambda hh: slice(hh * HEAD_DIM, (hh + 1) * HEAD_DIM)
    uw_all, qk_all, a_mat, t_mat, t_a = {}, {}, {}, {}, {}
    for ch, hh in pairs:
        rs, hs = rows_of(ch), lanes_of(hh)
        gcol = egc_s[rs, hh * HEAD_DIM:hh * HEAD_DIM + c]
        grow = gcr_s[hh:hh + 1, rs]
        decay = jnp.exp(gcol - grow + causal_bias)
        kq = jnp.concatenate([kb_s[rs, hs], qn_s[rs, hs]], axis=0).astype(BF16)
        kk = _dot_nt(kq, kn_s[rs, hs].astype(BF16)) * jnp.concatenate([decay, decay], axis=0)
        a_mat[ch, hh] = kk[:c] * strict
        qk_all[ch, hh] = kk[c:].astype(BF16)
        t_mat[ch, hh] = eye - a_mat[ch, hh] * pair_mask
    probes = [a_mat[pairs[-1]][0:1, 0:1]]
    if carry:
        stage_decays()
    for mask in merge_masks:
        for p in pairs:
            t_a[p] = _dot(t_mat[p].astype(BF16), (a_mat[p] * mask).astype(BF16))
        for p in pairs:
            t_mat[p] = t_mat[p] - _dot(t_a[p].astype(BF16), t_mat[p].astype(BF16))
        probes.append(t_mat[pairs[-1]][0:1, 0:1])
    for ch, hh in pairs:
        rs, hs = rows_of(ch), lanes_of(hh)
        uw_all[ch, hh] = _dot(t_mat[ch, hh].astype(BF16),
                              jnp.concatenate([vb_s[rs, hs], kbe_s[rs, hs]], axis=1).astype(BF16))

    if carry:
        state = [s_ref[hh] for hh in range(GDN_HB)]
    new_states = {}
    heads = range(GDN_HB)
    tn_dot = lambda a, b: lax.dot_general(a, b, (((0,), (0,)), ((), ())), preferred_element_type=F32)
    for ch in range(nc):
        rs = rows_of(ch)
        w_s, q_s, v_new_b, o = {}, {}, {}, {}
        for hh in heads:
            w = uw_all[ch, hh][:, HEAD_DIM:]
            q_e = qe_s[rs, lanes_of(hh)]
            if carry:
                wq = _dot(jnp.concatenate([w, q_e], axis=0).astype(BF16), state[hh].astype(BF16))
                w_s[hh], q_s[hh] = wq[:c], wq[c:]
            else:
                parts = []
                for b in range(nseq):
                    bs = slice(b * seq, (b + 1) * seq)
                    parts.append(_dot(jnp.concatenate([w[bs], q_e[bs]], axis=0).astype(BF16),
                                      s0_ref[ch * nseq + b, hh].astype(BF16)))
                w_s[hh] = jnp.concatenate([p[:seq] for p in parts], axis=0)
                q_s[hh] = jnp.concatenate([p[seq:] for p in parts], axis=0)
        for hh in heads:
            v_new_b[hh] = (uw_all[ch, hh][:, :HEAD_DIM] - w_s[hh]).astype(BF16)
            o[hh] = q_s[hh] + _dot(qk_all[ch, hh], v_new_b[hh])
        for hh in heads:
            hs = lanes_of(hh)
            k_d = kd_s[rs, hs]
            if carry:
                state[hh] = (state[hh] * jnp.exp(egl_s[ch * c:ch * c + 1, hs])
                             + tn_dot(k_d.astype(BF16), v_new_b[hh]))
            else:
                for b in range(nseq):
                    k_db = jnp.where(seq_of_row == b, k_d, 0.0).astype(BF16)
                    r1 = ch * c + b * seq
                    new_states[ch * nseq + b, hh] = (
                        s0_ref[ch * nseq + b, hh] * jnp.exp(egl_s[r1:r1 + 1, hs]) + tn_dot(k_db, v_new_b[hh]))
        if carry:
            probes.append(_dot(qk_all[ch, 0], jnp.ones((c, HEAD_DIM), BF16))[0:1, 0:1])
        outs = []
        for hh in heads:
            z = z_ref[rs, lanes_of(hh)]
            outs.append(_rms(o[hh], gn_ref[...]) * (z * jax.nn.sigmoid(z)))
        og = jnp.concatenate(outs, axis=1)
        if carry:
            og_ref[rs, :] = og.astype(og_ref.dtype)
        else:
            for b in range(nseq):
                r0 = b * seq + live_from
                og_ref[(ch * nseq + b) * n_live:(ch * nseq + b + 1) * n_live, :] = (
                    og[r0:r0 + n_live].astype(og_ref.dtype))
    for n, stage in enumerate(pending):
        stage(after=probes[min(n, len(probes) - 1)])
    if carry:
        s_ref[...] = jnp.stack(state, axis=0)
    else:
        for (b, hh), s_new in new_states.items():
            s_ref[b, hh] = s_new


def _gdn_params(a_log, dt_bias):
    rows = jnp.stack([a_log.astype(F32), dt_bias.astype(F32)], axis=0)
    return jnp.pad(rows, ((0, SUBLANES - 2), (0, LANES - GDN_HEADS)))


def _gdn_scratch(rows, n_sets):
    lw = GDN_HB * HEAD_DIM
    staged = [(rows, lw)] * (GDN_N_STAGED - 1) + [(SUBLANES, rows)]
    return ([pltpu.VMEM((SUBLANES + rows, lw), F32) for _ in range(GDN_N_PADS - 1)] + [pltpu.VMEM((rows, lw), F32)]
            + [pltpu.VMEM(shape, F32) for _ in range(n_sets) for shape in staged])


def _gdn_prompt(proj, ab, conv_w, gnorm_w, params, rows):
    t = proj.shape[0]
    nb = t // rows
    lw = GDN_HB * HEAD_DIM
    head = lambda i: jnp.minimum(i, nb - 1)
    done = lambda i: jnp.maximum(i - 1, 0)
    x_spec = lambda off: pl.BlockSpec((rows, lw), lambda g, i: (head(i), off // lw + g))
    tail_spec = lambda off: pl.BlockSpec(
        (SUBLANES, lw), lambda g, i: (jnp.maximum(head(i) * (rows // SUBLANES) - 1, 0), off // lw + g))
    w_spec = lambda off: pl.BlockSpec((GDN_CONV, lw), lambda g, i: (0, (off - GQ_OFF) // lw + g))
    return pl.pallas_call(
        functools.partial(_gdn_kernel, rows=rows, seq=GDN_CHUNK, carry=True, live_from=0),
        grid=(GDN_NG, nb + 1),
        in_specs=[x_spec(GQ_OFF), x_spec(GK_OFF), x_spec(GV_OFF),
                  tail_spec(GQ_OFF), tail_spec(GK_OFF), tail_spec(GV_OFF),
                  pl.BlockSpec((rows, lw), lambda g, i: (done(i), Z_OFF // lw + g)),
                  w_spec(GQ_OFF), w_spec(GK_OFF), w_spec(GV_OFF),
                  pl.BlockSpec((1, HEAD_DIM), lambda g, i: (0, 0)),
                  pl.BlockSpec((rows, LANES), lambda g, i: (head(i), 0)),
                  pl.BlockSpec((SUBLANES, LANES), lambda g, i: (0, 0))],
        out_specs=[pl.BlockSpec((rows, lw), lambda g, i: (done(i), g)),
                   pl.BlockSpec((GDN_HB, HEAD_DIM, HEAD_DIM), lambda g, i: (g, 0, 0))],
        out_shape=[jax.ShapeDtypeStruct((t, GDN_WIDTH), BF16),
                   jax.ShapeDtypeStruct((GDN_HEADS, HEAD_DIM, HEAD_DIM), F32)],
        scratch_shapes=_gdn_scratch(rows, 2),
        compiler_params=_cparams(2),
        name="gdn_prompt",
    )(proj, proj, proj, proj, proj, proj, proj, conv_w, conv_w, conv_w,
      gnorm_w.reshape(1, HEAD_DIM).astype(F32), ab, params)


def _gdn_sample(proj, ab, conv_state, state, conv_w, gnorm_w, params, n_live):
    rows = GDN_CHUNK
    lw = GDN_HB * HEAD_DIM
    nseq = rows // SAMPLE_PAD
    n_in = nseq * n_live
    x_spec = lambda off: pl.BlockSpec((n_in, lw), lambda g, i: (i, off // lw + g))
    hist_spec = lambda off: pl.BlockSpec((nseq, GDN_CONV - 1, lw), lambda g, i: (i, 0, (off - GQ_OFF) // lw + g))
    w_spec = lambda off: pl.BlockSpec((GDN_CONV, lw), lambda g, i: (0, (off - GQ_OFF) // lw + g))
    s_spec = pl.BlockSpec((nseq, GDN_HB, HEAD_DIM, HEAD_DIM), lambda g, i: (i, g, 0, 0))
    return pl.pallas_call(
        functools.partial(_gdn_kernel, rows=rows, seq=SAMPLE_PAD, carry=False, live_from=SAMPLE_PAD - n_live),
        grid=(GDN_NG, proj.shape[0] // n_in),
        in_specs=[x_spec(GQ_OFF), x_spec(GK_OFF), x_spec(GV_OFF),
                  hist_spec(GQ_OFF), hist_spec(GK_OFF), hist_spec(GV_OFF),
                  x_spec(Z_OFF), w_spec(GQ_OFF), w_spec(GK_OFF), w_spec(GV_OFF),
                  pl.BlockSpec((1, HEAD_DIM), lambda g, i: (0, 0)),
                  pl.BlockSpec((n_in, LANES), lambda g, i: (i, 0)),
                  pl.BlockSpec((SUBLANES, LANES), lambda g, i: (0, 0)),
                  s_spec],
        out_specs=[pl.BlockSpec((n_in, lw), lambda g, i: (i, g)), s_spec],
        out_shape=[jax.ShapeDtypeStruct((proj.shape[0], GDN_WIDTH), F32),
                   jax.ShapeDtypeStruct(state.shape, F32)],
        scratch_shapes=[pltpu.VMEM((rows, lw), F32), pltpu.VMEM((rows, LANES), F32)] + _gdn_scratch(rows, 1),
        compiler_params=_cparams(2),
        name="gdn_sample",
    )(proj, proj, proj, conv_state, conv_state, conv_state, proj, conv_w, conv_w, conv_w,
      gnorm_w.reshape(1, HEAD_DIM).astype(F32), ab, params, state)


def _layer(x_prompt, x_sample, win_k, win_v, conv_state, gdn_state, rel_bias,
           w_in, sinks, conv_w, a_log, dt_bias, gnorm_w, w_out,
           n_pre_mix, n_post_mix, n_pre_ffn, n_post_ffn, w_gate, w_up, w_down, tiles):
    _, t, d = x_prompt.shape
    nb, ts, _ = x_sample.shape
    ms = nb * ts
    l = win_k.shape[1]
    ff = w_gate.shape[1]
    n_mt = tiles["n_mt"]
    xp, xs = x_prompt.reshape(t, d), x_sample.reshape(ms, d)

    w_in_t = w_in.T
    w_tail = jnp.pad(w_in_t[PROJ_MAIN:], ((0, LANES - AB_WIDTH), (0, 0))).astype(BF16)
    hp, ab_p = _rmsnorm_proj_tail(xp, n_pre_mix, w_tail)
    hs, ab_s = _rmsnorm_proj_tail(xs, n_pre_mix, w_tail)
    proj_p, proj_s = _proj([(hp, hs)], [(w_in_t, 0, d)], PROJ_MAIN, tiles["in_bn"], F32, n_mt, "in_proj",
                           w_transposed=True)

    qi = jnp.arange(WINDOW)[:, None]
    kj = jnp.arange(2 * WINDOW)[None, :]
    dist_p = WINDOW + qi - kj
    dist_s = l + jnp.arange(SUBLANES)[:, None] - kj
    dist_s = jnp.where(kj < l + ts, dist_s, -1)
    bias_all = _bias_table(rel_bias, jnp.concatenate([dist_p, dist_s], axis=0))
    attn_p = _attn_prompt(proj_p, bias_all[:, :WINDOW], sinks)

    bias_s = bias_all[:, WINDOW:WINDOW + ts].reshape(ATTN_KV_HEADS, ATTN_GROUP * ts, 2 * WINDOW)
    sink_rows = jnp.repeat(sinks.astype(F32), ts).reshape(ATTN_KV_HEADS, ATTN_GROUP * ts, 1)
    attn_s, new_k, new_v = _attn_sample(proj_s, win_k, win_v, bias_s, sink_rows, ts, tiles["attn_gb"])

    params = _gdn_params(a_log, dt_bias)
    og_p, s_p = _gdn_prompt(proj_p, ab_p, conv_w, gnorm_w, params, tiles["gdn_rows"])
    og_s, s_s = _gdn_sample(proj_s, ab_s, conv_state.astype(F32), gdn_state, conv_w, gnorm_w, params, ts)

    mix_p, mix_s = _proj([(attn_p, attn_s), (og_p, og_s)], [(w_out, 0, ATTN_WIDTH), (w_out, 1, GDN_WIDTH)],
                         d, tiles["out_bn"], BF16, n_mt, "out_proj")
    x1_p, h2_p = _post_mix(xp, mix_p, n_post_mix, n_pre_ffn, tiles["post_rows"])
    x1_s, h2_s = _post_mix(xs, mix_s, n_post_mix, n_pre_ffn, tiles["sample_rows"])
    act_p, act_s = _proj([(h2_p, h2_s)], [(w_gate, 0, d), (w_up, 0, d)], ff, tiles["ff_bn"], BF16, n_mt,
                         "ffn_gate_up", swiglu=True)
    ffn_p, ffn_s = _down(act_p, act_s, w_down, tiles["ff_bk"], n_mt)
    y_p = _residual_norm(x1_p, ffn_p, n_post_ffn, tiles["post_rows"])
    y_s = _residual_norm(x1_s, ffn_s, n_post_ffn, tiles["sample_rows"])

    lw = min(WINDOW, t)
    prompt_k = proj_p[t - lw:, K_OFF:K_OFF + KV_WIDTH].reshape(1, lw, ATTN_KV_HEADS, HEAD_DIM)
    prompt_v = proj_p[t - lw:, V_OFF:V_OFF + KV_WIDTH].reshape(1, lw, ATTN_KV_HEADS, HEAD_DIM)
    prompt_conv = proj_p[t - (GDN_CONV - 1):, GQ_OFF:GQ_OFF + GDN_CONV_CH][None]
    sample_conv = proj_s[:, GQ_OFF:GQ_OFF + GDN_CONV_CH].reshape(nb, ts, GDN_CONV_CH)[:, ts - (GDN_CONV - 1):]
    return (y_p.reshape(1, t, d), y_s.reshape(nb, ts, d),
            prompt_k, prompt_v, prompt_conv, s_p[None],
            new_k.reshape(nb, l, ATTN_KV_HEADS, HEAD_DIM), new_v.reshape(nb, l, ATTN_KV_HEADS, HEAD_DIM),
            sample_conv, s_s)


def _tiles(t, ms):
    n_mt = 8 if (t % (8 * 16) == 0 and ms % (8 * 16) == 0) else 1
    return dict(n_mt=n_mt, in_bn=512, out_bn=512, ff_bn=512, ff_bk=512, norm_rows=512, post_rows=256,
                sample_rows=ms, gdn_rows=256, attn_gb=8)


def kernel(x_prompt, x_sample, cache_win_k, cache_win_v, state_conv, state_gdn, rel_bias, w_in, attn_sinks,
           gdn_conv_w, gdn_a_log, gdn_dt_bias, gdn_norm_w, w_out, norm_pre_mix, norm_post_mix, norm_pre_ffn,
           norm_post_ffn, w_gate, w_up, w_down):
    depth = w_in.shape[0]
    assert depth == 1 and x_prompt.shape[0] == 1
    tiles = _tiles(x_prompt.shape[1], x_sample.shape[0] * x_sample.shape[1])
    outs = _layer(x_prompt, x_sample, cache_win_k[0], cache_win_v[0], state_conv[0], state_gdn[0], rel_bias,
                  w_in[0], attn_sinks[0], gdn_conv_w[0], gdn_a_log[0], gdn_dt_bias[0], gdn_norm_w[0], w_out[0],
                  norm_pre_mix[0], norm_post_mix[0], norm_pre_ffn[0], norm_post_ffn[0],
                  w_gate[0], w_up[0], w_down[0], tiles)
    yp, ys, pk, pv, pc, ps, sk, sv, sc, ss = outs
    return (yp, ys, pk[None], pv[None], pc[None], ps[None], sk[None], sv[None], sc[None], ss[None])
```

```python
import functools
import math

import jax
import jax.numpy as jnp
from jax import lax
from jax.experimental import pallas as pl
from jax.experimental.pallas import tpu as pltpu

F32 = jnp.float32
BF16 = jnp.bfloat16

HEAD_DIM = 128
ATTN_HEADS = 16
ATTN_KV_HEADS = 4
ATTN_GROUP = ATTN_HEADS // ATTN_KV_HEADS
ATTN_WIDTH = ATTN_HEADS * HEAD_DIM
KV_WIDTH = ATTN_KV_HEADS * HEAD_DIM
WINDOW = 128
REL_BUCKETS = 32
REL_MAX_DIST = 128
GDN_HEADS = 16
GDN_WIDTH = GDN_HEADS * HEAD_DIM
GDN_CONV = 4
GDN_CONV_CH = 3 * GDN_WIDTH
GDN_CHUNK = 64
NORM_EPS = 1e-6
NEG_INF = -1e30

Q_OFF = 0
K_OFF = ATTN_WIDTH
V_OFF = K_OFF + KV_WIDTH
GQ_OFF = V_OFF + KV_WIDTH
GK_OFF = GQ_OFF + GDN_WIDTH
GV_OFF = GK_OFF + GDN_WIDTH
Z_OFF = GV_OFF + GDN_WIDTH
PROJ_MAIN = Z_OFF + GDN_WIDTH
AB_WIDTH = 2 * GDN_HEADS

LANES = 128
SUBLANES = 8
GDN_HB = 4
GDN_NG = GDN_HEADS // GDN_HB
SAMPLE_PAD = 8
VMEM_LIMIT = 56 * 1024 * 1024


def _cparams(n_axes, vmem=VMEM_LIMIT):
    return pltpu.CompilerParams(dimension_semantics=("arbitrary",) * n_axes, vmem_limit_bytes=vmem)


def _dot(a, b):
    return jnp.dot(a, b, preferred_element_type=F32)


def _rms(x, w):
    return x * lax.rsqrt(jnp.mean(x * x, axis=-1, keepdims=True) + NORM_EPS) * w


def _rmsnorm_kernel(x_ref, w_ref, wt_ref, o_ref, tail_ref):
    h = _rms(x_ref[...].astype(F32), w_ref[...]).astype(o_ref.dtype)
    o_ref[...] = h
    tail_ref[...] = _dot_nt(h, wt_ref[...])


def _rmsnorm_proj_tail(x, w, w_tail):
    m, d = x.shape
    rows = min(m, 512)
    n = w_tail.shape[0]
    return pl.pallas_call(
        _rmsnorm_kernel,
        grid=(m // rows,),
        in_specs=[pl.BlockSpec((rows, d), lambda i: (i, 0)), pl.BlockSpec((1, d), lambda i: (0, 0)),
                  pl.BlockSpec((n, d), lambda i: (0, 0))],
        out_specs=[pl.BlockSpec((rows, d), lambda i: (i, 0)), pl.BlockSpec((rows, n), lambda i: (i, 0))],
        out_shape=[jax.ShapeDtypeStruct((m, d), BF16), jax.ShapeDtypeStruct((m, n), F32)],
        compiler_params=_cparams(1),
        name="rmsnorm",
    )(x, w.reshape(1, d).astype(F32), w_tail)


def _post_mix_kernel(x_ref, mix_ref, wpost_ref, wpre_ref, x1_ref, h2_ref):
    x1 = x_ref[...] + _rms(mix_ref[...].astype(F32), wpost_ref[...])
    x1_ref[...] = x1
    h2_ref[...] = _rms(x1, wpre_ref[...]).astype(h2_ref.dtype)


def _post_mix(x, mix, w_post, w_pre, rows):
    m, d = x.shape
    row_spec = pl.BlockSpec((rows, d), lambda i: (i, 0))
    w_spec = pl.BlockSpec((1, d), lambda i: (0, 0))
    return pl.pallas_call(
        _post_mix_kernel,
        grid=(m // rows,),
        in_specs=[row_spec, row_spec, w_spec, w_spec],
        out_specs=[row_spec, row_spec],
        out_shape=[jax.ShapeDtypeStruct((m, d), F32), jax.ShapeDtypeStruct((m, d), BF16)],
        compiler_params=_cparams(1),
        name="post_mix",
    )(x, mix, w_post.reshape(1, d).astype(F32), w_pre.reshape(1, d).astype(F32))


def _residual_norm_kernel(x_ref, y_ref, w_ref, o_ref):
    o_ref[...] = x_ref[...] + _rms(y_ref[...].astype(F32), w_ref[...])


def _residual_norm(x, y, w, rows):
    m, d = x.shape
    row_spec = pl.BlockSpec((rows, d), lambda i: (i, 0))
    return pl.pallas_call(
        _residual_norm_kernel,
        grid=(m // rows,),
        in_specs=[row_spec, row_spec, pl.BlockSpec((1, d), lambda i: (0, 0))],
        out_specs=row_spec,
        out_shape=jax.ShapeDtypeStruct((m, d), F32),
        compiler_params=_cparams(1),
        name="residual_norm",
    )(x, y, w.reshape(1, d).astype(F32))


def _stack_rows(p_ref, s_ref, cols=slice(None)):
    return jnp.concatenate([p_ref[:, cols].astype(BF16), s_ref[:, cols].astype(BF16)], axis=0)


def _dot_nt(a, b):
    return lax.dot_general(a, b, (((1,), (1,)), ((), ())), preferred_element_type=F32)


def _proj_kernel(*refs, n_a, n_w, swiglu, w_transposed, row_blocks, n_mt, n_tiles, last_cols):
    a_refs = refs[:2 * n_a]
    w_refs = refs[2 * n_a:2 * n_a + n_w]
    op_ref, os_ref = refs[2 * n_a + n_w:2 * n_a + n_w + 2]
    scratch = refs[2 * n_a + n_w + 2:]
    stage_refs, wb_refs, sem = scratch[:n_w], scratch[n_w:2 * n_w], scratch[2 * n_w]
    mm = _dot_nt if w_transposed else _dot
    j, i = pl.program_id(0), pl.program_id(1)
    nj = pl.num_programs(0)

    def slice_copy(w, tile, s, slot, narrow):
        rows, cols = stage_refs[w].shape[1:]
        if w_transposed:
            src = w_refs[w].at[pl.ds(tile * (rows * n_mt) + s * rows, rows), pl.ds(row_blocks[w] * cols, cols)]
            dst = stage_refs[w].at[slot]
        else:
            width = last_cols if narrow else cols
            src = w_refs[w].at[pl.ds((row_blocks[w] * n_mt + s) * rows, rows), pl.ds(tile * cols, width)]
            dst = stage_refs[w].at[slot, :, pl.ds(0, width)]
        return pltpu.make_async_copy(src, dst, sem.at[w, slot])

    def with_tile(tile, fn):
        if last_cols == stage_refs[0].shape[2] or w_transposed:
            fn(False)
        elif isinstance(tile, int):
            fn(tile == n_tiles - 1)
        else:
            pl.when(tile == n_tiles - 1)(lambda: fn(True))
            pl.when(tile != n_tiles - 1)(lambda: fn(False))

    def cast_slice(w, half, s, slot):
        rows = stage_refs[w].shape[1]
        wb_refs[w][half, pl.ds(pl.multiple_of(s * rows, rows), rows), :] = stage_refs[w][slot].astype(BF16)

    @pl.when(jnp.logical_and(j == 0, i == 0))
    def _():
        if last_cols != stage_refs[0].shape[2]:
            for w in range(n_w):
                stage_refs[w][...] = jnp.zeros_like(stage_refs[w])
        for w in range(n_w):
            for s in range(n_mt):
                def fetch_now(narrow, w=w, s=s):
                    copy = slice_copy(w, 0, s, s % 2, narrow)
                    copy.start()
                    copy.wait()
                with_tile(0, fetch_now)
                cast_slice(w, 0, s, s % 2)
        if n_tiles > 1:
            for w in range(n_w):
                with_tile(1, lambda narrow, w=w: slice_copy(w, 1, 0, 0, narrow).start())

    @pl.when(j + 1 < nj)
    def _():
        slot = (j * n_mt + i) % 2
        for w in range(n_w):
            with_tile(j + 1, lambda narrow, w=w: slice_copy(w, j + 1, i, slot, narrow).wait())
            cast_slice(w, (j + 1) % 2, i, slot)
        last = i + 1 == n_mt
        next_tile = jnp.where(last, j + 2, j + 1)
        next_s = jnp.where(last, 0, i + 1)

        @pl.when(next_tile < nj)
        def _():
            for w in range(n_w):
                with_tile(next_tile, lambda narrow, w=w: slice_copy(w, next_tile, next_s, 1 - slot, narrow).start())

    half = j % 2
    bmp = op_ref.shape[0]

    def apply(width):
        cols = slice(None) if w_transposed else slice(0, width)
        rhs = [wb_ref[half, :, cols] for wb_ref in wb_refs]
        lhs = [_stack_rows(a_refs[2 * p], a_refs[2 * p + 1]) for p in range(n_a)]
        if swiglu:
            g = mm(lhs[0], rhs[0])
            o = g * jax.nn.sigmoid(g) * mm(lhs[0], rhs[1])
        else:
            o = mm(lhs[0], rhs[0])
            for l, r in zip(lhs[1:], rhs[1:]):
                o = o + mm(l, r)
        op_ref[:, :width] = o[:bmp].astype(op_ref.dtype)
        os_ref[:, :width] = o[bmp:].astype(os_ref.dtype)

    full_cols = op_ref.shape[1]
    if last_cols == full_cols:
        apply(full_cols)
    else:
        pl.when(j != n_tiles - 1)(lambda: apply(full_cols))
        pl.when(j == n_tiles - 1)(lambda: apply(last_cols))


def _proj(a_pairs, w_blocks, n, bn, out_dtype, n_mt, name, swiglu=False, w_transposed=False):
    tp, ts = a_pairs[0][0].shape[0], a_pairs[0][1].shape[0]
    bmp, bms = tp // n_mt, ts // n_mt
    in_specs = []
    for a_p, a_s in a_pairs:
        in_specs.append(pl.BlockSpec((bmp, a_p.shape[1]), lambda j, i: (i, 0)))
        in_specs.append(pl.BlockSpec((bms, a_s.shape[1]), lambda j, i: (i, 0)))
    in_specs += [pl.BlockSpec(memory_space=pl.ANY) for _ in w_blocks]
    n_tiles = pl.cdiv(n, bn)
    tile = lambda k: (bn, k) if w_transposed else (k, bn)
    stage = lambda k: (2, bn // n_mt, k) if w_transposed else (2, k // n_mt, bn)
    operands = [a for pair in a_pairs for a in pair] + [w for w, _, _ in w_blocks]
    return pl.pallas_call(
        functools.partial(_proj_kernel, n_a=len(a_pairs), n_w=len(w_blocks), swiglu=swiglu,
                          w_transposed=w_transposed, row_blocks=tuple(rb for _, rb, _ in w_blocks), n_mt=n_mt,
                          n_tiles=n_tiles, last_cols=n - (n_tiles - 1) * bn),
        grid=(n_tiles, n_mt),
        in_specs=in_specs,
        out_specs=[pl.BlockSpec((bmp, bn), lambda j, i: (i, j)), pl.BlockSpec((bms, bn), lambda j, i: (i, j))],
        out_shape=[jax.ShapeDtypeStruct((tp, n), out_dtype), jax.ShapeDtypeStruct((ts, n), out_dtype)],
        scratch_shapes=([pltpu.VMEM(stage(k), F32) for _, _, k in w_blocks]
                        + [pltpu.VMEM((2,) + tile(k), BF16) for _, _, k in w_blocks]
                        + [pltpu.SemaphoreType.DMA((len(w_blocks), 2))]),
        compiler_params=_cparams(2),
        name=name,
    )(*operands)


def _down_kernel(ap_ref, as_ref, b_ref, op_ref, os_ref, accp_ref, accs_ref, *, bn, k_total):
    k = pl.program_id(1)
    nk = pl.num_programs(1)
    bk = b_ref.shape[0]
    bmp = ap_ref.shape[0]
    last_rows = k_total - (k_total // bk) * bk

    @pl.when(k == 0)
    def _():
        accp_ref[...] = jnp.zeros_like(accp_ref)
        accs_ref[...] = jnp.zeros_like(accs_ref)

    def accumulate(kk):
        a = _stack_rows(ap_ref, as_ref, slice(0, kk))
        for n0 in range(0, accp_ref.shape[1], bn):
            d = _dot(a, b_ref[:kk, n0:n0 + bn].astype(BF16))
            accp_ref[:, n0:n0 + bn] += d[:bmp]
            accs_ref[:, n0:n0 + bn] += d[bmp:]

    if last_rows == 0:
        accumulate(bk)
    else:
        pl.when(k < nk - 1)(lambda: accumulate(bk))
        pl.when(k == nk - 1)(lambda: accumulate(last_rows))

    @pl.when(k == nk - 1)
    def _():
        op_ref[...] = accp_ref[...].astype(op_ref.dtype)
        os_ref[...] = accs_ref[...].astype(os_ref.dtype)


def _down(a_p, a_s, b, bk, n_mt):
    (tp, k_total), ts = a_p.shape, a_s.shape[0]
    n = b.shape[1]
    bmp, bms = tp // n_mt, ts // n_mt
    return pl.pallas_call(
        functools.partial(_down_kernel, bn=min(n, 512), k_total=k_total),
        grid=(n_mt, pl.cdiv(k_total, bk)),
        in_specs=[pl.BlockSpec((bmp, bk), lambda i, k: (i, k)), pl.BlockSpec((bms, bk), lambda i, k: (i, k)),
                  pl.BlockSpec((bk, n), lambda i, k: (k, 0))],
        out_specs=[pl.BlockSpec((bmp, n), lambda i, k: (i, 0), pipeline_mode=pl.Buffered(1)),
                   pl.BlockSpec((bms, n), lambda i, k: (i, 0))],
        out_shape=[jax.ShapeDtypeStruct((tp, n), BF16), jax.ShapeDtypeStruct((ts, n), BF16)],
        scratch_shapes=[pltpu.VMEM((bmp, n), F32), pltpu.VMEM((bms, n), F32)],
        compiler_params=_cparams(2),
        name="ffn_down",
    )(a_p, a_s, b)


def _rel_bucket(d):
    n = jnp.maximum(d, 0)
    max_exact = REL_BUCKETS // 2
    nf = jnp.maximum(n, 1).astype(F32)
    large = max_exact + (jnp.log(nf / max_exact) / math.log(REL_MAX_DIST / max_exact)
                         * (REL_BUCKETS - max_exact)).astype(jnp.int32)
    large = jnp.minimum(large, REL_BUCKETS - 1)
    return jnp.where(n < max_exact, n, large)


def _bias_kernel(rel_ref, bucket_ref, allowed_ref, o_ref):
    h = pl.program_id(0)
    bucket = bucket_ref[...]
    acc = jnp.zeros(bucket.shape, F32)
    for n in range(REL_BUCKETS):
        acc = jnp.where(bucket == n, rel_ref[n, h], acc)
    o_ref[0] = jnp.where(allowed_ref[...] > 0, acc, NEG_INF)


def _bias_table(rel_bias, dist):
    tq, tk = dist.shape
    allowed = ((dist >= 0) & (dist <= WINDOW)).astype(jnp.int32)
    full = pl.BlockSpec((tq, tk), lambda h: (0, 0))
    return pl.pallas_call(
        _bias_kernel,
        grid=(ATTN_HEADS,),
        in_specs=[pl.BlockSpec(memory_space=pltpu.SMEM), full, full],
        out_specs=pl.BlockSpec((1, tq, tk), lambda h: (h, 0, 0)),
        out_shape=jax.ShapeDtypeStruct((ATTN_HEADS, tq, tk), F32),
        compiler_params=_cparams(1),
        name="rel_bias_table",
    )(rel_bias.astype(F32), _rel_bucket(dist), allowed)


def _softmax_pv(s, sink):
    m = jnp.maximum(jnp.max(s, axis=-1, keepdims=True), sink)
    p = jnp.exp(s - m)
    denom = jnp.sum(p, axis=-1, keepdims=True) + jnp.exp(sink - m)
    return p.astype(BF16), denom


def _attn_prompt_kernel(sink_ref, cur_ref, prev_ref, bias_ref, o_ref):
    i = pl.program_id(0)
    c = WINDOW
    col = lax.broadcasted_iota(jnp.int32, (c, 2 * c), 1)
    no_prev = jnp.logical_and(i == 0, col < c)
    head_cols = lambda base, h: slice(base + h * HEAD_DIM, base + (h + 1) * HEAD_DIM)
    kk, vv = [], []
    for h in range(ATTN_KV_HEADS):
        kk.append(jnp.concatenate([prev_ref[:, head_cols(0, h)], cur_ref[:, head_cols(K_OFF, h)]],
                                  axis=0).astype(BF16))
        vv.append(jnp.concatenate([prev_ref[:, head_cols(KV_WIDTH, h)], cur_ref[:, head_cols(V_OFF, h)]],
                                  axis=0).astype(BF16))
    heads = range(ATTN_HEADS)
    s = [_dot_nt(cur_ref[:, head_cols(Q_OFF, hd)].astype(BF16), kk[hd // ATTN_GROUP]) for hd in heads]
    pd = []
    for hd in heads:
        sc = jnp.where(no_prev, NEG_INF, s[hd] * (HEAD_DIM ** -0.5) + bias_ref[hd])
        pd.append(_softmax_pv(sc, sink_ref[hd]))
    outs = [_dot(pd[hd][0], vv[hd // ATTN_GROUP]) / pd[hd][1] for hd in heads]
    o_ref[...] = jnp.concatenate(outs, axis=1).astype(o_ref.dtype)


def _attn_prompt(proj, bias, sinks):
    t = proj.shape[0]
    c = WINDOW
    qkv = V_OFF + KV_WIDTH
    return pl.pallas_call(
        _attn_prompt_kernel,
        grid=(t // c,),
        in_specs=[pl.BlockSpec(memory_space=pltpu.SMEM),
                  pl.BlockSpec((c, qkv), lambda i: (i, 0)),
                  pl.BlockSpec((c, 2 * KV_WIDTH), lambda i: (jnp.maximum(i - 1, 0), K_OFF // (2 * KV_WIDTH))),
                  pl.BlockSpec((ATTN_HEADS, c, 2 * c), lambda i: (0, 0, 0))],
        out_specs=pl.BlockSpec((c, ATTN_WIDTH), lambda i: (i, 0)),
        out_shape=jax.ShapeDtypeStruct((t, ATTN_WIDTH), BF16),
        compiler_params=_cparams(1),
        name="attn_prompt",
    )(sinks.astype(F32), proj, proj, bias)


def _attn_sample_kernel(x_ref, ck_ref, cv_ref, bias_c_ref, bias_n_ref, sink_ref, o_ref, ok_ref, ov_ref,
                        q_s, kn_s, vn_s, *, t_new):
    gb, l, kvh, hd = ck_ref.shape
    kc = ck_ref[...].reshape(gb, l * kvh, hd)
    vc = cv_ref[...].reshape(gb, l * kvh, hd)
    for b in range(gb):
        rows = slice(b * t_new, (b + 1) * t_new)
        for h in range(kvh):
            kn_s[b, h * t_new:(h + 1) * t_new, :] = x_ref[rows, K_OFF + h * hd:K_OFF + (h + 1) * hd]
            vn_s[b, h * t_new:(h + 1) * t_new, :] = x_ref[rows, V_OFF + h * hd:V_OFF + (h + 1) * hd]
        for head in range(ATTN_HEADS):
            q_s[b, head * t_new:(head + 1) * t_new, :] = x_ref[rows, head * hd:(head + 1) * hd]

    q = q_s[...].astype(BF16)
    scale = hd ** -0.5
    s_c = jnp.einsum("bqd,bkd->bqk", q, kc.astype(BF16), preferred_element_type=F32) * scale + bias_c_ref[...]
    s_n = jnp.einsum("bqd,bkd->bqk", q, kn_s[...].astype(BF16), preferred_element_type=F32) * scale + bias_n_ref[...]
    sink = sink_ref[...]
    m = jnp.maximum(jnp.maximum(jnp.max(s_c, axis=-1, keepdims=True), jnp.max(s_n, axis=-1, keepdims=True)), sink)
    p_c = jnp.exp(s_c - m)
    p_n = jnp.exp(s_n - m)
    denom = jnp.sum(p_c, axis=-1, keepdims=True) + jnp.sum(p_n, axis=-1, keepdims=True) + jnp.exp(sink - m)
    o = (jnp.einsum("bqk,bkd->bqd", p_c.astype(BF16), vc.astype(BF16), preferred_element_type=F32)
         + jnp.einsum("bqk,bkd->bqd", p_n.astype(BF16), vn_s[...].astype(BF16), preferred_element_type=F32)) / denom
    for b in range(gb):
        for head in range(ATTN_HEADS):
            o_ref[b * t_new:(b + 1) * t_new, head * hd:(head + 1) * hd] = o[b, head * t_new:(head + 1) * t_new, :]

    shift = t_new * kvh
    for c_val, n_s, out_ref in ((kc, kn_s, ok_ref), (vc, vn_s, ov_ref)):
        new_rows = [n_s[:, h * t_new + t:h * t_new + t + 1, :] for t in range(t_new) for h in range(kvh)]
        win = jnp.concatenate([c_val[:, shift:, :]] + new_rows, axis=1)
        out_ref[...] = win.reshape(gb, l, kvh, hd)


def _attn_sample(proj, cache_k, cache_v, bias_c, bias_n, sink_rows, t_new, gb):
    b, l = cache_k.shape[:2]
    rows = ATTN_HEADS * t_new
    qkv = V_OFF + KV_WIDTH
    cache_spec = pl.BlockSpec((gb, l, ATTN_KV_HEADS, HEAD_DIM), lambda i: (i, 0, 0, 0))
    full = lambda a: pl.BlockSpec(a.shape, lambda i: (0,) * a.ndim)
    return pl.pallas_call(
        functools.partial(_attn_sample_kernel, t_new=t_new),
        grid=(b // gb,),
        in_specs=[pl.BlockSpec((gb * t_new, qkv), lambda i: (i, 0)), cache_spec, cache_spec,
                  full(bias_c), full(bias_n), full(sink_rows)],
        out_specs=[pl.BlockSpec((gb * t_new, ATTN_WIDTH), lambda i: (i, 0)), cache_spec, cache_spec],
        out_shape=[jax.ShapeDtypeStruct((b * t_new, ATTN_WIDTH), F32),
                   jax.ShapeDtypeStruct(cache_k.shape, F32),
                   jax.ShapeDtypeStruct(cache_v.shape, F32)],
        scratch_shapes=[pltpu.VMEM((gb, rows, HEAD_DIM), F32),
                        pltpu.VMEM((gb, ATTN_KV_HEADS * t_new, HEAD_DIM), F32),
                        pltpu.VMEM((gb, ATTN_KV_HEADS * t_new, HEAD_DIM), F32)],
        compiler_params=_cparams(1),
        name="attn_sample",
    )(proj, cache_k, cache_v, bias_c, bias_n, sink_rows)


def _split3(x):
    hi = x.astype(BF16)
    r = x - hi.astype(F32)
    mid = r.astype(BF16)
    lo = (r - mid.astype(F32)).astype(BF16)
    return hi, mid, lo


def _select_dot(sel, x):
    hi, mid, lo = _split3(x)
    return _dot(sel, hi) + _dot(sel, mid) + _dot(sel, lo)


def _dot_select(x, sel):
    hi, mid, lo = _split3(x)
    return _dot(hi, sel) + _dot(mid, sel) + _dot(lo, sel)


def _softplus(x):
    return jnp.maximum(x, 0.0) + jnp.log1p(jnp.exp(-jnp.abs(x)))


def _conv_silu(pad_ref, w, r0, n):
    top = SUBLANES + r0
    y = pad_ref[top:top + n, :] * w[GDN_CONV - 1:GDN_CONV]
    for s in range(1, GDN_CONV):
        y = y + pad_ref[top - s:top - s + n, :] * w[GDN_CONV - 1 - s:GDN_CONV - s]
    return y * jax.nn.sigmoid(y)


def _l2norm_heads(x, scale):
    outs = []
    for hh in range(GDN_HB):
        xh = x[:, hh * HEAD_DIM:(hh + 1) * HEAD_DIM]
        outs.append(xh * (lax.rsqrt(jnp.sum(xh * xh, axis=-1, keepdims=True) + NORM_EPS) * scale))
    return jnp.concatenate(outs, axis=1)


GDN_N_PADS = 4
STAGE_ROWS = 32
PROBE_BOUND = 1e30
GDN_N_STAGED = 10


def _gdn_kernel(*refs, rows, seq, carry, live_from):
    kw = dict(rows=rows, seq=seq, carry=carry, live_from=live_from)
    if not carry:
        io, pads, staged = refs[:-GDN_N_PADS - GDN_N_STAGED], refs[-GDN_N_PADS - GDN_N_STAGED:-GDN_N_STAGED], \
            refs[-GDN_N_STAGED:]
        _gdn_body(io, pads, staged, staged, **kw)
        return
    n_scratch = GDN_N_PADS + 2 * GDN_N_STAGED
    io, pads = refs[:-n_scratch], refs[-n_scratch:-2 * GDN_N_STAGED]
    set_a, set_b = refs[-2 * GDN_N_STAGED:-GDN_N_STAGED], refs[-GDN_N_STAGED:]
    s_ref = io[-1]
    step = pl.program_id(1)

    @pl.when(step == 0)
    def _():
        s_ref[...] = jnp.zeros_like(s_ref)
        for ref in set_b:
            ref[...] = jnp.zeros_like(ref)

    pl.when(step % 2 == 0)(lambda: _gdn_body(io, pads, set_a, set_b, **kw))
    pl.when(step % 2 == 1)(lambda: _gdn_body(io, pads, set_b, set_a, **kw))


def _gdn_body(refs, pads, staged_w, staged_r, *, rows, seq, carry, live_from):
    c = GDN_CHUNK
    nc = rows // c
    nseq = c // seq
    n_live = seq - live_from
    lw = GDN_HB * HEAD_DIM
    if carry:
        (xq_ref, xk_ref, xv_ref, tq_ref, tk_ref, tv_ref, z_ref, wq_ref, wk_ref, wv_ref, gn_ref,
         ab_ref, pc_ref, og_ref, s_ref) = refs
        s0_ref = None
    else:
        (xq_ref, xk_ref, xv_ref, cq_ref, ck_ref, cv_ref, z_in, wq_ref, wk_ref, wv_ref, gn_ref,
         ab_in, pc_ref, s0_ref, og_ref, s_ref, z_ref, ab_ref) = refs
    padq_s, padk_s, padv_s, eb_s = pads
    kn_w, kb_w, qn_w, vb_w, kbe_w, qe_w, kd_w, egc_w, egl_w, gcr_w = staged_w
    kn_s, kb_s, qn_s, vb_s, kbe_s, qe_s, kd_s, egc_s, egl_s, gcr_s = staged_r

    group = pl.program_id(0)
    step = pl.program_id(1)
    row = lax.broadcasted_iota(jnp.int32, (rows, 1), 0)
    live = (row % seq) >= live_from
    keep_live = (lambda x: jnp.where(live, x, 0.0)) if live_from else (lambda x: x)

    if carry:
        for pad_ref, x_ref, t_ref in ((padq_s, xq_ref, tq_ref), (padk_s, xk_ref, tk_ref), (padv_s, xv_ref, tv_ref)):
            pad_ref[0:SUBLANES, :] = jnp.where(step == 0, 0.0, t_ref[...])
            pad_ref[SUBLANES:, :] = x_ref[...]
    else:
        def gather(dst_ref, x_ref, hist_ref, top):
            dst_ref[...] = jnp.zeros_like(dst_ref)
            for b in range(rows // seq):
                r0 = top + b * seq + live_from
                if hist_ref is not None:
                    dst_ref[r0 - (GDN_CONV - 1):r0, :] = hist_ref[b]
                dst_ref[r0:r0 + n_live, :] = x_ref[b * n_live:(b + 1) * n_live, :]

        gather(padq_s, xq_ref, cq_ref, SUBLANES)
        gather(padk_s, xk_ref, ck_ref, SUBLANES)
        gather(padv_s, xv_ref, cv_ref, SUBLANES)
        gather(z_ref, z_in, None, 0)
        gather(ab_ref, ab_in, None, 0)

    def stage_decays():
        ab = ab_ref[...]
        g_col = keep_live(-jnp.exp(pc_ref[0:1, :]) * _softplus(ab + pc_ref[1:2, :]))
        beta_col = keep_live(jax.nn.sigmoid(ab))
        ri = lax.broadcasted_iota(jnp.int32, (rows, rows), 0)
        ci = lax.broadcasted_iota(jnp.int32, (rows, rows), 1)
        same = (ri // seq) == (ci // seq)
        cum_sel = jnp.where(jnp.logical_and(same, ci <= ri), 1.0, 0.0).astype(BF16)
        tot_sel = jnp.where(same, 1.0, 0.0).astype(BF16)
        gc_col = _select_dot(cum_sel, g_col)
        gl_col = _select_dot(tot_sel, g_col)
        first = group * GDN_HB
        li = lax.broadcasted_iota(jnp.int32, (LANES, lw), 0)
        ni = lax.broadcasted_iota(jnp.int32, (LANES, lw), 1)
        spread_g = jnp.where(li == ni // HEAD_DIM + first, 1.0, 0.0).astype(BF16)
        spread_b = jnp.where(li == ni // HEAD_DIM + first + GDN_HEADS, 1.0, 0.0).astype(BF16)
        egc_w[...] = _dot_select(gc_col, spread_g)
        egl_w[...] = _dot_select(gl_col, spread_g)
        eb_s[...] = _dot_select(beta_col, spread_b)
        pi = lax.broadcasted_iota(jnp.int32, (LANES, LANES), 0)
        pj = lax.broadcasted_iota(jnp.int32, (LANES, LANES), 1)
        pick = jnp.where(jnp.logical_and(pi == pj + first, pj < GDN_HB), 1.0, 0.0).astype(BF16)
        gc_heads = _dot_select(gc_col, pick)
        if rows % LANES:
            gc_heads = jnp.concatenate([gc_heads, jnp.zeros((LANES - rows % LANES, LANES), F32)], axis=0)
        gcr_w[...] = gc_heads.T[:SUBLANES, :rows]

    def stage_rows(r0, n, after=None):
        sl = slice(r0, r0 + n)
        keep = (lambda x: jnp.where(live[sl], x, 0.0)) if live_from else (lambda x: x)
        gate = 1.0 if after is None else jnp.where(after > PROBE_BOUND, 0.0, 1.0)
        conv = lambda pad_ref, w_ref: keep(_conv_silu(pad_ref, w_ref[...] * gate, r0, n))
        qn = _l2norm_heads(conv(padq_s, wq_ref), HEAD_DIM ** -0.5)
        kn = _l2norm_heads(conv(padk_s, wk_ref), 1.0)
        v = conv(padv_s, wv_ref)
        egc, egl, eb = egc_w[sl, :], egl_w[sl, :], eb_s[sl, :]
        decay_in = jnp.exp(egc)
        kb = kn * eb
        kn_w[sl, :] = kn
        kb_w[sl, :] = kb
        qn_w[sl, :] = qn
        vb_w[sl, :] = v * eb
        kbe_w[sl, :] = kb * decay_in
        qe_w[sl, :] = qn * decay_in
        kd_w[sl, :] = kn * jnp.exp(egl - egc)

    n_slices = rows // STAGE_ROWS if carry else 1
    pending = [functools.partial(stage_rows, r * (rows // n_slices), rows // n_slices) for r in range(n_slices)]
    if not carry:
        stage_decays()
        pending.pop(0)()

    cr = lax.broadcasted_iota(jnp.int32, (c, c), 0)
    cc = lax.broadcasted_iota(jnp.int32, (c, c), 1)
    same_c = (cr // seq) == (cc // seq)
    causal_bias = jnp.where(jnp.logical_and(same_c, cr >= cc), 0.0, NEG_INF)
    strict = jnp.where(jnp.logical_and(same_c, cr > cc), 1.0, 0.0)

    eye = jnp.where(cr == cc, 1.0, 0.0)
    seq_of_row = lax.broadcasted_iota(jnp.int32, (c, HEAD_DIM), 0) // seq

    def lower_left(s):
        return jnp.logical_and(jnp.logical_and(cr // (2 * s) == cc // (2 * s), (cr // s) % 2 == 1),
                               (cc // s) % 2 == 0)

    merge_sizes = [2 ** e for e in range(1, max(1, math.ceil(math.log2(n_live))))]
    pair_mask = jnp.where(lower_left(1), 1.0, 0.0)
    merge_masks = [jnp.where(lower_left(s), 1.0, 0.0) for s in merge_sizes]

    pairs = [(ch, hh) for ch in range(nc) for hh in range(GDN_HB)]
    rows_of = lambda ch: slice(ch * c, (ch + 1) * c)
    lanes_of = lambda hh: slice(hh * HEAD_DIM, (hh + 1) * HEAD_DIM)
    uw_all, qk_all, a_mat, t_mat, t_a = {}, {}, {}, {}, {}
    for ch, hh in pairs:
        rs, hs = rows_of(ch), lanes_of(hh)
        gcol = egc_s[rs, hh * HEAD_DIM:hh * HEAD_DIM + c]
        grow = gcr_s[hh:hh + 1, rs]
        decay = jnp.exp(gcol - grow + causal_bias)
        kq = jnp.concatenate([kb_s[rs, hs], qn_s[rs, hs]], axis=0).astype(BF16)
        kk = _dot_nt(kq, kn_s[rs, hs].astype(BF16)) * jnp.concatenate([decay, decay], axis=0)
        a_mat[ch, hh] = kk[:c] * strict
        qk_all[ch, hh] = kk[c:].astype(BF16)
        t_mat[ch, hh] = eye - a_mat[ch, hh] * pair_mask
    probes = [a_mat[pairs[-1]][0:1, 0:1]]
    if carry:
        stage_decays()
    for mask in merge_masks:
        for p in pairs:
            t_a[p] = _dot(t_mat[p].astype(BF16), (a_mat[p] * mask).astype(BF16))
        for p in pairs:
            t_mat[p] = t_mat[p] - _dot(t_a[p].astype(BF16), t_mat[p].astype(BF16))
        probes.append(t_mat[pairs[-1]][0:1, 0:1])
    for ch, hh in pairs:
        rs, hs = rows_of(ch), lanes_of(hh)
        uw_all[ch, hh] = _dot(t_mat[ch, hh].astype(BF16),
                              jnp.concatenate([vb_s[rs, hs], kbe_s[rs, hs]], axis=1).astype(BF16))

    if carry:
        state = [s_ref[hh] for hh in range(GDN_HB)]
    new_states = {}
    heads = range(GDN_HB)
    tn_dot = lambda a, b: lax.dot_general(a, b, (((0,), (0,)), ((), ())), preferred_element_type=F32)
    for ch in range(nc):
        rs = rows_of(ch)
        w_s, q_s, v_new_b, o = {}, {}, {}, {}
        for hh in heads:
            w = uw_all[ch, hh][:, HEAD_DIM:]
            q_e = qe_s[rs, lanes_of(hh)]
            if carry:
                wq = _dot(jnp.concatenate([w, q_e], axis=0).astype(BF16), state[hh].astype(BF16))
                w_s[hh], q_s[hh] = wq[:c], wq[c:]
            else:
                parts = []
                for b in range(nseq):
                    bs = slice(b * seq, (b + 1) * seq)
                    parts.append(_dot(jnp.concatenate([w[bs], q_e[bs]], axis=0).astype(BF16),
                                      s0_ref[ch * nseq + b, hh].astype(BF16)))
                w_s[hh] = jnp.concatenate([p[:seq] for p in parts], axis=0)
                q_s[hh] = jnp.concatenate([p[seq:] for p in parts], axis=0)
        for hh in heads:
            v_new_b[hh] = (uw_all[ch, hh][:, :HEAD_DIM] - w_s[hh]).astype(BF16)
            o[hh] = q_s[hh] + _dot(qk_all[ch, hh], v_new_b[hh])
        for hh in heads:
            hs = lanes_of(hh)
            k_d = kd_s[rs, hs]
            if carry:
                state[hh] = (state[hh] * jnp.exp(egl_s[ch * c:ch * c + 1, hs])
                             + tn_dot(k_d.astype(BF16), v_new_b[hh]))
            else:
                for b in range(nseq):
                    k_db = jnp.where(seq_of_row == b, k_d, 0.0).astype(BF16)
                    r1 = ch * c + b * seq
                    new_states[ch * nseq + b, hh] = (
                        s0_ref[ch * nseq + b, hh] * jnp.exp(egl_s[r1:r1 + 1, hs]) + tn_dot(k_db, v_new_b[hh]))
        if carry:
            probes.append(_dot(qk_all[ch, 0], jnp.ones((c, HEAD_DIM), BF16))[0:1, 0:1])
        outs = []
        for hh in heads:
            z = z_ref[rs, lanes_of(hh)]
            outs.append(_rms(o[hh], gn_ref[...]) * (z * jax.nn.sigmoid(z)))
        og = jnp.concatenate(outs, axis=1)
        if carry:
            og_ref[rs, :] = og.astype(og_ref.dtype)
        else:
            for b in range(nseq):
                r0 = b * seq + live_from
                og_ref[(ch * nseq + b) * n_live:(ch * nseq + b + 1) * n_live, :] = (
                    og[r0:r0 + n_live].astype(og_ref.dtype))
    for n, stage in enumerate(pending):
        stage(after=probes[min(n, len(probes) - 1)])
    if carry:
        s_ref[...] = jnp.stack(state, axis=0)
    else:
        for (b, hh), s_new in new_states.items():
            s_ref[b, hh] = s_new


def _gdn_params(a_log, dt_bias):
    rows = jnp.stack([a_log.astype(F32), dt_bias.astype(F32)], axis=0)
    return jnp.pad(rows, ((0, SUBLANES - 2), (0, LANES - GDN_HEADS)))


def _gdn_scratch(rows, n_sets):
    lw = GDN_HB * HEAD_DIM
    staged = [(rows, lw)] * (GDN_N_STAGED - 1) + [(SUBLANES, rows)]
    return ([pltpu.VMEM((SUBLANES + rows, lw), F32) for _ in range(GDN_N_PADS - 1)] + [pltpu.VMEM((rows, lw), F32)]
            + [pltpu.VMEM(shape, F32) for _ in range(n_sets) for shape in staged])


def _gdn_prompt(proj, ab, conv_w, gnorm_w, params, rows):
    t = proj.shape[0]
    nb = t // rows
    lw = GDN_HB * HEAD_DIM
    head = lambda i: jnp.minimum(i, nb - 1)
    done = lambda i: jnp.maximum(i - 1, 0)
    x_spec = lambda off: pl.BlockSpec((rows, lw), lambda g, i: (head(i), off // lw + g))
    tail_spec = lambda off: pl.BlockSpec(
        (SUBLANES, lw), lambda g, i: (jnp.maximum(head(i) * (rows // SUBLANES) - 1, 0), off // lw + g))
    w_spec = lambda off: pl.BlockSpec((GDN_CONV, lw), lambda g, i: (0, (off - GQ_OFF) // lw + g))
    return pl.pallas_call(
        functools.partial(_gdn_kernel, rows=rows, seq=GDN_CHUNK, carry=True, live_from=0),
        grid=(GDN_NG, nb + 1),
        in_specs=[x_spec(GQ_OFF), x_spec(GK_OFF), x_spec(GV_OFF),
                  tail_spec(GQ_OFF), tail_spec(GK_OFF), tail_spec(GV_OFF),
                  pl.BlockSpec((rows, lw), lambda g, i: (done(i), Z_OFF // lw + g)),
                  w_spec(GQ_OFF), w_spec(GK_OFF), w_spec(GV_OFF),
                  pl.BlockSpec((1, HEAD_DIM), lambda g, i: (0, 0)),
                  pl.BlockSpec((rows, LANES), lambda g, i: (head(i), 0)),
                  pl.BlockSpec((SUBLANES, LANES), lambda g, i: (0, 0))],
        out_specs=[pl.BlockSpec((rows, lw), lambda g, i: (done(i), g)),
                   pl.BlockSpec((GDN_HB, HEAD_DIM, HEAD_DIM), lambda g, i: (g, 0, 0))],
        out_shape=[jax.ShapeDtypeStruct((t, GDN_WIDTH), BF16),
                   jax.ShapeDtypeStruct((GDN_HEADS, HEAD_DIM, HEAD_DIM), F32)],
        scratch_shapes=_gdn_scratch(rows, 2),
        compiler_params=_cparams(2),
        name="gdn_prompt",
    )(proj, proj, proj, proj, proj, proj, proj, conv_w, conv_w, conv_w,
      gnorm_w.reshape(1, HEAD_DIM).astype(F32), ab, params)


def _gdn_sample(proj, ab, conv_state, state, conv_w, gnorm_w, params, n_live, rows):
    lw = GDN_HB * HEAD_DIM
    nseq = rows // SAMPLE_PAD
    n_in = nseq * n_live
    x_spec = lambda off: pl.BlockSpec((n_in, lw), lambda g, i: (i, off // lw + g))
    hist_spec = lambda off: pl.BlockSpec((nseq, GDN_CONV - 1, lw), lambda g, i: (i, 0, (off - GQ_OFF) // lw + g))
    w_spec = lambda off: pl.BlockSpec((GDN_CONV, lw), lambda g, i: (0, (off - GQ_OFF) // lw + g))
    s_spec = pl.BlockSpec((nseq, GDN_HB, HEAD_DIM, HEAD_DIM), lambda g, i: (i, g, 0, 0))
    return pl.pallas_call(
        functools.partial(_gdn_kernel, rows=rows, seq=SAMPLE_PAD, carry=False, live_from=SAMPLE_PAD - n_live),
        grid=(GDN_NG, proj.shape[0] // n_in),
        in_specs=[x_spec(GQ_OFF), x_spec(GK_OFF), x_spec(GV_OFF),
                  hist_spec(GQ_OFF), hist_spec(GK_OFF), hist_spec(GV_OFF),
                  x_spec(Z_OFF), w_spec(GQ_OFF), w_spec(GK_OFF), w_spec(GV_OFF),
                  pl.BlockSpec((1, HEAD_DIM), lambda g, i: (0, 0)),
                  pl.BlockSpec((n_in, LANES), lambda g, i: (i, 0)),
                  pl.BlockSpec((SUBLANES, LANES), lambda g, i: (0, 0)),
                  s_spec],
        out_specs=[pl.BlockSpec((n_in, lw), lambda g, i: (i, g)), s_spec],
        out_shape=[jax.ShapeDtypeStruct((proj.shape[0], GDN_WIDTH), F32),
                   jax.ShapeDtypeStruct(state.shape, F32)],
        scratch_shapes=[pltpu.VMEM((rows, lw), F32), pltpu.VMEM((rows, LANES), F32)] + _gdn_scratch(rows, 1),
        compiler_params=_cparams(2),
        name="gdn_sample",
    )(proj, proj, proj, conv_state, conv_state, conv_state, proj, conv_w, conv_w, conv_w,
      gnorm_w.reshape(1, HEAD_DIM).astype(F32), ab, params, state)


def _layer(x_prompt, x_sample, win_k, win_v, conv_state, gdn_state, rel_bias,
           w_in, sinks, conv_w, a_log, dt_bias, gnorm_w, w_out,
           n_pre_mix, n_post_mix, n_pre_ffn, n_post_ffn, w_gate, w_up, w_down, tiles):
    _, t, d = x_prompt.shape
    nb, ts, _ = x_sample.shape
    ms = nb * ts
    l = win_k.shape[1]
    ff = w_gate.shape[1]
    n_mt = tiles["n_mt"]
    xp, xs = x_prompt.reshape(t, d), x_sample.reshape(ms, d)

    w_in_t = w_in.T
    w_tail = jnp.pad(w_in_t[PROJ_MAIN:], ((0, LANES - AB_WIDTH), (0, 0))).astype(BF16)
    hp, ab_p = _rmsnorm_proj_tail(xp, n_pre_mix, w_tail)
    hs, ab_s = _rmsnorm_proj_tail(xs, n_pre_mix, w_tail)
    proj_p, proj_s = _proj([(hp, hs)], [(w_in_t, 0, d)], PROJ_MAIN, tiles["in_bn"], F32, n_mt, "in_proj",
                           w_transposed=True)

    qi = jnp.arange(WINDOW)[:, None]
    kj = jnp.arange(2 * WINDOW)[None, :]
    dist_p = WINDOW + qi - kj
    dist_s = l + jnp.arange(SUBLANES)[:, None] - kj
    dist_s = jnp.where(kj < l + ts, dist_s, -1)
    bias_all = _bias_table(rel_bias, jnp.concatenate([dist_p, dist_s], axis=0))
    attn_p = _attn_prompt(proj_p, bias_all[:, :WINDOW], sinks)

    b_s = bias_all[:, WINDOW:WINDOW + ts]
    own_kv = (jnp.arange(ATTN_HEADS)[:, None] // ATTN_GROUP) == jnp.arange(ATTN_KV_HEADS)[None, :]
    bias_c = jnp.where(own_kv[:, None, None, :], b_s[:, :, :l, None], NEG_INF)
    bias_c = bias_c.reshape(ATTN_HEADS * ts, l * ATTN_KV_HEADS)
    bias_n = jnp.where(own_kv[:, None, :, None], b_s[:, :, None, l:l + ts], NEG_INF)
    bias_n = bias_n.reshape(ATTN_HEADS * ts, ATTN_KV_HEADS * ts)
    sink_rows = jnp.repeat(sinks.astype(F32), ts).reshape(ATTN_HEADS * ts, 1)
    attn_s, new_k, new_v = _attn_sample(proj_s, win_k, win_v, bias_c, bias_n, sink_rows, ts, tiles["attn_gb"])

    params = _gdn_params(a_log, dt_bias)
    og_p, s_p = _gdn_prompt(proj_p, ab_p, conv_w, gnorm_w, params, tiles["gdn_rows"])
    og_s, s_s = _gdn_sample(proj_s, ab_s, conv_state.astype(F32), gdn_state, conv_w, gnorm_w, params, ts,
                            tiles["gdn_sample_rows"])

    mix_p, mix_s = _proj([(attn_p, attn_s), (og_p, og_s)], [(w_out, 0, ATTN_WIDTH), (w_out, 1, GDN_WIDTH)],
                         d, tiles["out_bn"], BF16, n_mt, "out_proj")
    x1_p, h2_p = _post_mix(xp, mix_p, n_post_mix, n_pre_ffn, tiles["post_rows"])
    x1_s, h2_s = _post_mix(xs, mix_s, n_post_mix, n_pre_ffn, tiles["sample_rows"])
    act_p, act_s = _proj([(h2_p, h2_s)], [(w_gate, 0, d), (w_up, 0, d)], ff, tiles["ff_bn"], BF16, n_mt,
                         "ffn_gate_up", swiglu=True)
    ffn_p, ffn_s = _down(act_p, act_s, w_down, tiles["ff_bk"], n_mt)
    y_p = _residual_norm(x1_p, ffn_p, n_post_ffn, tiles["post_rows"])
    y_s = _residual_norm(x1_s, ffn_s, n_post_ffn, tiles["sample_rows"])

    lw = min(WINDOW, t)
    prompt_k = proj_p[t - lw:, K_OFF:K_OFF + KV_WIDTH].reshape(1, lw, ATTN_KV_HEADS, HEAD_DIM)
    prompt_v = proj_p[t - lw:, V_OFF:V_OFF + KV_WIDTH].reshape(1, lw, ATTN_KV_HEADS, HEAD_DIM)
    prompt_conv = proj_p[t - (GDN_CONV - 1):, GQ_OFF:GQ_OFF + GDN_CONV_CH][None]
    sample_conv = proj_s[:, GQ_OFF:GQ_OFF + GDN_CONV_CH].reshape(nb, ts, GDN_CONV_CH)[:, ts - (GDN_CONV - 1):]
    return (y_p.reshape(1, t, d), y_s.reshape(nb, ts, d),
            prompt_k, prompt_v, prompt_conv, s_p[None],
            new_k.reshape(nb, l, ATTN_KV_HEADS, HEAD_DIM), new_v.reshape(nb, l, ATTN_KV_HEADS, HEAD_DIM),
            sample_conv, s_s)


def _tiles(t, ms):
    n_mt = 8 if (t % (8 * 16) == 0 and ms % (8 * 16) == 0) else 1
    return dict(n_mt=n_mt, in_bn=512, out_bn=512, ff_bn=512, ff_bk=512, norm_rows=512, post_rows=256,
                sample_rows=ms, gdn_rows=256, gdn_sample_rows=2 * GDN_CHUNK, attn_gb=8)


def kernel(x_prompt, x_sample, cache_win_k, cache_win_v, state_conv, state_gdn, rel_bias, w_in, attn_sinks,
           gdn_conv_w, gdn_a_log, gdn_dt_bias, gdn_norm_w, w_out, norm_pre_mix, norm_post_mix, norm_pre_ffn,
           norm_post_ffn, w_gate, w_up, w_down):
    depth = w_in.shape[0]
    assert depth == 1 and x_prompt.shape[0] == 1
    tiles = _tiles(x_prompt.shape[1], x_sample.shape[0] * x_sample.shape[1])
    outs = _layer(x_prompt, x_sample, cache_win_k[0], cache_win_v[0], state_conv[0], state_gdn[0], rel_bias,
                  w_in[0], attn_sinks[0], gdn_conv_w[0], gdn_a_log[0], gdn_dt_bias[0], gdn_norm_w[0], w_out[0],
                  norm_pre_mix[0], norm_post_mix[0], norm_pre_ffn[0], norm_post_ffn[0],
                  w_gate[0], w_up[0], w_down[0], tiles)
    yp, ys, pk, pv, pc, ps, sk, sv, sc, ss = outs
    return (yp, ys, pk[None], pv[None], pc[None], ps[None], sk[None], sv[None], sc[None], ss[None])
```

```python
import functools
import math

import jax
import jax.numpy as jnp
from jax import lax
from jax.experimental import pallas as pl
from jax.experimental.pallas import tpu as pltpu

F32 = jnp.float32
BF16 = jnp.bfloat16

HEAD_DIM = 128
ATTN_HEADS = 16
ATTN_KV_HEADS = 4
ATTN_GROUP = ATTN_HEADS // ATTN_KV_HEADS
ATTN_WIDTH = ATTN_HEADS * HEAD_DIM
KV_WIDTH = ATTN_KV_HEADS * HEAD_DIM
WINDOW = 128
REL_BUCKETS = 32
REL_MAX_DIST = 128
GDN_HEADS = 16
GDN_WIDTH = GDN_HEADS * HEAD_DIM
GDN_CONV = 4
GDN_CONV_CH = 3 * GDN_WIDTH
GDN_CHUNK = 64
NORM_EPS = 1e-6
NEG_INF = -1e30

Q_OFF = 0
K_OFF = ATTN_WIDTH
V_OFF = K_OFF + KV_WIDTH
GQ_OFF = V_OFF + KV_WIDTH
GK_OFF = GQ_OFF + GDN_WIDTH
GV_OFF = GK_OFF + GDN_WIDTH
Z_OFF = GV_OFF + GDN_WIDTH
PROJ_MAIN = Z_OFF + GDN_WIDTH
AB_WIDTH = 2 * GDN_HEADS

LANES = 128
SUBLANES = 8
GDN_HB = 8
GDN_NG = GDN_HEADS // GDN_HB
SAMPLE_PAD = 8
VMEM_LIMIT = 56 * 1024 * 1024


def _cparams(n_axes, vmem=VMEM_LIMIT):
    return pltpu.CompilerParams(dimension_semantics=("arbitrary",) * n_axes, vmem_limit_bytes=vmem)


def _dot(a, b):
    return jnp.dot(a, b, preferred_element_type=F32)


def _rms(x, w):
    return x * lax.rsqrt(jnp.mean(x * x, axis=-1, keepdims=True) + NORM_EPS) * w


def _rmsnorm_kernel(x_ref, w_ref, wt_ref, o_ref, tail_ref):
    h = _rms(x_ref[...].astype(F32), w_ref[...]).astype(o_ref.dtype)
    o_ref[...] = h
    tail_ref[...] = _dot_nt(h, wt_ref[...])


def _rmsnorm_proj_tail(x, w, w_tail):
    m, d = x.shape
    rows = min(m, 512)
    n = w_tail.shape[0]
    return pl.pallas_call(
        _rmsnorm_kernel,
        grid=(m // rows,),
        in_specs=[pl.BlockSpec((rows, d), lambda i: (i, 0)), pl.BlockSpec((1, d), lambda i: (0, 0)),
                  pl.BlockSpec((n, d), lambda i: (0, 0))],
        out_specs=[pl.BlockSpec((rows, d), lambda i: (i, 0)), pl.BlockSpec((rows, n), lambda i: (i, 0))],
        out_shape=[jax.ShapeDtypeStruct((m, d), BF16), jax.ShapeDtypeStruct((m, n), F32)],
        compiler_params=_cparams(1),
        name="rmsnorm",
    )(x, w.reshape(1, d).astype(F32), w_tail)


def _post_mix_kernel(x_ref, mix_ref, wpost_ref, wpre_ref, x1_ref, h2_ref):
    x1 = x_ref[...] + _rms(mix_ref[...].astype(F32), wpost_ref[...])
    x1_ref[...] = x1
    h2_ref[...] = _rms(x1, wpre_ref[...]).astype(h2_ref.dtype)


def _post_mix(x, mix, w_post, w_pre, rows):
    m, d = x.shape
    row_spec = pl.BlockSpec((rows, d), lambda i: (i, 0))
    w_spec = pl.BlockSpec((1, d), lambda i: (0, 0))
    return pl.pallas_call(
        _post_mix_kernel,
        grid=(m // rows,),
        in_specs=[row_spec, row_spec, w_spec, w_spec],
        out_specs=[row_spec, row_spec],
        out_shape=[jax.ShapeDtypeStruct((m, d), F32), jax.ShapeDtypeStruct((m, d), BF16)],
        compiler_params=_cparams(1),
        name="post_mix",
    )(x, mix, w_post.reshape(1, d).astype(F32), w_pre.reshape(1, d).astype(F32))


def _residual_norm_kernel(x_ref, y_ref, w_ref, o_ref):
    o_ref[...] = x_ref[...] + _rms(y_ref[...].astype(F32), w_ref[...])


def _residual_norm(x, y, w, rows):
    m, d = x.shape
    row_spec = pl.BlockSpec((rows, d), lambda i: (i, 0))
    return pl.pallas_call(
        _residual_norm_kernel,
        grid=(m // rows,),
        in_specs=[row_spec, row_spec, pl.BlockSpec((1, d), lambda i: (0, 0))],
        out_specs=row_spec,
        out_shape=jax.ShapeDtypeStruct((m, d), F32),
        compiler_params=_cparams(1),
        name="residual_norm",
    )(x, y, w.reshape(1, d).astype(F32))


def _stack_rows(p_ref, s_ref, cols=slice(None)):
    return jnp.concatenate([p_ref[:, cols].astype(BF16), s_ref[:, cols].astype(BF16)], axis=0)


def _dot_nt(a, b):
    return lax.dot_general(a, b, (((1,), (1,)), ((), ())), preferred_element_type=F32)


def _proj_kernel(*refs, n_a, n_w, swiglu, w_transposed, row_blocks, n_mt, n_tiles, last_cols):
    a_refs = refs[:2 * n_a]
    w_refs = refs[2 * n_a:2 * n_a + n_w]
    op_ref, os_ref = refs[2 * n_a + n_w:2 * n_a + n_w + 2]
    scratch = refs[2 * n_a + n_w + 2:]
    stage_refs, wb_refs, sem = scratch[:n_w], scratch[n_w:2 * n_w], scratch[2 * n_w]
    mm = _dot_nt if w_transposed else _dot
    j, i = pl.program_id(0), pl.program_id(1)
    nj = pl.num_programs(0)

    def slice_copy(w, tile, s, slot, narrow):
        rows, cols = stage_refs[w].shape[1:]
        if w_transposed:
            src = w_refs[w].at[pl.ds(tile * (rows * n_mt) + s * rows, rows), pl.ds(row_blocks[w] * cols, cols)]
            dst = stage_refs[w].at[slot]
        else:
            width = last_cols if narrow else cols
            src = w_refs[w].at[pl.ds((row_blocks[w] * n_mt + s) * rows, rows), pl.ds(tile * cols, width)]
            dst = stage_refs[w].at[slot, :, pl.ds(0, width)]
        return pltpu.make_async_copy(src, dst, sem.at[w, slot])

    def with_tile(tile, fn):
        if last_cols == stage_refs[0].shape[2] or w_transposed:
            fn(False)
        elif isinstance(tile, int):
            fn(tile == n_tiles - 1)
        else:
            pl.when(tile == n_tiles - 1)(lambda: fn(True))
            pl.when(tile != n_tiles - 1)(lambda: fn(False))

    def cast_slice(w, half, s, slot):
        rows = stage_refs[w].shape[1]
        wb_refs[w][half, pl.ds(pl.multiple_of(s * rows, rows), rows), :] = stage_refs[w][slot].astype(BF16)

    @pl.when(jnp.logical_and(j == 0, i == 0))
    def _():
        if last_cols != stage_refs[0].shape[2]:
            for w in range(n_w):
                stage_refs[w][...] = jnp.zeros_like(stage_refs[w])
        for w in range(n_w):
            for s in range(n_mt):
                def fetch_now(narrow, w=w, s=s):
                    copy = slice_copy(w, 0, s, s % 2, narrow)
                    copy.start()
                    copy.wait()
                with_tile(0, fetch_now)
                cast_slice(w, 0, s, s % 2)
        if n_tiles > 1:
            for w in range(n_w):
                with_tile(1, lambda narrow, w=w: slice_copy(w, 1, 0, 0, narrow).start())

    @pl.when(j + 1 < nj)
    def _():
        slot = (j * n_mt + i) % 2
        for w in range(n_w):
            with_tile(j + 1, lambda narrow, w=w: slice_copy(w, j + 1, i, slot, narrow).wait())
            cast_slice(w, (j + 1) % 2, i, slot)
        last = i + 1 == n_mt
        next_tile = jnp.where(last, j + 2, j + 1)
        next_s = jnp.where(last, 0, i + 1)

        @pl.when(next_tile < nj)
        def _():
            for w in range(n_w):
                with_tile(next_tile, lambda narrow, w=w: slice_copy(w, next_tile, next_s, 1 - slot, narrow).start())

    half = j % 2
    bmp = op_ref.shape[0]

    def apply(width):
        cols = slice(None) if w_transposed else slice(0, width)
        rhs = [wb_ref[half, :, cols] for wb_ref in wb_refs]
        lhs = [_stack_rows(a_refs[2 * p], a_refs[2 * p + 1]) for p in range(n_a)]
        if swiglu:
            g = mm(lhs[0], rhs[0])
            o = g * jax.nn.sigmoid(g) * mm(lhs[0], rhs[1])
        else:
            o = mm(lhs[0], rhs[0])
            for l, r in zip(lhs[1:], rhs[1:]):
                o = o + mm(l, r)
        op_ref[:, :width] = o[:bmp].astype(op_ref.dtype)
        os_ref[:, :width] = o[bmp:].astype(os_ref.dtype)

    full_cols = op_ref.shape[1]
    if last_cols == full_cols:
        apply(full_cols)
    else:
        pl.when(j != n_tiles - 1)(lambda: apply(full_cols))
        pl.when(j == n_tiles - 1)(lambda: apply(last_cols))


def _proj(a_pairs, w_blocks, n, bn, out_dtype, n_mt, name, swiglu=False, w_transposed=False):
    tp, ts = a_pairs[0][0].shape[0], a_pairs[0][1].shape[0]
    bmp, bms = tp // n_mt, ts // n_mt
    in_specs = []
    for a_p, a_s in a_pairs:
        in_specs.append(pl.BlockSpec((bmp, a_p.shape[1]), lambda j, i: (i, 0)))
        in_specs.append(pl.BlockSpec((bms, a_s.shape[1]), lambda j, i: (i, 0)))
    in_specs += [pl.BlockSpec(memory_space=pl.ANY) for _ in w_blocks]
    n_tiles = pl.cdiv(n, bn)
    tile = lambda k: (bn, k) if w_transposed else (k, bn)
    stage = lambda k: (2, bn // n_mt, k) if w_transposed else (2, k // n_mt, bn)
    operands = [a for pair in a_pairs for a in pair] + [w for w, _, _ in w_blocks]
    return pl.pallas_call(
        functools.partial(_proj_kernel, n_a=len(a_pairs), n_w=len(w_blocks), swiglu=swiglu,
                          w_transposed=w_transposed, row_blocks=tuple(rb for _, rb, _ in w_blocks), n_mt=n_mt,
                          n_tiles=n_tiles, last_cols=n - (n_tiles - 1) * bn),
        grid=(n_tiles, n_mt),
        in_specs=in_specs,
        out_specs=[pl.BlockSpec((bmp, bn), lambda j, i: (i, j)), pl.BlockSpec((bms, bn), lambda j, i: (i, j))],
        out_shape=[jax.ShapeDtypeStruct((tp, n), out_dtype), jax.ShapeDtypeStruct((ts, n), out_dtype)],
        scratch_shapes=([pltpu.VMEM(stage(k), F32) for _, _, k in w_blocks]
                        + [pltpu.VMEM((2,) + tile(k), BF16) for _, _, k in w_blocks]
                        + [pltpu.SemaphoreType.DMA((len(w_blocks), 2))]),
        compiler_params=_cparams(2),
        name=name,
    )(*operands)


def _down_kernel(ap_ref, as_ref, b_ref, op_ref, os_ref, accp_ref, accs_ref, *, bn, k_total):
    k = pl.program_id(1)
    nk = pl.num_programs(1)
    bk = b_ref.shape[0]
    bmp = ap_ref.shape[0]
    last_rows = k_total - (k_total // bk) * bk

    @pl.when(k == 0)
    def _():
        accp_ref[...] = jnp.zeros_like(accp_ref)
        accs_ref[...] = jnp.zeros_like(accs_ref)

    def accumulate(kk):
        a = _stack_rows(ap_ref, as_ref, slice(0, kk))
        for n0 in range(0, accp_ref.shape[1], bn):
            d = _dot(a, b_ref[:kk, n0:n0 + bn].astype(BF16))
            accp_ref[:, n0:n0 + bn] += d[:bmp]
            accs_ref[:, n0:n0 + bn] += d[bmp:]

    if last_rows == 0:
        accumulate(bk)
    else:
        pl.when(k < nk - 1)(lambda: accumulate(bk))
        pl.when(k == nk - 1)(lambda: accumulate(last_rows))

    @pl.when(k == nk - 1)
    def _():
        op_ref[...] = accp_ref[...].astype(op_ref.dtype)
        os_ref[...] = accs_ref[...].astype(os_ref.dtype)


def _down(a_p, a_s, b, bk, n_mt):
    (tp, k_total), ts = a_p.shape, a_s.shape[0]
    n = b.shape[1]
    bmp, bms = tp // n_mt, ts // n_mt
    return pl.pallas_call(
        functools.partial(_down_kernel, bn=min(n, 512), k_total=k_total),
        grid=(n_mt, pl.cdiv(k_total, bk)),
        in_specs=[pl.BlockSpec((bmp, bk), lambda i, k: (i, k)), pl.BlockSpec((bms, bk), lambda i, k: (i, k)),
                  pl.BlockSpec((bk, n), lambda i, k: (k, 0))],
        out_specs=[pl.BlockSpec((bmp, n), lambda i, k: (i, 0), pipeline_mode=pl.Buffered(1)),
                   pl.BlockSpec((bms, n), lambda i, k: (i, 0))],
        out_shape=[jax.ShapeDtypeStruct((tp, n), BF16), jax.ShapeDtypeStruct((ts, n), BF16)],
        scratch_shapes=[pltpu.VMEM((bmp, n), F32), pltpu.VMEM((bms, n), F32)],
        compiler_params=_cparams(2),
        name="ffn_down",
    )(a_p, a_s, b)


def _rel_bucket(d):
    n = jnp.maximum(d, 0)
    max_exact = REL_BUCKETS // 2
    nf = jnp.maximum(n, 1).astype(F32)
    large = max_exact + (jnp.log(nf / max_exact) / math.log(REL_MAX_DIST / max_exact)
                         * (REL_BUCKETS - max_exact)).astype(jnp.int32)
    large = jnp.minimum(large, REL_BUCKETS - 1)
    return jnp.where(n < max_exact, n, large)


def _bias_kernel(rel_ref, bucket_ref, allowed_ref, o_ref):
    h = pl.program_id(0)
    bucket = bucket_ref[...]
    acc = jnp.zeros(bucket.shape, F32)
    for n in range(REL_BUCKETS):
        acc = jnp.where(bucket == n, rel_ref[n, h], acc)
    o_ref[0] = jnp.where(allowed_ref[...] > 0, acc, NEG_INF)


def _bias_table(rel_bias, dist):
    tq, tk = dist.shape
    allowed = ((dist >= 0) & (dist <= WINDOW)).astype(jnp.int32)
    full = pl.BlockSpec((tq, tk), lambda h: (0, 0))
    return pl.pallas_call(
        _bias_kernel,
        grid=(ATTN_HEADS,),
        in_specs=[pl.BlockSpec(memory_space=pltpu.SMEM), full, full],
        out_specs=pl.BlockSpec((1, tq, tk), lambda h: (h, 0, 0)),
        out_shape=jax.ShapeDtypeStruct((ATTN_HEADS, tq, tk), F32),
        compiler_params=_cparams(1),
        name="rel_bias_table",
    )(rel_bias.astype(F32), _rel_bucket(dist), allowed)


def _softmax_pv(s, sink):
    m = jnp.maximum(jnp.max(s, axis=-1, keepdims=True), sink)
    p = jnp.exp(s - m)
    denom = jnp.sum(p, axis=-1, keepdims=True) + jnp.exp(sink - m)
    return p.astype(BF16), denom


def _attn_prompt_kernel(sink_ref, cur_ref, prev_ref, bias_ref, o_ref):
    i = pl.program_id(0)
    c = WINDOW
    col = lax.broadcasted_iota(jnp.int32, (c, 2 * c), 1)
    no_prev = jnp.logical_and(i == 0, col < c)
    head_cols = lambda base, h: slice(base + h * HEAD_DIM, base + (h + 1) * HEAD_DIM)
    kk, vv = [], []
    for h in range(ATTN_KV_HEADS):
        kk.append(jnp.concatenate([prev_ref[:, head_cols(0, h)], cur_ref[:, head_cols(K_OFF, h)]],
                                  axis=0).astype(BF16))
        vv.append(jnp.concatenate([prev_ref[:, head_cols(KV_WIDTH, h)], cur_ref[:, head_cols(V_OFF, h)]],
                                  axis=0).astype(BF16))
    heads = range(ATTN_HEADS)
    s = [_dot_nt(cur_ref[:, head_cols(Q_OFF, hd)].astype(BF16), kk[hd // ATTN_GROUP]) for hd in heads]
    pd = []
    for hd in heads:
        sc = jnp.where(no_prev, NEG_INF, s[hd] * (HEAD_DIM ** -0.5) + bias_ref[hd])
        pd.append(_softmax_pv(sc, sink_ref[hd]))
    outs = [_dot(pd[hd][0], vv[hd // ATTN_GROUP]) / pd[hd][1] for hd in heads]
    o_ref[...] = jnp.concatenate(outs, axis=1).astype(o_ref.dtype)


def _attn_prompt(proj, bias, sinks):
    t = proj.shape[0]
    c = WINDOW
    qkv = V_OFF + KV_WIDTH
    return pl.pallas_call(
        _attn_prompt_kernel,
        grid=(t // c,),
        in_specs=[pl.BlockSpec(memory_space=pltpu.SMEM),
                  pl.BlockSpec((c, qkv), lambda i: (i, 0)),
                  pl.BlockSpec((c, 2 * KV_WIDTH), lambda i: (jnp.maximum(i - 1, 0), K_OFF // (2 * KV_WIDTH))),
                  pl.BlockSpec((ATTN_HEADS, c, 2 * c), lambda i: (0, 0, 0))],
        out_specs=pl.BlockSpec((c, ATTN_WIDTH), lambda i: (i, 0)),
        out_shape=jax.ShapeDtypeStruct((t, ATTN_WIDTH), BF16),
        compiler_params=_cparams(1),
        name="attn_prompt",
    )(sinks.astype(F32), proj, proj, bias)


def _attn_sample_kernel(x_ref, ck_ref, cv_ref, bias_c_ref, bias_n_ref, sink_ref, o_ref, ok_ref, ov_ref,
                        q_s, kn_s, vn_s, *, t_new):
    gb, l, kvh, hd = ck_ref.shape
    kc = ck_ref[...].reshape(gb, l * kvh, hd)
    vc = cv_ref[...].reshape(gb, l * kvh, hd)
    for b in range(gb):
        rows = slice(b * t_new, (b + 1) * t_new)
        for h in range(kvh):
            kn_s[b, h * t_new:(h + 1) * t_new, :] = x_ref[rows, K_OFF + h * hd:K_OFF + (h + 1) * hd]
            vn_s[b, h * t_new:(h + 1) * t_new, :] = x_ref[rows, V_OFF + h * hd:V_OFF + (h + 1) * hd]
        for head in range(ATTN_HEADS):
            q_s[b, head * t_new:(head + 1) * t_new, :] = x_ref[rows, head * hd:(head + 1) * hd]

    q = q_s[...].astype(BF16)
    scale = hd ** -0.5
    s_c = jnp.einsum("bqd,bkd->bqk", q, kc.astype(BF16), preferred_element_type=F32) * scale + bias_c_ref[...]
    s_n = jnp.einsum("bqd,bkd->bqk", q, kn_s[...].astype(BF16), preferred_element_type=F32) * scale + bias_n_ref[...]
    sink = sink_ref[...]
    m = jnp.maximum(jnp.maximum(jnp.max(s_c, axis=-1, keepdims=True), jnp.max(s_n, axis=-1, keepdims=True)), sink)
    p_c = jnp.exp(s_c - m)
    p_n = jnp.exp(s_n - m)
    denom = jnp.sum(p_c, axis=-1, keepdims=True) + jnp.sum(p_n, axis=-1, keepdims=True) + jnp.exp(sink - m)
    o = (jnp.einsum("bqk,bkd->bqd", p_c.astype(BF16), vc.astype(BF16), preferred_element_type=F32)
         + jnp.einsum("bqk,bkd->bqd", p_n.astype(BF16), vn_s[...].astype(BF16), preferred_element_type=F32)) / denom
    for b in range(gb):
        for head in range(ATTN_HEADS):
            o_ref[b * t_new:(b + 1) * t_new, head * hd:(head + 1) * hd] = o[b, head * t_new:(head + 1) * t_new, :]

    shift = t_new * kvh
    for c_val, n_s, out_ref in ((kc, kn_s, ok_ref), (vc, vn_s, ov_ref)):
        new_rows = [n_s[:, h * t_new + t:h * t_new + t + 1, :] for t in range(t_new) for h in range(kvh)]
        win = jnp.concatenate([c_val[:, shift:, :]] + new_rows, axis=1)
        out_ref[...] = win.reshape(gb, l, kvh, hd)


def _attn_sample(proj, cache_k, cache_v, bias_c, bias_n, sink_rows, t_new, gb):
    b, l = cache_k.shape[:2]
    rows = ATTN_HEADS * t_new
    qkv = V_OFF + KV_WIDTH
    cache_spec = pl.BlockSpec((gb, l, ATTN_KV_HEADS, HEAD_DIM), lambda i: (i, 0, 0, 0))
    full = lambda a: pl.BlockSpec(a.shape, lambda i: (0,) * a.ndim)
    return pl.pallas_call(
        functools.partial(_attn_sample_kernel, t_new=t_new),
        grid=(b // gb,),
        in_specs=[pl.BlockSpec((gb * t_new, qkv), lambda i: (i, 0)), cache_spec, cache_spec,
                  full(bias_c), full(bias_n), full(sink_rows)],
        out_specs=[pl.BlockSpec((gb * t_new, ATTN_WIDTH), lambda i: (i, 0)), cache_spec, cache_spec],
        out_shape=[jax.ShapeDtypeStruct((b * t_new, ATTN_WIDTH), F32),
                   jax.ShapeDtypeStruct(cache_k.shape, F32),
                   jax.ShapeDtypeStruct(cache_v.shape, F32)],
        scratch_shapes=[pltpu.VMEM((gb, rows, HEAD_DIM), F32),
                        pltpu.VMEM((gb, ATTN_KV_HEADS * t_new, HEAD_DIM), F32),
                        pltpu.VMEM((gb, ATTN_KV_HEADS * t_new, HEAD_DIM), F32)],
        compiler_params=_cparams(1),
        name="attn_sample",
    )(proj, cache_k, cache_v, bias_c, bias_n, sink_rows)


def _split3(x):
    hi = x.astype(BF16)
    r = x - hi.astype(F32)
    mid = r.astype(BF16)
    lo = (r - mid.astype(F32)).astype(BF16)
    return hi, mid, lo


def _select_dot(sel, x):
    hi, mid, lo = _split3(x)
    return _dot(sel, hi) + _dot(sel, mid) + _dot(sel, lo)


def _dot_select(x, sel):
    hi, mid, lo = _split3(x)
    return _dot(hi, sel) + _dot(mid, sel) + _dot(lo, sel)


def _softplus(x):
    return jnp.maximum(x, 0.0) + jnp.log1p(jnp.exp(-jnp.abs(x)))


def _conv_silu(pad_ref, w, r0, n):
    top = SUBLANES + r0
    y = pad_ref[top:top + n, :] * w[GDN_CONV - 1:GDN_CONV]
    for s in range(1, GDN_CONV):
        y = y + pad_ref[top - s:top - s + n, :] * w[GDN_CONV - 1 - s:GDN_CONV - s]
    return y * jax.nn.sigmoid(y)


def _l2norm_heads(x, scale):
    outs = []
    for hh in range(GDN_HB):
        xh = x[:, hh * HEAD_DIM:(hh + 1) * HEAD_DIM]
        outs.append(xh * (lax.rsqrt(jnp.sum(xh * xh, axis=-1, keepdims=True) + NORM_EPS) * scale))
    return jnp.concatenate(outs, axis=1)


GDN_N_PADS = 4
STAGE_ROWS = 32
PROBE_BOUND = 1e30
GDN_N_STAGED = 10


def _gdn_kernel(*refs, rows, seq, carry, live_from):
    kw = dict(rows=rows, seq=seq, carry=carry, live_from=live_from)
    if not carry:
        io, pads, staged = refs[:-GDN_N_PADS - GDN_N_STAGED], refs[-GDN_N_PADS - GDN_N_STAGED:-GDN_N_STAGED], \
            refs[-GDN_N_STAGED:]
        _gdn_body(io, pads, staged, staged, **kw)
        return
    n_scratch = GDN_N_PADS + 2 * GDN_N_STAGED
    io, pads = refs[:-n_scratch], refs[-n_scratch:-2 * GDN_N_STAGED]
    set_a, set_b = refs[-2 * GDN_N_STAGED:-GDN_N_STAGED], refs[-GDN_N_STAGED:]
    s_ref = io[-1]
    step = pl.program_id(1)

    @pl.when(step == 0)
    def _():
        s_ref[...] = jnp.zeros_like(s_ref)
        for ref in set_b:
            ref[...] = jnp.zeros_like(ref)

    pl.when(step % 2 == 0)(lambda: _gdn_body(io, pads, set_a, set_b, **kw))
    pl.when(step % 2 == 1)(lambda: _gdn_body(io, pads, set_b, set_a, **kw))


def _gdn_body(refs, pads, staged_w, staged_r, *, rows, seq, carry, live_from):
    c = GDN_CHUNK
    nc = rows // c
    nseq = c // seq
    n_live = seq - live_from
    lw = GDN_HB * HEAD_DIM
    if carry:
        (xq_ref, xk_ref, xv_ref, tq_ref, tk_ref, tv_ref, z_ref, wq_ref, wk_ref, wv_ref, gn_ref,
         ab_ref, pc_ref, og_ref, s_ref) = refs
        s0_ref = None
    else:
        (xq_ref, xk_ref, xv_ref, cq_ref, ck_ref, cv_ref, z_in, wq_ref, wk_ref, wv_ref, gn_ref,
         ab_in, pc_ref, s0_ref, og_ref, s_ref, z_ref, ab_ref) = refs
    padq_s, padk_s, padv_s, eb_s = pads
    kn_w, kb_w, qn_w, vb_w, kbe_w, qe_w, kd_w, egc_w, egl_w, gcr_w = staged_w
    kn_s, kb_s, qn_s, vb_s, kbe_s, qe_s, kd_s, egc_s, egl_s, gcr_s = staged_r

    group = pl.program_id(0)
    step = pl.program_id(1)
    row = lax.broadcasted_iota(jnp.int32, (rows, 1), 0)
    live = (row % seq) >= live_from
    keep_live = (lambda x: jnp.where(live, x, 0.0)) if live_from else (lambda x: x)

    if carry:
        for pad_ref, x_ref, t_ref in ((padq_s, xq_ref, tq_ref), (padk_s, xk_ref, tk_ref), (padv_s, xv_ref, tv_ref)):
            pad_ref[0:SUBLANES, :] = jnp.where(step == 0, 0.0, t_ref[...])
            pad_ref[SUBLANES:, :] = x_ref[...]
    else:
        def gather(dst_ref, x_ref, hist_ref, top):
            dst_ref[...] = jnp.zeros_like(dst_ref)
            for b in range(rows // seq):
                r0 = top + b * seq + live_from
                if hist_ref is not None:
                    dst_ref[r0 - (GDN_CONV - 1):r0, :] = hist_ref[b]
                dst_ref[r0:r0 + n_live, :] = x_ref[b * n_live:(b + 1) * n_live, :]

        gather(padq_s, xq_ref, cq_ref, SUBLANES)
        gather(padk_s, xk_ref, ck_ref, SUBLANES)
        gather(padv_s, xv_ref, cv_ref, SUBLANES)
        gather(z_ref, z_in, None, 0)
        gather(ab_ref, ab_in, None, 0)

    def stage_decays():
        ab = ab_ref[...]
        g_col = keep_live(-jnp.exp(pc_ref[0:1, :]) * _softplus(ab + pc_ref[1:2, :]))
        beta_col = keep_live(jax.nn.sigmoid(ab))
        ri = lax.broadcasted_iota(jnp.int32, (rows, rows), 0)
        ci = lax.broadcasted_iota(jnp.int32, (rows, rows), 1)
        same = (ri // seq) == (ci // seq)
        cum_sel = jnp.where(jnp.logical_and(same, ci <= ri), 1.0, 0.0).astype(BF16)
        tot_sel = jnp.where(same, 1.0, 0.0).astype(BF16)
        gc_col = _select_dot(cum_sel, g_col)
        gl_col = _select_dot(tot_sel, g_col)
        first = group * GDN_HB
        li = lax.broadcasted_iota(jnp.int32, (LANES, lw), 0)
        ni = lax.broadcasted_iota(jnp.int32, (LANES, lw), 1)
        spread_g = jnp.where(li == ni // HEAD_DIM + first, 1.0, 0.0).astype(BF16)
        spread_b = jnp.where(li == ni // HEAD_DIM + first + GDN_HEADS, 1.0, 0.0).astype(BF16)
        egc_w[...] = _dot_select(gc_col, spread_g)
        egl_w[...] = _dot_select(gl_col, spread_g)
        eb_s[...] = _dot_select(beta_col, spread_b)
        pi = lax.broadcasted_iota(jnp.int32, (LANES, LANES), 0)
        pj = lax.broadcasted_iota(jnp.int32, (LANES, LANES), 1)
        pick = jnp.where(jnp.logical_and(pi == pj + first, pj < GDN_HB), 1.0, 0.0).astype(BF16)
        gc_heads = _dot_select(gc_col, pick)
        if rows % LANES:
            gc_heads = jnp.concatenate([gc_heads, jnp.zeros((LANES - rows % LANES, LANES), F32)], axis=0)
        gcr_w[...] = gc_heads.T[:SUBLANES, :rows]

    def stage_rows(r0, n, after=None):
        sl = slice(r0, r0 + n)
        keep = (lambda x: jnp.where(live[sl], x, 0.0)) if live_from else (lambda x: x)
        gate = 1.0 if after is None else jnp.where(after > PROBE_BOUND, 0.0, 1.0)
        conv = lambda pad_ref, w_ref: keep(_conv_silu(pad_ref, w_ref[...] * gate, r0, n))
        qn = _l2norm_heads(conv(padq_s, wq_ref), HEAD_DIM ** -0.5)
        kn = _l2norm_heads(conv(padk_s, wk_ref), 1.0)
        v = conv(padv_s, wv_ref)
        egc, egl, eb = egc_w[sl, :], egl_w[sl, :], eb_s[sl, :]
        decay_in = jnp.exp(egc)
        kb = kn * eb
        kn_w[sl, :] = kn
        kb_w[sl, :] = kb
        qn_w[sl, :] = qn
        vb_w[sl, :] = v * eb
        kbe_w[sl, :] = kb * decay_in
        qe_w[sl, :] = qn * decay_in
        kd_w[sl, :] = kn * jnp.exp(egl - egc)

    n_slices = rows // STAGE_ROWS if carry else 1
    pending = [functools.partial(stage_rows, r * (rows // n_slices), rows // n_slices) for r in range(n_slices)]
    if not carry:
        stage_decays()
        pending.pop(0)()

    cr = lax.broadcasted_iota(jnp.int32, (c, c), 0)
    cc = lax.broadcasted_iota(jnp.int32, (c, c), 1)
    same_c = (cr // seq) == (cc // seq)
    causal_bias = jnp.where(jnp.logical_and(same_c, cr >= cc), 0.0, NEG_INF)
    strict = jnp.where(jnp.logical_and(same_c, cr > cc), 1.0, 0.0)

    eye = jnp.where(cr == cc, 1.0, 0.0)
    seq_of_row = lax.broadcasted_iota(jnp.int32, (c, HEAD_DIM), 0) // seq

    def lower_left(s):
        return jnp.logical_and(jnp.logical_and(cr // (2 * s) == cc // (2 * s), (cr // s) % 2 == 1),
                               (cc // s) % 2 == 0)

    merge_sizes = [2 ** e for e in range(1, max(1, math.ceil(math.log2(n_live))))]
    pair_mask = jnp.where(lower_left(1), 1.0, 0.0)
    merge_masks = [jnp.where(lower_left(s), 1.0, 0.0) for s in merge_sizes]

    pairs = [(ch, hh) for ch in range(nc) for hh in range(GDN_HB)]
    rows_of = lambda ch: slice(ch * c, (ch + 1) * c)
    lanes_of = lambda hh: slice(hh * HEAD_DIM, (hh + 1) * HEAD_DIM)
    uw_all, qk_all, a_mat, t_mat, t_a = {}, {}, {}, {}, {}
    for ch, hh in pairs:
        rs, hs = rows_of(ch), lanes_of(hh)
        gcol = egc_s[rs, hh * HEAD_DIM:hh * HEAD_DIM + c]
        grow = gcr_s[hh:hh + 1, rs]
        decay = jnp.exp(gcol - grow + causal_bias)
        kq = jnp.concatenate([kb_s[rs, hs], qn_s[rs, hs]], axis=0).astype(BF16)
        kk = _dot_nt(kq, kn_s[rs, hs].astype(BF16)) * jnp.concatenate([decay, decay], axis=0)
        a_mat[ch, hh] = kk[:c] * strict
        qk_all[ch, hh] = kk[c:].astype(BF16)
        t_mat[ch, hh] = eye - a_mat[ch, hh] * pair_mask
    probes = [a_mat[pairs[-1]][0:1, 0:1]]
    if carry:
        stage_decays()
    for mask in merge_masks:
        for p in pairs:
            t_a[p] = _dot(t_mat[p].astype(BF16), (a_mat[p] * mask).astype(BF16))
        for p in pairs:
            t_mat[p] = t_mat[p] - _dot(t_a[p].astype(BF16), t_mat[p].astype(BF16))
        probes.append(t_mat[pairs[-1]][0:1, 0:1])
    for ch, hh in pairs:
        rs, hs = rows_of(ch), lanes_of(hh)
        uw_all[ch, hh] = _dot(t_mat[ch, hh].astype(BF16),
                              jnp.concatenate([vb_s[rs, hs], kbe_s[rs, hs]], axis=1).astype(BF16))

    if carry:
        state = [s_ref[hh] for hh in range(GDN_HB)]
    new_states = {}
    heads = range(GDN_HB)
    tn_dot = lambda a, b: lax.dot_general(a, b, (((0,), (0,)), ((), ())), preferred_element_type=F32)
    for ch in range(nc):
        rs = rows_of(ch)
        w_s, q_s, v_new_b, o = {}, {}, {}, {}
        for hh in heads:
            w = uw_all[ch, hh][:, HEAD_DIM:]
            q_e = qe_s[rs, lanes_of(hh)]
            if carry:
                wq = _dot(jnp.concatenate([w, q_e], axis=0).astype(BF16), state[hh].astype(BF16))
                w_s[hh], q_s[hh] = wq[:c], wq[c:]
            else:
                parts = []
                for b in range(nseq):
                    bs = slice(b * seq, (b + 1) * seq)
                    parts.append(_dot(jnp.concatenate([w[bs], q_e[bs]], axis=0).astype(BF16),
                                      s0_ref[ch * nseq + b, hh].astype(BF16)))
                w_s[hh] = jnp.concatenate([p[:seq] for p in parts], axis=0)
                q_s[hh] = jnp.concatenate([p[seq:] for p in parts], axis=0)
        for hh in heads:
            v_new_b[hh] = (uw_all[ch, hh][:, :HEAD_DIM] - w_s[hh]).astype(BF16)
            o[hh] = q_s[hh] + _dot(qk_all[ch, hh], v_new_b[hh])
        for hh in heads:
            hs = lanes_of(hh)
            k_d = kd_s[rs, hs]
            if carry:
                state[hh] = (state[hh] * jnp.exp(egl_s[ch * c:ch * c + 1, hs])
                             + tn_dot(k_d.astype(BF16), v_new_b[hh]))
            else:
                for b in range(nseq):
                    k_db = jnp.where(seq_of_row == b, k_d, 0.0).astype(BF16)
                    r1 = ch * c + b * seq
                    new_states[ch * nseq + b, hh] = (
                        s0_ref[ch * nseq + b, hh] * jnp.exp(egl_s[r1:r1 + 1, hs]) + tn_dot(k_db, v_new_b[hh]))
        if carry:
            probes.append(_dot(qk_all[ch, 0], jnp.ones((c, HEAD_DIM), BF16))[0:1, 0:1])
        outs = []
        for hh in heads:
            z = z_ref[rs, lanes_of(hh)]
            outs.append(_rms(o[hh], gn_ref[...]) * (z * jax.nn.sigmoid(z)))
        og = jnp.concatenate(outs, axis=1)
        if carry:
            og_ref[rs, :] = og.astype(og_ref.dtype)
        else:
            for b in range(nseq):
                r0 = b * seq + live_from
                og_ref[(ch * nseq + b) * n_live:(ch * nseq + b + 1) * n_live, :] = (
                    og[r0:r0 + n_live].astype(og_ref.dtype))
    for n, stage in enumerate(pending):
        stage(after=probes[min(n, len(probes) - 1)])
    if carry:
        s_ref[...] = jnp.stack(state, axis=0)
    else:
        for (b, hh), s_new in new_states.items():
            s_ref[b, hh] = s_new


def _gdn_params(a_log, dt_bias):
    rows = jnp.stack([a_log.astype(F32), dt_bias.astype(F32)], axis=0)
    return jnp.pad(rows, ((0, SUBLANES - 2), (0, LANES - GDN_HEADS)))


def _gdn_scratch(rows, n_sets):
    lw = GDN_HB * HEAD_DIM
    staged = [(rows, lw)] * (GDN_N_STAGED - 1) + [(SUBLANES, rows)]
    return ([pltpu.VMEM((SUBLANES + rows, lw), F32) for _ in range(GDN_N_PADS - 1)] + [pltpu.VMEM((rows, lw), F32)]
            + [pltpu.VMEM(shape, F32) for _ in range(n_sets) for shape in staged])


def _gdn_prompt(proj, ab, conv_w, gnorm_w, params, rows):
    t = proj.shape[0]
    nb = t // rows
    lw = GDN_HB * HEAD_DIM
    head = lambda i: jnp.minimum(i, nb - 1)
    done = lambda i: jnp.maximum(i - 1, 0)
    x_spec = lambda off: pl.BlockSpec((rows, lw), lambda g, i: (head(i), off // lw + g))
    tail_spec = lambda off: pl.BlockSpec(
        (SUBLANES, lw), lambda g, i: (jnp.maximum(head(i) * (rows // SUBLANES) - 1, 0), off // lw + g))
    w_spec = lambda off: pl.BlockSpec((GDN_CONV, lw), lambda g, i: (0, (off - GQ_OFF) // lw + g))
    return pl.pallas_call(
        functools.partial(_gdn_kernel, rows=rows, seq=GDN_CHUNK, carry=True, live_from=0),
        grid=(GDN_NG, nb + 1),
        in_specs=[x_spec(GQ_OFF), x_spec(GK_OFF), x_spec(GV_OFF),
                  tail_spec(GQ_OFF), tail_spec(GK_OFF), tail_spec(GV_OFF),
                  pl.BlockSpec((rows, lw), lambda g, i: (done(i), Z_OFF // lw + g)),
                  w_spec(GQ_OFF), w_spec(GK_OFF), w_spec(GV_OFF),
                  pl.BlockSpec((1, HEAD_DIM), lambda g, i: (0, 0)),
                  pl.BlockSpec((rows, LANES), lambda g, i: (head(i), 0)),
                  pl.BlockSpec((SUBLANES, LANES), lambda g, i: (0, 0))],
        out_specs=[pl.BlockSpec((rows, lw), lambda g, i: (done(i), g)),
                   pl.BlockSpec((GDN_HB, HEAD_DIM, HEAD_DIM), lambda g, i: (g, 0, 0))],
        out_shape=[jax.ShapeDtypeStruct((t, GDN_WIDTH), BF16),
                   jax.ShapeDtypeStruct((GDN_HEADS, HEAD_DIM, HEAD_DIM), F32)],
        scratch_shapes=_gdn_scratch(rows, 2),
        compiler_params=_cparams(2),
        name="gdn_prompt",
    )(proj, proj, proj, proj, proj, proj, proj, conv_w, conv_w, conv_w,
      gnorm_w.reshape(1, HEAD_DIM).astype(F32), ab, params)


def _gdn_sample(proj, ab, conv_state, state, conv_w, gnorm_w, params, n_live, rows):
    lw = GDN_HB * HEAD_DIM
    nseq = rows // SAMPLE_PAD
    n_in = nseq * n_live
    x_spec = lambda off: pl.BlockSpec((n_in, lw), lambda g, i: (i, off // lw + g))
    hist_spec = lambda off: pl.BlockSpec((nseq, GDN_CONV - 1, lw), lambda g, i: (i, 0, (off - GQ_OFF) // lw + g))
    w_spec = lambda off: pl.BlockSpec((GDN_CONV, lw), lambda g, i: (0, (off - GQ_OFF) // lw + g))
    s_spec = pl.BlockSpec((nseq, GDN_HB, HEAD_DIM, HEAD_DIM), lambda g, i: (i, g, 0, 0))
    return pl.pallas_call(
        functools.partial(_gdn_kernel, rows=rows, seq=SAMPLE_PAD, carry=False, live_from=SAMPLE_PAD - n_live),
        grid=(GDN_NG, proj.shape[0] // n_in),
        in_specs=[x_spec(GQ_OFF), x_spec(GK_OFF), x_spec(GV_OFF),
                  hist_spec(GQ_OFF), hist_spec(GK_OFF), hist_spec(GV_OFF),
                  x_spec(Z_OFF), w_spec(GQ_OFF), w_spec(GK_OFF), w_spec(GV_OFF),
                  pl.BlockSpec((1, HEAD_DIM), lambda g, i: (0, 0)),
                  pl.BlockSpec((n_in, LANES), lambda g, i: (i, 0)),
                  pl.BlockSpec((SUBLANES, LANES), lambda g, i: (0, 0)),
                  s_spec],
        out_specs=[pl.BlockSpec((n_in, lw), lambda g, i: (i, g)), s_spec],
        out_shape=[jax.ShapeDtypeStruct((proj.shape[0], GDN_WIDTH), F32),
                   jax.ShapeDtypeStruct(state.shape, F32)],
        scratch_shapes=[pltpu.VMEM((rows, lw), F32), pltpu.VMEM((rows, LANES), F32)] + _gdn_scratch(rows, 1),
        compiler_params=_cparams(2),
        name="gdn_sample",
    )(proj, proj, proj, conv_state, conv_state, conv_state, proj, conv_w, conv_w, conv_w,
      gnorm_w.reshape(1, HEAD_DIM).astype(F32), ab, params, state)


def _layer(x_prompt, x_sample, win_k, win_v, conv_state, gdn_state, rel_bias,
           w_in, sinks, conv_w, a_log, dt_bias, gnorm_w, w_out,
           n_pre_mix, n_post_mix, n_pre_ffn, n_post_ffn, w_gate, w_up, w_down, tiles):
    _, t, d = x_prompt.shape
    nb, ts, _ = x_sample.shape
    ms = nb * ts
    l = win_k.shape[1]
    ff = w_gate.shape[1]
    n_mt = tiles["n_mt"]
    xp, xs = x_prompt.reshape(t, d), x_sample.reshape(ms, d)

    w_in_t = w_in.T
    w_tail = jnp.pad(w_in_t[PROJ_MAIN:], ((0, LANES - AB_WIDTH), (0, 0))).astype(BF16)
    hp, ab_p = _rmsnorm_proj_tail(xp, n_pre_mix, w_tail)
    hs, ab_s = _rmsnorm_proj_tail(xs, n_pre_mix, w_tail)
    proj_p, proj_s = _proj([(hp, hs)], [(w_in_t, 0, d)], PROJ_MAIN, tiles["in_bn"], F32, n_mt, "in_proj",
                           w_transposed=True)

    qi = jnp.arange(WINDOW)[:, None]
    kj = jnp.arange(2 * WINDOW)[None, :]
    dist_p = WINDOW + qi - kj
    dist_s = l + jnp.arange(SUBLANES)[:, None] - kj
    dist_s = jnp.where(kj < l + ts, dist_s, -1)
    bias_all = _bias_table(rel_bias, jnp.concatenate([dist_p, dist_s], axis=0))
    attn_p = _attn_prompt(proj_p, bias_all[:, :WINDOW], sinks)

    b_s = bias_all[:, WINDOW:WINDOW + ts]
    own_kv = (jnp.arange(ATTN_HEADS)[:, None] // ATTN_GROUP) == jnp.arange(ATTN_KV_HEADS)[None, :]
    bias_c = jnp.where(own_kv[:, None, None, :], b_s[:, :, :l, None], NEG_INF)
    bias_c = bias_c.reshape(ATTN_HEADS * ts, l * ATTN_KV_HEADS)
    bias_n = jnp.where(own_kv[:, None, :, None], b_s[:, :, None, l:l + ts], NEG_INF)
    bias_n = bias_n.reshape(ATTN_HEADS * ts, ATTN_KV_HEADS * ts)
    sink_rows = jnp.repeat(sinks.astype(F32), ts).reshape(ATTN_HEADS * ts, 1)
    attn_s, new_k, new_v = _attn_sample(proj_s, win_k, win_v, bias_c, bias_n, sink_rows, ts, tiles["attn_gb"])

    params = _gdn_params(a_log, dt_bias)
    og_p, s_p = _gdn_prompt(proj_p, ab_p, conv_w, gnorm_w, params, tiles["gdn_rows"])
    og_s, s_s = _gdn_sample(proj_s, ab_s, conv_state.astype(F32), gdn_state, conv_w, gnorm_w, params, ts,
                            tiles["gdn_sample_rows"])

    mix_p, mix_s = _proj([(attn_p, attn_s), (og_p, og_s)], [(w_out, 0, ATTN_WIDTH), (w_out, 1, GDN_WIDTH)],
                         d, tiles["out_bn"], BF16, n_mt, "out_proj")
    x1_p, h2_p = _post_mix(xp, mix_p, n_post_mix, n_pre_ffn, tiles["post_rows"])
    x1_s, h2_s = _post_mix(xs, mix_s, n_post_mix, n_pre_ffn, tiles["sample_rows"])
    act_p, act_s = _proj([(h2_p, h2_s)], [(w_gate, 0, d), (w_up, 0, d)], ff, tiles["ff_bn"], BF16, n_mt,
                         "ffn_gate_up", swiglu=True)
    ffn_p, ffn_s = _down(act_p, act_s, w_down, tiles["ff_bk"], n_mt)
    y_p = _residual_norm(x1_p, ffn_p, n_post_ffn, tiles["post_rows"])
    y_s = _residual_norm(x1_s, ffn_s, n_post_ffn, tiles["sample_rows"])

    lw = min(WINDOW, t)
    prompt_k = proj_p[t - lw:, K_OFF:K_OFF + KV_WIDTH].reshape(1, lw, ATTN_KV_HEADS, HEAD_DIM)
    prompt_v = proj_p[t - lw:, V_OFF:V_OFF + KV_WIDTH].reshape(1, lw, ATTN_KV_HEADS, HEAD_DIM)
    prompt_conv = proj_p[t - (GDN_CONV - 1):, GQ_OFF:GQ_OFF + GDN_CONV_CH][None]
    sample_conv = proj_s[:, GQ_OFF:GQ_OFF + GDN_CONV_CH].reshape(nb, ts, GDN_CONV_CH)[:, ts - (GDN_CONV - 1):]
    return (y_p.reshape(1, t, d), y_s.reshape(nb, ts, d),
            prompt_k, prompt_v, prompt_conv, s_p[None],
            new_k.reshape(nb, l, ATTN_KV_HEADS, HEAD_DIM), new_v.reshape(nb, l, ATTN_KV_HEADS, HEAD_DIM),
            sample_conv, s_s)


def _tiles(t, ms):
    n_mt = 8 if (t % (8 * 16) == 0 and ms % (8 * 16) == 0) else 1
    return dict(n_mt=n_mt, in_bn=512, out_bn=512, ff_bn=512, ff_bk=512, norm_rows=512, post_rows=256,
                sample_rows=ms, gdn_rows=256, gdn_sample_rows=GDN_CHUNK, attn_gb=8)


def kernel(x_prompt, x_sample, cache_win_k, cache_win_v, state_conv, state_gdn, rel_bias, w_in, attn_sinks,
           gdn_conv_w, gdn_a_log, gdn_dt_bias, gdn_norm_w, w_out, norm_pre_mix, norm_post_mix, norm_pre_ffn,
           norm_post_ffn, w_gate, w_up, w_down):
    depth = w_in.shape[0]
    assert depth == 1 and x_prompt.shape[0] == 1
    tiles = _tiles(x_prompt.shape[1], x_sample.shape[0] * x_sample.shape[1])
    outs = _layer(x_prompt, x_sample, cache_win_k[0], cache_win_v[0], state_conv[0], state_gdn[0], rel_bias,
                  w_in[0], attn_sinks[0], gdn_conv_w[0], gdn_a_log[0], gdn_dt_bias[0], gdn_norm_w[0], w_out[0],
                  norm_pre_mix[0], norm_post_mix[0], norm_pre_ffn[0], norm_post_ffn[0],
                  w_gate[0], w_up[0], w_down[0], tiles)
    yp, ys, pk, pv, pc, ps, sk, sv, sc, ss = outs
    return (yp, ys, pk[None], pv[None], pc[None], ps[None], sk[None], sv[None], sc[None], ss[None])
```

```python
import functools
import math

import jax
import jax.numpy as jnp
from jax import lax
from jax.experimental import pallas as pl
from jax.experimental.pallas import tpu as pltpu

F32 = jnp.float32
BF16 = jnp.bfloat16

HEAD_DIM = 128
ATTN_HEADS = 16
ATTN_KV_HEADS = 4
ATTN_GROUP = ATTN_HEADS // ATTN_KV_HEADS
ATTN_WIDTH = ATTN_HEADS * HEAD_DIM
KV_WIDTH = ATTN_KV_HEADS * HEAD_DIM
WINDOW = 128
REL_BUCKETS = 32
REL_MAX_DIST = 128
GDN_HEADS = 16
GDN_WIDTH = GDN_HEADS * HEAD_DIM
GDN_CONV = 4
GDN_CONV_CH = 3 * GDN_WIDTH
GDN_CHUNK = 64
NORM_EPS = 1e-6
NEG_INF = -1e30

Q_OFF = 0
K_OFF = ATTN_WIDTH
V_OFF = K_OFF + KV_WIDTH
GQ_OFF = V_OFF + KV_WIDTH
GK_OFF = GQ_OFF + GDN_WIDTH
GV_OFF = GK_OFF + GDN_WIDTH
Z_OFF = GV_OFF + GDN_WIDTH
PROJ_MAIN = Z_OFF + GDN_WIDTH
AB_WIDTH = 2 * GDN_HEADS

LANES = 128
SUBLANES = 8
GDN_HB = 8
GDN_NG = GDN_HEADS // GDN_HB
SAMPLE_PAD = 8
SAMPLE_BLOCK_ROWS = 128
VMEM_LIMIT = 56 * 1024 * 1024


def _cparams(n_axes, vmem=VMEM_LIMIT):
    return pltpu.CompilerParams(dimension_semantics=("arbitrary",) * n_axes, vmem_limit_bytes=vmem)


def _dot(a, b):
    return jnp.dot(a, b, preferred_element_type=F32)


def _rms(x, w):
    return x * lax.rsqrt(jnp.mean(x * x, axis=-1, keepdims=True) + NORM_EPS) * w


def _rowwise_kernel(*refs, body, n_in, n_const, n_out, prompt_steps):
    p_in, s_in = refs[:n_in], refs[n_in:2 * n_in]
    consts = refs[2 * n_in:2 * n_in + n_const]
    p_out = refs[2 * n_in + n_const:2 * n_in + n_const + n_out]
    s_out = refs[2 * n_in + n_const + n_out:]

    def run(ins, outs):
        rows2d = [r[...].reshape(-1, r.shape[-1]) for r in ins]
        for o_ref, v in zip(outs, body(*rows2d, *[c[...] for c in consts])):
            o_ref[...] = v.reshape(o_ref.shape).astype(o_ref.dtype)

    i = pl.program_id(0)
    pl.when(i < prompt_steps)(lambda: run(p_in, p_out))
    pl.when(i >= prompt_steps)(lambda: run(s_in, s_out))


def _rowwise(body, pairs, consts, outs, rows, name):
    t = pairs[0][0].shape[0]
    n_sample = math.prod(pairs[0][1].shape[:-1])
    rows = min(rows, t)
    sample_rows = min(SAMPLE_BLOCK_ROWS, n_sample)
    prompt_steps, sample_steps = t // rows, n_sample // sample_rows

    def spec(shape, is_prompt):
        n = rows if is_prompt else sample_rows
        lead = n if len(shape) == 2 else n // shape[1]
        block = (lead,) + tuple(shape[1:])
        if is_prompt:
            index = lambda i: (jnp.minimum(i, prompt_steps - 1),) + (0,) * (len(shape) - 1)
        else:
            index = lambda i: (jnp.maximum(i - prompt_steps, 0),) + (0,) * (len(shape) - 1)
        return pl.BlockSpec(block, index)

    in_specs = ([spec(p.shape, True) for p, _ in pairs] + [spec(s.shape, False) for _, s in pairs]
                + [pl.BlockSpec(c.shape, lambda i, nd=c.ndim: (0,) * nd) for c in consts])
    out_specs = [spec(p.shape, True) for p, _ in outs] + [spec(s.shape, False) for _, s in outs]
    res = pl.pallas_call(
        functools.partial(_rowwise_kernel, body=body, n_in=len(pairs), n_const=len(consts), n_out=len(outs),
                          prompt_steps=prompt_steps),
        grid=(prompt_steps + sample_steps,),
        in_specs=in_specs,
        out_specs=out_specs,
        out_shape=[p for p, _ in outs] + [s for _, s in outs],
        compiler_params=_cparams(1),
        name=name,
    )(*[p for p, _ in pairs], *[s for _, s in pairs], *consts)
    return res[:len(outs)], res[len(outs):]


def _rmsnorm_tail_body(x, w, w_tail):
    h = _rms(x, w).astype(BF16)
    return h, _dot_nt(h, w_tail)


def _post_mix_body(x, mix, w_post, w_pre):
    x1 = x + _rms(mix.astype(F32), w_post)
    return x1, _rms(x1, w_pre)


def _residual_norm_body(x, y, w):
    return (x + _rms(y.astype(F32), w),)


def _stack_rows(p_ref, s_ref, cols=slice(None)):
    return jnp.concatenate([p_ref[:, cols].astype(BF16), s_ref[:, cols].astype(BF16)], axis=0)


def _dot_nt(a, b):
    return lax.dot_general(a, b, (((1,), (1,)), ((), ())), preferred_element_type=F32)


def _proj_kernel(*refs, n_a, n_w, swiglu, w_transposed, row_blocks, n_mt, n_tiles, last_cols):
    a_refs = refs[:2 * n_a]
    w_refs = refs[2 * n_a:2 * n_a + n_w]
    op_ref, os_ref = refs[2 * n_a + n_w:2 * n_a + n_w + 2]
    scratch = refs[2 * n_a + n_w + 2:]
    stage_refs, wb_refs, sem = scratch[:n_w], scratch[n_w:2 * n_w], scratch[2 * n_w]
    mm = _dot_nt if w_transposed else _dot
    j, i = pl.program_id(0), pl.program_id(1)
    nj = pl.num_programs(0)

    def slice_copy(w, tile, s, slot, narrow):
        rows, cols = stage_refs[w].shape[1:]
        if w_transposed:
            src = w_refs[w].at[pl.ds(tile * (rows * n_mt) + s * rows, rows), pl.ds(row_blocks[w] * cols, cols)]
            dst = stage_refs[w].at[slot]
        else:
            width = last_cols if narrow else cols
            src = w_refs[w].at[pl.ds((row_blocks[w] * n_mt + s) * rows, rows), pl.ds(tile * cols, width)]
            dst = stage_refs[w].at[slot, :, pl.ds(0, width)]
        return pltpu.make_async_copy(src, dst, sem.at[w, slot])

    def with_tile(tile, fn):
        if last_cols == stage_refs[0].shape[2] or w_transposed:
            fn(False)
        elif isinstance(tile, int):
            fn(tile == n_tiles - 1)
        else:
            pl.when(tile == n_tiles - 1)(lambda: fn(True))
            pl.when(tile != n_tiles - 1)(lambda: fn(False))

    def cast_slice(w, half, s, slot):
        rows = stage_refs[w].shape[1]
        wb_refs[w][half, pl.ds(pl.multiple_of(s * rows, rows), rows), :] = stage_refs[w][slot].astype(BF16)

    @pl.when(jnp.logical_and(j == 0, i == 0))
    def _():
        if last_cols != stage_refs[0].shape[2]:
            for w in range(n_w):
                stage_refs[w][...] = jnp.zeros_like(stage_refs[w])
        for w in range(n_w):
            for s in range(n_mt):
                def fetch_now(narrow, w=w, s=s):
                    copy = slice_copy(w, 0, s, s % 2, narrow)
                    copy.start()
                    copy.wait()
                with_tile(0, fetch_now)
                cast_slice(w, 0, s, s % 2)
        if n_tiles > 1:
            for w in range(n_w):
                with_tile(1, lambda narrow, w=w: slice_copy(w, 1, 0, 0, narrow).start())

    @pl.when(j + 1 < nj)
    def _():
        slot = (j * n_mt + i) % 2
        for w in range(n_w):
            with_tile(j + 1, lambda narrow, w=w: slice_copy(w, j + 1, i, slot, narrow).wait())
            cast_slice(w, (j + 1) % 2, i, slot)
        last = i + 1 == n_mt
        next_tile = jnp.where(last, j + 2, j + 1)
        next_s = jnp.where(last, 0, i + 1)

        @pl.when(next_tile < nj)
        def _():
            for w in range(n_w):
                with_tile(next_tile, lambda narrow, w=w: slice_copy(w, next_tile, next_s, 1 - slot, narrow).start())

    half = j % 2
    bmp = op_ref.shape[0]

    def apply(width):
        cols = slice(None) if w_transposed else slice(0, width)
        rhs = [wb_ref[half, :, cols] for wb_ref in wb_refs]
        lhs = [_stack_rows(a_refs[2 * p], a_refs[2 * p + 1]) for p in range(n_a)]
        if swiglu:
            g = mm(lhs[0], rhs[0])
            o = g * jax.nn.sigmoid(g) * mm(lhs[0], rhs[1])
        else:
            o = mm(lhs[0], rhs[0])
            for l, r in zip(lhs[1:], rhs[1:]):
                o = o + mm(l, r)
        op_ref[:, :width] = o[:bmp].astype(op_ref.dtype)
        os_ref[:, :width] = o[bmp:].astype(os_ref.dtype)

    full_cols = op_ref.shape[1]
    if last_cols == full_cols:
        apply(full_cols)
    else:
        pl.when(j != n_tiles - 1)(lambda: apply(full_cols))
        pl.when(j == n_tiles - 1)(lambda: apply(last_cols))


def _proj(a_pairs, w_blocks, n, bn, out_dtype, n_mt, name, swiglu=False, w_transposed=False):
    tp, ts = a_pairs[0][0].shape[0], a_pairs[0][1].shape[0]
    bmp, bms = tp // n_mt, ts // n_mt
    in_specs = []
    for a_p, a_s in a_pairs:
        in_specs.append(pl.BlockSpec((bmp, a_p.shape[1]), lambda j, i: (i, 0)))
        in_specs.append(pl.BlockSpec((bms, a_s.shape[1]), lambda j, i: (i, 0)))
    in_specs += [pl.BlockSpec(memory_space=pl.ANY) for _ in w_blocks]
    n_tiles = pl.cdiv(n, bn)
    tile = lambda k: (bn, k) if w_transposed else (k, bn)
    stage = lambda k: (2, bn // n_mt, k) if w_transposed else (2, k // n_mt, bn)
    operands = [a for pair in a_pairs for a in pair] + [w for w, _, _ in w_blocks]
    return pl.pallas_call(
        functools.partial(_proj_kernel, n_a=len(a_pairs), n_w=len(w_blocks), swiglu=swiglu,
                          w_transposed=w_transposed, row_blocks=tuple(rb for _, rb, _ in w_blocks), n_mt=n_mt,
                          n_tiles=n_tiles, last_cols=n - (n_tiles - 1) * bn),
        grid=(n_tiles, n_mt),
        in_specs=in_specs,
        out_specs=[pl.BlockSpec((bmp, bn), lambda j, i: (i, j)), pl.BlockSpec((bms, bn), lambda j, i: (i, j))],
        out_shape=[jax.ShapeDtypeStruct((tp, n), out_dtype), jax.ShapeDtypeStruct((ts, n), out_dtype)],
        scratch_shapes=([pltpu.VMEM(stage(k), F32) for _, _, k in w_blocks]
                        + [pltpu.VMEM((2,) + tile(k), BF16) for _, _, k in w_blocks]
                        + [pltpu.SemaphoreType.DMA((len(w_blocks), 2))]),
        compiler_params=_cparams(2),
        name=name,
    )(*operands)


def _down_kernel(ap_ref, as_ref, b_ref, op_ref, os_ref, accp_ref, accs_ref, *, bn, k_total):
    k = pl.program_id(1)
    nk = pl.num_programs(1)
    bk = b_ref.shape[0]
    bmp = ap_ref.shape[0]
    last_rows = k_total - (k_total // bk) * bk

    @pl.when(k == 0)
    def _():
        accp_ref[...] = jnp.zeros_like(accp_ref)
        accs_ref[...] = jnp.zeros_like(accs_ref)

    def accumulate(kk):
        a = _stack_rows(ap_ref, as_ref, slice(0, kk))
        for n0 in range(0, accp_ref.shape[1], bn):
            d = _dot(a, b_ref[:kk, n0:n0 + bn].astype(BF16))
            accp_ref[:, n0:n0 + bn] += d[:bmp]
            accs_ref[:, n0:n0 + bn] += d[bmp:]

    if last_rows == 0:
        accumulate(bk)
    else:
        pl.when(k < nk - 1)(lambda: accumulate(bk))
        pl.when(k == nk - 1)(lambda: accumulate(last_rows))

    @pl.when(k == nk - 1)
    def _():
        op_ref[...] = accp_ref[...].astype(op_ref.dtype)
        os_ref[...] = accs_ref[...].astype(os_ref.dtype)


def _down(a_p, a_s, b, bk, n_mt):
    (tp, k_total), ts = a_p.shape, a_s.shape[0]
    n = b.shape[1]
    bmp, bms = tp // n_mt, ts // n_mt
    return pl.pallas_call(
        functools.partial(_down_kernel, bn=min(n, 512), k_total=k_total),
        grid=(n_mt, pl.cdiv(k_total, bk)),
        in_specs=[pl.BlockSpec((bmp, bk), lambda i, k: (i, k)), pl.BlockSpec((bms, bk), lambda i, k: (i, k)),
                  pl.BlockSpec((bk, n), lambda i, k: (k, 0))],
        out_specs=[pl.BlockSpec((bmp, n), lambda i, k: (i, 0), pipeline_mode=pl.Buffered(1)),
                   pl.BlockSpec((bms, n), lambda i, k: (i, 0))],
        out_shape=[jax.ShapeDtypeStruct((tp, n), BF16), jax.ShapeDtypeStruct((ts, n), BF16)],
        scratch_shapes=[pltpu.VMEM((bmp, n), F32), pltpu.VMEM((bms, n), F32)],
        compiler_params=_cparams(2),
        name="ffn_down",
    )(a_p, a_s, b)


def _rel_bucket(d):
    n = jnp.maximum(d, 0)
    max_exact = REL_BUCKETS // 2
    nf = jnp.maximum(n, 1).astype(F32)
    large = max_exact + (jnp.log(nf / max_exact) / math.log(REL_MAX_DIST / max_exact)
                         * (REL_BUCKETS - max_exact)).astype(jnp.int32)
    large = jnp.minimum(large, REL_BUCKETS - 1)
    return jnp.where(n < max_exact, n, large)


def _bias_kernel(rel_ref, bucket_ref, allowed_ref, o_ref):
    h = pl.program_id(0)
    bucket = bucket_ref[...]
    acc = jnp.zeros(bucket.shape, F32)
    for n in range(REL_BUCKETS):
        acc = jnp.where(bucket == n, rel_ref[n, h], acc)
    o_ref[0] = jnp.where(allowed_ref[...] > 0, acc, NEG_INF)


def _bias_table(rel_bias, dist):
    tq, tk = dist.shape
    allowed = ((dist >= 0) & (dist <= WINDOW)).astype(jnp.int32)
    full = pl.BlockSpec((tq, tk), lambda h: (0, 0))
    return pl.pallas_call(
        _bias_kernel,
        grid=(ATTN_HEADS,),
        in_specs=[pl.BlockSpec(memory_space=pltpu.SMEM), full, full],
        out_specs=pl.BlockSpec((1, tq, tk), lambda h: (h, 0, 0)),
        out_shape=jax.ShapeDtypeStruct((ATTN_HEADS, tq, tk), F32),
        compiler_params=_cparams(1),
        name="rel_bias_table",
    )(rel_bias.astype(F32), _rel_bucket(dist), allowed)


def _softmax_pv(s, sink):
    m = jnp.maximum(jnp.max(s, axis=-1, keepdims=True), sink)
    p = jnp.exp(s - m)
    denom = jnp.sum(p, axis=-1, keepdims=True) + jnp.exp(sink - m)
    return p.astype(BF16), denom


def _attn_prompt_kernel(sink_ref, cur_ref, prev_ref, bias_ref, o_ref):
    c = WINDOW
    head_cols = lambda base, h: slice(base + h * HEAD_DIM, base + (h + 1) * HEAD_DIM)
    kk, vv = [], []
    for h in range(ATTN_KV_HEADS):
        kk.append(jnp.concatenate([prev_ref[:, head_cols(0, h)], cur_ref[:, head_cols(K_OFF, h)]],
                                  axis=0).astype(BF16))
        vv.append(jnp.concatenate([prev_ref[:, head_cols(KV_WIDTH, h)], cur_ref[:, head_cols(V_OFF, h)]],
                                  axis=0).astype(BF16))
    heads = range(ATTN_HEADS)
    scale = HEAD_DIM ** -0.5
    s = [_dot_nt((cur_ref[:, head_cols(Q_OFF, hd)] * scale).astype(BF16), kk[hd // ATTN_GROUP]) for hd in heads]
    pd = [_softmax_pv(s[hd] + bias_ref[hd], sink_ref[hd]) for hd in heads]
    outs = [_dot(pd[hd][0], vv[hd // ATTN_GROUP]) / pd[hd][1] for hd in heads]
    o_ref[...] = jnp.concatenate(outs, axis=1).astype(o_ref.dtype)


def _attn_prompt(proj, bias, sinks):
    t = proj.shape[0]
    c = WINDOW
    qkv = V_OFF + KV_WIDTH
    first = jnp.where(jnp.arange(2 * c) < c, NEG_INF, bias)
    tables = jnp.stack([first, bias])
    return pl.pallas_call(
        _attn_prompt_kernel,
        grid=(t // c,),
        in_specs=[pl.BlockSpec(memory_space=pltpu.SMEM),
                  pl.BlockSpec((c, qkv), lambda i: (i, 0)),
                  pl.BlockSpec((c, 2 * KV_WIDTH), lambda i: (jnp.maximum(i - 1, 0), K_OFF // (2 * KV_WIDTH))),
                  pl.BlockSpec((None, ATTN_HEADS, c, 2 * c), lambda i: (jnp.minimum(i, 1), 0, 0, 0))],
        out_specs=pl.BlockSpec((c, ATTN_WIDTH), lambda i: (i, 0)),
        out_shape=jax.ShapeDtypeStruct((t, ATTN_WIDTH), BF16),
        compiler_params=_cparams(1),
        name="attn_prompt",
    )(sinks.astype(F32), proj, proj, tables)


def _attn_sample_kernel(x_ref, ck_ref, cv_ref, bias_c_ref, bias_n_ref, sink_ref, o_ref, ok_ref, ov_ref,
                        q_s, kn_s, vn_s, *, t_new):
    gb, l, kvh, hd = ck_ref.shape
    kc = ck_ref[...].reshape(gb, l * kvh, hd)
    vc = cv_ref[...].reshape(gb, l * kvh, hd)
    for b in range(gb):
        rows = slice(b * t_new, (b + 1) * t_new)
        for h in range(kvh):
            kn_s[b, h * t_new:(h + 1) * t_new, :] = x_ref[rows, K_OFF + h * hd:K_OFF + (h + 1) * hd]
            vn_s[b, h * t_new:(h + 1) * t_new, :] = x_ref[rows, V_OFF + h * hd:V_OFF + (h + 1) * hd]
        for head in range(ATTN_HEADS):
            q_s[b, head * t_new:(head + 1) * t_new, :] = x_ref[rows, head * hd:(head + 1) * hd]

    q = q_s[...].astype(BF16)
    scale = hd ** -0.5
    s_c = jnp.einsum("bqd,bkd->bqk", q, kc.astype(BF16), preferred_element_type=F32) * scale + bias_c_ref[...]
    s_n = jnp.einsum("bqd,bkd->bqk", q, kn_s[...].astype(BF16), preferred_element_type=F32) * scale + bias_n_ref[...]
    sink = sink_ref[...]
    m = jnp.maximum(jnp.maximum(jnp.max(s_c, axis=-1, keepdims=True), jnp.max(s_n, axis=-1, keepdims=True)), sink)
    p_c = jnp.exp(s_c - m)
    p_n = jnp.exp(s_n - m)
    denom = jnp.sum(p_c, axis=-1, keepdims=True) + jnp.sum(p_n, axis=-1, keepdims=True) + jnp.exp(sink - m)
    o = (jnp.einsum("bqk,bkd->bqd", p_c.astype(BF16), vc.astype(BF16), preferred_element_type=F32)
         + jnp.einsum("bqk,bkd->bqd", p_n.astype(BF16), vn_s[...].astype(BF16), preferred_element_type=F32)) / denom
    for b in range(gb):
        for head in range(ATTN_HEADS):
            o_ref[b * t_new:(b + 1) * t_new, head * hd:(head + 1) * hd] = o[b, head * t_new:(head + 1) * t_new, :]

    shift = t_new * kvh
    for c_val, n_s, out_ref in ((kc, kn_s, ok_ref), (vc, vn_s, ov_ref)):
        new_rows = [n_s[:, h * t_new + t:h * t_new + t + 1, :] for t in range(t_new) for h in range(kvh)]
        win = jnp.concatenate([c_val[:, shift:, :]] + new_rows, axis=1)
        out_ref[...] = win.reshape(gb, l, kvh, hd)


def _attn_sample(proj, cache_k, cache_v, bias_c, bias_n, sink_rows, t_new, gb):
    b, l = cache_k.shape[:2]
    rows = ATTN_HEADS * t_new
    qkv = V_OFF + KV_WIDTH
    cache_spec = pl.BlockSpec((gb, l, ATTN_KV_HEADS, HEAD_DIM), lambda i: (i, 0, 0, 0))
    full = lambda a: pl.BlockSpec(a.shape, lambda i: (0,) * a.ndim)
    return pl.pallas_call(
        functools.partial(_attn_sample_kernel, t_new=t_new),
        grid=(b // gb,),
        in_specs=[pl.BlockSpec((gb * t_new, qkv), lambda i: (i, 0)), cache_spec, cache_spec,
                  full(bias_c), full(bias_n), full(sink_rows)],
        out_specs=[pl.BlockSpec((gb * t_new, ATTN_WIDTH), lambda i: (i, 0)), cache_spec, cache_spec],
        out_shape=[jax.ShapeDtypeStruct((b * t_new, ATTN_WIDTH), F32),
                   jax.ShapeDtypeStruct(cache_k.shape, F32),
                   jax.ShapeDtypeStruct(cache_v.shape, F32)],
        scratch_shapes=[pltpu.VMEM((gb, rows, HEAD_DIM), F32),
                        pltpu.VMEM((gb, ATTN_KV_HEADS * t_new, HEAD_DIM), F32),
                        pltpu.VMEM((gb, ATTN_KV_HEADS * t_new, HEAD_DIM), F32)],
        compiler_params=_cparams(1),
        name="attn_sample",
    )(proj, cache_k, cache_v, bias_c, bias_n, sink_rows)


def _split3(x):
    hi = x.astype(BF16)
    r = x - hi.astype(F32)
    mid = r.astype(BF16)
    lo = (r - mid.astype(F32)).astype(BF16)
    return hi, mid, lo


def _select_dot(sel, x):
    hi, mid, lo = _split3(x)
    return _dot(sel, hi) + _dot(sel, mid) + _dot(sel, lo)


def _dot_select(x, sel):
    hi, mid, lo = _split3(x)
    return _dot(hi, sel) + _dot(mid, sel) + _dot(lo, sel)


def _softplus(x):
    return jnp.maximum(x, 0.0) + jnp.log1p(jnp.exp(-jnp.abs(x)))


def _conv_silu(pad_ref, w, r0, n):
    top = SUBLANES + r0
    y = pad_ref[top:top + n, :] * w[GDN_CONV - 1:GDN_CONV]
    for s in range(1, GDN_CONV):
        y = y + pad_ref[top - s:top - s + n, :] * w[GDN_CONV - 1 - s:GDN_CONV - s]
    return y * jax.nn.sigmoid(y)


def _l2norm_heads(x, scale):
    outs = []
    for hh in range(GDN_HB):
        xh = x[:, hh * HEAD_DIM:(hh + 1) * HEAD_DIM]
        outs.append(xh * (lax.rsqrt(jnp.sum(xh * xh, axis=-1, keepdims=True) + NORM_EPS) * scale))
    return jnp.concatenate(outs, axis=1)


GDN_N_PADS = 4
STAGE_ROWS = 32
PROBE_BOUND = 1e30
GDN_N_STAGED = 10


def _gdn_kernel(*refs, rows, seq, carry, live_from):
    kw = dict(rows=rows, seq=seq, carry=carry, live_from=live_from)
    if not carry:
        io, pads, staged = refs[:-GDN_N_PADS - GDN_N_STAGED], refs[-GDN_N_PADS - GDN_N_STAGED:-GDN_N_STAGED], \
            refs[-GDN_N_STAGED:]
        _gdn_body(io, pads, staged, staged, **kw)
        return
    n_scratch = GDN_N_PADS + 2 * GDN_N_STAGED
    io, pads = refs[:-n_scratch], refs[-n_scratch:-2 * GDN_N_STAGED]
    set_a, set_b = refs[-2 * GDN_N_STAGED:-GDN_N_STAGED], refs[-GDN_N_STAGED:]
    s_ref = io[-1]
    step = pl.program_id(1)

    @pl.when(step == 0)
    def _():
        s_ref[...] = jnp.zeros_like(s_ref)
        for ref in set_b:
            ref[...] = jnp.zeros_like(ref)

    pl.when(step % 2 == 0)(lambda: _gdn_body(io, pads, set_a, set_b, **kw))
    pl.when(step % 2 == 1)(lambda: _gdn_body(io, pads, set_b, set_a, **kw))


def _gdn_body(refs, pads, staged_w, staged_r, *, rows, seq, carry, live_from):
    c = GDN_CHUNK
    nc = rows // c
    nseq = c // seq
    n_live = seq - live_from
    lw = GDN_HB * HEAD_DIM
    if carry:
        (xq_ref, xk_ref, xv_ref, tq_ref, tk_ref, tv_ref, z_ref, wq_ref, wk_ref, wv_ref, gn_ref,
         ab_ref, pc_ref, og_ref, s_ref) = refs
        s0_ref = None
    else:
        (xq_ref, xk_ref, xv_ref, cq_ref, ck_ref, cv_ref, z_in, wq_ref, wk_ref, wv_ref, gn_ref,
         ab_in, pc_ref, s0_ref, og_ref, s_ref, z_ref, ab_ref) = refs
    padq_s, padk_s, padv_s, eb_s = pads
    kn_w, kb_w, qn_w, vb_w, kbe_w, qe_w, kd_w, egc_w, egl_w, gcr_w = staged_w
    kn_s, kb_s, qn_s, vb_s, kbe_s, qe_s, kd_s, egc_s, egl_s, gcr_s = staged_r

    group = pl.program_id(0)
    step = pl.program_id(1)
    row = lax.broadcasted_iota(jnp.int32, (rows, 1), 0)
    live = (row % seq) >= live_from
    keep_live = (lambda x: jnp.where(live, x, 0.0)) if live_from else (lambda x: x)

    if carry:
        for pad_ref, x_ref, t_ref in ((padq_s, xq_ref, tq_ref), (padk_s, xk_ref, tk_ref), (padv_s, xv_ref, tv_ref)):
            pad_ref[0:SUBLANES, :] = jnp.where(step == 0, 0.0, t_ref[...])
            pad_ref[SUBLANES:, :] = x_ref[...]
    else:
        def gather(dst_ref, x_ref, hist_ref, top):
            dst_ref[...] = jnp.zeros_like(dst_ref)
            for b in range(rows // seq):
                r0 = top + b * seq + live_from
                if hist_ref is not None:
                    dst_ref[r0 - (GDN_CONV - 1):r0, :] = hist_ref[:, b, :]
                dst_ref[r0:r0 + n_live, :] = x_ref[b * n_live:(b + 1) * n_live, :]

        gather(padq_s, xq_ref, cq_ref, SUBLANES)
        gather(padk_s, xk_ref, ck_ref, SUBLANES)
        gather(padv_s, xv_ref, cv_ref, SUBLANES)
        gather(z_ref, z_in, None, 0)
        gather(ab_ref, ab_in, None, 0)

    def stage_decays():
        ab = ab_ref[...]
        g_col = keep_live(-jnp.exp(pc_ref[0:1, :]) * _softplus(ab + pc_ref[1:2, :]))
        beta_col = keep_live(jax.nn.sigmoid(ab))
        ri = lax.broadcasted_iota(jnp.int32, (rows, rows), 0)
        ci = lax.broadcasted_iota(jnp.int32, (rows, rows), 1)
        same = (ri // seq) == (ci // seq)
        cum_sel = jnp.where(jnp.logical_and(same, ci <= ri), 1.0, 0.0).astype(BF16)
        tot_sel = jnp.where(same, 1.0, 0.0).astype(BF16)
        gc_col = _select_dot(cum_sel, g_col)
        gl_col = _select_dot(tot_sel, g_col)
        first = group * GDN_HB
        li = lax.broadcasted_iota(jnp.int32, (LANES, lw), 0)
        ni = lax.broadcasted_iota(jnp.int32, (LANES, lw), 1)
        spread_g = jnp.where(li == ni // HEAD_DIM + first, 1.0, 0.0).astype(BF16)
        spread_b = jnp.where(li == ni // HEAD_DIM + first + GDN_HEADS, 1.0, 0.0).astype(BF16)
        egc_w[...] = _dot_select(gc_col, spread_g)
        egl_w[...] = _dot_select(gl_col, spread_g)
        eb_s[...] = _dot_select(beta_col, spread_b)
        pi = lax.broadcasted_iota(jnp.int32, (LANES, LANES), 0)
        pj = lax.broadcasted_iota(jnp.int32, (LANES, LANES), 1)
        pick = jnp.where(jnp.logical_and(pi == pj + first, pj < GDN_HB), 1.0, 0.0).astype(BF16)
        gc_heads = _dot_select(gc_col, pick)
        if rows % LANES:
            gc_heads = jnp.concatenate([gc_heads, jnp.zeros((LANES - rows % LANES, LANES), F32)], axis=0)
        gcr_w[...] = gc_heads.T[:SUBLANES, :rows]

    def stage_rows(r0, n, after=None):
        sl = slice(r0, r0 + n)
        keep = (lambda x: jnp.where(live[sl], x, 0.0)) if live_from else (lambda x: x)
        gate = 1.0 if after is None else jnp.where(after > PROBE_BOUND, 0.0, 1.0)
        conv = lambda pad_ref, w_ref: keep(_conv_silu(pad_ref, w_ref[...] * gate, r0, n))
        qn = _l2norm_heads(conv(padq_s, wq_ref), HEAD_DIM ** -0.5)
        kn = _l2norm_heads(conv(padk_s, wk_ref), 1.0)
        v = conv(padv_s, wv_ref)
        egc, egl, eb = egc_w[sl, :], egl_w[sl, :], eb_s[sl, :]
        decay_in = jnp.exp(egc)
        kb = kn * eb
        kn_w[sl, :] = kn
        kb_w[sl, :] = kb
        qn_w[sl, :] = qn
        vb_w[sl, :] = v * eb
        kbe_w[sl, :] = kb * decay_in
        qe_w[sl, :] = qn * decay_in
        kd_w[sl, :] = kn * jnp.exp(egl - egc)

    n_slices = rows // STAGE_ROWS if carry else 1
    pending = [functools.partial(stage_rows, r * (rows // n_slices), rows // n_slices) for r in range(n_slices)]
    if not carry:
        stage_decays()
        pending.pop(0)()

    cr = lax.broadcasted_iota(jnp.int32, (c, c), 0)
    cc = lax.broadcasted_iota(jnp.int32, (c, c), 1)
    same_c = (cr // seq) == (cc // seq)
    causal_bias = jnp.where(jnp.logical_and(same_c, cr >= cc), 0.0, NEG_INF)
    strict = jnp.where(jnp.logical_and(same_c, cr > cc), 1.0, 0.0)

    eye = jnp.where(cr == cc, 1.0, 0.0)
    seq_of_row = lax.broadcasted_iota(jnp.int32, (c, HEAD_DIM), 0) // seq

    def lower_left(s):
        return jnp.logical_and(jnp.logical_and(cr // (2 * s) == cc // (2 * s), (cr // s) % 2 == 1),
                               (cc // s) % 2 == 0)

    merge_sizes = [2 ** e for e in range(1, max(1, math.ceil(math.log2(n_live))))]
    pair_mask = jnp.where(lower_left(1), 1.0, 0.0)
    merge_masks = [jnp.where(lower_left(s), 1.0, 0.0) for s in merge_sizes]

    pairs = [(ch, hh) for ch in range(nc) for hh in range(GDN_HB)]
    rows_of = lambda ch: slice(ch * c, (ch + 1) * c)
    lanes_of = lambda hh: slice(hh * HEAD_DIM, (hh + 1) * HEAD_DIM)
    uw_all, qk_all, a_mat, t_mat, t_a = {}, {}, {}, {}, {}
    for ch, hh in pairs:
        rs, hs = rows_of(ch), lanes_of(hh)
        gcol = egc_s[rs, hh * HEAD_DIM:hh * HEAD_DIM + c]
        grow = gcr_s[hh:hh + 1, rs]
        decay = jnp.exp(gcol - grow + causal_bias)
        kq = jnp.concatenate([kb_s[rs, hs], qn_s[rs, hs]], axis=0).astype(BF16)
        kk = _dot_nt(kq, kn_s[rs, hs].astype(BF16)) * jnp.concatenate([decay, decay], axis=0)
        a_mat[ch, hh] = kk[:c] * strict
        qk_all[ch, hh] = kk[c:].astype(BF16)
        t_mat[ch, hh] = eye - a_mat[ch, hh] * pair_mask
    probes = [a_mat[pairs[-1]][0:1, 0:1]]
    if carry:
        stage_decays()
    for mask in merge_masks:
        for p in pairs:
            t_a[p] = _dot(t_mat[p].astype(BF16), (a_mat[p] * mask).astype(BF16))
        for p in pairs:
            t_mat[p] = t_mat[p] - _dot(t_a[p].astype(BF16), t_mat[p].astype(BF16))
        probes.append(t_mat[pairs[-1]][0:1, 0:1])
    for ch, hh in pairs:
        rs, hs = rows_of(ch), lanes_of(hh)
        uw_all[ch, hh] = _dot(t_mat[ch, hh].astype(BF16),
                              jnp.concatenate([vb_s[rs, hs], kbe_s[rs, hs]], axis=1).astype(BF16))

    if carry:
        state = [s_ref[hh] for hh in range(GDN_HB)]
    new_states = {}
    heads = range(GDN_HB)
    tn_dot = lambda a, b: lax.dot_general(a, b, (((0,), (0,)), ((), ())), preferred_element_type=F32)
    for ch in range(nc):
        rs = rows_of(ch)
        w_s, q_s, v_new_b, o = {}, {}, {}, {}
        for hh in heads:
            w = uw_all[ch, hh][:, HEAD_DIM:]
            q_e = qe_s[rs, lanes_of(hh)]
            if carry:
                wq = _dot(jnp.concatenate([w, q_e], axis=0).astype(BF16), state[hh].astype(BF16))
                w_s[hh], q_s[hh] = wq[:c], wq[c:]
            else:
                parts = []
                for b in range(nseq):
                    bs = slice(b * seq, (b + 1) * seq)
                    parts.append(_dot(jnp.concatenate([w[bs], q_e[bs]], axis=0).astype(BF16),
                                      s0_ref[ch * nseq + b, hh].astype(BF16)))
                w_s[hh] = jnp.concatenate([p[:seq] for p in parts], axis=0)
                q_s[hh] = jnp.concatenate([p[seq:] for p in parts], axis=0)
        for hh in heads:
            v_new_b[hh] = (uw_all[ch, hh][:, :HEAD_DIM] - w_s[hh]).astype(BF16)
            o[hh] = q_s[hh] + _dot(qk_all[ch, hh], v_new_b[hh])
        for hh in heads:
            hs = lanes_of(hh)
            k_d = kd_s[rs, hs]
            if carry:
                state[hh] = (state[hh] * jnp.exp(egl_s[ch * c:ch * c + 1, hs])
                             + tn_dot(k_d.astype(BF16), v_new_b[hh]))
            else:
                for b in range(nseq):
                    k_db = jnp.where(seq_of_row == b, k_d, 0.0).astype(BF16)
                    r1 = ch * c + b * seq
                    new_states[ch * nseq + b, hh] = (
                        s0_ref[ch * nseq + b, hh] * jnp.exp(egl_s[r1:r1 + 1, hs]) + tn_dot(k_db, v_new_b[hh]))
        if carry:
            probes.append(_dot(qk_all[ch, 0], jnp.ones((c, HEAD_DIM), BF16))[0:1, 0:1])
        outs = []
        for hh in heads:
            z = z_ref[rs, lanes_of(hh)]
            outs.append(_rms(o[hh], gn_ref[...]) * (z * jax.nn.sigmoid(z)))
        og = jnp.concatenate(outs, axis=1)
        if carry:
            og_ref[rs, :] = og.astype(og_ref.dtype)
        else:
            for b in range(nseq):
                r0 = b * seq + live_from
                og_ref[(ch * nseq + b) * n_live:(ch * nseq + b + 1) * n_live, :] = (
                    og[r0:r0 + n_live].astype(og_ref.dtype))
    for n, stage in enumerate(pending):
        stage(after=probes[min(n, len(probes) - 1)])
    if carry:
        s_ref[...] = jnp.stack(state, axis=0)
    else:
        for (b, hh), s_new in new_states.items():
            s_ref[b, hh] = s_new


def _gdn_params(a_log, dt_bias):
    rows = jnp.stack([a_log.astype(F32), dt_bias.astype(F32)], axis=0)
    return jnp.pad(rows, ((0, SUBLANES - 2), (0, LANES - GDN_HEADS)))


def _gdn_scratch(rows, n_sets):
    lw = GDN_HB * HEAD_DIM
    staged = [(rows, lw)] * (GDN_N_STAGED - 1) + [(SUBLANES, rows)]
    return ([pltpu.VMEM((SUBLANES + rows, lw), F32) for _ in range(GDN_N_PADS - 1)] + [pltpu.VMEM((rows, lw), F32)]
            + [pltpu.VMEM(shape, F32) for _ in range(n_sets) for shape in staged])


def _gdn_prompt(proj, ab, conv_w, gnorm_w, params, rows):
    t = proj.shape[0]
    nb = t // rows
    lw = GDN_HB * HEAD_DIM
    head = lambda i: jnp.minimum(i, nb - 1)
    done = lambda i: jnp.maximum(i - 1, 0)
    x_spec = lambda off: pl.BlockSpec((rows, lw), lambda g, i: (head(i), off // lw + g))
    tail_spec = lambda off: pl.BlockSpec(
        (SUBLANES, lw), lambda g, i: (jnp.maximum(head(i) * (rows // SUBLANES) - 1, 0), off // lw + g))
    w_spec = lambda off: pl.BlockSpec((GDN_CONV, lw), lambda g, i: (0, (off - GQ_OFF) // lw + g))
    return pl.pallas_call(
        functools.partial(_gdn_kernel, rows=rows, seq=GDN_CHUNK, carry=True, live_from=0),
        grid=(GDN_NG, nb + 1),
        in_specs=[x_spec(GQ_OFF), x_spec(GK_OFF), x_spec(GV_OFF),
                  tail_spec(GQ_OFF), tail_spec(GK_OFF), tail_spec(GV_OFF),
                  pl.BlockSpec((rows, lw), lambda g, i: (done(i), Z_OFF // lw + g)),
                  w_spec(GQ_OFF), w_spec(GK_OFF), w_spec(GV_OFF),
                  pl.BlockSpec((1, HEAD_DIM), lambda g, i: (0, 0)),
                  pl.BlockSpec((rows, LANES), lambda g, i: (head(i), 0)),
                  pl.BlockSpec((SUBLANES, LANES), lambda g, i: (0, 0))],
        out_specs=[pl.BlockSpec((rows, lw), lambda g, i: (done(i), g)),
                   pl.BlockSpec((GDN_HB, HEAD_DIM, HEAD_DIM), lambda g, i: (g, 0, 0))],
        out_shape=[jax.ShapeDtypeStruct((t, GDN_WIDTH), BF16),
                   jax.ShapeDtypeStruct((GDN_HEADS, HEAD_DIM, HEAD_DIM), F32)],
        scratch_shapes=_gdn_scratch(rows, 2),
        compiler_params=_cparams(2),
        name="gdn_prompt",
    )(proj, proj, proj, proj, proj, proj, proj, conv_w, conv_w, conv_w,
      gnorm_w.reshape(1, HEAD_DIM).astype(F32), ab, params)


def _gdn_sample(proj, ab, conv_state, state, conv_w, gnorm_w, params, n_live, rows):
    lw = GDN_HB * HEAD_DIM
    nseq = rows // SAMPLE_PAD
    n_in = nseq * n_live
    x_spec = lambda off: pl.BlockSpec((n_in, lw), lambda g, i: (i, off // lw + g))
    hist_spec = lambda off: pl.BlockSpec((GDN_CONV - 1, nseq, lw), lambda g, i: (0, i, (off - GQ_OFF) // lw + g))
    w_spec = lambda off: pl.BlockSpec((GDN_CONV, lw), lambda g, i: (0, (off - GQ_OFF) // lw + g))
    s_spec = pl.BlockSpec((nseq, GDN_HB, HEAD_DIM, HEAD_DIM), lambda g, i: (i, g, 0, 0))
    return pl.pallas_call(
        functools.partial(_gdn_kernel, rows=rows, seq=SAMPLE_PAD, carry=False, live_from=SAMPLE_PAD - n_live),
        grid=(GDN_NG, proj.shape[0] // n_in),
        in_specs=[x_spec(GQ_OFF), x_spec(GK_OFF), x_spec(GV_OFF),
                  hist_spec(GQ_OFF), hist_spec(GK_OFF), hist_spec(GV_OFF),
                  x_spec(Z_OFF), w_spec(GQ_OFF), w_spec(GK_OFF), w_spec(GV_OFF),
                  pl.BlockSpec((1, HEAD_DIM), lambda g, i: (0, 0)),
                  pl.BlockSpec((n_in, LANES), lambda g, i: (i, 0)),
                  pl.BlockSpec((SUBLANES, LANES), lambda g, i: (0, 0)),
                  s_spec],
        out_specs=[pl.BlockSpec((n_in, lw), lambda g, i: (i, g)), s_spec],
        out_shape=[jax.ShapeDtypeStruct((proj.shape[0], GDN_WIDTH), F32),
                   jax.ShapeDtypeStruct(state.shape, F32)],
        scratch_shapes=[pltpu.VMEM((rows, lw), F32), pltpu.VMEM((rows, LANES), F32)] + _gdn_scratch(rows, 1),
        compiler_params=_cparams(2),
        name="gdn_sample",
    )(proj, proj, proj, conv_state, conv_state, conv_state, proj, conv_w, conv_w, conv_w,
      gnorm_w.reshape(1, HEAD_DIM).astype(F32), ab, params, state)


def _layer(x_prompt, x_sample, win_k, win_v, conv_state, gdn_state, rel_bias,
           w_in, sinks, conv_w, a_log, dt_bias, gnorm_w, w_out,
           n_pre_mix, n_post_mix, n_pre_ffn, n_post_ffn, w_gate, w_up, w_down, tiles):
    _, t, d = x_prompt.shape
    nb, ts, _ = x_sample.shape
    ms = nb * ts
    l = win_k.shape[1]
    ff = w_gate.shape[1]
    n_mt = tiles["n_mt"]
    xp = x_prompt.reshape(t, d)
    sds = jax.ShapeDtypeStruct
    norm_w = lambda w: w.reshape(1, d).astype(F32)

    w_in_t = w_in.T
    w_tail = jnp.pad(w_in_t[PROJ_MAIN:], ((0, LANES - AB_WIDTH), (0, 0))).astype(BF16)
    (hp, ab_p), (hs, ab_s) = _rowwise(
        _rmsnorm_tail_body, [(xp, x_sample)], [norm_w(n_pre_mix), w_tail],
        [(sds((t, d), BF16), sds((ms, d), BF16)), (sds((t, LANES), F32), sds((ms, LANES), F32))],
        tiles["norm_rows"], "rmsnorm")
    proj_p, proj_s = _proj([(hp, hs)], [(w_in_t, 0, d)], PROJ_MAIN, tiles["in_bn"], F32, n_mt, "in_proj",
                           w_transposed=True)

    qi = jnp.arange(WINDOW)[:, None]
    kj = jnp.arange(2 * WINDOW)[None, :]
    dist_p = WINDOW + qi - kj
    dist_s = l + jnp.arange(SUBLANES)[:, None] - kj
    dist_s = jnp.where(kj < l + ts, dist_s, -1)
    bias_all = _bias_table(rel_bias, jnp.concatenate([dist_p, dist_s], axis=0))
    attn_p = _attn_prompt(proj_p, bias_all[:, :WINDOW], sinks)

    b_s = bias_all[:, WINDOW:WINDOW + ts]
    own_kv = (jnp.arange(ATTN_HEADS)[:, None] // ATTN_GROUP) == jnp.arange(ATTN_KV_HEADS)[None, :]
    bias_c = jnp.where(own_kv[:, None, None, :], b_s[:, :, :l, None], NEG_INF)
    bias_c = bias_c.reshape(ATTN_HEADS * ts, l * ATTN_KV_HEADS)
    bias_n = jnp.where(own_kv[:, None, :, None], b_s[:, :, None, l:l + ts], NEG_INF)
    bias_n = bias_n.reshape(ATTN_HEADS * ts, ATTN_KV_HEADS * ts)
    sink_rows = jnp.repeat(sinks.astype(F32), ts).reshape(ATTN_HEADS * ts, 1)
    attn_s, new_k, new_v = _attn_sample(proj_s, win_k, win_v, bias_c, bias_n, sink_rows, ts, tiles["attn_gb"])

    params = _gdn_params(a_log, dt_bias)
    og_p, s_p = _gdn_prompt(proj_p, ab_p, conv_w, gnorm_w, params, tiles["gdn_rows"])
    og_s, s_s = _gdn_sample(proj_s, ab_s, jnp.swapaxes(conv_state, 0, 1).astype(F32), gdn_state, conv_w, gnorm_w, params, ts,
                            tiles["gdn_sample_rows"])

    mix_p, mix_s = _proj([(attn_p, attn_s), (og_p, og_s)], [(w_out, 0, ATTN_WIDTH), (w_out, 1, GDN_WIDTH)],
                         d, tiles["out_bn"], BF16, n_mt, "out_proj")
    (x1_p, h2_p), (x1_s, h2_s) = _rowwise(
        _post_mix_body, [(xp, x_sample), (mix_p, mix_s)], [norm_w(n_post_mix), norm_w(n_pre_ffn)],
        [(sds((t, d), F32), sds((ms, d), F32)), (sds((t, d), BF16), sds((ms, d), BF16))],
        tiles["post_rows"], "post_mix")
    act_p, act_s = _proj([(h2_p, h2_s)], [(w_gate, 0, d), (w_up, 0, d)], ff, tiles["ff_bn"], BF16, n_mt,
                         "ffn_gate_up", swiglu=True)
    ffn_p, ffn_s = _down(act_p, act_s, w_down, tiles["ff_bk"], n_mt)
    (y_p,), (y_s,) = _rowwise(
        _residual_norm_body, [(x1_p, x1_s), (ffn_p, ffn_s)], [norm_w(n_post_ffn)],
        [(sds((t, d), F32), sds(x_sample.shape, F32))], tiles["post_rows"], "residual_norm")

    lw = min(WINDOW, t)
    prompt_k = proj_p[t - lw:, K_OFF:K_OFF + KV_WIDTH].reshape(1, lw, ATTN_KV_HEADS, HEAD_DIM)
    prompt_v = proj_p[t - lw:, V_OFF:V_OFF + KV_WIDTH].reshape(1, lw, ATTN_KV_HEADS, HEAD_DIM)
    prompt_conv = proj_p[t - (GDN_CONV - 1):, GQ_OFF:GQ_OFF + GDN_CONV_CH][None]
    sample_conv = proj_s[:, GQ_OFF:GQ_OFF + GDN_CONV_CH].reshape(nb, ts, GDN_CONV_CH)[:, ts - (GDN_CONV - 1):]
    return (y_p.reshape(1, t, d), y_s,
            prompt_k, prompt_v, prompt_conv, s_p[None],
            new_k.reshape(nb, l, ATTN_KV_HEADS, HEAD_DIM), new_v.reshape(nb, l, ATTN_KV_HEADS, HEAD_DIM),
            sample_conv, s_s)


def _tiles(t, ms):
    n_mt = 8 if (t % (8 * 16) == 0 and ms % (8 * 16) == 0) else 1
    return dict(n_mt=n_mt, in_bn=512, out_bn=512, ff_bn=512, ff_bk=512, norm_rows=512, post_rows=256,
                sample_rows=ms, gdn_rows=256, gdn_sample_rows=2 * GDN_CHUNK, attn_gb=8)


def kernel(x_prompt, x_sample, cache_win_k, cache_win_v, state_conv, state_gdn, rel_bias, w_in, attn_sinks,
           gdn_conv_w, gdn_a_log, gdn_dt_bias, gdn_norm_w, w_out, norm_pre_mix, norm_post_mix, norm_pre_ffn,
           norm_post_ffn, w_gate, w_up, w_down):
    depth = w_in.shape[0]
    assert depth == 1 and x_prompt.shape[0] == 1
    tiles = _tiles(x_prompt.shape[1], x_sample.shape[0] * x_sample.shape[1])
    outs = _layer(x_prompt, x_sample, cache_win_k[0], cache_win_v[0], state_conv[0], state_gdn[0], rel_bias,
                  w_in[0], attn_sinks[0], gdn_conv_w[0], gdn_a_log[0], gdn_dt_bias[0], gdn_norm_w[0], w_out[0],
                  norm_pre_mix[0], norm_post_mix[0], norm_pre_ffn[0], norm_post_ffn[0],
                  w_gate[0], w_up[0], w_down[0], tiles)
    yp, ys, pk, pv, pc, ps, sk, sv, sc, ss = outs
    return (yp, ys, pk[None], pv[None], pc[None], ps[None], sk[None], sv[None], sc[None], ss[None])
```

```python
import functools
import math

import jax
import jax.numpy as jnp
from jax import lax
from jax.experimental import pallas as pl
from jax.experimental.pallas import tpu as pltpu

F32 = jnp.float32
BF16 = jnp.bfloat16

HEAD_DIM = 128
ATTN_HEADS = 16
ATTN_KV_HEADS = 4
ATTN_GROUP = ATTN_HEADS // ATTN_KV_HEADS
ATTN_WIDTH = ATTN_HEADS * HEAD_DIM
KV_WIDTH = ATTN_KV_HEADS * HEAD_DIM
WINDOW = 128
REL_BUCKETS = 32
REL_MAX_DIST = 128
GDN_HEADS = 16
GDN_WIDTH = GDN_HEADS * HEAD_DIM
GDN_CONV = 4
GDN_CONV_CH = 3 * GDN_WIDTH
GDN_CHUNK = 64
NORM_EPS = 1e-6
NEG_INF = -1e30

Q_OFF = 0
K_OFF = ATTN_WIDTH
V_OFF = K_OFF + KV_WIDTH
GQ_OFF = V_OFF + KV_WIDTH
GK_OFF = GQ_OFF + GDN_WIDTH
GV_OFF = GK_OFF + GDN_WIDTH
Z_OFF = GV_OFF + GDN_WIDTH
PROJ_MAIN = Z_OFF + GDN_WIDTH
AB_WIDTH = 2 * GDN_HEADS

LANES = 128
SUBLANES = 8
GDN_HB = 8
GDN_NG = GDN_HEADS // GDN_HB
SAMPLE_PAD = 8
SAMPLE_BLOCK_ROWS = 128
VMEM_LIMIT = 56 * 1024 * 1024


def _cparams(n_axes, vmem=VMEM_LIMIT):
    return pltpu.CompilerParams(dimension_semantics=("arbitrary",) * n_axes, vmem_limit_bytes=vmem)


def _dot(a, b):
    return jnp.dot(a, b, preferred_element_type=F32)


def _rms(x, w):
    return x * lax.rsqrt(jnp.mean(x * x, axis=-1, keepdims=True) + NORM_EPS) * w


def _rowwise_kernel(*refs, body, n_in, n_const, n_out, prompt_steps):
    p_in, s_in = refs[:n_in], refs[n_in:2 * n_in]
    consts = refs[2 * n_in:2 * n_in + n_const]
    p_out = refs[2 * n_in + n_const:2 * n_in + n_const + n_out]
    s_out = refs[2 * n_in + n_const + n_out:]

    def run(ins, outs):
        rows2d = [r[...].reshape(-1, r.shape[-1]) for r in ins]
        for o_ref, v in zip(outs, body(*rows2d, *[c[...] for c in consts])):
            o_ref[...] = v.reshape(o_ref.shape).astype(o_ref.dtype)

    i = pl.program_id(0)
    pl.when(i < prompt_steps)(lambda: run(p_in, p_out))
    pl.when(i >= prompt_steps)(lambda: run(s_in, s_out))


def _rowwise(body, pairs, consts, outs, rows, name):
    t = pairs[0][0].shape[0]
    n_sample = math.prod(pairs[0][1].shape[:-1])
    rows = min(rows, t)
    sample_rows = min(SAMPLE_BLOCK_ROWS, n_sample)
    prompt_steps, sample_steps = t // rows, n_sample // sample_rows

    def spec(shape, is_prompt):
        n = rows if is_prompt else sample_rows
        lead = n if len(shape) == 2 else n // shape[1]
        block = (lead,) + tuple(shape[1:])
        if is_prompt:
            index = lambda i: (jnp.minimum(i, prompt_steps - 1),) + (0,) * (len(shape) - 1)
        else:
            index = lambda i: (jnp.maximum(i - prompt_steps, 0),) + (0,) * (len(shape) - 1)
        return pl.BlockSpec(block, index)

    in_specs = ([spec(p.shape, True) for p, _ in pairs] + [spec(s.shape, False) for _, s in pairs]
                + [pl.BlockSpec(c.shape, lambda i, nd=c.ndim: (0,) * nd) for c in consts])
    out_specs = [spec(p.shape, True) for p, _ in outs] + [spec(s.shape, False) for _, s in outs]
    res = pl.pallas_call(
        functools.partial(_rowwise_kernel, body=body, n_in=len(pairs), n_const=len(consts), n_out=len(outs),
                          prompt_steps=prompt_steps),
        grid=(prompt_steps + sample_steps,),
        in_specs=in_specs,
        out_specs=out_specs,
        out_shape=[p for p, _ in outs] + [s for _, s in outs],
        compiler_params=_cparams(1),
        name=name,
    )(*[p for p, _ in pairs], *[s for _, s in pairs], *consts)
    return res[:len(outs)], res[len(outs):]


def _rmsnorm_tail_body(x, w, w_tail):
    h = _rms(x, w).astype(BF16)
    return h, _dot_nt(h, w_tail)


def _post_mix_body(x, mix, w_post, w_pre):
    return (_rms(x + _rms(mix.astype(F32), w_post), w_pre),)


def _residual_body(x, mix, ffn, w_post_mix, w_post_ffn):
    return (x + _rms(mix.astype(F32), w_post_mix) + _rms(ffn.astype(F32), w_post_ffn),)


def _stack_rows(p_ref, s_ref, cols=slice(None)):
    return jnp.concatenate([p_ref[:, cols].astype(BF16), s_ref[:, cols].astype(BF16)], axis=0)


def _dot_nt(a, b):
    return lax.dot_general(a, b, (((1,), (1,)), ((), ())), preferred_element_type=F32)


def _proj_kernel(*refs, n_a, n_w, swiglu, w_transposed, row_blocks, n_mt, n_tiles, last_cols):
    a_refs = refs[:2 * n_a]
    w_refs = refs[2 * n_a:2 * n_a + n_w]
    op_ref, os_ref = refs[2 * n_a + n_w:2 * n_a + n_w + 2]
    scratch = refs[2 * n_a + n_w + 2:]
    stage_refs, wb_refs, sem = scratch[:n_w], scratch[n_w:2 * n_w], scratch[2 * n_w]
    mm = _dot_nt if w_transposed else _dot
    j, i = pl.program_id(0), pl.program_id(1)
    nj = pl.num_programs(0)

    def slice_copy(w, tile, s, slot, narrow):
        rows, cols = stage_refs[w].shape[1:]
        if w_transposed:
            src = w_refs[w].at[pl.ds(tile * (rows * n_mt) + s * rows, rows), pl.ds(row_blocks[w] * cols, cols)]
            dst = stage_refs[w].at[slot]
        else:
            width = last_cols if narrow else cols
            src = w_refs[w].at[pl.ds((row_blocks[w] * n_mt + s) * rows, rows), pl.ds(tile * cols, width)]
            dst = stage_refs[w].at[slot, :, pl.ds(0, width)]
        return pltpu.make_async_copy(src, dst, sem.at[w, slot])

    def with_tile(tile, fn):
        if last_cols == stage_refs[0].shape[2] or w_transposed:
            fn(False)
        elif isinstance(tile, int):
            fn(tile == n_tiles - 1)
        else:
            pl.when(tile == n_tiles - 1)(lambda: fn(True))
            pl.when(tile != n_tiles - 1)(lambda: fn(False))

    def cast_slice(w, half, s, slot):
        rows = stage_refs[w].shape[1]
        wb_refs[w][half, pl.ds(pl.multiple_of(s * rows, rows), rows), :] = stage_refs[w][slot].astype(BF16)

    @pl.when(jnp.logical_and(j == 0, i == 0))
    def _():
        if last_cols != stage_refs[0].shape[2]:
            for w in range(n_w):
                stage_refs[w][...] = jnp.zeros_like(stage_refs[w])
        for w in range(n_w):
            for s in range(n_mt):
                def fetch_now(narrow, w=w, s=s):
                    copy = slice_copy(w, 0, s, s % 2, narrow)
                    copy.start()
                    copy.wait()
                with_tile(0, fetch_now)
                cast_slice(w, 0, s, s % 2)
        if n_tiles > 1:
            for w in range(n_w):
                with_tile(1, lambda narrow, w=w: slice_copy(w, 1, 0, 0, narrow).start())

    @pl.when(j + 1 < nj)
    def _():
        slot = (j * n_mt + i) % 2
        for w in range(n_w):
            with_tile(j + 1, lambda narrow, w=w: slice_copy(w, j + 1, i, slot, narrow).wait())
            cast_slice(w, (j + 1) % 2, i, slot)
        last = i + 1 == n_mt
        next_tile = jnp.where(last, j + 2, j + 1)
        next_s = jnp.where(last, 0, i + 1)

        @pl.when(next_tile < nj)
        def _():
            for w in range(n_w):
                with_tile(next_tile, lambda narrow, w=w: slice_copy(w, next_tile, next_s, 1 - slot, narrow).start())

    half = j % 2
    bmp = op_ref.shape[0]

    def apply(width):
        cols = slice(None) if w_transposed else slice(0, width)
        rhs = [wb_ref[half, :, cols] for wb_ref in wb_refs]
        lhs = [_stack_rows(a_refs[2 * p], a_refs[2 * p + 1]) for p in range(n_a)]
        if swiglu:
            g = mm(lhs[0], rhs[0])
            o = g * jax.nn.sigmoid(g) * mm(lhs[0], rhs[1])
        else:
            o = mm(lhs[0], rhs[0])
            for l, r in zip(lhs[1:], rhs[1:]):
                o = o + mm(l, r)
        op_ref[:, :width] = o[:bmp].astype(op_ref.dtype)
        os_ref[:, :width] = o[bmp:].astype(os_ref.dtype)

    full_cols = op_ref.shape[1]
    if last_cols == full_cols:
        apply(full_cols)
    else:
        pl.when(j != n_tiles - 1)(lambda: apply(full_cols))
        pl.when(j == n_tiles - 1)(lambda: apply(last_cols))


def _proj(a_pairs, w_blocks, n, bn, out_dtype, n_mt, name, swiglu=False, w_transposed=False):
    tp, ts = a_pairs[0][0].shape[0], a_pairs[0][1].shape[0]
    bmp, bms = tp // n_mt, ts // n_mt
    in_specs = []
    for a_p, a_s in a_pairs:
        in_specs.append(pl.BlockSpec((bmp, a_p.shape[1]), lambda j, i: (i, 0)))
        in_specs.append(pl.BlockSpec((bms, a_s.shape[1]), lambda j, i: (i, 0)))
    in_specs += [pl.BlockSpec(memory_space=pl.ANY) for _ in w_blocks]
    n_tiles = pl.cdiv(n, bn)
    tile = lambda k: (bn, k) if w_transposed else (k, bn)
    stage = lambda k: (2, bn // n_mt, k) if w_transposed else (2, k // n_mt, bn)
    operands = [a for pair in a_pairs for a in pair] + [w for w, _, _ in w_blocks]
    return pl.pallas_call(
        functools.partial(_proj_kernel, n_a=len(a_pairs), n_w=len(w_blocks), swiglu=swiglu,
                          w_transposed=w_transposed, row_blocks=tuple(rb for _, rb, _ in w_blocks), n_mt=n_mt,
                          n_tiles=n_tiles, last_cols=n - (n_tiles - 1) * bn),
        grid=(n_tiles, n_mt),
        in_specs=in_specs,
        out_specs=[pl.BlockSpec((bmp, bn), lambda j, i: (i, j)), pl.BlockSpec((bms, bn), lambda j, i: (i, j))],
        out_shape=[jax.ShapeDtypeStruct((tp, n), out_dtype), jax.ShapeDtypeStruct((ts, n), out_dtype)],
        scratch_shapes=([pltpu.VMEM(stage(k), F32) for _, _, k in w_blocks]
                        + [pltpu.VMEM((2,) + tile(k), BF16) for _, _, k in w_blocks]
                        + [pltpu.SemaphoreType.DMA((len(w_blocks), 2))]),
        compiler_params=_cparams(2),
        name=name,
    )(*operands)


def _down_kernel(ap_ref, as_ref, b_ref, op_ref, os_ref, accp_ref, accs_ref, *, bn, k_total):
    k = pl.program_id(1)
    nk = pl.num_programs(1)
    bk = b_ref.shape[0]
    bmp = ap_ref.shape[0]
    last_rows = k_total - (k_total // bk) * bk

    @pl.when(k == 0)
    def _():
        accp_ref[...] = jnp.zeros_like(accp_ref)
        accs_ref[...] = jnp.zeros_like(accs_ref)

    def accumulate(kk):
        a = _stack_rows(ap_ref, as_ref, slice(0, kk))
        for n0 in range(0, accp_ref.shape[1], bn):
            d = _dot(a, b_ref[:kk, n0:n0 + bn].astype(BF16))
            accp_ref[:, n0:n0 + bn] += d[:bmp]
            accs_ref[:, n0:n0 + bn] += d[bmp:]

    if last_rows == 0:
        accumulate(bk)
    else:
        pl.when(k < nk - 1)(lambda: accumulate(bk))
        pl.when(k == nk - 1)(lambda: accumulate(last_rows))

    @pl.when(k == nk - 1)
    def _():
        op_ref[...] = accp_ref[...].astype(op_ref.dtype)
        os_ref[...] = accs_ref[...].astype(os_ref.dtype)


def _down(a_p, a_s, b, bk, n_mt):
    (tp, k_total), ts = a_p.shape, a_s.shape[0]
    n = b.shape[1]
    bmp, bms = tp // n_mt, ts // n_mt
    return pl.pallas_call(
        functools.partial(_down_kernel, bn=min(n, 512), k_total=k_total),
        grid=(n_mt, pl.cdiv(k_total, bk)),
        in_specs=[pl.BlockSpec((bmp, bk), lambda i, k: (i, k)), pl.BlockSpec((bms, bk), lambda i, k: (i, k)),
                  pl.BlockSpec((bk, n), lambda i, k: (k, 0))],
        out_specs=[pl.BlockSpec((bmp, n), lambda i, k: (i, 0), pipeline_mode=pl.Buffered(1)),
                   pl.BlockSpec((bms, n), lambda i, k: (i, 0))],
        out_shape=[jax.ShapeDtypeStruct((tp, n), BF16), jax.ShapeDtypeStruct((ts, n), BF16)],
        scratch_shapes=[pltpu.VMEM((bmp, n), F32), pltpu.VMEM((bms, n), F32)],
        compiler_params=_cparams(2),
        name="ffn_down",
    )(a_p, a_s, b)


def _rel_bucket(d):
    n = jnp.maximum(d, 0)
    max_exact = REL_BUCKETS // 2
    nf = jnp.maximum(n, 1).astype(F32)
    large = max_exact + (jnp.log(nf / max_exact) / math.log(REL_MAX_DIST / max_exact)
                         * (REL_BUCKETS - max_exact)).astype(jnp.int32)
    large = jnp.minimum(large, REL_BUCKETS - 1)
    return jnp.where(n < max_exact, n, large)


def _bias_kernel(rel_ref, bucket_ref, allowed_ref, o_ref):
    h = pl.program_id(0)
    bucket = bucket_ref[...]
    acc = jnp.zeros(bucket.shape, F32)
    for n in range(REL_BUCKETS):
        acc = jnp.where(bucket == n, rel_ref[n, h], acc)
    o_ref[0] = jnp.where(allowed_ref[...] > 0, acc, NEG_INF)


def _bias_table(rel_bias, dist):
    tq, tk = dist.shape
    allowed = ((dist >= 0) & (dist <= WINDOW)).astype(jnp.int32)
    full = pl.BlockSpec((tq, tk), lambda h: (0, 0))
    return pl.pallas_call(
        _bias_kernel,
        grid=(ATTN_HEADS,),
        in_specs=[pl.BlockSpec(memory_space=pltpu.SMEM), full, full],
        out_specs=pl.BlockSpec((1, tq, tk), lambda h: (h, 0, 0)),
        out_shape=jax.ShapeDtypeStruct((ATTN_HEADS, tq, tk), F32),
        compiler_params=_cparams(1),
        name="rel_bias_table",
    )(rel_bias.astype(F32), _rel_bucket(dist), allowed)


def _softmax_pv(s, sink):
    m = jnp.maximum(jnp.max(s, axis=-1, keepdims=True), sink)
    p = jnp.exp(s - m)
    denom = jnp.sum(p, axis=-1, keepdims=True) + jnp.exp(sink - m)
    return p.astype(BF16), denom


def _attn_prompt_kernel(sink_ref, cur_ref, prev_ref, bias_ref, o_ref):
    c = WINDOW
    head_cols = lambda base, h: slice(base + h * HEAD_DIM, base + (h + 1) * HEAD_DIM)
    kk, vv = [], []
    for h in range(ATTN_KV_HEADS):
        kk.append(jnp.concatenate([prev_ref[:, head_cols(0, h)], cur_ref[:, head_cols(K_OFF, h)]],
                                  axis=0).astype(BF16))
        vv.append(jnp.concatenate([prev_ref[:, head_cols(KV_WIDTH, h)], cur_ref[:, head_cols(V_OFF, h)]],
                                  axis=0).astype(BF16))
    heads = range(ATTN_HEADS)
    scale = HEAD_DIM ** -0.5
    s = [_dot_nt((cur_ref[:, head_cols(Q_OFF, hd)] * scale).astype(BF16), kk[hd // ATTN_GROUP]) for hd in heads]
    pd = [_softmax_pv(s[hd] + bias_ref[hd], sink_ref[hd]) for hd in heads]
    outs = [_dot(pd[hd][0], vv[hd // ATTN_GROUP]) / pd[hd][1] for hd in heads]
    o_ref[...] = jnp.concatenate(outs, axis=1).astype(o_ref.dtype)


def _attn_prompt(proj, bias, sinks):
    t = proj.shape[0]
    c = WINDOW
    qkv = V_OFF + KV_WIDTH
    first = jnp.where(jnp.arange(2 * c) < c, NEG_INF, bias)
    tables = jnp.stack([first, bias])
    return pl.pallas_call(
        _attn_prompt_kernel,
        grid=(t // c,),
        in_specs=[pl.BlockSpec(memory_space=pltpu.SMEM),
                  pl.BlockSpec((c, qkv), lambda i: (i, 0)),
                  pl.BlockSpec((c, 2 * KV_WIDTH), lambda i: (jnp.maximum(i - 1, 0), K_OFF // (2 * KV_WIDTH))),
                  pl.BlockSpec((None, ATTN_HEADS, c, 2 * c), lambda i: (jnp.minimum(i, 1), 0, 0, 0))],
        out_specs=pl.BlockSpec((c, ATTN_WIDTH), lambda i: (i, 0)),
        out_shape=jax.ShapeDtypeStruct((t, ATTN_WIDTH), BF16),
        compiler_params=_cparams(1),
        name="attn_prompt",
    )(sinks.astype(F32), proj, proj, tables)


def _attn_sample_kernel(x_ref, ck_ref, cv_ref, bias_c_ref, bias_n_ref, sink_ref, o_ref, ok_ref, ov_ref,
                        q_s, kn_s, vn_s, *, t_new):
    gb, l, kvh, hd = ck_ref.shape
    kc = ck_ref[...].reshape(gb, l * kvh, hd)
    vc = cv_ref[...].reshape(gb, l * kvh, hd)
    for b in range(gb):
        rows = slice(b * t_new, (b + 1) * t_new)
        for h in range(kvh):
            kn_s[b, h * t_new:(h + 1) * t_new, :] = x_ref[rows, K_OFF + h * hd:K_OFF + (h + 1) * hd]
            vn_s[b, h * t_new:(h + 1) * t_new, :] = x_ref[rows, V_OFF + h * hd:V_OFF + (h + 1) * hd]
        for head in range(ATTN_HEADS):
            q_s[b, head * t_new:(head + 1) * t_new, :] = x_ref[rows, head * hd:(head + 1) * hd]

    q = q_s[...].astype(BF16)
    scale = hd ** -0.5
    s_c = jnp.einsum("bqd,bkd->bqk", q, kc.astype(BF16), preferred_element_type=F32) * scale + bias_c_ref[...]
    s_n = jnp.einsum("bqd,bkd->bqk", q, kn_s[...].astype(BF16), preferred_element_type=F32) * scale + bias_n_ref[...]
    sink = sink_ref[...]
    m = jnp.maximum(jnp.maximum(jnp.max(s_c, axis=-1, keepdims=True), jnp.max(s_n, axis=-1, keepdims=True)), sink)
    p_c = jnp.exp(s_c - m)
    p_n = jnp.exp(s_n - m)
    denom = jnp.sum(p_c, axis=-1, keepdims=True) + jnp.sum(p_n, axis=-1, keepdims=True) + jnp.exp(sink - m)
    o = (jnp.einsum("bqk,bkd->bqd", p_c.astype(BF16), vc.astype(BF16), preferred_element_type=F32)
         + jnp.einsum("bqk,bkd->bqd", p_n.astype(BF16), vn_s[...].astype(BF16), preferred_element_type=F32)) / denom
    for b in range(gb):
        for head in range(ATTN_HEADS):
            o_ref[b * t_new:(b + 1) * t_new, head * hd:(head + 1) * hd] = o[b, head * t_new:(head + 1) * t_new, :]

    shift = t_new * kvh
    for c_val, n_s, out_ref in ((kc, kn_s, ok_ref), (vc, vn_s, ov_ref)):
        new_rows = [n_s[:, h * t_new + t:h * t_new + t + 1, :] for t in range(t_new) for h in range(kvh)]
        win = jnp.concatenate([c_val[:, shift:, :]] + new_rows, axis=1)
        out_ref[...] = win.reshape(gb, l, kvh, hd)


def _attn_sample(proj, cache_k, cache_v, bias_c, bias_n, sink_rows, t_new, gb):
    b, l = cache_k.shape[:2]
    rows = ATTN_HEADS * t_new
    qkv = V_OFF + KV_WIDTH
    cache_spec = pl.BlockSpec((gb, l, ATTN_KV_HEADS, HEAD_DIM), lambda i: (i, 0, 0, 0))
    full = lambda a: pl.BlockSpec(a.shape, lambda i: (0,) * a.ndim)
    return pl.pallas_call(
        functools.partial(_attn_sample_kernel, t_new=t_new),
        grid=(b // gb,),
        in_specs=[pl.BlockSpec((gb * t_new, qkv), lambda i: (i, 0)), cache_spec, cache_spec,
                  full(bias_c), full(bias_n), full(sink_rows)],
        out_specs=[pl.BlockSpec((gb * t_new, ATTN_WIDTH), lambda i: (i, 0)), cache_spec, cache_spec],
        out_shape=[jax.ShapeDtypeStruct((b * t_new, ATTN_WIDTH), F32),
                   jax.ShapeDtypeStruct(cache_k.shape, F32),
                   jax.ShapeDtypeStruct(cache_v.shape, F32)],
        scratch_shapes=[pltpu.VMEM((gb, rows, HEAD_DIM), F32),
                        pltpu.VMEM((gb, ATTN_KV_HEADS * t_new, HEAD_DIM), F32),
                        pltpu.VMEM((gb, ATTN_KV_HEADS * t_new, HEAD_DIM), F32)],
        compiler_params=_cparams(1),
        name="attn_sample",
    )(proj, cache_k, cache_v, bias_c, bias_n, sink_rows)


def _split3(x):
    hi = x.astype(BF16)
    r = x - hi.astype(F32)
    mid = r.astype(BF16)
    lo = (r - mid.astype(F32)).astype(BF16)
    return hi, mid, lo


def _select_dot(sel, x):
    hi, mid, lo = _split3(x)
    return _dot(sel, hi) + _dot(sel, mid) + _dot(sel, lo)


def _dot_select(x, sel):
    hi, mid, lo = _split3(x)
    return _dot(hi, sel) + _dot(mid, sel) + _dot(lo, sel)


def _softplus(x):
    return jnp.maximum(x, 0.0) + jnp.log1p(jnp.exp(-jnp.abs(x)))


def _conv_silu(pad_ref, w, r0, n):
    top = SUBLANES + r0
    y = pad_ref[top:top + n, :] * w[GDN_CONV - 1:GDN_CONV]
    for s in range(1, GDN_CONV):
        y = y + pad_ref[top - s:top - s + n, :] * w[GDN_CONV - 1 - s:GDN_CONV - s]
    return y * jax.nn.sigmoid(y)


def _l2norm_heads(x, scale):
    outs = []
    for hh in range(GDN_HB):
        xh = x[:, hh * HEAD_DIM:(hh + 1) * HEAD_DIM]
        outs.append(xh * (lax.rsqrt(jnp.sum(xh * xh, axis=-1, keepdims=True) + NORM_EPS) * scale))
    return jnp.concatenate(outs, axis=1)


GDN_N_PADS = 4
STAGE_ROWS = 32
PROBE_BOUND = 1e30
GDN_N_STAGED = 10


def _gdn_kernel(*refs, rows, seq, carry, live_from):
    kw = dict(rows=rows, seq=seq, carry=carry, live_from=live_from)
    if not carry:
        io, pads, staged = refs[:-GDN_N_PADS - GDN_N_STAGED], refs[-GDN_N_PADS - GDN_N_STAGED:-GDN_N_STAGED], \
            refs[-GDN_N_STAGED:]
        _gdn_body(io, pads, staged, staged, **kw)
        return
    n_scratch = GDN_N_PADS + 2 * GDN_N_STAGED
    io, pads = refs[:-n_scratch], refs[-n_scratch:-2 * GDN_N_STAGED]
    set_a, set_b = refs[-2 * GDN_N_STAGED:-GDN_N_STAGED], refs[-GDN_N_STAGED:]
    s_ref = io[-1]
    step = pl.program_id(1)

    @pl.when(step == 0)
    def _():
        s_ref[...] = jnp.zeros_like(s_ref)
        for ref in set_b:
            ref[...] = jnp.zeros_like(ref)

    pl.when(step % 2 == 0)(lambda: _gdn_body(io, pads, set_a, set_b, **kw))
    pl.when(step % 2 == 1)(lambda: _gdn_body(io, pads, set_b, set_a, **kw))


def _gdn_body(refs, pads, staged_w, staged_r, *, rows, seq, carry, live_from):
    c = GDN_CHUNK
    nc = rows // c
    nseq = c // seq
    n_live = seq - live_from
    lw = GDN_HB * HEAD_DIM
    if carry:
        (xq_ref, xk_ref, xv_ref, tq_ref, tk_ref, tv_ref, z_ref, wq_ref, wk_ref, wv_ref, gn_ref,
         ab_ref, pc_ref, og_ref, s_ref) = refs
        s0_ref = None
    else:
        (xq_ref, xk_ref, xv_ref, cq_ref, ck_ref, cv_ref, z_in, wq_ref, wk_ref, wv_ref, gn_ref,
         ab_in, pc_ref, s0_ref, og_ref, s_ref, z_ref, ab_ref) = refs
    padq_s, padk_s, padv_s, eb_s = pads
    kn_w, kb_w, qn_w, vb_w, kbe_w, qe_w, kd_w, egc_w, egl_w, gcr_w = staged_w
    kn_s, kb_s, qn_s, vb_s, kbe_s, qe_s, kd_s, egc_s, egl_s, gcr_s = staged_r

    group = pl.program_id(0)
    step = pl.program_id(1)
    row = lax.broadcasted_iota(jnp.int32, (rows, 1), 0)
    live = (row % seq) >= live_from
    keep_live = (lambda x: jnp.where(live, x, 0.0)) if live_from else (lambda x: x)

    if carry:
        for pad_ref, x_ref, t_ref in ((padq_s, xq_ref, tq_ref), (padk_s, xk_ref, tk_ref), (padv_s, xv_ref, tv_ref)):
            pad_ref[0:SUBLANES, :] = jnp.where(step == 0, 0.0, t_ref[...])
            pad_ref[SUBLANES:, :] = x_ref[...]
    else:
        def gather(dst_ref, x_ref, hist_ref, top):
            dst_ref[...] = jnp.zeros_like(dst_ref)
            for b in range(rows // seq):
                r0 = top + b * seq + live_from
                if hist_ref is not None:
                    dst_ref[r0 - (GDN_CONV - 1):r0, :] = hist_ref[:, b, :]
                dst_ref[r0:r0 + n_live, :] = x_ref[b * n_live:(b + 1) * n_live, :]

        gather(padq_s, xq_ref, cq_ref, SUBLANES)
        gather(padk_s, xk_ref, ck_ref, SUBLANES)
        gather(padv_s, xv_ref, cv_ref, SUBLANES)
        gather(z_ref, z_in, None, 0)
        gather(ab_ref, ab_in, None, 0)

    def stage_decays():
        ab = ab_ref[...]
        g_col = keep_live(-jnp.exp(pc_ref[0:1, :]) * _softplus(ab + pc_ref[1:2, :]))
        beta_col = keep_live(jax.nn.sigmoid(ab))
        ri = lax.broadcasted_iota(jnp.int32, (rows, rows), 0)
        ci = lax.broadcasted_iota(jnp.int32, (rows, rows), 1)
        same = (ri // seq) == (ci // seq)
        cum_sel = jnp.where(jnp.logical_and(same, ci <= ri), 1.0, 0.0).astype(BF16)
        tot_sel = jnp.where(same, 1.0, 0.0).astype(BF16)
        gc_col = _select_dot(cum_sel, g_col)
        gl_col = _select_dot(tot_sel, g_col)
        first = group * GDN_HB
        li = lax.broadcasted_iota(jnp.int32, (LANES, lw), 0)
        ni = lax.broadcasted_iota(jnp.int32, (LANES, lw), 1)
        spread_g = jnp.where(li == ni // HEAD_DIM + first, 1.0, 0.0).astype(BF16)
        spread_b = jnp.where(li == ni // HEAD_DIM + first + GDN_HEADS, 1.0, 0.0).astype(BF16)
        egc_w[...] = _dot_select(gc_col, spread_g)
        egl_w[...] = _dot_select(gl_col, spread_g)
        eb_s[...] = _dot_select(beta_col, spread_b)
        pi = lax.broadcasted_iota(jnp.int32, (LANES, LANES), 0)
        pj = lax.broadcasted_iota(jnp.int32, (LANES, LANES), 1)
        pick = jnp.where(jnp.logical_and(pi == pj + first, pj < GDN_HB), 1.0, 0.0).astype(BF16)
        gc_heads = _dot_select(gc_col, pick)
        if rows % LANES:
            gc_heads = jnp.concatenate([gc_heads, jnp.zeros((LANES - rows % LANES, LANES), F32)], axis=0)
        gcr_w[...] = gc_heads.T[:SUBLANES, :rows]

    def stage_rows(r0, n, after=None):
        sl = slice(r0, r0 + n)
        keep = (lambda x: jnp.where(live[sl], x, 0.0)) if live_from else (lambda x: x)
        gate = 1.0 if after is None else jnp.where(after > PROBE_BOUND, 0.0, 1.0)
        conv = lambda pad_ref, w_ref: keep(_conv_silu(pad_ref, w_ref[...] * gate, r0, n))
        qn = _l2norm_heads(conv(padq_s, wq_ref), HEAD_DIM ** -0.5)
        kn = _l2norm_heads(conv(padk_s, wk_ref), 1.0)
        v = conv(padv_s, wv_ref)
        egc, egl, eb = egc_w[sl, :], egl_w[sl, :], eb_s[sl, :]
        decay_in = jnp.exp(egc)
        kb = kn * eb
        kn_w[sl, :] = kn
        kb_w[sl, :] = kb
        qn_w[sl, :] = qn
        vb_w[sl, :] = v * eb
        kbe_w[sl, :] = kb * decay_in
        qe_w[sl, :] = qn * decay_in
        kd_w[sl, :] = kn * jnp.exp(egl - egc)

    n_slices = rows // STAGE_ROWS if carry else 1
    pending = [functools.partial(stage_rows, r * (rows // n_slices), rows // n_slices) for r in range(n_slices)]
    if not carry:
        stage_decays()
        pending.pop(0)()

    cr = lax.broadcasted_iota(jnp.int32, (c, c), 0)
    cc = lax.broadcasted_iota(jnp.int32, (c, c), 1)
    same_c = (cr // seq) == (cc // seq)
    causal_bias = jnp.where(jnp.logical_and(same_c, cr >= cc), 0.0, NEG_INF)
    strict = jnp.where(jnp.logical_and(same_c, cr > cc), 1.0, 0.0)

    eye = jnp.where(cr == cc, 1.0, 0.0)
    seq_of_row = lax.broadcasted_iota(jnp.int32, (c, HEAD_DIM), 0) // seq

    def lower_left(s):
        return jnp.logical_and(jnp.logical_and(cr // (2 * s) == cc // (2 * s), (cr // s) % 2 == 1),
                               (cc // s) % 2 == 0)

    merge_sizes = [2 ** e for e in range(1, max(1, math.ceil(math.log2(n_live))))]
    pair_mask = jnp.where(lower_left(1), 1.0, 0.0)
    merge_masks = [jnp.where(lower_left(s), 1.0, 0.0) for s in merge_sizes]

    pairs = [(ch, hh) for ch in range(nc) for hh in range(GDN_HB)]
    rows_of = lambda ch: slice(ch * c, (ch + 1) * c)
    lanes_of = lambda hh: slice(hh * HEAD_DIM, (hh + 1) * HEAD_DIM)
    uw_all, qk_all, a_mat, t_mat, t_a = {}, {}, {}, {}, {}
    for ch, hh in pairs:
        rs, hs = rows_of(ch), lanes_of(hh)
        gcol = egc_s[rs, hh * HEAD_DIM:hh * HEAD_DIM + c]
        grow = gcr_s[hh:hh + 1, rs]
        decay = jnp.exp(gcol - grow + causal_bias)
        kq = jnp.concatenate([kb_s[rs, hs], qn_s[rs, hs]], axis=0).astype(BF16)
        kk = _dot_nt(kq, kn_s[rs, hs].astype(BF16)) * jnp.concatenate([decay, decay], axis=0)
        a_mat[ch, hh] = kk[:c] * strict
        qk_all[ch, hh] = kk[c:].astype(BF16)
        t_mat[ch, hh] = eye - a_mat[ch, hh] * pair_mask
    probes = [a_mat[pairs[-1]][0:1, 0:1]]
    if carry:
        stage_decays()
    for mask in merge_masks:
        for p in pairs:
            t_a[p] = _dot(t_mat[p].astype(BF16), (a_mat[p] * mask).astype(BF16))
        for p in pairs:
            t_mat[p] = t_mat[p] - _dot(t_a[p].astype(BF16), t_mat[p].astype(BF16))
        probes.append(t_mat[pairs[-1]][0:1, 0:1])
    for ch, hh in pairs:
        rs, hs = rows_of(ch), lanes_of(hh)
        uw_all[ch, hh] = _dot(t_mat[ch, hh].astype(BF16),
                              jnp.concatenate([vb_s[rs, hs], kbe_s[rs, hs]], axis=1).astype(BF16))

    if carry:
        state = [s_ref[hh] for hh in range(GDN_HB)]
    new_states = {}
    heads = range(GDN_HB)
    tn_dot = lambda a, b: lax.dot_general(a, b, (((0,), (0,)), ((), ())), preferred_element_type=F32)
    for ch in range(nc):
        rs = rows_of(ch)
        w_s, q_s, v_new_b, o = {}, {}, {}, {}
        for hh in heads:
            w = uw_all[ch, hh][:, HEAD_DIM:]
            q_e = qe_s[rs, lanes_of(hh)]
            if carry:
                wq = _dot(jnp.concatenate([w, q_e], axis=0).astype(BF16), state[hh].astype(BF16))
                w_s[hh], q_s[hh] = wq[:c], wq[c:]
            else:
                parts = []
                for b in range(nseq):
                    bs = slice(b * seq, (b + 1) * seq)
                    parts.append(_dot(jnp.concatenate([w[bs], q_e[bs]], axis=0).astype(BF16),
                                      s0_ref[ch * nseq + b, hh].astype(BF16)))
                w_s[hh] = jnp.concatenate([p[:seq] for p in parts], axis=0)
                q_s[hh] = jnp.concatenate([p[seq:] for p in parts], axis=0)
        for hh in heads:
            v_new_b[hh] = (uw_all[ch, hh][:, :HEAD_DIM] - w_s[hh]).astype(BF16)
            o[hh] = q_s[hh] + _dot(qk_all[ch, hh], v_new_b[hh])
        for hh in heads:
            hs = lanes_of(hh)
            k_d = kd_s[rs, hs]
            if carry:
                state[hh] = (state[hh] * jnp.exp(egl_s[ch * c:ch * c + 1, hs])
                             + tn_dot(k_d.astype(BF16), v_new_b[hh]))
            else:
                for b in range(nseq):
                    k_db = jnp.where(seq_of_row == b, k_d, 0.0).astype(BF16)
                    r1 = ch * c + b * seq
                    new_states[ch * nseq + b, hh] = (
                        s0_ref[ch * nseq + b, hh] * jnp.exp(egl_s[r1:r1 + 1, hs]) + tn_dot(k_db, v_new_b[hh]))
        if carry:
            probes.append(_dot(qk_all[ch, 0], jnp.ones((c, HEAD_DIM), BF16))[0:1, 0:1])
        outs = []
        for hh in heads:
            z = z_ref[rs, lanes_of(hh)]
            outs.append(_rms(o[hh], gn_ref[...]) * (z * jax.nn.sigmoid(z)))
        og = jnp.concatenate(outs, axis=1)
        if carry:
            og_ref[rs, :] = og.astype(og_ref.dtype)
        else:
            for b in range(nseq):
                r0 = b * seq + live_from
                og_ref[(ch * nseq + b) * n_live:(ch * nseq + b + 1) * n_live, :] = (
                    og[r0:r0 + n_live].astype(og_ref.dtype))
    for n, stage in enumerate(pending):
        stage(after=probes[min(n, len(probes) - 1)])
    if carry:
        s_ref[...] = jnp.stack(state, axis=0)
    else:
        for (b, hh), s_new in new_states.items():
            s_ref[b, hh] = s_new


def _gdn_params(a_log, dt_bias):
    rows = jnp.stack([a_log.astype(F32), dt_bias.astype(F32)], axis=0)
    return jnp.pad(rows, ((0, SUBLANES - 2), (0, LANES - GDN_HEADS)))


def _gdn_scratch(rows, n_sets):
    lw = GDN_HB * HEAD_DIM
    staged = [(rows, lw)] * (GDN_N_STAGED - 1) + [(SUBLANES, rows)]
    return ([pltpu.VMEM((SUBLANES + rows, lw), F32) for _ in range(GDN_N_PADS - 1)] + [pltpu.VMEM((rows, lw), F32)]
            + [pltpu.VMEM(shape, F32) for _ in range(n_sets) for shape in staged])


def _gdn_prompt(proj, ab, conv_w, gnorm_w, params, rows):
    t = proj.shape[0]
    nb = t // rows
    lw = GDN_HB * HEAD_DIM
    head = lambda i: jnp.minimum(i, nb - 1)
    done = lambda i: jnp.maximum(i - 1, 0)
    x_spec = lambda off: pl.BlockSpec((rows, lw), lambda g, i: (head(i), off // lw + g))
    tail_spec = lambda off: pl.BlockSpec(
        (SUBLANES, lw), lambda g, i: (jnp.maximum(head(i) * (rows // SUBLANES) - 1, 0), off // lw + g))
    w_spec = lambda off: pl.BlockSpec((GDN_CONV, lw), lambda g, i: (0, (off - GQ_OFF) // lw + g))
    return pl.pallas_call(
        functools.partial(_gdn_kernel, rows=rows, seq=GDN_CHUNK, carry=True, live_from=0),
        grid=(GDN_NG, nb + 1),
        in_specs=[x_spec(GQ_OFF), x_spec(GK_OFF), x_spec(GV_OFF),
                  tail_spec(GQ_OFF), tail_spec(GK_OFF), tail_spec(GV_OFF),
                  pl.BlockSpec((rows, lw), lambda g, i: (done(i), Z_OFF // lw + g)),
                  w_spec(GQ_OFF), w_spec(GK_OFF), w_spec(GV_OFF),
                  pl.BlockSpec((1, HEAD_DIM), lambda g, i: (0, 0)),
                  pl.BlockSpec((rows, LANES), lambda g, i: (head(i), 0)),
                  pl.BlockSpec((SUBLANES, LANES), lambda g, i: (0, 0))],
        out_specs=[pl.BlockSpec((rows, lw), lambda g, i: (done(i), g)),
                   pl.BlockSpec((GDN_HB, HEAD_DIM, HEAD_DIM), lambda g, i: (g, 0, 0))],
        out_shape=[jax.ShapeDtypeStruct((t, GDN_WIDTH), BF16),
                   jax.ShapeDtypeStruct((GDN_HEADS, HEAD_DIM, HEAD_DIM), F32)],
        scratch_shapes=_gdn_scratch(rows, 2),
        compiler_params=_cparams(2),
        name="gdn_prompt",
    )(proj, proj, proj, proj, proj, proj, proj, conv_w, conv_w, conv_w,
      gnorm_w.reshape(1, HEAD_DIM).astype(F32), ab, params)


def _gdn_sample(proj, ab, conv_state, state, conv_w, gnorm_w, params, n_live, rows):
    lw = GDN_HB * HEAD_DIM
    nseq = rows // SAMPLE_PAD
    n_in = nseq * n_live
    x_spec = lambda off: pl.BlockSpec((n_in, lw), lambda g, i: (i, off // lw + g))
    hist_spec = lambda off: pl.BlockSpec((GDN_CONV - 1, nseq, lw), lambda g, i: (0, i, (off - GQ_OFF) // lw + g))
    w_spec = lambda off: pl.BlockSpec((GDN_CONV, lw), lambda g, i: (0, (off - GQ_OFF) // lw + g))
    s_spec = pl.BlockSpec((nseq, GDN_HB, HEAD_DIM, HEAD_DIM), lambda g, i: (i, g, 0, 0))
    return pl.pallas_call(
        functools.partial(_gdn_kernel, rows=rows, seq=SAMPLE_PAD, carry=False, live_from=SAMPLE_PAD - n_live),
        grid=(GDN_NG, proj.shape[0] // n_in),
        in_specs=[x_spec(GQ_OFF), x_spec(GK_OFF), x_spec(GV_OFF),
                  hist_spec(GQ_OFF), hist_spec(GK_OFF), hist_spec(GV_OFF),
                  x_spec(Z_OFF), w_spec(GQ_OFF), w_spec(GK_OFF), w_spec(GV_OFF),
                  pl.BlockSpec((1, HEAD_DIM), lambda g, i: (0, 0)),
                  pl.BlockSpec((n_in, LANES), lambda g, i: (i, 0)),
                  pl.BlockSpec((SUBLANES, LANES), lambda g, i: (0, 0)),
                  s_spec],
        out_specs=[pl.BlockSpec((n_in, lw), lambda g, i: (i, g)), s_spec],
        out_shape=[jax.ShapeDtypeStruct((proj.shape[0], GDN_WIDTH), F32),
                   jax.ShapeDtypeStruct(state.shape, F32)],
        scratch_shapes=[pltpu.VMEM((rows, lw), F32), pltpu.VMEM((rows, LANES), F32)] + _gdn_scratch(rows, 1),
        compiler_params=_cparams(2),
        name="gdn_sample",
    )(proj, proj, proj, conv_state, conv_state, conv_state, proj, conv_w, conv_w, conv_w,
      gnorm_w.reshape(1, HEAD_DIM).astype(F32), ab, params, state)


def _layer(x_prompt, x_sample, win_k, win_v, conv_state, gdn_state, rel_bias,
           w_in, sinks, conv_w, a_log, dt_bias, gnorm_w, w_out,
           n_pre_mix, n_post_mix, n_pre_ffn, n_post_ffn, w_gate, w_up, w_down, tiles):
    _, t, d = x_prompt.shape
    nb, ts, _ = x_sample.shape
    ms = nb * ts
    l = win_k.shape[1]
    ff = w_gate.shape[1]
    n_mt = tiles["n_mt"]
    xp = x_prompt.reshape(t, d)
    sds = jax.ShapeDtypeStruct
    norm_w = lambda w: w.reshape(1, d).astype(F32)

    w_in_t = w_in.T
    w_tail = jnp.pad(w_in_t[PROJ_MAIN:], ((0, LANES - AB_WIDTH), (0, 0))).astype(BF16)
    (hp, ab_p), (hs, ab_s) = _rowwise(
        _rmsnorm_tail_body, [(xp, x_sample)], [norm_w(n_pre_mix), w_tail],
        [(sds((t, d), BF16), sds((ms, d), BF16)), (sds((t, LANES), F32), sds((ms, LANES), F32))],
        tiles["norm_rows"], "rmsnorm")
    proj_p, proj_s = _proj([(hp, hs)], [(w_in_t, 0, d)], PROJ_MAIN, tiles["in_bn"], F32, n_mt, "in_proj",
                           w_transposed=True)

    qi = jnp.arange(WINDOW)[:, None]
    kj = jnp.arange(2 * WINDOW)[None, :]
    dist_p = WINDOW + qi - kj
    dist_s = l + jnp.arange(SUBLANES)[:, None] - kj
    dist_s = jnp.where(kj < l + ts, dist_s, -1)
    bias_all = _bias_table(rel_bias, jnp.concatenate([dist_p, dist_s], axis=0))
    attn_p = _attn_prompt(proj_p, bias_all[:, :WINDOW], sinks)

    b_s = bias_all[:, WINDOW:WINDOW + ts]
    own_kv = (jnp.arange(ATTN_HEADS)[:, None] // ATTN_GROUP) == jnp.arange(ATTN_KV_HEADS)[None, :]
    bias_c = jnp.where(own_kv[:, None, None, :], b_s[:, :, :l, None], NEG_INF)
    bias_c = bias_c.reshape(ATTN_HEADS * ts, l * ATTN_KV_HEADS)
    bias_n = jnp.where(own_kv[:, None, :, None], b_s[:, :, None, l:l + ts], NEG_INF)
    bias_n = bias_n.reshape(ATTN_HEADS * ts, ATTN_KV_HEADS * ts)
    sink_rows = jnp.repeat(sinks.astype(F32), ts).reshape(ATTN_HEADS * ts, 1)
    attn_s, new_k, new_v = _attn_sample(proj_s, win_k, win_v, bias_c, bias_n, sink_rows, ts, tiles["attn_gb"])

    params = _gdn_params(a_log, dt_bias)
    og_p, s_p = _gdn_prompt(proj_p, ab_p, conv_w, gnorm_w, params, tiles["gdn_rows"])
    og_s, s_s = _gdn_sample(proj_s, ab_s, jnp.swapaxes(conv_state, 0, 1).astype(F32), gdn_state, conv_w, gnorm_w, params, ts,
                            tiles["gdn_sample_rows"])

    mix_p, mix_s = _proj([(attn_p, attn_s), (og_p, og_s)], [(w_out, 0, ATTN_WIDTH), (w_out, 1, GDN_WIDTH)],
                         d, tiles["out_bn"], BF16, n_mt, "out_proj")
    (h2_p,), (h2_s,) = _rowwise(
        _post_mix_body, [(xp, x_sample), (mix_p, mix_s)], [norm_w(n_post_mix), norm_w(n_pre_ffn)],
        [(sds((t, d), BF16), sds((ms, d), BF16))], tiles["post_rows"], "post_mix")
    act_p, act_s = _proj([(h2_p, h2_s)], [(w_gate, 0, d), (w_up, 0, d)], ff, tiles["ff_bn"], BF16, n_mt,
                         "ffn_gate_up", swiglu=True)
    ffn_p, ffn_s = _down(act_p, act_s, w_down, tiles["ff_bk"], n_mt)
    (y_p,), (y_s,) = _rowwise(
        _residual_body, [(xp, x_sample), (mix_p, mix_s), (ffn_p, ffn_s)], [norm_w(n_post_mix), norm_w(n_post_ffn)],
        [(sds((t, d), F32), sds(x_sample.shape, F32))], tiles["post_rows"], "residual_norm")

    lw = min(WINDOW, t)
    prompt_k = proj_p[t - lw:, K_OFF:K_OFF + KV_WIDTH].reshape(1, lw, ATTN_KV_HEADS, HEAD_DIM)
    prompt_v = proj_p[t - lw:, V_OFF:V_OFF + KV_WIDTH].reshape(1, lw, ATTN_KV_HEADS, HEAD_DIM)
    prompt_conv = proj_p[t - (GDN_CONV - 1):, GQ_OFF:GQ_OFF + GDN_CONV_CH][None]
    sample_conv = proj_s[:, GQ_OFF:GQ_OFF + GDN_CONV_CH].reshape(nb, ts, GDN_CONV_CH)[:, ts - (GDN_CONV - 1):]
    return (y_p.reshape(1, t, d), y_s,
            prompt_k, prompt_v, prompt_conv, s_p[None],
            new_k.reshape(nb, l, ATTN_KV_HEADS, HEAD_DIM), new_v.reshape(nb, l, ATTN_KV_HEADS, HEAD_DIM),
            sample_conv, s_s)


def _tiles(t, ms):
    n_mt = 8 if (t % (8 * 16) == 0 and ms % (8 * 16) == 0) else 1
    return dict(n_mt=n_mt, in_bn=512, out_bn=512, ff_bn=512, ff_bk=512, norm_rows=512, post_rows=256,
                gdn_rows=256, gdn_sample_rows=2 * GDN_CHUNK, attn_gb=8)


def kernel(x_prompt, x_sample, cache_win_k, cache_win_v, state_conv, state_gdn, rel_bias, w_in, attn_sinks,
           gdn_conv_w, gdn_a_log, gdn_dt_bias, gdn_norm_w, w_out, norm_pre_mix, norm_post_mix, norm_pre_ffn,
           norm_post_ffn, w_gate, w_up, w_down):
    depth = w_in.shape[0]
    assert depth == 1 and x_prompt.shape[0] == 1
    tiles = _tiles(x_prompt.shape[1], x_sample.shape[0] * x_sample.shape[1])
    outs = _layer(x_prompt, x_sample, cache_win_k[0], cache_win_v[0], state_conv[0], state_gdn[0], rel_bias,
                  w_in[0], attn_sinks[0], gdn_conv_w[0], gdn_a_log[0], gdn_dt_bias[0], gdn_norm_w[0], w_out[0],
                  norm_pre_mix[0], norm_post_mix[0], norm_pre_ffn[0], norm_post_ffn[0],
                  w_gate[0], w_up[0], w_down[0], tiles)
    yp, ys, pk, pv, pc, ps, sk, sv, sc, ss = outs
    return (yp, ys, pk[None], pv[None], pc[None], ps[None], sk[None], sv[None], sc[None], ss[None])
```

```python
import functools
import math

import jax
import jax.numpy as jnp
from jax import lax
from jax.experimental import pallas as pl
from jax.experimental.pallas import tpu as pltpu

F32 = jnp.float32
BF16 = jnp.bfloat16

HEAD_DIM = 128
ATTN_HEADS = 16
ATTN_KV_HEADS = 4
ATTN_GROUP = ATTN_HEADS // ATTN_KV_HEADS
ATTN_WIDTH = ATTN_HEADS * HEAD_DIM
KV_WIDTH = ATTN_KV_HEADS * HEAD_DIM
WINDOW = 128
REL_BUCKETS = 32
REL_MAX_DIST = 128
GDN_HEADS = 16
GDN_WIDTH = GDN_HEADS * HEAD_DIM
GDN_CONV = 4
GDN_CONV_CH = 3 * GDN_WIDTH
GDN_CHUNK = 64
NORM_EPS = 1e-6
NEG_INF = -1e30

Q_OFF = 0
K_OFF = ATTN_WIDTH
V_OFF = K_OFF + KV_WIDTH
GQ_OFF = V_OFF + KV_WIDTH
GK_OFF = GQ_OFF + GDN_WIDTH
GV_OFF = GK_OFF + GDN_WIDTH
Z_OFF = GV_OFF + GDN_WIDTH
PROJ_MAIN = Z_OFF + GDN_WIDTH
AB_WIDTH = 2 * GDN_HEADS

LANES = 128
SUBLANES = 8
GDN_HB = 8
GDN_NG = GDN_HEADS // GDN_HB
SAMPLE_PAD = 8
SAMPLE_BLOCK_ROWS = 128
VMEM_LIMIT = 56 * 1024 * 1024


def _cparams(n_axes, vmem=VMEM_LIMIT):
    return pltpu.CompilerParams(dimension_semantics=("arbitrary",) * n_axes, vmem_limit_bytes=vmem)


def _dot(a, b):
    return jnp.dot(a, b, preferred_element_type=F32)


def _rms(x, w):
    return x * lax.rsqrt(jnp.mean(x * x, axis=-1, keepdims=True) + NORM_EPS) * w


def _rowwise_kernel(*refs, body, n_in, n_const, n_out, prompt_steps):
    p_in, s_in = refs[:n_in], refs[n_in:2 * n_in]
    consts = refs[2 * n_in:2 * n_in + n_const]
    p_out = refs[2 * n_in + n_const:2 * n_in + n_const + n_out]
    s_out = refs[2 * n_in + n_const + n_out:]

    def run(ins, outs):
        rows2d = [r[...].reshape(-1, r.shape[-1]) for r in ins]
        for o_ref, v in zip(outs, body(*rows2d, *[c[...] for c in consts])):
            o_ref[...] = v.reshape(o_ref.shape).astype(o_ref.dtype)

    i = pl.program_id(0)
    pl.when(i < prompt_steps)(lambda: run(p_in, p_out))
    pl.when(i >= prompt_steps)(lambda: run(s_in, s_out))


def _rowwise(body, pairs, consts, outs, rows, name):
    t = pairs[0][0].shape[0]
    n_sample = math.prod(pairs[0][1].shape[:-1])
    rows = min(rows, t)
    sample_rows = min(SAMPLE_BLOCK_ROWS, n_sample)
    prompt_steps, sample_steps = t // rows, n_sample // sample_rows

    def spec(shape, is_prompt):
        n = rows if is_prompt else sample_rows
        lead = n if len(shape) == 2 else n // shape[1]
        block = (lead,) + tuple(shape[1:])
        if is_prompt:
            index = lambda i: (jnp.minimum(i, prompt_steps - 1),) + (0,) * (len(shape) - 1)
        else:
            index = lambda i: (jnp.maximum(i - prompt_steps, 0),) + (0,) * (len(shape) - 1)
        return pl.BlockSpec(block, index)

    in_specs = ([spec(p.shape, True) for p, _ in pairs] + [spec(s.shape, False) for _, s in pairs]
                + [pl.BlockSpec(c.shape, lambda i, nd=c.ndim: (0,) * nd) for c in consts])
    out_specs = [spec(p.shape, True) for p, _ in outs] + [spec(s.shape, False) for _, s in outs]
    res = pl.pallas_call(
        functools.partial(_rowwise_kernel, body=body, n_in=len(pairs), n_const=len(consts), n_out=len(outs),
                          prompt_steps=prompt_steps),
        grid=(prompt_steps + sample_steps,),
        in_specs=in_specs,
        out_specs=out_specs,
        out_shape=[p for p, _ in outs] + [s for _, s in outs],
        compiler_params=_cparams(1),
        name=name,
    )(*[p for p, _ in pairs], *[s for _, s in pairs], *consts)
    return res[:len(outs)], res[len(outs):]


def _rmsnorm_tail_body(x, w, w_tail):
    h = _rms(x, w).astype(BF16)
    return h, _dot_nt(h, w_tail)


def _post_mix_body(x, mix, w_post, w_pre):
    return (_rms(x + _rms(mix.astype(F32), w_post), w_pre),)


def _residual_body(x, mix, ffn, w_post_mix, w_post_ffn):
    return (x + _rms(mix.astype(F32), w_post_mix) + _rms(ffn.astype(F32), w_post_ffn),)


def _stack_rows(p_ref, s_ref, cols=slice(None)):
    return jnp.concatenate([p_ref[:, cols].astype(BF16), s_ref[:, cols].astype(BF16)], axis=0)


def _dot_nt(a, b):
    return lax.dot_general(a, b, (((1,), (1,)), ((), ())), preferred_element_type=F32)


def _proj_kernel(*refs, n_a, n_w, swiglu, w_transposed, row_blocks, n_mt, n_tiles, last_cols):
    a_refs = refs[:2 * n_a]
    w_refs = refs[2 * n_a:2 * n_a + n_w]
    op_ref, os_ref = refs[2 * n_a + n_w:2 * n_a + n_w + 2]
    scratch = refs[2 * n_a + n_w + 2:]
    stage_refs, wb_refs, sem = scratch[:n_w], scratch[n_w:2 * n_w], scratch[2 * n_w]
    mm = _dot_nt if w_transposed else _dot
    j, i = pl.program_id(0), pl.program_id(1)
    nj = pl.num_programs(0)

    def slice_copy(w, tile, s, slot, narrow):
        rows, cols = stage_refs[w].shape[1:]
        if w_transposed:
            src = w_refs[w].at[pl.ds(tile * (rows * n_mt) + s * rows, rows), pl.ds(row_blocks[w] * cols, cols)]
            dst = stage_refs[w].at[slot]
        else:
            width = last_cols if narrow else cols
            src = w_refs[w].at[pl.ds((row_blocks[w] * n_mt + s) * rows, rows), pl.ds(tile * cols, width)]
            dst = stage_refs[w].at[slot, :, pl.ds(0, width)]
        return pltpu.make_async_copy(src, dst, sem.at[w, slot])

    def with_tile(tile, fn):
        if last_cols == stage_refs[0].shape[2] or w_transposed:
            fn(False)
        elif isinstance(tile, int):
            fn(tile == n_tiles - 1)
        else:
            pl.when(tile == n_tiles - 1)(lambda: fn(True))
            pl.when(tile != n_tiles - 1)(lambda: fn(False))

    def cast_slice(w, half, s, slot):
        rows = stage_refs[w].shape[1]
        wb_refs[w][half, pl.ds(pl.multiple_of(s * rows, rows), rows), :] = stage_refs[w][slot].astype(BF16)

    @pl.when(jnp.logical_and(j == 0, i == 0))
    def _():
        if last_cols != stage_refs[0].shape[2]:
            for w in range(n_w):
                stage_refs[w][...] = jnp.zeros_like(stage_refs[w])
        for w in range(n_w):
            for s in range(n_mt):
                def fetch_now(narrow, w=w, s=s):
                    copy = slice_copy(w, 0, s, s % 2, narrow)
                    copy.start()
                    copy.wait()
                with_tile(0, fetch_now)
                cast_slice(w, 0, s, s % 2)
        if n_tiles > 1:
            for w in range(n_w):
                with_tile(1, lambda narrow, w=w: slice_copy(w, 1, 0, 0, narrow).start())

    @pl.when(j + 1 < nj)
    def _():
        slot = (j * n_mt + i) % 2
        for w in range(n_w):
            with_tile(j + 1, lambda narrow, w=w: slice_copy(w, j + 1, i, slot, narrow).wait())
            cast_slice(w, (j + 1) % 2, i, slot)
        last = i + 1 == n_mt
        next_tile = jnp.where(last, j + 2, j + 1)
        next_s = jnp.where(last, 0, i + 1)

        @pl.when(next_tile < nj)
        def _():
            for w in range(n_w):
                with_tile(next_tile, lambda narrow, w=w: slice_copy(w, next_tile, next_s, 1 - slot, narrow).start())

    half = j % 2
    bmp = op_ref.shape[0]

    def apply(width):
        cols = slice(None) if w_transposed else slice(0, width)
        rhs = [wb_ref[half, :, cols] for wb_ref in wb_refs]
        lhs = [_stack_rows(a_refs[2 * p], a_refs[2 * p + 1]) for p in range(n_a)]
        if swiglu:
            g = mm(lhs[0], rhs[0])
            o = g * jax.nn.sigmoid(g) * mm(lhs[0], rhs[1])
        else:
            o = mm(lhs[0], rhs[0])
            for l, r in zip(lhs[1:], rhs[1:]):
                o = o + mm(l, r)
        op_ref[:, :width] = o[:bmp].astype(op_ref.dtype)
        os_ref[:, :width] = o[bmp:].astype(os_ref.dtype)

    full_cols = op_ref.shape[1]
    if last_cols == full_cols:
        apply(full_cols)
    else:
        pl.when(j != n_tiles - 1)(lambda: apply(full_cols))
        pl.when(j == n_tiles - 1)(lambda: apply(last_cols))


def _proj(a_pairs, w_blocks, n, bn, out_dtype, n_mt, name, swiglu=False, w_transposed=False):
    tp, ts = a_pairs[0][0].shape[0], a_pairs[0][1].shape[0]
    bmp, bms = tp // n_mt, ts // n_mt
    in_specs = []
    for a_p, a_s in a_pairs:
        in_specs.append(pl.BlockSpec((bmp, a_p.shape[1]), lambda j, i: (i, 0)))
        in_specs.append(pl.BlockSpec((bms, a_s.shape[1]), lambda j, i: (i, 0)))
    in_specs += [pl.BlockSpec(memory_space=pl.ANY) for _ in w_blocks]
    n_tiles = pl.cdiv(n, bn)
    tile = lambda k: (bn, k) if w_transposed else (k, bn)
    stage = lambda k: (2, bn // n_mt, k) if w_transposed else (2, k // n_mt, bn)
    operands = [a for pair in a_pairs for a in pair] + [w for w, _, _ in w_blocks]
    return pl.pallas_call(
        functools.partial(_proj_kernel, n_a=len(a_pairs), n_w=len(w_blocks), swiglu=swiglu,
                          w_transposed=w_transposed, row_blocks=tuple(rb for _, rb, _ in w_blocks), n_mt=n_mt,
                          n_tiles=n_tiles, last_cols=n - (n_tiles - 1) * bn),
        grid=(n_tiles, n_mt),
        in_specs=in_specs,
        out_specs=[pl.BlockSpec((bmp, bn), lambda j, i: (i, j)), pl.BlockSpec((bms, bn), lambda j, i: (i, j))],
        out_shape=[jax.ShapeDtypeStruct((tp, n), out_dtype), jax.ShapeDtypeStruct((ts, n), out_dtype)],
        scratch_shapes=([pltpu.VMEM(stage(k), F32) for _, _, k in w_blocks]
                        + [pltpu.VMEM((2,) + tile(k), BF16) for _, _, k in w_blocks]
                        + [pltpu.SemaphoreType.DMA((len(w_blocks), 2))]),
        compiler_params=_cparams(2),
        name=name,
    )(*operands)


def _down_kernel(ap_ref, as_ref, b_ref, op_ref, os_ref, accp_ref, accs_ref, *, bn, k_total):
    k = pl.program_id(1)
    nk = pl.num_programs(1)
    bk = b_ref.shape[0]
    bmp = ap_ref.shape[0]
    last_rows = k_total - (k_total // bk) * bk

    @pl.when(k == 0)
    def _():
        accp_ref[...] = jnp.zeros_like(accp_ref)
        accs_ref[...] = jnp.zeros_like(accs_ref)

    def accumulate(kk):
        a = _stack_rows(ap_ref, as_ref, slice(0, kk))
        for n0 in range(0, accp_ref.shape[1], bn):
            d = _dot(a, b_ref[:kk, n0:n0 + bn].astype(BF16))
            accp_ref[:, n0:n0 + bn] += d[:bmp]
            accs_ref[:, n0:n0 + bn] += d[bmp:]

    if last_rows == 0:
        accumulate(bk)
    else:
        pl.when(k < nk - 1)(lambda: accumulate(bk))
        pl.when(k == nk - 1)(lambda: accumulate(last_rows))

    @pl.when(k == nk - 1)
    def _():
        op_ref[...] = accp_ref[...].astype(op_ref.dtype)
        os_ref[...] = accs_ref[...].astype(os_ref.dtype)


def _down(a_p, a_s, b, bk, n_mt):
    (tp, k_total), ts = a_p.shape, a_s.shape[0]
    n = b.shape[1]
    bmp, bms = tp // n_mt, ts // n_mt
    return pl.pallas_call(
        functools.partial(_down_kernel, bn=min(n, 512), k_total=k_total),
        grid=(n_mt, pl.cdiv(k_total, bk)),
        in_specs=[pl.BlockSpec((bmp, bk), lambda i, k: (i, k)), pl.BlockSpec((bms, bk), lambda i, k: (i, k)),
                  pl.BlockSpec((bk, n), lambda i, k: (k, 0))],
        out_specs=[pl.BlockSpec((bmp, n), lambda i, k: (i, 0), pipeline_mode=pl.Buffered(1)),
                   pl.BlockSpec((bms, n), lambda i, k: (i, 0))],
        out_shape=[jax.ShapeDtypeStruct((tp, n), BF16), jax.ShapeDtypeStruct((ts, n), BF16)],
        scratch_shapes=[pltpu.VMEM((bmp, n), F32), pltpu.VMEM((bms, n), F32)],
        compiler_params=_cparams(2),
        name="ffn_down",
    )(a_p, a_s, b)


def _rel_bucket(d):
    n = jnp.maximum(d, 0)
    max_exact = REL_BUCKETS // 2
    nf = jnp.maximum(n, 1).astype(F32)
    large = max_exact + (jnp.log(nf / max_exact) / math.log(REL_MAX_DIST / max_exact)
                         * (REL_BUCKETS - max_exact)).astype(jnp.int32)
    large = jnp.minimum(large, REL_BUCKETS - 1)
    return jnp.where(n < max_exact, n, large)


def _bias_kernel(rel_ref, bucket_ref, allowed_ref, o_ref):
    h = pl.program_id(0)
    bucket = bucket_ref[...]
    acc = jnp.zeros(bucket.shape, F32)
    for n in range(REL_BUCKETS):
        acc = jnp.where(bucket == n, rel_ref[n, h], acc)
    o_ref[0] = jnp.where(allowed_ref[...] > 0, acc, NEG_INF)


def _bias_table(rel_bias, dist):
    tq, tk = dist.shape
    allowed = ((dist >= 0) & (dist <= WINDOW)).astype(jnp.int32)
    full = pl.BlockSpec((tq, tk), lambda h: (0, 0))
    return pl.pallas_call(
        _bias_kernel,
        grid=(ATTN_HEADS,),
        in_specs=[pl.BlockSpec(memory_space=pltpu.SMEM), full, full],
        out_specs=pl.BlockSpec((1, tq, tk), lambda h: (h, 0, 0)),
        out_shape=jax.ShapeDtypeStruct((ATTN_HEADS, tq, tk), F32),
        compiler_params=_cparams(1),
        name="rel_bias_table",
    )(rel_bias.astype(F32), _rel_bucket(dist), allowed)


def _softmax_pv(s, sink):
    m = jnp.maximum(jnp.max(s, axis=-1, keepdims=True), sink)
    p = jnp.exp(s - m)
    denom = jnp.sum(p, axis=-1, keepdims=True) + jnp.exp(sink - m)
    return p.astype(BF16), denom


def _attn_prompt_kernel(sink_ref, cur_ref, prev_ref, bias_ref, o_ref):
    c = WINDOW
    head_cols = lambda base, h: slice(base + h * HEAD_DIM, base + (h + 1) * HEAD_DIM)
    kk, vv = [], []
    for h in range(ATTN_KV_HEADS):
        kk.append(jnp.concatenate([prev_ref[:, head_cols(0, h)], cur_ref[:, head_cols(K_OFF, h)]],
                                  axis=0).astype(BF16))
        vv.append(jnp.concatenate([prev_ref[:, head_cols(KV_WIDTH, h)], cur_ref[:, head_cols(V_OFF, h)]],
                                  axis=0).astype(BF16))
    heads = range(ATTN_HEADS)
    scale = HEAD_DIM ** -0.5
    s = [_dot_nt((cur_ref[:, head_cols(Q_OFF, hd)] * scale).astype(BF16), kk[hd // ATTN_GROUP]) for hd in heads]
    pd = [_softmax_pv(s[hd] + bias_ref[hd], sink_ref[hd]) for hd in heads]
    outs = [_dot(pd[hd][0], vv[hd // ATTN_GROUP]) / pd[hd][1] for hd in heads]
    o_ref[...] = jnp.concatenate(outs, axis=1).astype(o_ref.dtype)


def _attn_prompt(proj, bias, sinks):
    t = proj.shape[0]
    c = WINDOW
    qkv = V_OFF + KV_WIDTH
    first = jnp.where(jnp.arange(2 * c) < c, NEG_INF, bias)
    tables = jnp.stack([first, bias])
    return pl.pallas_call(
        _attn_prompt_kernel,
        grid=(t // c,),
        in_specs=[pl.BlockSpec(memory_space=pltpu.SMEM),
                  pl.BlockSpec((c, qkv), lambda i: (i, 0)),
                  pl.BlockSpec((c, 2 * KV_WIDTH), lambda i: (jnp.maximum(i - 1, 0), K_OFF // (2 * KV_WIDTH))),
                  pl.BlockSpec((None, ATTN_HEADS, c, 2 * c), lambda i: (jnp.minimum(i, 1), 0, 0, 0))],
        out_specs=pl.BlockSpec((c, ATTN_WIDTH), lambda i: (i, 0)),
        out_shape=jax.ShapeDtypeStruct((t, ATTN_WIDTH), BF16),
        compiler_params=_cparams(1),
        name="attn_prompt",
    )(sinks.astype(F32), proj, proj, tables)


def _attn_sample_kernel(x_ref, ck_ref, cv_ref, bias_c_ref, bias_n_ref, sink_ref, o_ref, ok_ref, ov_ref,
                        q_s, kn_s, vn_s, *, t_new):
    gb, l, kvh, hd = ck_ref.shape
    kc = ck_ref[...].reshape(gb, l * kvh, hd)
    vc = cv_ref[...].reshape(gb, l * kvh, hd)
    for b in range(gb):
        rows = slice(b * t_new, (b + 1) * t_new)
        for h in range(kvh):
            kn_s[b, h * t_new:(h + 1) * t_new, :] = x_ref[rows, K_OFF + h * hd:K_OFF + (h + 1) * hd]
            vn_s[b, h * t_new:(h + 1) * t_new, :] = x_ref[rows, V_OFF + h * hd:V_OFF + (h + 1) * hd]
        for head in range(ATTN_HEADS):
            q_s[b, head * t_new:(head + 1) * t_new, :] = x_ref[rows, head * hd:(head + 1) * hd]

    q = q_s[...].astype(BF16)
    scale = hd ** -0.5
    s_c = jnp.einsum("bqd,bkd->bqk", q, kc.astype(BF16), preferred_element_type=F32) * scale + bias_c_ref[...]
    s_n = jnp.einsum("bqd,bkd->bqk", q, kn_s[...].astype(BF16), preferred_element_type=F32) * scale + bias_n_ref[...]
    sink = sink_ref[...]
    m = jnp.maximum(jnp.maximum(jnp.max(s_c, axis=-1, keepdims=True), jnp.max(s_n, axis=-1, keepdims=True)), sink)
    p_c = jnp.exp(s_c - m)
    p_n = jnp.exp(s_n - m)
    denom = jnp.sum(p_c, axis=-1, keepdims=True) + jnp.sum(p_n, axis=-1, keepdims=True) + jnp.exp(sink - m)
    o = (jnp.einsum("bqk,bkd->bqd", p_c.astype(BF16), vc.astype(BF16), preferred_element_type=F32)
         + jnp.einsum("bqk,bkd->bqd", p_n.astype(BF16), vn_s[...].astype(BF16), preferred_element_type=F32)) / denom
    for b in range(gb):
        for head in range(ATTN_HEADS):
            o_ref[b * t_new:(b + 1) * t_new, head * hd:(head + 1) * hd] = o[b, head * t_new:(head + 1) * t_new, :]

    shift = t_new * kvh
    for c_val, n_s, out_ref in ((kc, kn_s, ok_ref), (vc, vn_s, ov_ref)):
        new_rows = [n_s[:, h * t_new + t:h * t_new + t + 1, :] for t in range(t_new) for h in range(kvh)]
        win = jnp.concatenate([c_val[:, shift:, :]] + new_rows, axis=1)
        out_ref[...] = win.reshape(gb, l, kvh, hd)


def _attn_sample(proj, cache_k, cache_v, bias_c, bias_n, sink_rows, t_new, gb):
    b, l = cache_k.shape[:2]
    rows = ATTN_HEADS * t_new
    qkv = V_OFF + KV_WIDTH
    cache_spec = pl.BlockSpec((gb, l, ATTN_KV_HEADS, HEAD_DIM), lambda i: (i, 0, 0, 0))
    full = lambda a: pl.BlockSpec(a.shape, lambda i: (0,) * a.ndim)
    return pl.pallas_call(
        functools.partial(_attn_sample_kernel, t_new=t_new),
        grid=(b // gb,),
        in_specs=[pl.BlockSpec((gb * t_new, qkv), lambda i: (i, 0)), cache_spec, cache_spec,
                  full(bias_c), full(bias_n), full(sink_rows)],
        out_specs=[pl.BlockSpec((gb * t_new, ATTN_WIDTH), lambda i: (i, 0)), cache_spec, cache_spec],
        out_shape=[jax.ShapeDtypeStruct((b * t_new, ATTN_WIDTH), F32),
                   jax.ShapeDtypeStruct(cache_k.shape, F32),
                   jax.ShapeDtypeStruct(cache_v.shape, F32)],
        scratch_shapes=[pltpu.VMEM((gb, rows, HEAD_DIM), F32),
                        pltpu.VMEM((gb, ATTN_KV_HEADS * t_new, HEAD_DIM), F32),
                        pltpu.VMEM((gb, ATTN_KV_HEADS * t_new, HEAD_DIM), F32)],
        compiler_params=_cparams(1),
        name="attn_sample",
    )(proj, cache_k, cache_v, bias_c, bias_n, sink_rows)


def _split3(x):
    hi = x.astype(BF16)
    r = x - hi.astype(F32)
    mid = r.astype(BF16)
    lo = (r - mid.astype(F32)).astype(BF16)
    return hi, mid, lo


def _select_dot(sel, x):
    hi, mid, lo = _split3(x)
    return _dot(sel, hi) + _dot(sel, mid) + _dot(sel, lo)


def _dot_select(x, sel):
    hi, mid, lo = _split3(x)
    return _dot(hi, sel) + _dot(mid, sel) + _dot(lo, sel)


def _softplus(x):
    return jnp.maximum(x, 0.0) + jnp.log1p(jnp.exp(-jnp.abs(x)))


def _conv_silu(pad_ref, w, r0, n):
    top = SUBLANES + r0
    y = pad_ref[top:top + n, :] * w[GDN_CONV - 1:GDN_CONV]
    for s in range(1, GDN_CONV):
        y = y + pad_ref[top - s:top - s + n, :] * w[GDN_CONV - 1 - s:GDN_CONV - s]
    return y * jax.nn.sigmoid(y)


def _l2norm_heads(x, scale):
    outs = []
    for hh in range(GDN_HB):
        xh = x[:, hh * HEAD_DIM:(hh + 1) * HEAD_DIM]
        outs.append(xh * (lax.rsqrt(jnp.sum(xh * xh, axis=-1, keepdims=True) + NORM_EPS) * scale))
    return jnp.concatenate(outs, axis=1)


GDN_N_PADS = 4
STAGE_ROWS = 32
PROBE_BOUND = 1e30
GDN_N_STAGED = 10


def _gdn_kernel(*refs, rows, seq, carry, live_from):
    kw = dict(rows=rows, seq=seq, carry=carry, live_from=live_from)
    if not carry:
        io, pads, staged = refs[:-GDN_N_PADS - GDN_N_STAGED], refs[-GDN_N_PADS - GDN_N_STAGED:-GDN_N_STAGED], \
            refs[-GDN_N_STAGED:]
        _gdn_body(io, pads, staged, staged, **kw)
        return
    n_scratch = GDN_N_PADS + 2 * GDN_N_STAGED
    io, pads = refs[:-n_scratch], refs[-n_scratch:-2 * GDN_N_STAGED]
    set_a, set_b = refs[-2 * GDN_N_STAGED:-GDN_N_STAGED], refs[-GDN_N_STAGED:]
    s_ref = io[-1]
    step = pl.program_id(1)

    @pl.when(step == 0)
    def _():
        s_ref[...] = jnp.zeros_like(s_ref)
        for ref in set_b:
            ref[...] = jnp.zeros_like(ref)

    pl.when(step % 2 == 0)(lambda: _gdn_body(io, pads, set_a, set_b, **kw))
    pl.when(step % 2 == 1)(lambda: _gdn_body(io, pads, set_b, set_a, **kw))


def _gdn_body(refs, pads, staged_w, staged_r, *, rows, seq, carry, live_from):
    c = GDN_CHUNK
    nc = rows // c
    nseq = c // seq
    n_live = seq - live_from
    lw = GDN_HB * HEAD_DIM
    if carry:
        (xq_ref, xk_ref, xv_ref, tq_ref, tk_ref, tv_ref, z_ref, wq_ref, wk_ref, wv_ref, gn_ref,
         ab_ref, pc_ref, og_ref, s_ref) = refs
        s0_ref = None
    else:
        (xq_ref, xk_ref, xv_ref, cq_ref, ck_ref, cv_ref, z_in, wq_ref, wk_ref, wv_ref, gn_ref,
         ab_in, pc_ref, s0_ref, og_ref, s_ref, z_ref, ab_ref) = refs
    padq_s, padk_s, padv_s, eb_s = pads
    kn_w, kb_w, qn_w, vb_w, kbe_w, qe_w, kd_w, egc_w, egl_w, gcr_w = staged_w
    kn_s, kb_s, qn_s, vb_s, kbe_s, qe_s, kd_s, egc_s, egl_s, gcr_s = staged_r

    group = pl.program_id(0)
    step = pl.program_id(1)
    row = lax.broadcasted_iota(jnp.int32, (rows, 1), 0)
    live = (row % seq) >= live_from
    keep_live = (lambda x: jnp.where(live, x, 0.0)) if live_from else (lambda x: x)

    if carry:
        for pad_ref, x_ref, t_ref in ((padq_s, xq_ref, tq_ref), (padk_s, xk_ref, tk_ref), (padv_s, xv_ref, tv_ref)):
            pad_ref[0:SUBLANES, :] = jnp.where(step == 0, 0.0, t_ref[...])
            pad_ref[SUBLANES:, :] = x_ref[...]
    else:
        def gather(dst_ref, x_ref, hist_ref, top):
            dst_ref[...] = jnp.zeros_like(dst_ref)
            for b in range(rows // seq):
                r0 = top + b * seq + live_from
                if hist_ref is not None:
                    dst_ref[r0 - (GDN_CONV - 1):r0, :] = hist_ref[:, b, :]
                dst_ref[r0:r0 + n_live, :] = x_ref[b * n_live:(b + 1) * n_live, :]

        gather(padq_s, xq_ref, cq_ref, SUBLANES)
        gather(padk_s, xk_ref, ck_ref, SUBLANES)
        gather(padv_s, xv_ref, cv_ref, SUBLANES)
        gather(z_ref, z_in, None, 0)
        gather(ab_ref, ab_in, None, 0)

    def stage_decays():
        ab = ab_ref[...]
        g_col = keep_live(-jnp.exp(pc_ref[0:1, :]) * _softplus(ab + pc_ref[1:2, :]))
        beta_col = keep_live(jax.nn.sigmoid(ab))
        ri = lax.broadcasted_iota(jnp.int32, (rows, rows), 0)
        ci = lax.broadcasted_iota(jnp.int32, (rows, rows), 1)
        same = (ri // seq) == (ci // seq)
        cum_sel = jnp.where(jnp.logical_and(same, ci <= ri), 1.0, 0.0).astype(BF16)
        tot_sel = jnp.where(same, 1.0, 0.0).astype(BF16)
        gc_col = _select_dot(cum_sel, g_col)
        gl_col = _select_dot(tot_sel, g_col)
        first = group * GDN_HB
        li = lax.broadcasted_iota(jnp.int32, (LANES, lw), 0)
        ni = lax.broadcasted_iota(jnp.int32, (LANES, lw), 1)
        spread_g = jnp.where(li == ni // HEAD_DIM + first, 1.0, 0.0).astype(BF16)
        spread_b = jnp.where(li == ni // HEAD_DIM + first + GDN_HEADS, 1.0, 0.0).astype(BF16)
        egc_w[...] = _dot_select(gc_col, spread_g)
        egl_w[...] = _dot_select(gl_col, spread_g)
        eb_s[...] = _dot_select(beta_col, spread_b)
        pi = lax.broadcasted_iota(jnp.int32, (LANES, LANES), 0)
        pj = lax.broadcasted_iota(jnp.int32, (LANES, LANES), 1)
        pick = jnp.where(jnp.logical_and(pi == pj + first, pj < GDN_HB), 1.0, 0.0).astype(BF16)
        gc_heads = _dot_select(gc_col, pick)
        if rows % LANES:
            gc_heads = jnp.concatenate([gc_heads, jnp.zeros((LANES - rows % LANES, LANES), F32)], axis=0)
        gcr_w[...] = gc_heads.T[:SUBLANES, :rows]

    def stage_rows(r0, n, after=None):
        sl = slice(r0, r0 + n)
        keep = (lambda x: jnp.where(live[sl], x, 0.0)) if live_from else (lambda x: x)
        gate = 1.0 if after is None else jnp.where(after > PROBE_BOUND, 0.0, 1.0)
        conv = lambda pad_ref, w_ref: keep(_conv_silu(pad_ref, w_ref[...] * gate, r0, n))
        qn = _l2norm_heads(conv(padq_s, wq_ref), HEAD_DIM ** -0.5)
        kn = _l2norm_heads(conv(padk_s, wk_ref), 1.0)
        v = conv(padv_s, wv_ref)
        egc, egl, eb = egc_w[sl, :], egl_w[sl, :], eb_s[sl, :]
        decay_in = jnp.exp(egc)
        kb = kn * eb
        kn_w[sl, :] = kn
        kb_w[sl, :] = kb
        qn_w[sl, :] = qn
        vb_w[sl, :] = v * eb
        kbe_w[sl, :] = kb * decay_in
        qe_w[sl, :] = qn * decay_in
        kd_w[sl, :] = kn * jnp.exp(egl - egc)

    n_slices = rows // STAGE_ROWS if carry else 1
    pending = [functools.partial(stage_rows, r * (rows // n_slices), rows // n_slices) for r in range(n_slices)]
    if not carry:
        stage_decays()
        pending.pop(0)()

    cr = lax.broadcasted_iota(jnp.int32, (c, c), 0)
    cc = lax.broadcasted_iota(jnp.int32, (c, c), 1)
    same_c = (cr // seq) == (cc // seq)
    causal_bias = jnp.where(jnp.logical_and(same_c, cr >= cc), 0.0, NEG_INF)
    strict = jnp.where(jnp.logical_and(same_c, cr > cc), 1.0, 0.0)

    eye = jnp.where(cr == cc, 1.0, 0.0)
    seq_of_row = lax.broadcasted_iota(jnp.int32, (c, HEAD_DIM), 0) // seq

    def lower_left(s):
        return jnp.logical_and(jnp.logical_and(cr // (2 * s) == cc // (2 * s), (cr // s) % 2 == 1),
                               (cc // s) % 2 == 0)

    merge_sizes = [2 ** e for e in range(1, max(1, math.ceil(math.log2(n_live))))]
    pair_mask = jnp.where(lower_left(1), 1.0, 0.0)
    merge_masks = [jnp.where(lower_left(s), 1.0, 0.0) for s in merge_sizes]

    pairs = [(ch, hh) for ch in range(nc) for hh in range(GDN_HB)]
    rows_of = lambda ch: slice(ch * c, (ch + 1) * c)
    lanes_of = lambda hh: slice(hh * HEAD_DIM, (hh + 1) * HEAD_DIM)
    uw_all, qk_all, a_mat, t_mat, t_a = {}, {}, {}, {}, {}
    for ch, hh in pairs:
        rs, hs = rows_of(ch), lanes_of(hh)
        gcol = egc_s[rs, hh * HEAD_DIM:hh * HEAD_DIM + c]
        grow = gcr_s[hh:hh + 1, rs]
        decay = jnp.exp(gcol - grow + causal_bias)
        kq = jnp.concatenate([kb_s[rs, hs], qn_s[rs, hs]], axis=0).astype(BF16)
        kk = _dot_nt(kq, kn_s[rs, hs].astype(BF16)) * jnp.concatenate([decay, decay], axis=0)
        a_mat[ch, hh] = kk[:c] * strict
        qk_all[ch, hh] = kk[c:].astype(BF16)
        t_mat[ch, hh] = eye - a_mat[ch, hh] * pair_mask
    probes = [a_mat[pairs[-1]][0:1, 0:1]]
    if carry:
        stage_decays()
    for mask in merge_masks:
        for p in pairs:
            t_a[p] = _dot(t_mat[p].astype(BF16), (a_mat[p] * mask).astype(BF16))
        for p in pairs:
            t_mat[p] = t_mat[p] - _dot(t_a[p].astype(BF16), t_mat[p].astype(BF16))
        probes.append(t_mat[pairs[-1]][0:1, 0:1])
    for ch, hh in pairs:
        rs, hs = rows_of(ch), lanes_of(hh)
        uw_all[ch, hh] = _dot(t_mat[ch, hh].astype(BF16),
                              jnp.concatenate([vb_s[rs, hs], kbe_s[rs, hs]], axis=1).astype(BF16))

    if carry:
        state = [s_ref[hh] for hh in range(GDN_HB)]
    new_states = {}
    heads = range(GDN_HB)
    tn_dot = lambda a, b: lax.dot_general(a, b, (((0,), (0,)), ((), ())), preferred_element_type=F32)
    for ch in range(nc):
        rs = rows_of(ch)
        w_s, q_s, v_new_b, o = {}, {}, {}, {}
        for hh in heads:
            w = uw_all[ch, hh][:, HEAD_DIM:]
            q_e = qe_s[rs, lanes_of(hh)]
            if carry:
                wq = _dot(jnp.concatenate([w, q_e], axis=0).astype(BF16), state[hh].astype(BF16))
                w_s[hh], q_s[hh] = wq[:c], wq[c:]
            else:
                parts = []
                for b in range(nseq):
                    bs = slice(b * seq, (b + 1) * seq)
                    parts.append(_dot(jnp.concatenate([w[bs], q_e[bs]], axis=0).astype(BF16),
                                      s0_ref[ch * nseq + b, hh].astype(BF16)))
                w_s[hh] = jnp.concatenate([p[:seq] for p in parts], axis=0)
                q_s[hh] = jnp.concatenate([p[seq:] for p in parts], axis=0)
        for hh in heads:
            v_new_b[hh] = (uw_all[ch, hh][:, :HEAD_DIM] - w_s[hh]).astype(BF16)
            o[hh] = q_s[hh] + _dot(qk_all[ch, hh], v_new_b[hh])
        for hh in heads:
            hs = lanes_of(hh)
            k_d = kd_s[rs, hs]
            if carry:
                state[hh] = (state[hh] * jnp.exp(egl_s[ch * c:ch * c + 1, hs])
                             + tn_dot(k_d.astype(BF16), v_new_b[hh]))
            else:
                for b in range(nseq):
                    k_db = jnp.where(seq_of_row == b, k_d, 0.0).astype(BF16)
                    r1 = ch * c + b * seq
                    new_states[ch * nseq + b, hh] = (
                        s0_ref[ch * nseq + b, hh] * jnp.exp(egl_s[r1:r1 + 1, hs]) + tn_dot(k_db, v_new_b[hh]))
        if carry:
            probes.append(_dot(qk_all[ch, 0], jnp.ones((c, HEAD_DIM), BF16))[0:1, 0:1])
        outs = []
        for hh in heads:
            z = z_ref[rs, lanes_of(hh)]
            outs.append(_rms(o[hh], gn_ref[...]) * (z * jax.nn.sigmoid(z)))
        og = jnp.concatenate(outs, axis=1)
        if carry:
            og_ref[rs, :] = og.astype(og_ref.dtype)
        else:
            for b in range(nseq):
                r0 = b * seq + live_from
                og_ref[(ch * nseq + b) * n_live:(ch * nseq + b + 1) * n_live, :] = (
                    og[r0:r0 + n_live].astype(og_ref.dtype))
    for n, stage in enumerate(pending):
        stage(after=probes[min(n, len(probes) - 1)])
    if carry:
        s_ref[...] = jnp.stack(state, axis=0)
    else:
        for (b, hh), s_new in new_states.items():
            s_ref[b, hh] = s_new


def _gdn_params(a_log, dt_bias):
    rows = jnp.stack([a_log.astype(F32), dt_bias.astype(F32)], axis=0)
    return jnp.pad(rows, ((0, SUBLANES - 2), (0, LANES - GDN_HEADS)))


def _gdn_scratch(rows, n_sets):
    lw = GDN_HB * HEAD_DIM
    staged = [(rows, lw)] * (GDN_N_STAGED - 1) + [(SUBLANES, rows)]
    return ([pltpu.VMEM((SUBLANES + rows, lw), F32) for _ in range(GDN_N_PADS - 1)] + [pltpu.VMEM((rows, lw), F32)]
            + [pltpu.VMEM(shape, F32) for _ in range(n_sets) for shape in staged])


def _gdn_prompt(proj, ab, conv_w, gnorm_w, params, rows):
    t = proj.shape[0]
    nb = t // rows
    lw = GDN_HB * HEAD_DIM
    head = lambda i: jnp.minimum(i, nb - 1)
    done = lambda i: jnp.maximum(i - 1, 0)
    x_spec = lambda off: pl.BlockSpec((rows, lw), lambda g, i: (head(i), off // lw + g))
    tail_spec = lambda off: pl.BlockSpec(
        (SUBLANES, lw), lambda g, i: (jnp.maximum(head(i) * (rows // SUBLANES) - 1, 0), off // lw + g))
    w_spec = lambda off: pl.BlockSpec((GDN_CONV, lw), lambda g, i: (0, (off - GQ_OFF) // lw + g))
    return pl.pallas_call(
        functools.partial(_gdn_kernel, rows=rows, seq=GDN_CHUNK, carry=True, live_from=0),
        grid=(GDN_NG, nb + 1),
        in_specs=[x_spec(GQ_OFF), x_spec(GK_OFF), x_spec(GV_OFF),
                  tail_spec(GQ_OFF), tail_spec(GK_OFF), tail_spec(GV_OFF),
                  pl.BlockSpec((rows, lw), lambda g, i: (done(i), Z_OFF // lw + g)),
                  w_spec(GQ_OFF), w_spec(GK_OFF), w_spec(GV_OFF),
                  pl.BlockSpec((1, HEAD_DIM), lambda g, i: (0, 0)),
                  pl.BlockSpec((rows, LANES), lambda g, i: (head(i), 0)),
                  pl.BlockSpec((SUBLANES, LANES), lambda g, i: (0, 0))],
        out_specs=[pl.BlockSpec((rows, lw), lambda g, i: (done(i), g)),
                   pl.BlockSpec((GDN_HB, HEAD_DIM, HEAD_DIM), lambda g, i: (g, 0, 0))],
        out_shape=[jax.ShapeDtypeStruct((t, GDN_WIDTH), BF16),
                   jax.ShapeDtypeStruct((GDN_HEADS, HEAD_DIM, HEAD_DIM), F32)],
        scratch_shapes=_gdn_scratch(rows, 2),
        compiler_params=_cparams(2),
        name="gdn_prompt",
    )(proj, proj, proj, proj, proj, proj, proj, conv_w, conv_w, conv_w,
      gnorm_w.reshape(1, HEAD_DIM).astype(F32), ab, params)


def _gdn_sample(proj, ab, conv_state, state, conv_w, gnorm_w, params, n_live, rows):
    lw = GDN_HB * HEAD_DIM
    nseq = rows // SAMPLE_PAD
    n_in = nseq * n_live
    x_spec = lambda off: pl.BlockSpec((n_in, lw), lambda g, i: (i, off // lw + g))
    hist_spec = lambda off: pl.BlockSpec((GDN_CONV - 1, nseq, lw), lambda g, i: (0, i, (off - GQ_OFF) // lw + g))
    w_spec = lambda off: pl.BlockSpec((GDN_CONV, lw), lambda g, i: (0, (off - GQ_OFF) // lw + g))
    s_spec = pl.BlockSpec((nseq, GDN_HB, HEAD_DIM, HEAD_DIM), lambda g, i: (i, g, 0, 0))
    return pl.pallas_call(
        functools.partial(_gdn_kernel, rows=rows, seq=SAMPLE_PAD, carry=False, live_from=SAMPLE_PAD - n_live),
        grid=(GDN_NG, proj.shape[0] // n_in),
        in_specs=[x_spec(GQ_OFF), x_spec(GK_OFF), x_spec(GV_OFF),
                  hist_spec(GQ_OFF), hist_spec(GK_OFF), hist_spec(GV_OFF),
                  x_spec(Z_OFF), w_spec(GQ_OFF), w_spec(GK_OFF), w_spec(GV_OFF),
                  pl.BlockSpec((1, HEAD_DIM), lambda g, i: (0, 0)),
                  pl.BlockSpec((n_in, LANES), lambda g, i: (i, 0)),
                  pl.BlockSpec((SUBLANES, LANES), lambda g, i: (0, 0)),
                  s_spec],
        out_specs=[pl.BlockSpec((n_in, lw), lambda g, i: (i, g)), s_spec],
        out_shape=[jax.ShapeDtypeStruct((proj.shape[0], GDN_WIDTH), F32),
                   jax.ShapeDtypeStruct(state.shape, F32)],
        scratch_shapes=[pltpu.VMEM((rows, lw), F32), pltpu.VMEM((rows, LANES), F32)] + _gdn_scratch(rows, 1),
        compiler_params=_cparams(2),
        name="gdn_sample",
    )(proj, proj, proj, conv_state, conv_state, conv_state, proj, conv_w, conv_w, conv_w,
      gnorm_w.reshape(1, HEAD_DIM).astype(F32), ab, params, state)


def _layer(x_prompt, x_sample, win_k, win_v, conv_state, gdn_state, rel_bias,
           w_in, sinks, conv_w, a_log, dt_bias, gnorm_w, w_out,
           n_pre_mix, n_post_mix, n_pre_ffn, n_post_ffn, w_gate, w_up, w_down, tiles):
    _, t, d = x_prompt.shape
    nb, ts, _ = x_sample.shape
    ms = nb * ts
    l = win_k.shape[1]
    ff = w_gate.shape[1]
    n_mt = tiles["n_mt"]
    xp = x_prompt.reshape(t, d)
    sds = jax.ShapeDtypeStruct
    norm_w = lambda w: w.reshape(1, d).astype(F32)

    w_in_t = w_in.T
    w_tail = jnp.pad(w_in_t[PROJ_MAIN:], ((0, LANES - AB_WIDTH), (0, 0))).astype(BF16)
    (hp, ab_p), (hs, ab_s) = _rowwise(
        _rmsnorm_tail_body, [(xp, x_sample)], [norm_w(n_pre_mix), w_tail],
        [(sds((t, d), BF16), sds((ms, d), BF16)), (sds((t, LANES), F32), sds((ms, LANES), F32))],
        tiles["norm_rows"], "rmsnorm")
    proj_p, proj_s = _proj([(hp, hs)], [(w_in_t, 0, d)], PROJ_MAIN, tiles["in_bn"], F32, n_mt, "in_proj",
                           w_transposed=True)

    qi = jnp.arange(WINDOW)[:, None]
    kj = jnp.arange(2 * WINDOW)[None, :]
    dist_p = WINDOW + qi - kj
    dist_s = l + jnp.arange(SUBLANES)[:, None] - kj
    dist_s = jnp.where(kj < l + ts, dist_s, -1)
    bias_all = _bias_table(rel_bias, jnp.concatenate([dist_p, dist_s], axis=0))
    attn_p = _attn_prompt(proj_p, bias_all[:, :WINDOW], sinks)

    b_s = bias_all[:, WINDOW:WINDOW + ts]
    own_kv = (jnp.arange(ATTN_HEADS)[:, None] // ATTN_GROUP) == jnp.arange(ATTN_KV_HEADS)[None, :]
    bias_c = jnp.where(own_kv[:, None, None, :], b_s[:, :, :l, None], NEG_INF)
    bias_c = bias_c.reshape(ATTN_HEADS * ts, l * ATTN_KV_HEADS)
    bias_n = jnp.where(own_kv[:, None, :, None], b_s[:, :, None, l:l + ts], NEG_INF)
    bias_n = bias_n.reshape(ATTN_HEADS * ts, ATTN_KV_HEADS * ts)
    sink_rows = jnp.repeat(sinks.astype(F32), ts).reshape(ATTN_HEADS * ts, 1)
    attn_s, new_k, new_v = _attn_sample(proj_s, win_k, win_v, bias_c, bias_n, sink_rows, ts, tiles["attn_gb"])

    params = _gdn_params(a_log, dt_bias)
    og_p, s_p = _gdn_prompt(proj_p, ab_p, conv_w, gnorm_w, params, tiles["gdn_rows"])
    og_s, s_s = _gdn_sample(proj_s, ab_s, jnp.swapaxes(conv_state, 0, 1).astype(F32), gdn_state, conv_w, gnorm_w, params, ts,
                            tiles["gdn_sample_rows"])

    mix_p, mix_s = _proj([(attn_p, attn_s), (og_p, og_s)], [(w_out, 0, ATTN_WIDTH), (w_out, 1, GDN_WIDTH)],
                         d, tiles["out_bn"], BF16, n_mt, "out_proj")
    (h2_p,), (h2_s,) = _rowwise(
        _post_mix_body, [(xp, x_sample), (mix_p, mix_s)], [norm_w(n_post_mix), norm_w(n_pre_ffn)],
        [(sds((t, d), BF16), sds((ms, d), BF16))], tiles["post_rows"], "post_mix")
    act_p, act_s = _proj([(h2_p, h2_s)], [(w_gate, 0, d), (w_up, 0, d)], ff, tiles["ff_bn"], BF16, n_mt,
                         "ffn_gate_up", swiglu=True)
    ffn_p, ffn_s = _down(act_p, act_s, w_down, tiles["ff_bk"], n_mt)
    (y_p,), (y_s,) = _rowwise(
        _residual_body, [(xp, x_sample), (mix_p, mix_s), (ffn_p, ffn_s)], [norm_w(n_post_mix), norm_w(n_post_ffn)],
        [(sds((t, d), F32), sds(x_sample.shape, F32))], tiles["post_rows"], "residual_norm")

    lw = min(WINDOW, t)
    prompt_k = proj_p[t - lw:, K_OFF:K_OFF + KV_WIDTH].reshape(1, lw, ATTN_KV_HEADS, HEAD_DIM)
    prompt_v = proj_p[t - lw:, V_OFF:V_OFF + KV_WIDTH].reshape(1, lw, ATTN_KV_HEADS, HEAD_DIM)
    prompt_conv = proj_p[t - (GDN_CONV - 1):, GQ_OFF:GQ_OFF + GDN_CONV_CH][None]
    sample_conv = proj_s[:, GQ_OFF:GQ_OFF + GDN_CONV_CH].reshape(nb, ts, GDN_CONV_CH)[:, ts - (GDN_CONV - 1):]
    return (y_p.reshape(1, t, d), y_s,
            prompt_k, prompt_v, prompt_conv, s_p[None],
            new_k.reshape(nb, l, ATTN_KV_HEADS, HEAD_DIM), new_v.reshape(nb, l, ATTN_KV_HEADS, HEAD_DIM),
            sample_conv, s_s)


def _tiles(t, ms):
    n_mt = 8 if (t % (8 * 16) == 0 and ms % (8 * 16) == 0) else 1
    return dict(n_mt=n_mt, in_bn=1024, out_bn=1024, ff_bn=512, ff_bk=512, norm_rows=512, post_rows=256,
                gdn_rows=256, gdn_sample_rows=2 * GDN_CHUNK, attn_gb=8)


def kernel(x_prompt, x_sample, cache_win_k, cache_win_v, state_conv, state_gdn, rel_bias, w_in, attn_sinks,
           gdn_conv_w, gdn_a_log, gdn_dt_bias, gdn_norm_w, w_out, norm_pre_mix, norm_post_mix, norm_pre_ffn,
           norm_post_ffn, w_gate, w_up, w_down):
    depth = w_in.shape[0]
    assert depth == 1 and x_prompt.shape[0] == 1
    tiles = _tiles(x_prompt.shape[1], x_sample.shape[0] * x_sample.shape[1])
    outs = _layer(x_prompt, x_sample, cache_win_k[0], cache_win_v[0], state_conv[0], state_gdn[0], rel_bias,
                  w_in[0], attn_sinks[0], gdn_conv_w[0], gdn_a_log[0], gdn_dt_bias[0], gdn_norm_w[0], w_out[0],
                  norm_pre_mix[0], norm_post_mix[0], norm_pre_ffn[0], norm_post_ffn[0],
                  w_gate[0], w_up[0], w_down[0], tiles)
    yp, ys, pk, pv, pc, ps, sk, sv, sc, ss = outs
    return (yp, ys, pk[None], pv[None], pc[None], ps[None], sk[None], sv[None], sc[None], ss[None])
```

```python
import functools
import math

import jax
import jax.numpy as jnp
from jax import lax
from jax.experimental import pallas as pl
from jax.experimental.pallas import tpu as pltpu

F32 = jnp.float32
BF16 = jnp.bfloat16

HEAD_DIM = 128
ATTN_HEADS = 16
ATTN_KV_HEADS = 4
ATTN_GROUP = ATTN_HEADS // ATTN_KV_HEADS
ATTN_WIDTH = ATTN_HEADS * HEAD_DIM
KV_WIDTH = ATTN_KV_HEADS * HEAD_DIM
WINDOW = 128
REL_BUCKETS = 32
REL_MAX_DIST = 128
GDN_HEADS = 16
GDN_WIDTH = GDN_HEADS * HEAD_DIM
GDN_CONV = 4
GDN_CONV_CH = 3 * GDN_WIDTH
GDN_CHUNK = 64
NORM_EPS = 1e-6
NEG_INF = -1e30

Q_OFF = 0
K_OFF = ATTN_WIDTH
V_OFF = K_OFF + KV_WIDTH
GQ_OFF = V_OFF + KV_WIDTH
GK_OFF = GQ_OFF + GDN_WIDTH
GV_OFF = GK_OFF + GDN_WIDTH
Z_OFF = GV_OFF + GDN_WIDTH
PROJ_MAIN = Z_OFF + GDN_WIDTH
AB_WIDTH = 2 * GDN_HEADS

LANES = 128
SUBLANES = 8
GDN_HB = 8
GDN_NG = GDN_HEADS // GDN_HB
SAMPLE_PAD = 8
SAMPLE_BLOCK_ROWS = 128
VMEM_LIMIT = 56 * 1024 * 1024


def _cparams(n_axes, vmem=VMEM_LIMIT):
    return pltpu.CompilerParams(dimension_semantics=("arbitrary",) * n_axes, vmem_limit_bytes=vmem)


def _dot(a, b):
    return jnp.dot(a, b, preferred_element_type=F32)


def _rms(x, w):
    return x * lax.rsqrt(jnp.mean(x * x, axis=-1, keepdims=True) + NORM_EPS) * w


def _rowwise_kernel(*refs, body, n_in, n_const, n_out, prompt_steps):
    p_in, s_in = refs[:n_in], refs[n_in:2 * n_in]
    consts = refs[2 * n_in:2 * n_in + n_const]
    p_out = refs[2 * n_in + n_const:2 * n_in + n_const + n_out]
    s_out = refs[2 * n_in + n_const + n_out:]

    def run(ins, outs):
        rows2d = [r[...].reshape(-1, r.shape[-1]) for r in ins]
        for o_ref, v in zip(outs, body(*rows2d, *[c[...] for c in consts])):
            o_ref[...] = v.reshape(o_ref.shape).astype(o_ref.dtype)

    i = pl.program_id(0)
    pl.when(i < prompt_steps)(lambda: run(p_in, p_out))
    pl.when(i >= prompt_steps)(lambda: run(s_in, s_out))


def _rowwise(body, pairs, consts, outs, rows, name):
    t = pairs[0][0].shape[0]
    n_sample = math.prod(pairs[0][1].shape[:-1])
    rows = min(rows, t)
    sample_rows = min(SAMPLE_BLOCK_ROWS, n_sample)
    prompt_steps, sample_steps = t // rows, n_sample // sample_rows

    def spec(shape, is_prompt):
        n = rows if is_prompt else sample_rows
        lead = n if len(shape) == 2 else n // shape[1]
        block = (lead,) + tuple(shape[1:])
        if is_prompt:
            index = lambda i: (jnp.minimum(i, prompt_steps - 1),) + (0,) * (len(shape) - 1)
        else:
            index = lambda i: (jnp.maximum(i - prompt_steps, 0),) + (0,) * (len(shape) - 1)
        return pl.BlockSpec(block, index)

    in_specs = ([spec(p.shape, True) for p, _ in pairs] + [spec(s.shape, False) for _, s in pairs]
                + [pl.BlockSpec(c.shape, lambda i, nd=c.ndim: (0,) * nd) for c in consts])
    out_specs = [spec(p.shape, True) for p, _ in outs] + [spec(s.shape, False) for _, s in outs]
    res = pl.pallas_call(
        functools.partial(_rowwise_kernel, body=body, n_in=len(pairs), n_const=len(consts), n_out=len(outs),
                          prompt_steps=prompt_steps),
        grid=(prompt_steps + sample_steps,),
        in_specs=in_specs,
        out_specs=out_specs,
        out_shape=[p for p, _ in outs] + [s for _, s in outs],
        compiler_params=_cparams(1),
        name=name,
    )(*[p for p, _ in pairs], *[s for _, s in pairs], *consts)
    return res[:len(outs)], res[len(outs):]


def _rmsnorm_tail_body(x, w, w_tail):
    h = _rms(x, w).astype(BF16)
    return h, _dot_nt(h, w_tail)


def _post_mix_body(x, mix, w_post, w_pre):
    return (_rms(x + _rms(mix.astype(F32), w_post), w_pre),)


def _residual_body(x, mix, ffn, w_post_mix, w_post_ffn):
    return (x + _rms(mix.astype(F32), w_post_mix) + _rms(ffn.astype(F32), w_post_ffn),)


def _stack_rows(p_ref, s_ref, cols=slice(None)):
    return jnp.concatenate([p_ref[:, cols].astype(BF16), s_ref[:, cols].astype(BF16)], axis=0)


def _dot_nt(a, b):
    return lax.dot_general(a, b, (((1,), (1,)), ((), ())), preferred_element_type=F32)


def _proj_kernel(*refs, n_a, n_w, swiglu, w_transposed, row_blocks, n_mt, n_tiles, last_cols):
    a_refs = refs[:2 * n_a]
    w_refs = refs[2 * n_a:2 * n_a + n_w]
    op_ref, os_ref = refs[2 * n_a + n_w:2 * n_a + n_w + 2]
    scratch = refs[2 * n_a + n_w + 2:]
    stage_refs, wb_refs, sem = scratch[:n_w], scratch[n_w:2 * n_w], scratch[2 * n_w]
    mm = _dot_nt if w_transposed else _dot
    j, i = pl.program_id(0), pl.program_id(1)
    nj = pl.num_programs(0)

    def slice_copy(w, tile, s, slot, narrow):
        rows, cols = stage_refs[w].shape[1:]
        if w_transposed:
            src = w_refs[w].at[pl.ds(tile * (rows * n_mt) + s * rows, rows), pl.ds(row_blocks[w] * cols, cols)]
            dst = stage_refs[w].at[slot]
        else:
            width = last_cols if narrow else cols
            src = w_refs[w].at[pl.ds((row_blocks[w] * n_mt + s) * rows, rows), pl.ds(tile * cols, width)]
            dst = stage_refs[w].at[slot, :, pl.ds(0, width)]
        return pltpu.make_async_copy(src, dst, sem.at[w, slot])

    def with_tile(tile, fn):
        if last_cols == stage_refs[0].shape[2] or w_transposed:
            fn(False)
        elif isinstance(tile, int):
            fn(tile == n_tiles - 1)
        else:
            pl.when(tile == n_tiles - 1)(lambda: fn(True))
            pl.when(tile != n_tiles - 1)(lambda: fn(False))

    def cast_slice(w, half, s, slot):
        rows = stage_refs[w].shape[1]
        wb_refs[w][half, pl.ds(pl.multiple_of(s * rows, rows), rows), :] = stage_refs[w][slot].astype(BF16)

    @pl.when(jnp.logical_and(j == 0, i == 0))
    def _():
        if last_cols != stage_refs[0].shape[2]:
            for w in range(n_w):
                stage_refs[w][...] = jnp.zeros_like(stage_refs[w])
        for w in range(n_w):
            for s in range(n_mt):
                def fetch_now(narrow, w=w, s=s):
                    copy = slice_copy(w, 0, s, s % 2, narrow)
                    copy.start()
                    copy.wait()
                with_tile(0, fetch_now)
                cast_slice(w, 0, s, s % 2)
        if n_tiles > 1:
            for w in range(n_w):
                with_tile(1, lambda narrow, w=w: slice_copy(w, 1, 0, 0, narrow).start())

    @pl.when(j + 1 < nj)
    def _():
        slot = (j * n_mt + i) % 2
        for w in range(n_w):
            with_tile(j + 1, lambda narrow, w=w: slice_copy(w, j + 1, i, slot, narrow).wait())
            cast_slice(w, (j + 1) % 2, i, slot)
        last = i + 1 == n_mt
        next_tile = jnp.where(last, j + 2, j + 1)
        next_s = jnp.where(last, 0, i + 1)

        @pl.when(next_tile < nj)
        def _():
            for w in range(n_w):
                with_tile(next_tile, lambda narrow, w=w: slice_copy(w, next_tile, next_s, 1 - slot, narrow).start())

    half = j % 2
    bmp = op_ref.shape[0]

    def apply(width):
        cols = slice(None) if w_transposed else slice(0, width)
        rhs = [wb_ref[half, :, cols] for wb_ref in wb_refs]
        lhs = [_stack_rows(a_refs[2 * p], a_refs[2 * p + 1]) for p in range(n_a)]
        if swiglu:
            g = mm(lhs[0], rhs[0])
            o = g * jax.nn.sigmoid(g) * mm(lhs[0], rhs[1])
        else:
            o = mm(lhs[0], rhs[0])
            for l, r in zip(lhs[1:], rhs[1:]):
                o = o + mm(l, r)
        op_ref[:, :width] = o[:bmp].astype(op_ref.dtype)
        os_ref[:, :width] = o[bmp:].astype(os_ref.dtype)

    full_cols = op_ref.shape[1]
    if last_cols == full_cols:
        apply(full_cols)
    else:
        pl.when(j != n_tiles - 1)(lambda: apply(full_cols))
        pl.when(j == n_tiles - 1)(lambda: apply(last_cols))


def _proj(a_pairs, w_blocks, n, bn, out_dtype, n_mt, name, swiglu=False, w_transposed=False):
    tp, ts = a_pairs[0][0].shape[0], a_pairs[0][1].shape[0]
    bmp, bms = tp // n_mt, ts // n_mt
    in_specs = []
    for a_p, a_s in a_pairs:
        in_specs.append(pl.BlockSpec((bmp, a_p.shape[1]), lambda j, i: (i, 0)))
        in_specs.append(pl.BlockSpec((bms, a_s.shape[1]), lambda j, i: (i, 0)))
    in_specs += [pl.BlockSpec(memory_space=pl.ANY) for _ in w_blocks]
    n_tiles = pl.cdiv(n, bn)
    tile = lambda k: (bn, k) if w_transposed else (k, bn)
    stage = lambda k: (2, bn // n_mt, k) if w_transposed else (2, k // n_mt, bn)
    operands = [a for pair in a_pairs for a in pair] + [w for w, _, _ in w_blocks]
    return pl.pallas_call(
        functools.partial(_proj_kernel, n_a=len(a_pairs), n_w=len(w_blocks), swiglu=swiglu,
                          w_transposed=w_transposed, row_blocks=tuple(rb for _, rb, _ in w_blocks), n_mt=n_mt,
                          n_tiles=n_tiles, last_cols=n - (n_tiles - 1) * bn),
        grid=(n_tiles, n_mt),
        in_specs=in_specs,
        out_specs=[pl.BlockSpec((bmp, bn), lambda j, i: (i, j)), pl.BlockSpec((bms, bn), lambda j, i: (i, j))],
        out_shape=[jax.ShapeDtypeStruct((tp, n), out_dtype), jax.ShapeDtypeStruct((ts, n), out_dtype)],
        scratch_shapes=([pltpu.VMEM(stage(k), F32) for _, _, k in w_blocks]
                        + [pltpu.VMEM((2,) + tile(k), BF16) for _, _, k in w_blocks]
                        + [pltpu.SemaphoreType.DMA((len(w_blocks), 2))]),
        compiler_params=_cparams(2),
        name=name,
    )(*operands)


def _down_kernel(ap_ref, as_ref, b_ref, op_ref, os_ref, accp_ref, accs_ref, *, bn, k_total):
    k = pl.program_id(1)
    nk = pl.num_programs(1)
    bk = b_ref.shape[0]
    bmp = ap_ref.shape[0]
    last_rows = k_total - (k_total // bk) * bk

    @pl.when(k == 0)
    def _():
        accp_ref[...] = jnp.zeros_like(accp_ref)
        accs_ref[...] = jnp.zeros_like(accs_ref)

    def accumulate(kk):
        a = _stack_rows(ap_ref, as_ref, slice(0, kk))
        for n0 in range(0, accp_ref.shape[1], bn):
            d = _dot(a, b_ref[:kk, n0:n0 + bn].astype(BF16))
            accp_ref[:, n0:n0 + bn] += d[:bmp]
            accs_ref[:, n0:n0 + bn] += d[bmp:]

    if last_rows == 0:
        accumulate(bk)
    else:
        pl.when(k < nk - 1)(lambda: accumulate(bk))
        pl.when(k == nk - 1)(lambda: accumulate(last_rows))

    @pl.when(k == nk - 1)
    def _():
        op_ref[...] = accp_ref[...].astype(op_ref.dtype)
        os_ref[...] = accs_ref[...].astype(os_ref.dtype)


def _down(a_p, a_s, b, bk, n_mt):
    (tp, k_total), ts = a_p.shape, a_s.shape[0]
    n = b.shape[1]
    bmp, bms = tp // n_mt, ts // n_mt
    return pl.pallas_call(
        functools.partial(_down_kernel, bn=min(n, 512), k_total=k_total),
        grid=(n_mt, pl.cdiv(k_total, bk)),
        in_specs=[pl.BlockSpec((bmp, bk), lambda i, k: (i, k)), pl.BlockSpec((bms, bk), lambda i, k: (i, k)),
                  pl.BlockSpec((bk, n), lambda i, k: (k, 0))],
        out_specs=[pl.BlockSpec((bmp, n), lambda i, k: (i, 0), pipeline_mode=pl.Buffered(1)),
                   pl.BlockSpec((bms, n), lambda i, k: (i, 0))],
        out_shape=[jax.ShapeDtypeStruct((tp, n), BF16), jax.ShapeDtypeStruct((ts, n), BF16)],
        scratch_shapes=[pltpu.VMEM((bmp, n), F32), pltpu.VMEM((bms, n), F32)],
        compiler_params=_cparams(2),
        name="ffn_down",
    )(a_p, a_s, b)


def _rel_bucket(d):
    n = jnp.maximum(d, 0)
    max_exact = REL_BUCKETS // 2
    nf = jnp.maximum(n, 1).astype(F32)
    large = max_exact + (jnp.log(nf / max_exact) / math.log(REL_MAX_DIST / max_exact)
                         * (REL_BUCKETS - max_exact)).astype(jnp.int32)
    large = jnp.minimum(large, REL_BUCKETS - 1)
    return jnp.where(n < max_exact, n, large)


def _bias_kernel(rel_ref, bucket_ref, allowed_ref, o_ref):
    h = pl.program_id(0)
    bucket = bucket_ref[...]
    acc = jnp.zeros(bucket.shape, F32)
    for n in range(REL_BUCKETS):
        acc = jnp.where(bucket == n, rel_ref[n, h], acc)
    o_ref[0] = jnp.where(allowed_ref[...] > 0, acc, NEG_INF)


def _bias_table(rel_bias, dist):
    tq, tk = dist.shape
    allowed = ((dist >= 0) & (dist <= WINDOW)).astype(jnp.int32)
    full = pl.BlockSpec((tq, tk), lambda h: (0, 0))
    return pl.pallas_call(
        _bias_kernel,
        grid=(ATTN_HEADS,),
        in_specs=[pl.BlockSpec(memory_space=pltpu.SMEM), full, full],
        out_specs=pl.BlockSpec((1, tq, tk), lambda h: (h, 0, 0)),
        out_shape=jax.ShapeDtypeStruct((ATTN_HEADS, tq, tk), F32),
        compiler_params=_cparams(1),
        name="rel_bias_table",
    )(rel_bias.astype(F32), _rel_bucket(dist), allowed)


def _softmax_pv(s, sink):
    m = jnp.maximum(jnp.max(s, axis=-1, keepdims=True), sink)
    p = jnp.exp(s - m)
    denom = jnp.sum(p, axis=-1, keepdims=True) + jnp.exp(sink - m)
    return p.astype(BF16), denom


def _attn_prompt_kernel(sink_ref, cur_ref, prev_ref, tables_ref, o_ref, *, nq):
    i = pl.program_id(0)
    c = WINDOW
    head_cols = lambda base, h: slice(base + h * HEAD_DIM, base + (h + 1) * HEAD_DIM)
    blk = lambda b: slice(b * c, (b + 1) * c)

    def keys(b, h, cur_base, prev_base):
        before = prev_ref[:, head_cols(prev_base, h)] if b == 0 else cur_ref[blk(b - 1), head_cols(cur_base, h)]
        return jnp.concatenate([before, cur_ref[blk(b), head_cols(cur_base, h)]], axis=0).astype(BF16)

    kk = {(b, h): keys(b, h, K_OFF, 0) for b in range(nq) for h in range(ATTN_KV_HEADS)}
    vv = {(b, h): keys(b, h, V_OFF, KV_WIDTH) for b in range(nq) for h in range(ATTN_KV_HEADS)}
    pairs = [(b, hd) for b in range(nq) for hd in range(ATTN_HEADS)]
    scale = HEAD_DIM ** -0.5
    table = lambda b: jnp.minimum(i, 1) if b == 0 else 1
    s = {p: _dot_nt((cur_ref[blk(p[0]), head_cols(Q_OFF, p[1])] * scale).astype(BF16), kk[p[0], p[1] // ATTN_GROUP])
         for p in pairs}
    pd = {p: _softmax_pv(s[p] + tables_ref[table(p[0]), p[1]], sink_ref[p[1]]) for p in pairs}
    o = {p: _dot(pd[p][0], vv[p[0], p[1] // ATTN_GROUP]) / pd[p][1] for p in pairs}
    for b in range(nq):
        o_ref[blk(b), :] = jnp.concatenate([o[b, hd] for hd in range(ATTN_HEADS)], axis=1).astype(o_ref.dtype)


def _attn_prompt(proj, bias, sinks, nq):
    t = proj.shape[0]
    c = WINDOW
    qkv = V_OFF + KV_WIDTH
    first = jnp.where(jnp.arange(2 * c) < c, NEG_INF, bias)
    tables = jnp.stack([first, bias])
    return pl.pallas_call(
        functools.partial(_attn_prompt_kernel, nq=nq),
        grid=(t // (c * nq),),
        in_specs=[pl.BlockSpec(memory_space=pltpu.SMEM),
                  pl.BlockSpec((c * nq, qkv), lambda i: (i, 0)),
                  pl.BlockSpec((c, 2 * KV_WIDTH), lambda i: (jnp.maximum(i * nq - 1, 0), K_OFF // (2 * KV_WIDTH))),
                  pl.BlockSpec(tables.shape, lambda i: (0, 0, 0, 0))],
        out_specs=pl.BlockSpec((c * nq, ATTN_WIDTH), lambda i: (i, 0)),
        out_shape=jax.ShapeDtypeStruct((t, ATTN_WIDTH), BF16),
        compiler_params=_cparams(1),
        name="attn_prompt",
    )(sinks.astype(F32), proj, proj, tables)


def _attn_sample_kernel(x_ref, ck_ref, cv_ref, bias_c_ref, bias_n_ref, sink_ref, o_ref, ok_ref, ov_ref,
                        q_s, kn_s, vn_s, *, t_new):
    gb, l, kvh, hd = ck_ref.shape
    kc = ck_ref[...].reshape(gb, l * kvh, hd)
    vc = cv_ref[...].reshape(gb, l * kvh, hd)
    for b in range(gb):
        rows = slice(b * t_new, (b + 1) * t_new)
        for h in range(kvh):
            kn_s[b, h * t_new:(h + 1) * t_new, :] = x_ref[rows, K_OFF + h * hd:K_OFF + (h + 1) * hd]
            vn_s[b, h * t_new:(h + 1) * t_new, :] = x_ref[rows, V_OFF + h * hd:V_OFF + (h + 1) * hd]
        for head in range(ATTN_HEADS):
            q_s[b, head * t_new:(head + 1) * t_new, :] = x_ref[rows, head * hd:(head + 1) * hd]

    q = q_s[...].astype(BF16)
    scale = hd ** -0.5
    s_c = jnp.einsum("bqd,bkd->bqk", q, kc.astype(BF16), preferred_element_type=F32) * scale + bias_c_ref[...]
    s_n = jnp.einsum("bqd,bkd->bqk", q, kn_s[...].astype(BF16), preferred_element_type=F32) * scale + bias_n_ref[...]
    sink = sink_ref[...]
    m = jnp.maximum(jnp.maximum(jnp.max(s_c, axis=-1, keepdims=True), jnp.max(s_n, axis=-1, keepdims=True)), sink)
    p_c = jnp.exp(s_c - m)
    p_n = jnp.exp(s_n - m)
    denom = jnp.sum(p_c, axis=-1, keepdims=True) + jnp.sum(p_n, axis=-1, keepdims=True) + jnp.exp(sink - m)
    o = (jnp.einsum("bqk,bkd->bqd", p_c.astype(BF16), vc.astype(BF16), preferred_element_type=F32)
         + jnp.einsum("bqk,bkd->bqd", p_n.astype(BF16), vn_s[...].astype(BF16), preferred_element_type=F32)) / denom
    for b in range(gb):
        for head in range(ATTN_HEADS):
            o_ref[b * t_new:(b + 1) * t_new, head * hd:(head + 1) * hd] = o[b, head * t_new:(head + 1) * t_new, :]

    shift = t_new * kvh
    for c_val, n_s, out_ref in ((kc, kn_s, ok_ref), (vc, vn_s, ov_ref)):
        new_rows = [n_s[:, h * t_new + t:h * t_new + t + 1, :] for t in range(t_new) for h in range(kvh)]
        win = jnp.concatenate([c_val[:, shift:, :]] + new_rows, axis=1)
        out_ref[...] = win.reshape(gb, l, kvh, hd)


def _attn_sample(proj, cache_k, cache_v, bias_c, bias_n, sink_rows, t_new, gb):
    b, l = cache_k.shape[:2]
    rows = ATTN_HEADS * t_new
    qkv = V_OFF + KV_WIDTH
    cache_spec = pl.BlockSpec((gb, l, ATTN_KV_HEADS, HEAD_DIM), lambda i: (i, 0, 0, 0))
    full = lambda a: pl.BlockSpec(a.shape, lambda i: (0,) * a.ndim)
    return pl.pallas_call(
        functools.partial(_attn_sample_kernel, t_new=t_new),
        grid=(b // gb,),
        in_specs=[pl.BlockSpec((gb * t_new, qkv), lambda i: (i, 0)), cache_spec, cache_spec,
                  full(bias_c), full(bias_n), full(sink_rows)],
        out_specs=[pl.BlockSpec((gb * t_new, ATTN_WIDTH), lambda i: (i, 0)), cache_spec, cache_spec],
        out_shape=[jax.ShapeDtypeStruct((b * t_new, ATTN_WIDTH), F32),
                   jax.ShapeDtypeStruct(cache_k.shape, F32),
                   jax.ShapeDtypeStruct(cache_v.shape, F32)],
        scratch_shapes=[pltpu.VMEM((gb, rows, HEAD_DIM), F32),
                        pltpu.VMEM((gb, ATTN_KV_HEADS * t_new, HEAD_DIM), F32),
                        pltpu.VMEM((gb, ATTN_KV_HEADS * t_new, HEAD_DIM), F32)],
        compiler_params=_cparams(1),
        name="attn_sample",
    )(proj, cache_k, cache_v, bias_c, bias_n, sink_rows)


def _split3(x):
    hi = x.astype(BF16)
    r = x - hi.astype(F32)
    mid = r.astype(BF16)
    lo = (r - mid.astype(F32)).astype(BF16)
    return hi, mid, lo


def _select_dot(sel, x):
    hi, mid, lo = _split3(x)
    return _dot(sel, hi) + _dot(sel, mid) + _dot(sel, lo)


def _dot_select(x, sel):
    hi, mid, lo = _split3(x)
    return _dot(hi, sel) + _dot(mid, sel) + _dot(lo, sel)


def _softplus(x):
    return jnp.maximum(x, 0.0) + jnp.log1p(jnp.exp(-jnp.abs(x)))


def _conv_silu(pad_ref, w, r0, n):
    top = SUBLANES + r0
    y = pad_ref[top:top + n, :] * w[GDN_CONV - 1:GDN_CONV]
    for s in range(1, GDN_CONV):
        y = y + pad_ref[top - s:top - s + n, :] * w[GDN_CONV - 1 - s:GDN_CONV - s]
    return y * jax.nn.sigmoid(y)


def _l2norm_heads(x, scale):
    outs = []
    for hh in range(GDN_HB):
        xh = x[:, hh * HEAD_DIM:(hh + 1) * HEAD_DIM]
        outs.append(xh * (lax.rsqrt(jnp.sum(xh * xh, axis=-1, keepdims=True) + NORM_EPS) * scale))
    return jnp.concatenate(outs, axis=1)


GDN_N_PADS = 4
STAGE_ROWS = 32
PROBE_BOUND = 1e30
GDN_N_STAGED = 10


def _gdn_kernel(*refs, rows, seq, carry, live_from):
    kw = dict(rows=rows, seq=seq, carry=carry, live_from=live_from)
    if not carry:
        io, pads, staged = refs[:-GDN_N_PADS - GDN_N_STAGED], refs[-GDN_N_PADS - GDN_N_STAGED:-GDN_N_STAGED], \
            refs[-GDN_N_STAGED:]
        _gdn_body(io, pads, staged, staged, **kw)
        return
    n_scratch = GDN_N_PADS + 2 * GDN_N_STAGED
    io, pads = refs[:-n_scratch], refs[-n_scratch:-2 * GDN_N_STAGED]
    set_a, set_b = refs[-2 * GDN_N_STAGED:-GDN_N_STAGED], refs[-GDN_N_STAGED:]
    s_ref = io[-1]
    step = pl.program_id(1)

    @pl.when(step == 0)
    def _():
        s_ref[...] = jnp.zeros_like(s_ref)
        for ref in set_b:
            ref[...] = jnp.zeros_like(ref)

    pl.when(step % 2 == 0)(lambda: _gdn_body(io, pads, set_a, set_b, **kw))
    pl.when(step % 2 == 1)(lambda: _gdn_body(io, pads, set_b, set_a, **kw))


def _gdn_body(refs, pads, staged_w, staged_r, *, rows, seq, carry, live_from):
    c = GDN_CHUNK
    nc = rows // c
    nseq = c // seq
    n_live = seq - live_from
    lw = GDN_HB * HEAD_DIM
    if carry:
        (xq_ref, xk_ref, xv_ref, tq_ref, tk_ref, tv_ref, z_ref, wq_ref, wk_ref, wv_ref, gn_ref,
         ab_ref, pc_ref, og_ref, s_ref) = refs
        s0_ref = None
    else:
        (xq_ref, xk_ref, xv_ref, cq_ref, ck_ref, cv_ref, z_in, wq_ref, wk_ref, wv_ref, gn_ref,
         ab_in, pc_ref, s0_ref, og_ref, s_ref, z_ref, ab_ref) = refs
    padq_s, padk_s, padv_s, eb_s = pads
    kn_w, kb_w, qn_w, vb_w, kbe_w, qe_w, kd_w, egc_w, egl_w, gcr_w = staged_w
    kn_s, kb_s, qn_s, vb_s, kbe_s, qe_s, kd_s, egc_s, egl_s, gcr_s = staged_r

    group = pl.program_id(0)
    step = pl.program_id(1)
    row = lax.broadcasted_iota(jnp.int32, (rows, 1), 0)
    live = (row % seq) >= live_from
    keep_live = (lambda x: jnp.where(live, x, 0.0)) if live_from else (lambda x: x)

    if carry:
        for pad_ref, x_ref, t_ref in ((padq_s, xq_ref, tq_ref), (padk_s, xk_ref, tk_ref), (padv_s, xv_ref, tv_ref)):
            pad_ref[0:SUBLANES, :] = jnp.where(step == 0, 0.0, t_ref[...])
            pad_ref[SUBLANES:, :] = x_ref[...]
    else:
        def gather(dst_ref, x_ref, hist_ref, top):
            dst_ref[...] = jnp.zeros_like(dst_ref)
            for b in range(rows // seq):
                r0 = top + b * seq + live_from
                if hist_ref is not None:
                    dst_ref[r0 - (GDN_CONV - 1):r0, :] = hist_ref[:, b, :]
                dst_ref[r0:r0 + n_live, :] = x_ref[b * n_live:(b + 1) * n_live, :]

        gather(padq_s, xq_ref, cq_ref, SUBLANES)
        gather(padk_s, xk_ref, ck_ref, SUBLANES)
        gather(padv_s, xv_ref, cv_ref, SUBLANES)
        gather(z_ref, z_in, None, 0)
        gather(ab_ref, ab_in, None, 0)

    def stage_decays():
        ab = ab_ref[...]
        g_col = keep_live(-jnp.exp(pc_ref[0:1, :]) * _softplus(ab + pc_ref[1:2, :]))
        beta_col = keep_live(jax.nn.sigmoid(ab))
        ri = lax.broadcasted_iota(jnp.int32, (rows, rows), 0)
        ci = lax.broadcasted_iota(jnp.int32, (rows, rows), 1)
        same = (ri // seq) == (ci // seq)
        cum_sel = jnp.where(jnp.logical_and(same, ci <= ri), 1.0, 0.0).astype(BF16)
        tot_sel = jnp.where(same, 1.0, 0.0).astype(BF16)
        gc_col = _select_dot(cum_sel, g_col)
        gl_col = _select_dot(tot_sel, g_col)
        first = group * GDN_HB
        li = lax.broadcasted_iota(jnp.int32, (LANES, lw), 0)
        ni = lax.broadcasted_iota(jnp.int32, (LANES, lw), 1)
        spread_g = jnp.where(li == ni // HEAD_DIM + first, 1.0, 0.0).astype(BF16)
        spread_b = jnp.where(li == ni // HEAD_DIM + first + GDN_HEADS, 1.0, 0.0).astype(BF16)
        egc_w[...] = _dot_select(gc_col, spread_g)
        egl_w[...] = _dot_select(gl_col, spread_g)
        eb_s[...] = _dot_select(beta_col, spread_b)
        pi = lax.broadcasted_iota(jnp.int32, (LANES, LANES), 0)
        pj = lax.broadcasted_iota(jnp.int32, (LANES, LANES), 1)
        pick = jnp.where(jnp.logical_and(pi == pj + first, pj < GDN_HB), 1.0, 0.0).astype(BF16)
        gc_heads = _dot_select(gc_col, pick)
        if rows % LANES:
            gc_heads = jnp.concatenate([gc_heads, jnp.zeros((LANES - rows % LANES, LANES), F32)], axis=0)
        gcr_w[...] = gc_heads.T[:SUBLANES, :rows]

    def stage_rows(r0, n, after=None):
        sl = slice(r0, r0 + n)
        keep = (lambda x: jnp.where(live[sl], x, 0.0)) if live_from else (lambda x: x)
        gate = 1.0 if after is None else jnp.where(after > PROBE_BOUND, 0.0, 1.0)
        conv = lambda pad_ref, w_ref: keep(_conv_silu(pad_ref, w_ref[...] * gate, r0, n))
        qn = _l2norm_heads(conv(padq_s, wq_ref), HEAD_DIM ** -0.5)
        kn = _l2norm_heads(conv(padk_s, wk_ref), 1.0)
        v = conv(padv_s, wv_ref)
        egc, egl, eb = egc_w[sl, :], egl_w[sl, :], eb_s[sl, :]
        decay_in = jnp.exp(egc)
        kb = kn * eb
        kn_w[sl, :] = kn
        kb_w[sl, :] = kb
        qn_w[sl, :] = qn
        vb_w[sl, :] = v * eb
        kbe_w[sl, :] = kb * decay_in
        qe_w[sl, :] = qn * decay_in
        kd_w[sl, :] = kn * jnp.exp(egl - egc)

    n_slices = rows // STAGE_ROWS if carry else 1
    pending = [functools.partial(stage_rows, r * (rows // n_slices), rows // n_slices) for r in range(n_slices)]
    if not carry:
        stage_decays()
        pending.pop(0)()

    cr = lax.broadcasted_iota(jnp.int32, (c, c), 0)
    cc = lax.broadcasted_iota(jnp.int32, (c, c), 1)
    same_c = (cr // seq) == (cc // seq)
    causal_bias = jnp.where(jnp.logical_and(same_c, cr >= cc), 0.0, NEG_INF)
    strict = jnp.where(jnp.logical_and(same_c, cr > cc), 1.0, 0.0)

    eye = jnp.where(cr == cc, 1.0, 0.0)
    seq_of_row = lax.broadcasted_iota(jnp.int32, (c, HEAD_DIM), 0) // seq

    def lower_left(s):
        return jnp.logical_and(jnp.logical_and(cr // (2 * s) == cc // (2 * s), (cr // s) % 2 == 1),
                               (cc // s) % 2 == 0)

    merge_sizes = [2 ** e for e in range(1, max(1, math.ceil(math.log2(n_live))))]
    pair_mask = jnp.where(lower_left(1), 1.0, 0.0)
    merge_masks = [jnp.where(lower_left(s), 1.0, 0.0) for s in merge_sizes]

    pairs = [(ch, hh) for ch in range(nc) for hh in range(GDN_HB)]
    rows_of = lambda ch: slice(ch * c, (ch + 1) * c)
    lanes_of = lambda hh: slice(hh * HEAD_DIM, (hh + 1) * HEAD_DIM)
    uw_all, qk_all, a_mat, t_mat, t_a = {}, {}, {}, {}, {}
    for ch, hh in pairs:
        rs, hs = rows_of(ch), lanes_of(hh)
        gcol = egc_s[rs, hh * HEAD_DIM:hh * HEAD_DIM + c]
        grow = gcr_s[hh:hh + 1, rs]
        decay = jnp.exp(gcol - grow + causal_bias)
        kq = jnp.concatenate([kb_s[rs, hs], qn_s[rs, hs]], axis=0).astype(BF16)
        kk = _dot_nt(kq, kn_s[rs, hs].astype(BF16)) * jnp.concatenate([decay, decay], axis=0)
        a_mat[ch, hh] = kk[:c] * strict
        qk_all[ch, hh] = kk[c:].astype(BF16)
        t_mat[ch, hh] = eye - a_mat[ch, hh] * pair_mask
    probes = [a_mat[pairs[-1]][0:1, 0:1]]
    if carry:
        stage_decays()
    for mask in merge_masks:
        for p in pairs:
            t_a[p] = _dot(t_mat[p].astype(BF16), (a_mat[p] * mask).astype(BF16))
        for p in pairs:
            t_mat[p] = t_mat[p] - _dot(t_a[p].astype(BF16), t_mat[p].astype(BF16))
        probes.append(t_mat[pairs[-1]][0:1, 0:1])
    for ch, hh in pairs:
        rs, hs = rows_of(ch), lanes_of(hh)
        uw_all[ch, hh] = _dot(t_mat[ch, hh].astype(BF16),
                              jnp.concatenate([vb_s[rs, hs], kbe_s[rs, hs]], axis=1).astype(BF16))

    if carry:
        state = [s_ref[hh] for hh in range(GDN_HB)]
    new_states = {}
    heads = range(GDN_HB)
    tn_dot = lambda a, b: lax.dot_general(a, b, (((0,), (0,)), ((), ())), preferred_element_type=F32)
    for ch in range(nc):
        rs = rows_of(ch)
        w_s, q_s, v_new_b, o = {}, {}, {}, {}
        for hh in heads:
            w = uw_all[ch, hh][:, HEAD_DIM:]
            q_e = qe_s[rs, lanes_of(hh)]
            if carry:
                wq = _dot(jnp.concatenate([w, q_e], axis=0).astype(BF16), state[hh].astype(BF16))
                w_s[hh], q_s[hh] = wq[:c], wq[c:]
            else:
                parts = []
                for b in range(nseq):
                    bs = slice(b * seq, (b + 1) * seq)
                    parts.append(_dot(jnp.concatenate([w[bs], q_e[bs]], axis=0).astype(BF16),
                                      s0_ref[ch * nseq + b, hh].astype(BF16)))
                w_s[hh] = jnp.concatenate([p[:seq] for p in parts], axis=0)
                q_s[hh] = jnp.concatenate([p[seq:] for p in parts], axis=0)
        for hh in heads:
            v_new_b[hh] = (uw_all[ch, hh][:, :HEAD_DIM] - w_s[hh]).astype(BF16)
            o[hh] = q_s[hh] + _dot(qk_all[ch, hh], v_new_b[hh])
        for hh in heads:
            hs = lanes_of(hh)
            k_d = kd_s[rs, hs]
            if carry:
                state[hh] = (state[hh] * jnp.exp(egl_s[ch * c:ch * c + 1, hs])
                             + tn_dot(k_d.astype(BF16), v_new_b[hh]))
            else:
                for b in range(nseq):
                    k_db = jnp.where(seq_of_row == b, k_d, 0.0).astype(BF16)
                    r1 = ch * c + b * seq
                    new_states[ch * nseq + b, hh] = (
                        s0_ref[ch * nseq + b, hh] * jnp.exp(egl_s[r1:r1 + 1, hs]) + tn_dot(k_db, v_new_b[hh]))
        if carry:
            probes.append(_dot(qk_all[ch, 0], jnp.ones((c, HEAD_DIM), BF16))[0:1, 0:1])
        outs = []
        for hh in heads:
            z = z_ref[rs, lanes_of(hh)]
            outs.append(_rms(o[hh], gn_ref[...]) * (z * jax.nn.sigmoid(z)))
        og = jnp.concatenate(outs, axis=1)
        if carry:
            og_ref[rs, :] = og.astype(og_ref.dtype)
        else:
            for b in range(nseq):
                r0 = b * seq + live_from
                og_ref[(ch * nseq + b) * n_live:(ch * nseq + b + 1) * n_live, :] = (
                    og[r0:r0 + n_live].astype(og_ref.dtype))
    for n, stage in enumerate(pending):
        stage(after=probes[min(n, len(probes) - 1)])
    if carry:
        s_ref[...] = jnp.stack(state, axis=0)
    else:
        for (b, hh), s_new in new_states.items():
            s_ref[b, hh] = s_new


def _gdn_params(a_log, dt_bias):
    rows = jnp.stack([a_log.astype(F32), dt_bias.astype(F32)], axis=0)
    return jnp.pad(rows, ((0, SUBLANES - 2), (0, LANES - GDN_HEADS)))


def _gdn_scratch(rows, n_sets):
    lw = GDN_HB * HEAD_DIM
    staged = [(rows, lw)] * (GDN_N_STAGED - 1) + [(SUBLANES, rows)]
    return ([pltpu.VMEM((SUBLANES + rows, lw), F32) for _ in range(GDN_N_PADS - 1)] + [pltpu.VMEM((rows, lw), F32)]
            + [pltpu.VMEM(shape, F32) for _ in range(n_sets) for shape in staged])


def _gdn_prompt(proj, ab, conv_w, gnorm_w, params, rows):
    t = proj.shape[0]
    nb = t // rows
    lw = GDN_HB * HEAD_DIM
    head = lambda i: jnp.minimum(i, nb - 1)
    done = lambda i: jnp.maximum(i - 1, 0)
    x_spec = lambda off: pl.BlockSpec((rows, lw), lambda g, i: (head(i), off // lw + g))
    tail_spec = lambda off: pl.BlockSpec(
        (SUBLANES, lw), lambda g, i: (jnp.maximum(head(i) * (rows // SUBLANES) - 1, 0), off // lw + g))
    w_spec = lambda off: pl.BlockSpec((GDN_CONV, lw), lambda g, i: (0, (off - GQ_OFF) // lw + g))
    return pl.pallas_call(
        functools.partial(_gdn_kernel, rows=rows, seq=GDN_CHUNK, carry=True, live_from=0),
        grid=(GDN_NG, nb + 1),
        in_specs=[x_spec(GQ_OFF), x_spec(GK_OFF), x_spec(GV_OFF),
                  tail_spec(GQ_OFF), tail_spec(GK_OFF), tail_spec(GV_OFF),
                  pl.BlockSpec((rows, lw), lambda g, i: (done(i), Z_OFF // lw + g)),
                  w_spec(GQ_OFF), w_spec(GK_OFF), w_spec(GV_OFF),
                  pl.BlockSpec((1, HEAD_DIM), lambda g, i: (0, 0)),
                  pl.BlockSpec((rows, LANES), lambda g, i: (head(i), 0)),
                  pl.BlockSpec((SUBLANES, LANES), lambda g, i: (0, 0))],
        out_specs=[pl.BlockSpec((rows, lw), lambda g, i: (done(i), g)),
                   pl.BlockSpec((GDN_HB, HEAD_DIM, HEAD_DIM), lambda g, i: (g, 0, 0))],
        out_shape=[jax.ShapeDtypeStruct((t, GDN_WIDTH), BF16),
                   jax.ShapeDtypeStruct((GDN_HEADS, HEAD_DIM, HEAD_DIM), F32)],
        scratch_shapes=_gdn_scratch(rows, 2),
        compiler_params=_cparams(2),
        name="gdn_prompt",
    )(proj, proj, proj, proj, proj, proj, proj, conv_w, conv_w, conv_w,
      gnorm_w.reshape(1, HEAD_DIM).astype(F32), ab, params)


def _gdn_sample(proj, ab, conv_state, state, conv_w, gnorm_w, params, n_live, rows):
    lw = GDN_HB * HEAD_DIM
    nseq = rows // SAMPLE_PAD
    n_in = nseq * n_live
    x_spec = lambda off: pl.BlockSpec((n_in, lw), lambda g, i: (i, off // lw + g))
    hist_spec = lambda off: pl.BlockSpec((GDN_CONV - 1, nseq, lw), lambda g, i: (0, i, (off - GQ_OFF) // lw + g))
    w_spec = lambda off: pl.BlockSpec((GDN_CONV, lw), lambda g, i: (0, (off - GQ_OFF) // lw + g))
    s_spec = pl.BlockSpec((nseq, GDN_HB, HEAD_DIM, HEAD_DIM), lambda g, i: (i, g, 0, 0))
    return pl.pallas_call(
        functools.partial(_gdn_kernel, rows=rows, seq=SAMPLE_PAD, carry=False, live_from=SAMPLE_PAD - n_live),
        grid=(GDN_NG, proj.shape[0] // n_in),
        in_specs=[x_spec(GQ_OFF), x_spec(GK_OFF), x_spec(GV_OFF),
                  hist_spec(GQ_OFF), hist_spec(GK_OFF), hist_spec(GV_OFF),
                  x_spec(Z_OFF), w_spec(GQ_OFF), w_spec(GK_OFF), w_spec(GV_OFF),
                  pl.BlockSpec((1, HEAD_DIM), lambda g, i: (0, 0)),
                  pl.BlockSpec((n_in, LANES), lambda g, i: (i, 0)),
                  pl.BlockSpec((SUBLANES, LANES), lambda g, i: (0, 0)),
                  s_spec],
        out_specs=[pl.BlockSpec((n_in, lw), lambda g, i: (i, g)), s_spec],
        out_shape=[jax.ShapeDtypeStruct((proj.shape[0], GDN_WIDTH), F32),
                   jax.ShapeDtypeStruct(state.shape, F32)],
        scratch_shapes=[pltpu.VMEM((rows, lw), F32), pltpu.VMEM((rows, LANES), F32)] + _gdn_scratch(rows, 1),
        compiler_params=_cparams(2),
        name="gdn_sample",
    )(proj, proj, proj, conv_state, conv_state, conv_state, proj, conv_w, conv_w, conv_w,
      gnorm_w.reshape(1, HEAD_DIM).astype(F32), ab, params, state)


def _layer(x_prompt, x_sample, win_k, win_v, conv_state, gdn_state, rel_bias,
           w_in, sinks, conv_w, a_log, dt_bias, gnorm_w, w_out,
           n_pre_mix, n_post_mix, n_pre_ffn, n_post_ffn, w_gate, w_up, w_down, tiles):
    _, t, d = x_prompt.shape
    nb, ts, _ = x_sample.shape
    ms = nb * ts
    l = win_k.shape[1]
    ff = w_gate.shape[1]
    n_mt = tiles["n_mt"]
    xp = x_prompt.reshape(t, d)
    sds = jax.ShapeDtypeStruct
    norm_w = lambda w: w.reshape(1, d).astype(F32)

    w_in_t = w_in.T
    w_tail = jnp.pad(w_in_t[PROJ_MAIN:], ((0, LANES - AB_WIDTH), (0, 0))).astype(BF16)
    (hp, ab_p), (hs, ab_s) = _rowwise(
        _rmsnorm_tail_body, [(xp, x_sample)], [norm_w(n_pre_mix), w_tail],
        [(sds((t, d), BF16), sds((ms, d), BF16)), (sds((t, LANES), F32), sds((ms, LANES), F32))],
        tiles["norm_rows"], "rmsnorm")
    proj_p, proj_s = _proj([(hp, hs)], [(w_in_t, 0, d)], PROJ_MAIN, tiles["in_bn"], F32, n_mt, "in_proj",
                           w_transposed=True)

    qi = jnp.arange(WINDOW)[:, None]
    kj = jnp.arange(2 * WINDOW)[None, :]
    dist_p = WINDOW + qi - kj
    dist_s = l + jnp.arange(SUBLANES)[:, None] - kj
    dist_s = jnp.where(kj < l + ts, dist_s, -1)
    bias_all = _bias_table(rel_bias, jnp.concatenate([dist_p, dist_s], axis=0))
    attn_p = _attn_prompt(proj_p, bias_all[:, :WINDOW], sinks, tiles["attn_nq"])

    b_s = bias_all[:, WINDOW:WINDOW + ts]
    own_kv = (jnp.arange(ATTN_HEADS)[:, None] // ATTN_GROUP) == jnp.arange(ATTN_KV_HEADS)[None, :]
    bias_c = jnp.where(own_kv[:, None, None, :], b_s[:, :, :l, None], NEG_INF)
    bias_c = bias_c.reshape(ATTN_HEADS * ts, l * ATTN_KV_HEADS)
    bias_n = jnp.where(own_kv[:, None, :, None], b_s[:, :, None, l:l + ts], NEG_INF)
    bias_n = bias_n.reshape(ATTN_HEADS * ts, ATTN_KV_HEADS * ts)
    sink_rows = jnp.repeat(sinks.astype(F32), ts).reshape(ATTN_HEADS * ts, 1)
    attn_s, new_k, new_v = _attn_sample(proj_s, win_k, win_v, bias_c, bias_n, sink_rows, ts, tiles["attn_gb"])

    params = _gdn_params(a_log, dt_bias)
    og_p, s_p = _gdn_prompt(proj_p, ab_p, conv_w, gnorm_w, params, tiles["gdn_rows"])
    og_s, s_s = _gdn_sample(proj_s, ab_s, jnp.swapaxes(conv_state, 0, 1).astype(F32), gdn_state, conv_w, gnorm_w, params, ts,
                            tiles["gdn_sample_rows"])

    mix_p, mix_s = _proj([(attn_p, attn_s), (og_p, og_s)], [(w_out, 0, ATTN_WIDTH), (w_out, 1, GDN_WIDTH)],
                         d, tiles["out_bn"], BF16, n_mt, "out_proj")
    (h2_p,), (h2_s,) = _rowwise(
        _post_mix_body, [(xp, x_sample), (mix_p, mix_s)], [norm_w(n_post_mix), norm_w(n_pre_ffn)],
        [(sds((t, d), BF16), sds((ms, d), BF16))], tiles["post_rows"], "post_mix")
    act_p, act_s = _proj([(h2_p, h2_s)], [(w_gate, 0, d), (w_up, 0, d)], ff, tiles["ff_bn"], BF16, n_mt,
                         "ffn_gate_up", swiglu=True)
    ffn_p, ffn_s = _down(act_p, act_s, w_down, tiles["ff_bk"], n_mt)
    (y_p,), (y_s,) = _rowwise(
        _residual_body, [(xp, x_sample), (mix_p, mix_s), (ffn_p, ffn_s)], [norm_w(n_post_mix), norm_w(n_post_ffn)],
        [(sds((t, d), F32), sds(x_sample.shape, F32))], tiles["post_rows"], "residual_norm")

    lw = min(WINDOW, t)
    prompt_k = proj_p[t - lw:, K_OFF:K_OFF + KV_WIDTH].reshape(1, lw, ATTN_KV_HEADS, HEAD_DIM)
    prompt_v = proj_p[t - lw:, V_OFF:V_OFF + KV_WIDTH].reshape(1, lw, ATTN_KV_HEADS, HEAD_DIM)
    prompt_conv = proj_p[t - (GDN_CONV - 1):, GQ_OFF:GQ_OFF + GDN_CONV_CH][None]
    sample_conv = proj_s[:, GQ_OFF:GQ_OFF + GDN_CONV_CH].reshape(nb, ts, GDN_CONV_CH)[:, ts - (GDN_CONV - 1):]
    return (y_p.reshape(1, t, d), y_s,
            prompt_k, prompt_v, prompt_conv, s_p[None],
            new_k.reshape(nb, l, ATTN_KV_HEADS, HEAD_DIM), new_v.reshape(nb, l, ATTN_KV_HEADS, HEAD_DIM),
            sample_conv, s_s)


def _tiles(t, ms):
    n_mt = 8 if (t % (8 * 16) == 0 and ms % (8 * 16) == 0) else 1
    return dict(n_mt=n_mt, in_bn=1024, out_bn=1024, ff_bn=512, ff_bk=512, norm_rows=512, post_rows=256,
                gdn_rows=256, gdn_sample_rows=2 * GDN_CHUNK, attn_gb=8, attn_nq=2)


def kernel(x_prompt, x_sample, cache_win_k, cache_win_v, state_conv, state_gdn, rel_bias, w_in, attn_sinks,
           gdn_conv_w, gdn_a_log, gdn_dt_bias, gdn_norm_w, w_out, norm_pre_mix, norm_post_mix, norm_pre_ffn,
           norm_post_ffn, w_gate, w_up, w_down):
    depth = w_in.shape[0]
    assert depth == 1 and x_prompt.shape[0] == 1
    tiles = _tiles(x_prompt.shape[1], x_sample.shape[0] * x_sample.shape[1])
    outs = _layer(x_prompt, x_sample, cache_win_k[0], cache_win_v[0], state_conv[0], state_gdn[0], rel_bias,
                  w_in[0], attn_sinks[0], gdn_conv_w[0], gdn_a_log[0], gdn_dt_bias[0], gdn_norm_w[0], w_out[0],
                  norm_pre_mix[0], norm_post_mix[0], norm_pre_ffn[0], norm_post_ffn[0],
                  w_gate[0], w_up[0], w_down[0], tiles)
    yp, ys, pk, pv, pc, ps, sk, sv, sc, ss = outs
    return (yp, ys, pk[None], pv[None], pc[None], ps[None], sk[None], sv[None], sc[None], ss[None])
```

```python
import functools
import math

import jax
import jax.numpy as jnp
from jax import lax
from jax.experimental import pallas as pl
from jax.experimental.pallas import tpu as pltpu

F32 = jnp.float32
BF16 = jnp.bfloat16

HEAD_DIM = 128
ATTN_HEADS = 16
ATTN_KV_HEADS = 4
ATTN_GROUP = ATTN_HEADS // ATTN_KV_HEADS
ATTN_WIDTH = ATTN_HEADS * HEAD_DIM
KV_WIDTH = ATTN_KV_HEADS * HEAD_DIM
WINDOW = 128
REL_BUCKETS = 32
REL_MAX_DIST = 128
GDN_HEADS = 16
GDN_WIDTH = GDN_HEADS * HEAD_DIM
GDN_CONV = 4
GDN_CONV_CH = 3 * GDN_WIDTH
GDN_CHUNK = 64
NORM_EPS = 1e-6
NEG_INF = -1e30

Q_OFF = 0
K_OFF = ATTN_WIDTH
V_OFF = K_OFF + KV_WIDTH
GQ_OFF = V_OFF + KV_WIDTH
GK_OFF = GQ_OFF + GDN_WIDTH
GV_OFF = GK_OFF + GDN_WIDTH
Z_OFF = GV_OFF + GDN_WIDTH
PROJ_MAIN = Z_OFF + GDN_WIDTH
AB_WIDTH = 2 * GDN_HEADS

LANES = 128
SUBLANES = 8
GDN_HB = 8
GDN_NG = GDN_HEADS // GDN_HB
SAMPLE_PAD = 8
SAMPLE_BLOCK_ROWS = 128
SIDE_CHUNK_ROWS = 64
VMEM_LIMIT = 56 * 1024 * 1024


def _cparams(n_axes, vmem=VMEM_LIMIT):
    return pltpu.CompilerParams(dimension_semantics=("arbitrary",) * n_axes, vmem_limit_bytes=vmem)


def _dot(a, b):
    return jnp.dot(a, b, preferred_element_type=F32)


def _rms(x, w):
    return x * lax.rsqrt(jnp.mean(x * x, axis=-1, keepdims=True) + NORM_EPS) * w


def _rowwise_kernel(*refs, body, n_in, n_const, n_out, prompt_steps):
    p_in, s_in = refs[:n_in], refs[n_in:2 * n_in]
    consts = refs[2 * n_in:2 * n_in + n_const]
    p_out = refs[2 * n_in + n_const:2 * n_in + n_const + n_out]
    s_out = refs[2 * n_in + n_const + n_out:]

    def run(ins, outs):
        rows2d = [r[...].reshape(-1, r.shape[-1]) for r in ins]
        for o_ref, v in zip(outs, body(*rows2d, *[c[...] for c in consts])):
            o_ref[...] = v.reshape(o_ref.shape).astype(o_ref.dtype)

    i = pl.program_id(0)
    pl.when(i < prompt_steps)(lambda: run(p_in, p_out))
    pl.when(i >= prompt_steps)(lambda: run(s_in, s_out))


def _rowwise(body, pairs, consts, outs, rows, name):
    t = pairs[0][0].shape[0]
    n_sample = math.prod(pairs[0][1].shape[:-1])
    rows = min(rows, t)
    sample_rows = min(SAMPLE_BLOCK_ROWS, n_sample)
    prompt_steps, sample_steps = t // rows, n_sample // sample_rows

    def spec(shape, is_prompt):
        n = rows if is_prompt else sample_rows
        lead = n if len(shape) == 2 else n // shape[1]
        block = (lead,) + tuple(shape[1:])
        if is_prompt:
            index = lambda i: (jnp.minimum(i, prompt_steps - 1),) + (0,) * (len(shape) - 1)
        else:
            index = lambda i: (jnp.maximum(i - prompt_steps, 0),) + (0,) * (len(shape) - 1)
        return pl.BlockSpec(block, index)

    in_specs = ([spec(p.shape, True) for p, _ in pairs] + [spec(s.shape, False) for _, s in pairs]
                + [pl.BlockSpec(c.shape, lambda i, nd=c.ndim: (0,) * nd) for c in consts])
    out_specs = [spec(p.shape, True) for p, _ in outs] + [spec(s.shape, False) for _, s in outs]
    res = pl.pallas_call(
        functools.partial(_rowwise_kernel, body=body, n_in=len(pairs), n_const=len(consts), n_out=len(outs),
                          prompt_steps=prompt_steps),
        grid=(prompt_steps + sample_steps,),
        in_specs=in_specs,
        out_specs=out_specs,
        out_shape=[p for p, _ in outs] + [s for _, s in outs],
        compiler_params=_cparams(1),
        name=name,
    )(*[p for p, _ in pairs], *[s for _, s in pairs], *consts)
    return res[:len(outs)], res[len(outs):]


def _rmsnorm_tail_body(x, w, w_tail):
    h = _rms(x, w).astype(BF16)
    return h, _dot_nt(h, w_tail)


def _post_mix_body(x, mix, w_post, w_pre):
    return (_rms(x + _rms(mix.astype(F32), w_post), w_pre),)


def _residual_body(x, mix, ffn, w_post_mix, w_post_ffn):
    return (x + _rms(mix.astype(F32), w_post_mix) + _rms(ffn.astype(F32), w_post_ffn),)


def _stack_rows(p_ref, s_ref, cols=slice(None)):
    return jnp.concatenate([p_ref[:, cols].astype(BF16), s_ref[:, cols].astype(BF16)], axis=0)


def _dot_nt(a, b):
    return lax.dot_general(a, b, (((1,), (1,)), ((), ())), preferred_element_type=F32)


def _proj_kernel(*refs, n_a, n_w, swiglu, w_transposed, row_blocks, n_mt, n_tiles, last_cols, side_chunks):
    a_refs = refs[:2 * n_a]
    w_refs = refs[2 * n_a:2 * n_a + n_w]
    rest = refs[2 * n_a + n_w:]
    if side_chunks:
        side_src, rest = rest[0], rest[1:]
    op_ref, os_ref = rest[:2]
    if side_chunks:
        side_dst, rest = rest[2], rest[:2] + rest[3:]
    scratch = rest[2:]
    stage_refs, wb_refs, sem = scratch[:n_w], scratch[n_w:2 * n_w], scratch[2 * n_w]
    mm = _dot_nt if w_transposed else _dot
    j, i = pl.program_id(0), pl.program_id(1)
    nj = pl.num_programs(0)

    if side_chunks:
        side_in, side_out, side_sem = scratch[2 * n_w + 1:2 * n_w + 1 + 3]
        rows_c = side_in.shape[1]
        g = j * n_mt + i
        last_step = n_tiles * n_mt - 1
        chunk_rows = lambda c: pl.ds(pl.multiple_of(c * rows_c, rows_c), rows_c)
        fetch = lambda c: pltpu.make_async_copy(side_src.at[chunk_rows(c)], side_in.at[c % 2], side_sem.at[0, c % 2])
        flush = lambda c: pltpu.make_async_copy(side_out.at[c % 2], side_dst.at[chunk_rows(c)], side_sem.at[1, c % 2])

        pl.when(g == 0)(lambda: fetch(0).start())
        pl.when(jnp.logical_and(g >= 1, g <= side_chunks))(lambda: flush(g - 1).wait())

        @pl.when(g < side_chunks)
        def _():
            fetch(g).wait()
            side_out[g % 2] = side_in[g % 2].astype(BF16)
            flush(g).start()

        pl.when(g + 1 < side_chunks)(lambda: fetch(g + 1).start())
        if side_chunks > last_step:
            pl.when(g == last_step)(lambda: flush(g).wait())

    def slice_copy(w, tile, s, slot, narrow):
        rows, cols = stage_refs[w].shape[1:]
        if w_transposed:
            src = w_refs[w].at[pl.ds(tile * (rows * n_mt) + s * rows, rows), pl.ds(row_blocks[w] * cols, cols)]
            dst = stage_refs[w].at[slot]
        else:
            width = last_cols if narrow else cols
            src = w_refs[w].at[pl.ds((row_blocks[w] * n_mt + s) * rows, rows), pl.ds(tile * cols, width)]
            dst = stage_refs[w].at[slot, :, pl.ds(0, width)]
        return pltpu.make_async_copy(src, dst, sem.at[w, slot])

    def with_tile(tile, fn):
        if last_cols == stage_refs[0].shape[2] or w_transposed:
            fn(False)
        elif isinstance(tile, int):
            fn(tile == n_tiles - 1)
        else:
            pl.when(tile == n_tiles - 1)(lambda: fn(True))
            pl.when(tile != n_tiles - 1)(lambda: fn(False))

    def cast_slice(w, half, s, slot):
        rows = stage_refs[w].shape[1]
        wb_refs[w][half, pl.ds(pl.multiple_of(s * rows, rows), rows), :] = stage_refs[w][slot].astype(BF16)

    @pl.when(jnp.logical_and(j == 0, i == 0))
    def _():
        if last_cols != stage_refs[0].shape[2]:
            for w in range(n_w):
                stage_refs[w][...] = jnp.zeros_like(stage_refs[w])
        for w in range(n_w):
            for s in range(n_mt):
                def fetch_now(narrow, w=w, s=s):
                    copy = slice_copy(w, 0, s, s % 2, narrow)
                    copy.start()
                    copy.wait()
                with_tile(0, fetch_now)
                cast_slice(w, 0, s, s % 2)
        if n_tiles > 1:
            for w in range(n_w):
                with_tile(1, lambda narrow, w=w: slice_copy(w, 1, 0, 0, narrow).start())

    @pl.when(j + 1 < nj)
    def _():
        slot = (j * n_mt + i) % 2
        for w in range(n_w):
            with_tile(j + 1, lambda narrow, w=w: slice_copy(w, j + 1, i, slot, narrow).wait())
            cast_slice(w, (j + 1) % 2, i, slot)
        last = i + 1 == n_mt
        next_tile = jnp.where(last, j + 2, j + 1)
        next_s = jnp.where(last, 0, i + 1)

        @pl.when(next_tile < nj)
        def _():
            for w in range(n_w):
                with_tile(next_tile, lambda narrow, w=w: slice_copy(w, next_tile, next_s, 1 - slot, narrow).start())

    half = j % 2
    bmp = op_ref.shape[0]

    def apply(width):
        cols = slice(None) if w_transposed else slice(0, width)
        rhs = [wb_ref[half, :, cols] for wb_ref in wb_refs]
        lhs = [_stack_rows(a_refs[2 * p], a_refs[2 * p + 1]) for p in range(n_a)]
        if swiglu:
            g = mm(lhs[0], rhs[0])
            o = g * jax.nn.sigmoid(g) * mm(lhs[0], rhs[1])
        else:
            o = mm(lhs[0], rhs[0])
            for l, r in zip(lhs[1:], rhs[1:]):
                o = o + mm(l, r)
        op_ref[:, :width] = o[:bmp].astype(op_ref.dtype)
        os_ref[:, :width] = o[bmp:].astype(os_ref.dtype)

    full_cols = op_ref.shape[1]
    if last_cols == full_cols:
        apply(full_cols)
    else:
        pl.when(j != n_tiles - 1)(lambda: apply(full_cols))
        pl.when(j == n_tiles - 1)(lambda: apply(last_cols))


def _proj(a_pairs, w_blocks, n, bn, out_dtype, n_mt, name, swiglu=False, w_transposed=False, side=None):
    tp, ts = a_pairs[0][0].shape[0], a_pairs[0][1].shape[0]
    bmp, bms = tp // n_mt, ts // n_mt
    in_specs = []
    for a_p, a_s in a_pairs:
        in_specs.append(pl.BlockSpec((bmp, a_p.shape[1]), lambda j, i: (i, 0)))
        in_specs.append(pl.BlockSpec((bms, a_s.shape[1]), lambda j, i: (i, 0)))
    in_specs += [pl.BlockSpec(memory_space=pl.ANY) for _ in w_blocks]
    n_tiles = pl.cdiv(n, bn)
    tile = lambda k: (bn, k) if w_transposed else (k, bn)
    stage = lambda k: (2, bn // n_mt, k) if w_transposed else (2, k // n_mt, bn)
    operands = [a for pair in a_pairs for a in pair] + [w for w, _, _ in w_blocks]
    out_specs = [pl.BlockSpec((bmp, bn), lambda j, i: (i, j)), pl.BlockSpec((bms, bn), lambda j, i: (i, j))]
    out_shape = [jax.ShapeDtypeStruct((tp, n), out_dtype), jax.ShapeDtypeStruct((ts, n), out_dtype)]
    scratch = ([pltpu.VMEM(stage(k), F32) for _, _, k in w_blocks]
               + [pltpu.VMEM((2,) + tile(k), BF16) for _, _, k in w_blocks]
               + [pltpu.SemaphoreType.DMA((len(w_blocks), 2))])
    side_chunks = 0
    if side is not None:
        rows_c = SIDE_CHUNK_ROWS
        side_chunks = side.shape[0] // rows_c
        assert side.shape[0] == side_chunks * rows_c and side_chunks <= n_tiles * n_mt
        operands.append(side)
        in_specs.append(pl.BlockSpec(memory_space=pl.ANY))
        out_specs.append(pl.BlockSpec(memory_space=pl.ANY))
        out_shape.append(jax.ShapeDtypeStruct(side.shape, BF16))
        scratch += [pltpu.VMEM((2, rows_c, side.shape[1]), F32), pltpu.VMEM((2, rows_c, side.shape[1]), BF16),
                    pltpu.SemaphoreType.DMA((2, 2))]
    return pl.pallas_call(
        functools.partial(_proj_kernel, n_a=len(a_pairs), n_w=len(w_blocks), swiglu=swiglu,
                          w_transposed=w_transposed, row_blocks=tuple(rb for _, rb, _ in w_blocks), n_mt=n_mt,
                          n_tiles=n_tiles, last_cols=n - (n_tiles - 1) * bn, side_chunks=side_chunks),
        grid=(n_tiles, n_mt),
        in_specs=in_specs,
        out_specs=out_specs,
        out_shape=out_shape,
        scratch_shapes=scratch,
        compiler_params=_cparams(2),
        name=name,
    )(*operands)


def _down_kernel(ap_ref, as_ref, b_ref, op_ref, os_ref, accp_ref, accs_ref, *, bn, k_total):
    k = pl.program_id(1)
    nk = pl.num_programs(1)
    bk = b_ref.shape[0]
    bmp = ap_ref.shape[0]
    last_rows = k_total - (k_total // bk) * bk

    @pl.when(k == 0)
    def _():
        accp_ref[...] = jnp.zeros_like(accp_ref)
        accs_ref[...] = jnp.zeros_like(accs_ref)

    def accumulate(kk):
        a = _stack_rows(ap_ref, as_ref, slice(0, kk))
        for n0 in range(0, accp_ref.shape[1], bn):
            d = _dot(a, b_ref[:kk, n0:n0 + bn])
            accp_ref[:, n0:n0 + bn] += d[:bmp]
            accs_ref[:, n0:n0 + bn] += d[bmp:]

    if last_rows == 0:
        accumulate(bk)
    else:
        pl.when(k < nk - 1)(lambda: accumulate(bk))
        pl.when(k == nk - 1)(lambda: accumulate(last_rows))

    @pl.when(k == nk - 1)
    def _():
        op_ref[...] = accp_ref[...].astype(op_ref.dtype)
        os_ref[...] = accs_ref[...].astype(os_ref.dtype)


def _down(a_p, a_s, b, bk, n_mt):
    (tp, k_total), ts = a_p.shape, a_s.shape[0]
    n = b.shape[1]
    bmp, bms = tp // n_mt, ts // n_mt
    return pl.pallas_call(
        functools.partial(_down_kernel, bn=min(n, 512), k_total=k_total),
        grid=(n_mt, pl.cdiv(k_total, bk)),
        in_specs=[pl.BlockSpec((bmp, bk), lambda i, k: (i, k)), pl.BlockSpec((bms, bk), lambda i, k: (i, k)),
                  pl.BlockSpec((bk, n), lambda i, k: (k, 0))],
        out_specs=[pl.BlockSpec((bmp, n), lambda i, k: (i, 0), pipeline_mode=pl.Buffered(1)),
                   pl.BlockSpec((bms, n), lambda i, k: (i, 0))],
        out_shape=[jax.ShapeDtypeStruct((tp, n), BF16), jax.ShapeDtypeStruct((ts, n), BF16)],
        scratch_shapes=[pltpu.VMEM((bmp, n), F32), pltpu.VMEM((bms, n), F32)],
        compiler_params=_cparams(2),
        name="ffn_down",
    )(a_p, a_s, b)


def _rel_bucket(d):
    n = jnp.maximum(d, 0)
    max_exact = REL_BUCKETS // 2
    nf = jnp.maximum(n, 1).astype(F32)
    large = max_exact + (jnp.log(nf / max_exact) / math.log(REL_MAX_DIST / max_exact)
                         * (REL_BUCKETS - max_exact)).astype(jnp.int32)
    large = jnp.minimum(large, REL_BUCKETS - 1)
    return jnp.where(n < max_exact, n, large)


def _bias_kernel(rel_ref, bucket_ref, allowed_ref, o_ref):
    h = pl.program_id(0)
    bucket = bucket_ref[...]
    acc = jnp.zeros(bucket.shape, F32)
    for n in range(REL_BUCKETS):
        acc = jnp.where(bucket == n, rel_ref[n, h], acc)
    o_ref[0] = jnp.where(allowed_ref[...] > 0, acc, NEG_INF)


def _bias_table(rel_bias, dist):
    tq, tk = dist.shape
    allowed = ((dist >= 0) & (dist <= WINDOW)).astype(jnp.int32)
    full = pl.BlockSpec((tq, tk), lambda h: (0, 0))
    return pl.pallas_call(
        _bias_kernel,
        grid=(ATTN_HEADS,),
        in_specs=[pl.BlockSpec(memory_space=pltpu.SMEM), full, full],
        out_specs=pl.BlockSpec((1, tq, tk), lambda h: (h, 0, 0)),
        out_shape=jax.ShapeDtypeStruct((ATTN_HEADS, tq, tk), F32),
        compiler_params=_cparams(1),
        name="rel_bias_table",
    )(rel_bias.astype(F32), _rel_bucket(dist), allowed)


def _softmax_pv(s, sink):
    m = jnp.maximum(jnp.max(s, axis=-1, keepdims=True), sink)
    p = jnp.exp(s - m)
    denom = jnp.sum(p, axis=-1, keepdims=True) + jnp.exp(sink - m)
    return p.astype(BF16), denom


def _attn_prompt_kernel(sink_ref, cur_ref, prev_ref, tables_ref, o_ref, *, nq):
    i = pl.program_id(0)
    c = WINDOW
    head_cols = lambda base, h: slice(base + h * HEAD_DIM, base + (h + 1) * HEAD_DIM)
    blk = lambda b: slice(b * c, (b + 1) * c)

    def keys(b, h, cur_base, prev_base):
        before = prev_ref[:, head_cols(prev_base, h)] if b == 0 else cur_ref[blk(b - 1), head_cols(cur_base, h)]
        return jnp.concatenate([before, cur_ref[blk(b), head_cols(cur_base, h)]], axis=0).astype(BF16)

    kk = {(b, h): keys(b, h, K_OFF, 0) for b in range(nq) for h in range(ATTN_KV_HEADS)}
    vv = {(b, h): keys(b, h, V_OFF, KV_WIDTH) for b in range(nq) for h in range(ATTN_KV_HEADS)}
    pairs = [(b, hd) for b in range(nq) for hd in range(ATTN_HEADS)]
    scale = HEAD_DIM ** -0.5
    table = lambda b: jnp.minimum(i, 1) if b == 0 else 1
    s = {p: _dot_nt((cur_ref[blk(p[0]), head_cols(Q_OFF, p[1])] * scale).astype(BF16), kk[p[0], p[1] // ATTN_GROUP])
         for p in pairs}
    pd = {p: _softmax_pv(s[p] + tables_ref[table(p[0]), p[1]], sink_ref[p[1]]) for p in pairs}
    o = {p: _dot(pd[p][0], vv[p[0], p[1] // ATTN_GROUP]) / pd[p][1] for p in pairs}
    for b in range(nq):
        o_ref[blk(b), :] = jnp.concatenate([o[b, hd] for hd in range(ATTN_HEADS)], axis=1).astype(o_ref.dtype)


def _attn_prompt(proj, bias, sinks, nq):
    t = proj.shape[0]
    c = WINDOW
    qkv = V_OFF + KV_WIDTH
    first = jnp.where(jnp.arange(2 * c) < c, NEG_INF, bias)
    tables = jnp.stack([first, bias])
    return pl.pallas_call(
        functools.partial(_attn_prompt_kernel, nq=nq),
        grid=(t // (c * nq),),
        in_specs=[pl.BlockSpec(memory_space=pltpu.SMEM),
                  pl.BlockSpec((c * nq, qkv), lambda i: (i, 0)),
                  pl.BlockSpec((c, 2 * KV_WIDTH), lambda i: (jnp.maximum(i * nq - 1, 0), K_OFF // (2 * KV_WIDTH))),
                  pl.BlockSpec(tables.shape, lambda i: (0, 0, 0, 0))],
        out_specs=pl.BlockSpec((c * nq, ATTN_WIDTH), lambda i: (i, 0)),
        out_shape=jax.ShapeDtypeStruct((t, ATTN_WIDTH), BF16),
        compiler_params=_cparams(1),
        name="attn_prompt",
    )(sinks.astype(F32), proj, proj, tables)


def _attn_sample_kernel(x_ref, ck_ref, cv_ref, bias_c_ref, bias_n_ref, sink_ref, o_ref, ok_ref, ov_ref,
                        q_s, kn_s, vn_s, *, t_new):
    gb, l, kvh, hd = ck_ref.shape
    kc = ck_ref[...].reshape(gb, l * kvh, hd)
    vc = cv_ref[...].reshape(gb, l * kvh, hd)
    for b in range(gb):
        rows = slice(b * t_new, (b + 1) * t_new)
        for h in range(kvh):
            kn_s[b, h * t_new:(h + 1) * t_new, :] = x_ref[rows, K_OFF + h * hd:K_OFF + (h + 1) * hd]
            vn_s[b, h * t_new:(h + 1) * t_new, :] = x_ref[rows, V_OFF + h * hd:V_OFF + (h + 1) * hd]
        for head in range(ATTN_HEADS):
            q_s[b, head * t_new:(head + 1) * t_new, :] = x_ref[rows, head * hd:(head + 1) * hd]

    q = q_s[...].astype(BF16)
    scale = hd ** -0.5
    s_c = jnp.einsum("bqd,bkd->bqk", q, kc.astype(BF16), preferred_element_type=F32) * scale + bias_c_ref[...]
    s_n = jnp.einsum("bqd,bkd->bqk", q, kn_s[...].astype(BF16), preferred_element_type=F32) * scale + bias_n_ref[...]
    sink = sink_ref[...]
    m = jnp.maximum(jnp.maximum(jnp.max(s_c, axis=-1, keepdims=True), jnp.max(s_n, axis=-1, keepdims=True)), sink)
    p_c = jnp.exp(s_c - m)
    p_n = jnp.exp(s_n - m)
    denom = jnp.sum(p_c, axis=-1, keepdims=True) + jnp.sum(p_n, axis=-1, keepdims=True) + jnp.exp(sink - m)
    o = (jnp.einsum("bqk,bkd->bqd", p_c.astype(BF16), vc.astype(BF16), preferred_element_type=F32)
         + jnp.einsum("bqk,bkd->bqd", p_n.astype(BF16), vn_s[...].astype(BF16), preferred_element_type=F32)) / denom
    for b in range(gb):
        for head in range(ATTN_HEADS):
            o_ref[b * t_new:(b + 1) * t_new, head * hd:(head + 1) * hd] = o[b, head * t_new:(head + 1) * t_new, :]

    shift = t_new * kvh
    for c_val, n_s, out_ref in ((kc, kn_s, ok_ref), (vc, vn_s, ov_ref)):
        new_rows = [n_s[:, h * t_new + t:h * t_new + t + 1, :] for t in range(t_new) for h in range(kvh)]
        win = jnp.concatenate([c_val[:, shift:, :]] + new_rows, axis=1)
        out_ref[...] = win.reshape(gb, l, kvh, hd)


def _attn_sample(proj, cache_k, cache_v, bias_c, bias_n, sink_rows, t_new, gb):
    b, l = cache_k.shape[:2]
    rows = ATTN_HEADS * t_new
    qkv = V_OFF + KV_WIDTH
    cache_spec = pl.BlockSpec((gb, l, ATTN_KV_HEADS, HEAD_DIM), lambda i: (i, 0, 0, 0))
    full = lambda a: pl.BlockSpec(a.shape, lambda i: (0,) * a.ndim)
    return pl.pallas_call(
        functools.partial(_attn_sample_kernel, t_new=t_new),
        grid=(b // gb,),
        in_specs=[pl.BlockSpec((gb * t_new, qkv), lambda i: (i, 0)), cache_spec, cache_spec,
                  full(bias_c), full(bias_n), full(sink_rows)],
        out_specs=[pl.BlockSpec((gb * t_new, ATTN_WIDTH), lambda i: (i, 0)), cache_spec, cache_spec],
        out_shape=[jax.ShapeDtypeStruct((b * t_new, ATTN_WIDTH), F32),
                   jax.ShapeDtypeStruct(cache_k.shape, F32),
                   jax.ShapeDtypeStruct(cache_v.shape, F32)],
        scratch_shapes=[pltpu.VMEM((gb, rows, HEAD_DIM), F32),
                        pltpu.VMEM((gb, ATTN_KV_HEADS * t_new, HEAD_DIM), F32),
                        pltpu.VMEM((gb, ATTN_KV_HEADS * t_new, HEAD_DIM), F32)],
        compiler_params=_cparams(1),
        name="attn_sample",
    )(proj, cache_k, cache_v, bias_c, bias_n, sink_rows)


def _split3(x):
    hi = x.astype(BF16)
    r = x - hi.astype(F32)
    mid = r.astype(BF16)
    lo = (r - mid.astype(F32)).astype(BF16)
    return hi, mid, lo


def _select_dot(sel, x):
    hi, mid, lo = _split3(x)
    return _dot(sel, hi) + _dot(sel, mid) + _dot(sel, lo)


def _dot_select(x, sel):
    hi, mid, lo = _split3(x)
    return _dot(hi, sel) + _dot(mid, sel) + _dot(lo, sel)


def _softplus(x):
    return jnp.maximum(x, 0.0) + jnp.log1p(jnp.exp(-jnp.abs(x)))


def _conv_silu(pad_ref, w, r0, n):
    top = SUBLANES + r0
    y = pad_ref[top:top + n, :] * w[GDN_CONV - 1:GDN_CONV]
    for s in range(1, GDN_CONV):
        y = y + pad_ref[top - s:top - s + n, :] * w[GDN_CONV - 1 - s:GDN_CONV - s]
    return y * jax.nn.sigmoid(y)


def _l2norm_heads(x, scale):
    outs = []
    for hh in range(GDN_HB):
        xh = x[:, hh * HEAD_DIM:(hh + 1) * HEAD_DIM]
        outs.append(xh * (lax.rsqrt(jnp.sum(xh * xh, axis=-1, keepdims=True) + NORM_EPS) * scale))
    return jnp.concatenate(outs, axis=1)


GDN_N_PADS = 4
STAGE_ROWS = 32
PROBE_BOUND = 1e30
GDN_N_STAGED = 10


def _gdn_kernel(*refs, rows, seq, carry, live_from):
    kw = dict(rows=rows, seq=seq, carry=carry, live_from=live_from)
    if not carry:
        io, pads, staged = refs[:-GDN_N_PADS - GDN_N_STAGED], refs[-GDN_N_PADS - GDN_N_STAGED:-GDN_N_STAGED], \
            refs[-GDN_N_STAGED:]
        _gdn_body(io, pads, staged, staged, **kw)
        return
    n_scratch = GDN_N_PADS + 2 * GDN_N_STAGED
    io, pads = refs[:-n_scratch], refs[-n_scratch:-2 * GDN_N_STAGED]
    set_a, set_b = refs[-2 * GDN_N_STAGED:-GDN_N_STAGED], refs[-GDN_N_STAGED:]
    s_ref = io[-1]
    step = pl.program_id(1)

    @pl.when(step == 0)
    def _():
        s_ref[...] = jnp.zeros_like(s_ref)
        for ref in set_b:
            ref[...] = jnp.zeros_like(ref)

    pl.when(step % 2 == 0)(lambda: _gdn_body(io, pads, set_a, set_b, **kw))
    pl.when(step % 2 == 1)(lambda: _gdn_body(io, pads, set_b, set_a, **kw))


def _gdn_body(refs, pads, staged_w, staged_r, *, rows, seq, carry, live_from):
    c = GDN_CHUNK
    nc = rows // c
    nseq = c // seq
    n_live = seq - live_from
    lw = GDN_HB * HEAD_DIM
    if carry:
        (xq_ref, xk_ref, xv_ref, tq_ref, tk_ref, tv_ref, z_ref, wq_ref, wk_ref, wv_ref, gn_ref,
         ab_ref, pc_ref, og_ref, s_ref) = refs
        s0_ref = None
    else:
        (xq_ref, xk_ref, xv_ref, cq_ref, ck_ref, cv_ref, z_in, wq_ref, wk_ref, wv_ref, gn_ref,
         ab_in, pc_ref, s0_ref, og_ref, s_ref, z_ref, ab_ref) = refs
    padq_s, padk_s, padv_s, eb_s = pads
    kn_w, kb_w, qn_w, vb_w, kbe_w, qe_w, kd_w, egc_w, egl_w, gcr_w = staged_w
    kn_s, kb_s, qn_s, vb_s, kbe_s, qe_s, kd_s, egc_s, egl_s, gcr_s = staged_r

    group = pl.program_id(0)
    step = pl.program_id(1)
    row = lax.broadcasted_iota(jnp.int32, (rows, 1), 0)
    live = (row % seq) >= live_from
    keep_live = (lambda x: jnp.where(live, x, 0.0)) if live_from else (lambda x: x)

    if carry:
        for pad_ref, x_ref, t_ref in ((padq_s, xq_ref, tq_ref), (padk_s, xk_ref, tk_ref), (padv_s, xv_ref, tv_ref)):
            pad_ref[0:SUBLANES, :] = jnp.where(step == 0, 0.0, t_ref[...])
            pad_ref[SUBLANES:, :] = x_ref[...]
    else:
        def gather(dst_ref, x_ref, hist_ref, top):
            dst_ref[...] = jnp.zeros_like(dst_ref)
            for b in range(rows // seq):
                r0 = top + b * seq + live_from
                if hist_ref is not None:
                    dst_ref[r0 - (GDN_CONV - 1):r0, :] = hist_ref[:, b, :]
                dst_ref[r0:r0 + n_live, :] = x_ref[b * n_live:(b + 1) * n_live, :]

        gather(padq_s, xq_ref, cq_ref, SUBLANES)
        gather(padk_s, xk_ref, ck_ref, SUBLANES)
        gather(padv_s, xv_ref, cv_ref, SUBLANES)
        gather(z_ref, z_in, None, 0)
        gather(ab_ref, ab_in, None, 0)

    def stage_decays():
        ab = ab_ref[...]
        g_col = keep_live(-jnp.exp(pc_ref[0:1, :]) * _softplus(ab + pc_ref[1:2, :]))
        beta_col = keep_live(jax.nn.sigmoid(ab))
        ri = lax.broadcasted_iota(jnp.int32, (rows, rows), 0)
        ci = lax.broadcasted_iota(jnp.int32, (rows, rows), 1)
        same = (ri // seq) == (ci // seq)
        cum_sel = jnp.where(jnp.logical_and(same, ci <= ri), 1.0, 0.0).astype(BF16)
        tot_sel = jnp.where(same, 1.0, 0.0).astype(BF16)
        gc_col = _select_dot(cum_sel, g_col)
        gl_col = _select_dot(tot_sel, g_col)
        first = group * GDN_HB
        li = lax.broadcasted_iota(jnp.int32, (LANES, lw), 0)
        ni = lax.broadcasted_iota(jnp.int32, (LANES, lw), 1)
        spread_g = jnp.where(li == ni // HEAD_DIM + first, 1.0, 0.0).astype(BF16)
        spread_b = jnp.where(li == ni // HEAD_DIM + first + GDN_HEADS, 1.0, 0.0).astype(BF16)
        egc_w[...] = _dot_select(gc_col, spread_g)
        egl_w[...] = _dot_select(gl_col, spread_g)
        eb_s[...] = _dot_select(beta_col, spread_b)
        pi = lax.broadcasted_iota(jnp.int32, (LANES, LANES), 0)
        pj = lax.broadcasted_iota(jnp.int32, (LANES, LANES), 1)
        pick = jnp.where(jnp.logical_and(pi == pj + first, pj < GDN_HB), 1.0, 0.0).astype(BF16)
        gc_heads = _dot_select(gc_col, pick)
        if rows % LANES:
            gc_heads = jnp.concatenate([gc_heads, jnp.zeros((LANES - rows % LANES, LANES), F32)], axis=0)
        gcr_w[...] = gc_heads.T[:SUBLANES, :rows]

    def stage_rows(r0, n, after=None):
        sl = slice(r0, r0 + n)
        keep = (lambda x: jnp.where(live[sl], x, 0.0)) if live_from else (lambda x: x)
        gate = 1.0 if after is None else jnp.where(after > PROBE_BOUND, 0.0, 1.0)
        conv = lambda pad_ref, w_ref: keep(_conv_silu(pad_ref, w_ref[...] * gate, r0, n))
        qn = _l2norm_heads(conv(padq_s, wq_ref), HEAD_DIM ** -0.5)
        kn = _l2norm_heads(conv(padk_s, wk_ref), 1.0)
        v = conv(padv_s, wv_ref)
        egc, egl, eb = egc_w[sl, :], egl_w[sl, :], eb_s[sl, :]
        decay_in = jnp.exp(egc)
        kb = kn * eb
        kn_w[sl, :] = kn
        kb_w[sl, :] = kb
        qn_w[sl, :] = qn
        vb_w[sl, :] = v * eb
        kbe_w[sl, :] = kb * decay_in
        qe_w[sl, :] = qn * decay_in
        kd_w[sl, :] = kn * jnp.exp(egl - egc)

    n_slices = rows // STAGE_ROWS if carry else 1
    pending = [functools.partial(stage_rows, r * (rows // n_slices), rows // n_slices) for r in range(n_slices)]
    if not carry:
        stage_decays()
        pending.pop(0)()

    cr = lax.broadcasted_iota(jnp.int32, (c, c), 0)
    cc = lax.broadcasted_iota(jnp.int32, (c, c), 1)
    same_c = (cr // seq) == (cc // seq)
    causal_bias = jnp.where(jnp.logical_and(same_c, cr >= cc), 0.0, NEG_INF)
    strict = jnp.where(jnp.logical_and(same_c, cr > cc), 1.0, 0.0)

    eye = jnp.where(cr == cc, 1.0, 0.0)
    seq_of_row = lax.broadcasted_iota(jnp.int32, (c, HEAD_DIM), 0) // seq

    def lower_left(s):
        return jnp.logical_and(jnp.logical_and(cr // (2 * s) == cc // (2 * s), (cr // s) % 2 == 1),
                               (cc // s) % 2 == 0)

    merge_sizes = [2 ** e for e in range(1, max(1, math.ceil(math.log2(n_live))))]
    pair_mask = jnp.where(lower_left(1), 1.0, 0.0)
    merge_masks = [jnp.where(lower_left(s), 1.0, 0.0) for s in merge_sizes]

    pairs = [(ch, hh) for ch in range(nc) for hh in range(GDN_HB)]
    rows_of = lambda ch: slice(ch * c, (ch + 1) * c)
    lanes_of = lambda hh: slice(hh * HEAD_DIM, (hh + 1) * HEAD_DIM)
    uw_all, qk_all, a_mat, t_mat, t_a = {}, {}, {}, {}, {}
    for ch, hh in pairs:
        rs, hs = rows_of(ch), lanes_of(hh)
        gcol = egc_s[rs, hh * HEAD_DIM:hh * HEAD_DIM + c]
        grow = gcr_s[hh:hh + 1, rs]
        decay = jnp.exp(gcol - grow + causal_bias)
        kq = jnp.concatenate([kb_s[rs, hs], qn_s[rs, hs]], axis=0).astype(BF16)
        kk = _dot_nt(kq, kn_s[rs, hs].astype(BF16)) * jnp.concatenate([decay, decay], axis=0)
        a_mat[ch, hh] = kk[:c] * strict
        qk_all[ch, hh] = kk[c:].astype(BF16)
        t_mat[ch, hh] = eye - a_mat[ch, hh] * pair_mask
    probes = [a_mat[pairs[-1]][0:1, 0:1]]
    if carry:
        stage_decays()
    for mask in merge_masks:
        for p in pairs:
            t_a[p] = _dot(t_mat[p].astype(BF16), (a_mat[p] * mask).astype(BF16))
        for p in pairs:
            t_mat[p] = t_mat[p] - _dot(t_a[p].astype(BF16), t_mat[p].astype(BF16))
        probes.append(t_mat[pairs[-1]][0:1, 0:1])
    for ch, hh in pairs:
        rs, hs = rows_of(ch), lanes_of(hh)
        uw_all[ch, hh] = _dot(t_mat[ch, hh].astype(BF16),
                              jnp.concatenate([vb_s[rs, hs], kbe_s[rs, hs]], axis=1).astype(BF16))

    if carry:
        state = [s_ref[hh] for hh in range(GDN_HB)]
    new_states = {}
    heads = range(GDN_HB)
    tn_dot = lambda a, b: lax.dot_general(a, b, (((0,), (0,)), ((), ())), preferred_element_type=F32)
    for ch in range(nc):
        rs = rows_of(ch)
        w_s, q_s, v_new_b, o = {}, {}, {}, {}
        for hh in heads:
            w = uw_all[ch, hh][:, HEAD_DIM:]
            q_e = qe_s[rs, lanes_of(hh)]
            if carry:
                wq = _dot(jnp.concatenate([w, q_e], axis=0).astype(BF16), state[hh].astype(BF16))
                w_s[hh], q_s[hh] = wq[:c], wq[c:]
            else:
                parts = []
                for b in range(nseq):
                    bs = slice(b * seq, (b + 1) * seq)
                    parts.append(_dot(jnp.concatenate([w[bs], q_e[bs]], axis=0).astype(BF16),
                                      s0_ref[ch * nseq + b, hh].astype(BF16)))
                w_s[hh] = jnp.concatenate([p[:seq] for p in parts], axis=0)
                q_s[hh] = jnp.concatenate([p[seq:] for p in parts], axis=0)
        for hh in heads:
            v_new_b[hh] = (uw_all[ch, hh][:, :HEAD_DIM] - w_s[hh]).astype(BF16)
            o[hh] = q_s[hh] + _dot(qk_all[ch, hh], v_new_b[hh])
        for hh in heads:
            hs = lanes_of(hh)
            k_d = kd_s[rs, hs]
            if carry:
                state[hh] = (state[hh] * jnp.exp(egl_s[ch * c:ch * c + 1, hs])
                             + tn_dot(k_d.astype(BF16), v_new_b[hh]))
            else:
                for b in range(nseq):
                    k_db = jnp.where(seq_of_row == b, k_d, 0.0).astype(BF16)
                    r1 = ch * c + b * seq
                    new_states[ch * nseq + b, hh] = (
                        s0_ref[ch * nseq + b, hh] * jnp.exp(egl_s[r1:r1 + 1, hs]) + tn_dot(k_db, v_new_b[hh]))
        if carry:
            probes.append(_dot(qk_all[ch, 0], jnp.ones((c, HEAD_DIM), BF16))[0:1, 0:1])
        outs = []
        for hh in heads:
            z = z_ref[rs, lanes_of(hh)]
            outs.append(_rms(o[hh], gn_ref[...]) * (z * jax.nn.sigmoid(z)))
        og = jnp.concatenate(outs, axis=1)
        if carry:
            og_ref[rs, :] = og.astype(og_ref.dtype)
        else:
            for b in range(nseq):
                r0 = b * seq + live_from
                og_ref[(ch * nseq + b) * n_live:(ch * nseq + b + 1) * n_live, :] = (
                    og[r0:r0 + n_live].astype(og_ref.dtype))
    for n, stage in enumerate(pending):
        stage(after=probes[min(n, len(probes) - 1)])
    if carry:
        s_ref[...] = jnp.stack(state, axis=0)
    else:
        for (b, hh), s_new in new_states.items():
            s_ref[b, hh] = s_new


def _gdn_params(a_log, dt_bias):
    rows = jnp.stack([a_log.astype(F32), dt_bias.astype(F32)], axis=0)
    return jnp.pad(rows, ((0, SUBLANES - 2), (0, LANES - GDN_HEADS)))


def _gdn_scratch(rows, n_sets):
    lw = GDN_HB * HEAD_DIM
    staged = [(rows, lw)] * (GDN_N_STAGED - 1) + [(SUBLANES, rows)]
    return ([pltpu.VMEM((SUBLANES + rows, lw), F32) for _ in range(GDN_N_PADS - 1)] + [pltpu.VMEM((rows, lw), F32)]
            + [pltpu.VMEM(shape, F32) for _ in range(n_sets) for shape in staged])


def _gdn_prompt(proj, ab, conv_w, gnorm_w, params, rows):
    t = proj.shape[0]
    nb = t // rows
    lw = GDN_HB * HEAD_DIM
    head = lambda i: jnp.minimum(i, nb - 1)
    done = lambda i: jnp.maximum(i - 1, 0)
    x_spec = lambda off: pl.BlockSpec((rows, lw), lambda g, i: (head(i), off // lw + g))
    tail_spec = lambda off: pl.BlockSpec(
        (SUBLANES, lw), lambda g, i: (jnp.maximum(head(i) * (rows // SUBLANES) - 1, 0), off // lw + g))
    w_spec = lambda off: pl.BlockSpec((GDN_CONV, lw), lambda g, i: (0, (off - GQ_OFF) // lw + g))
    return pl.pallas_call(
        functools.partial(_gdn_kernel, rows=rows, seq=GDN_CHUNK, carry=True, live_from=0),
        grid=(GDN_NG, nb + 1),
        in_specs=[x_spec(GQ_OFF), x_spec(GK_OFF), x_spec(GV_OFF),
                  tail_spec(GQ_OFF), tail_spec(GK_OFF), tail_spec(GV_OFF),
                  pl.BlockSpec((rows, lw), lambda g, i: (done(i), Z_OFF // lw + g)),
                  w_spec(GQ_OFF), w_spec(GK_OFF), w_spec(GV_OFF),
                  pl.BlockSpec((1, HEAD_DIM), lambda g, i: (0, 0)),
                  pl.BlockSpec((rows, LANES), lambda g, i: (head(i), 0)),
                  pl.BlockSpec((SUBLANES, LANES), lambda g, i: (0, 0))],
        out_specs=[pl.BlockSpec((rows, lw), lambda g, i: (done(i), g)),
                   pl.BlockSpec((GDN_HB, HEAD_DIM, HEAD_DIM), lambda g, i: (g, 0, 0))],
        out_shape=[jax.ShapeDtypeStruct((t, GDN_WIDTH), BF16),
                   jax.ShapeDtypeStruct((GDN_HEADS, HEAD_DIM, HEAD_DIM), F32)],
        scratch_shapes=_gdn_scratch(rows, 2),
        compiler_params=_cparams(2),
        name="gdn_prompt",
    )(proj, proj, proj, proj, proj, proj, proj, conv_w, conv_w, conv_w,
      gnorm_w.reshape(1, HEAD_DIM).astype(F32), ab, params)


def _gdn_sample(proj, ab, conv_state, state, conv_w, gnorm_w, params, n_live, rows):
    lw = GDN_HB * HEAD_DIM
    nseq = rows // SAMPLE_PAD
    n_in = nseq * n_live
    x_spec = lambda off: pl.BlockSpec((n_in, lw), lambda g, i: (i, off // lw + g))
    hist_spec = lambda off: pl.BlockSpec((GDN_CONV - 1, nseq, lw), lambda g, i: (0, i, (off - GQ_OFF) // lw + g))
    w_spec = lambda off: pl.BlockSpec((GDN_CONV, lw), lambda g, i: (0, (off - GQ_OFF) // lw + g))
    s_spec = pl.BlockSpec((nseq, GDN_HB, HEAD_DIM, HEAD_DIM), lambda g, i: (i, g, 0, 0))
    return pl.pallas_call(
        functools.partial(_gdn_kernel, rows=rows, seq=SAMPLE_PAD, carry=False, live_from=SAMPLE_PAD - n_live),
        grid=(GDN_NG, proj.shape[0] // n_in),
        in_specs=[x_spec(GQ_OFF), x_spec(GK_OFF), x_spec(GV_OFF),
                  hist_spec(GQ_OFF), hist_spec(GK_OFF), hist_spec(GV_OFF),
                  x_spec(Z_OFF), w_spec(GQ_OFF), w_spec(GK_OFF), w_spec(GV_OFF),
                  pl.BlockSpec((1, HEAD_DIM), lambda g, i: (0, 0)),
                  pl.BlockSpec((n_in, LANES), lambda g, i: (i, 0)),
                  pl.BlockSpec((SUBLANES, LANES), lambda g, i: (0, 0)),
                  s_spec],
        out_specs=[pl.BlockSpec((n_in, lw), lambda g, i: (i, g)), s_spec],
        out_shape=[jax.ShapeDtypeStruct((proj.shape[0], GDN_WIDTH), F32),
                   jax.ShapeDtypeStruct(state.shape, F32)],
        scratch_shapes=[pltpu.VMEM((rows, lw), F32), pltpu.VMEM((rows, LANES), F32)] + _gdn_scratch(rows, 1),
        compiler_params=_cparams(2),
        name="gdn_sample",
    )(proj, proj, proj, conv_state, conv_state, conv_state, proj, conv_w, conv_w, conv_w,
      gnorm_w.reshape(1, HEAD_DIM).astype(F32), ab, params, state)


def _layer(x_prompt, x_sample, win_k, win_v, conv_state, gdn_state, rel_bias,
           w_in, sinks, conv_w, a_log, dt_bias, gnorm_w, w_out,
           n_pre_mix, n_post_mix, n_pre_ffn, n_post_ffn, w_gate, w_up, w_down, tiles):
    _, t, d = x_prompt.shape
    nb, ts, _ = x_sample.shape
    ms = nb * ts
    l = win_k.shape[1]
    ff = w_gate.shape[1]
    n_mt = tiles["n_mt"]
    xp = x_prompt.reshape(t, d)
    sds = jax.ShapeDtypeStruct
    norm_w = lambda w: w.reshape(1, d).astype(F32)

    w_in_t = w_in.T
    w_tail = jnp.pad(w_in_t[PROJ_MAIN:], ((0, LANES - AB_WIDTH), (0, 0))).astype(BF16)
    (hp, ab_p), (hs, ab_s) = _rowwise(
        _rmsnorm_tail_body, [(xp, x_sample)], [norm_w(n_pre_mix), w_tail],
        [(sds((t, d), BF16), sds((ms, d), BF16)), (sds((t, LANES), F32), sds((ms, LANES), F32))],
        tiles["norm_rows"], "rmsnorm")
    proj_p, proj_s = _proj([(hp, hs)], [(w_in_t, 0, d)], PROJ_MAIN, tiles["in_bn"], F32, n_mt, "in_proj",
                           w_transposed=True)

    qi = jnp.arange(WINDOW)[:, None]
    kj = jnp.arange(2 * WINDOW)[None, :]
    dist_p = WINDOW + qi - kj
    dist_s = l + jnp.arange(SUBLANES)[:, None] - kj
    dist_s = jnp.where(kj < l + ts, dist_s, -1)
    bias_all = _bias_table(rel_bias, jnp.concatenate([dist_p, dist_s], axis=0))
    attn_p = _attn_prompt(proj_p, bias_all[:, :WINDOW], sinks, tiles["attn_nq"])

    b_s = bias_all[:, WINDOW:WINDOW + ts]
    own_kv = (jnp.arange(ATTN_HEADS)[:, None] // ATTN_GROUP) == jnp.arange(ATTN_KV_HEADS)[None, :]
    bias_c = jnp.where(own_kv[:, None, None, :], b_s[:, :, :l, None], NEG_INF)
    bias_c = bias_c.reshape(ATTN_HEADS * ts, l * ATTN_KV_HEADS)
    bias_n = jnp.where(own_kv[:, None, :, None], b_s[:, :, None, l:l + ts], NEG_INF)
    bias_n = bias_n.reshape(ATTN_HEADS * ts, ATTN_KV_HEADS * ts)
    sink_rows = jnp.repeat(sinks.astype(F32), ts).reshape(ATTN_HEADS * ts, 1)
    attn_s, new_k, new_v = _attn_sample(proj_s, win_k, win_v, bias_c, bias_n, sink_rows, ts, tiles["attn_gb"])

    params = _gdn_params(a_log, dt_bias)
    og_p, s_p = _gdn_prompt(proj_p, ab_p, conv_w, gnorm_w, params, tiles["gdn_rows"])
    og_s, s_s = _gdn_sample(proj_s, ab_s, jnp.swapaxes(conv_state, 0, 1).astype(F32), gdn_state, conv_w, gnorm_w, params, ts,
                            tiles["gdn_sample_rows"])

    mix_p, mix_s = _proj([(attn_p, attn_s), (og_p, og_s)], [(w_out, 0, ATTN_WIDTH), (w_out, 1, GDN_WIDTH)],
                         d, tiles["out_bn"], BF16, n_mt, "out_proj")
    (h2_p,), (h2_s,) = _rowwise(
        _post_mix_body, [(xp, x_sample), (mix_p, mix_s)], [norm_w(n_post_mix), norm_w(n_pre_ffn)],
        [(sds((t, d), BF16), sds((ms, d), BF16))], tiles["post_rows"], "post_mix")
    act_p, act_s, w_down_b = _proj([(h2_p, h2_s)], [(w_gate, 0, d), (w_up, 0, d)], ff, tiles["ff_bn"], BF16, n_mt,
                                   "ffn_gate_up", swiglu=True, side=w_down)
    ffn_p, ffn_s = _down(act_p, act_s, w_down_b, tiles["ff_bk"], n_mt)
    (y_p,), (y_s,) = _rowwise(
        _residual_body, [(xp, x_sample), (mix_p, mix_s), (ffn_p, ffn_s)], [norm_w(n_post_mix), norm_w(n_post_ffn)],
        [(sds((t, d), F32), sds(x_sample.shape, F32))], tiles["post_rows"], "residual_norm")

    lw = min(WINDOW, t)
    prompt_k = proj_p[t - lw:, K_OFF:K_OFF + KV_WIDTH].reshape(1, lw, ATTN_KV_HEADS, HEAD_DIM)
    prompt_v = proj_p[t - lw:, V_OFF:V_OFF + KV_WIDTH].reshape(1, lw, ATTN_KV_HEADS, HEAD_DIM)
    prompt_conv = proj_p[t - (GDN_CONV - 1):, GQ_OFF:GQ_OFF + GDN_CONV_CH][None]
    sample_conv = proj_s[:, GQ_OFF:GQ_OFF + GDN_CONV_CH].reshape(nb, ts, GDN_CONV_CH)[:, ts - (GDN_CONV - 1):]
    return (y_p.reshape(1, t, d), y_s,
            prompt_k, prompt_v, prompt_conv, s_p[None],
            new_k.reshape(nb, l, ATTN_KV_HEADS, HEAD_DIM), new_v.reshape(nb, l, ATTN_KV_HEADS, HEAD_DIM),
            sample_conv, s_s)


def _tiles(t, ms):
    n_mt = 8 if (t % (8 * 16) == 0 and ms % (8 * 16) == 0) else 1
    return dict(n_mt=n_mt, in_bn=1024, out_bn=1024, ff_bn=512, ff_bk=1024, norm_rows=512, post_rows=256,
                gdn_rows=256, gdn_sample_rows=2 * GDN_CHUNK, attn_gb=8, attn_nq=2)


def kernel(x_prompt, x_sample, cache_win_k, cache_win_v, state_conv, state_gdn, rel_bias, w_in, attn_sinks,
           gdn_conv_w, gdn_a_log, gdn_dt_bias, gdn_norm_w, w_out, norm_pre_mix, norm_post_mix, norm_pre_ffn,
           norm_post_ffn, w_gate, w_up, w_down):
    depth = w_in.shape[0]
    assert depth == 1 and x_prompt.shape[0] == 1
    tiles = _tiles(x_prompt.shape[1], x_sample.shape[0] * x_sample.shape[1])
    outs = _layer(x_prompt, x_sample, cache_win_k[0], cache_win_v[0], state_conv[0], state_gdn[0], rel_bias,
                  w_in[0], attn_sinks[0], gdn_conv_w[0], gdn_a_log[0], gdn_dt_bias[0], gdn_norm_w[0], w_out[0],
                  norm_pre_mix[0], norm_post_mix[0], norm_pre_ffn[0], norm_post_ffn[0],
                  w_gate[0], w_up[0], w_down[0], tiles)
    yp, ys, pk, pv, pc, ps, sk, sv, sc, ss = outs
    return (yp, ys, pk[None], pv[None], pc[None], ps[None], sk[None], sv[None], sc[None], ss[None])
```

```python
import functools
import math

import jax
import jax.numpy as jnp
from jax import lax
from jax.experimental import pallas as pl
from jax.experimental.pallas import tpu as pltpu

F32 = jnp.float32
BF16 = jnp.bfloat16

HEAD_DIM = 128
ATTN_HEADS = 16
ATTN_KV_HEADS = 4
ATTN_GROUP = ATTN_HEADS // ATTN_KV_HEADS
ATTN_WIDTH = ATTN_HEADS * HEAD_DIM
KV_WIDTH = ATTN_KV_HEADS * HEAD_DIM
WINDOW = 128
REL_BUCKETS = 32
REL_MAX_DIST = 128
GDN_HEADS = 16
GDN_WIDTH = GDN_HEADS * HEAD_DIM
GDN_CONV = 4
GDN_CONV_CH = 3 * GDN_WIDTH
GDN_CHUNK = 64
NORM_EPS = 1e-6
NEG_INF = -1e30

Q_OFF = 0
K_OFF = ATTN_WIDTH
V_OFF = K_OFF + KV_WIDTH
GQ_OFF = V_OFF + KV_WIDTH
GK_OFF = GQ_OFF + GDN_WIDTH
GV_OFF = GK_OFF + GDN_WIDTH
Z_OFF = GV_OFF + GDN_WIDTH
PROJ_MAIN = Z_OFF + GDN_WIDTH
AB_WIDTH = 2 * GDN_HEADS

LANES = 128
SUBLANES = 8
GDN_HB = 8
GDN_NG = GDN_HEADS // GDN_HB
SAMPLE_PAD = 8
SAMPLE_BLOCK_ROWS = 128
SIDE_CHUNK_ROWS = 64
VMEM_LIMIT = 56 * 1024 * 1024


def _cparams(n_axes, vmem=VMEM_LIMIT):
    return pltpu.CompilerParams(dimension_semantics=("arbitrary",) * n_axes, vmem_limit_bytes=vmem)


def _dot(a, b):
    return jnp.dot(a, b, preferred_element_type=F32)


def _rms(x, w):
    return x * lax.rsqrt(jnp.mean(x * x, axis=-1, keepdims=True) + NORM_EPS) * w


def _rowwise_kernel(*refs, body, n_in, n_const, n_out, prompt_steps):
    p_in, s_in = refs[:n_in], refs[n_in:2 * n_in]
    consts = refs[2 * n_in:2 * n_in + n_const]
    p_out = refs[2 * n_in + n_const:2 * n_in + n_const + n_out]
    s_out = refs[2 * n_in + n_const + n_out:]

    def run(ins, outs):
        rows2d = [r[...].reshape(-1, r.shape[-1]) for r in ins]
        for o_ref, v in zip(outs, body(*rows2d, *[c[...] for c in consts])):
            o_ref[...] = v.reshape(o_ref.shape).astype(o_ref.dtype)

    i = pl.program_id(0)
    pl.when(i < prompt_steps)(lambda: run(p_in, p_out))
    pl.when(i >= prompt_steps)(lambda: run(s_in, s_out))


def _rowwise(body, pairs, consts, outs, rows, name):
    t = pairs[0][0].shape[0]
    n_sample = math.prod(pairs[0][1].shape[:-1])
    rows = min(rows, t)
    sample_rows = min(SAMPLE_BLOCK_ROWS, n_sample)
    prompt_steps, sample_steps = t // rows, n_sample // sample_rows

    def spec(shape, is_prompt):
        n = rows if is_prompt else sample_rows
        lead = n if len(shape) == 2 else n // shape[1]
        block = (lead,) + tuple(shape[1:])
        if is_prompt:
            index = lambda i: (jnp.minimum(i, prompt_steps - 1),) + (0,) * (len(shape) - 1)
        else:
            index = lambda i: (jnp.maximum(i - prompt_steps, 0),) + (0,) * (len(shape) - 1)
        return pl.BlockSpec(block, index)

    in_specs = ([spec(p.shape, True) for p, _ in pairs] + [spec(s.shape, False) for _, s in pairs]
                + [pl.BlockSpec(c.shape, lambda i, nd=c.ndim: (0,) * nd) for c in consts])
    out_specs = [spec(p.shape, True) for p, _ in outs] + [spec(s.shape, False) for _, s in outs]
    res = pl.pallas_call(
        functools.partial(_rowwise_kernel, body=body, n_in=len(pairs), n_const=len(consts), n_out=len(outs),
                          prompt_steps=prompt_steps),
        grid=(prompt_steps + sample_steps,),
        in_specs=in_specs,
        out_specs=out_specs,
        out_shape=[p for p, _ in outs] + [s for _, s in outs],
        compiler_params=_cparams(1),
        name=name,
    )(*[p for p, _ in pairs], *[s for _, s in pairs], *consts)
    return res[:len(outs)], res[len(outs):]


def _rmsnorm_tail_body(x, w, w_tail):
    h = _rms(x, w).astype(BF16)
    return h, _dot_nt(h, w_tail)


def _post_mix_body(x, mix, w_post, w_pre):
    return (_rms(x + _rms(mix.astype(F32), w_post), w_pre),)


def _residual_body(x, mix, ffn, w_post_mix, w_post_ffn):
    return (x + _rms(mix.astype(F32), w_post_mix) + _rms(ffn.astype(F32), w_post_ffn),)


def _stack_rows(p_ref, s_ref, cols=slice(None)):
    return jnp.concatenate([p_ref[:, cols].astype(BF16), s_ref[:, cols].astype(BF16)], axis=0)


def _dot_nt(a, b):
    return lax.dot_general(a, b, (((1,), (1,)), ((), ())), preferred_element_type=F32)


def _proj_kernel(*refs, n_a, n_w, swiglu, w_transposed, row_blocks, n_mt, n_tiles, last_cols, side_chunks):
    a_refs = refs[:2 * n_a]
    w_refs = refs[2 * n_a:2 * n_a + n_w]
    rest = refs[2 * n_a + n_w:]
    if side_chunks:
        side_src, rest = rest[0], rest[1:]
    op_ref, os_ref = rest[:2]
    if side_chunks:
        side_dst, rest = rest[2], rest[:2] + rest[3:]
    scratch = rest[2:]
    stage_refs, wb_refs, sem = scratch[:n_w], scratch[n_w:2 * n_w], scratch[2 * n_w]
    mm = _dot_nt if w_transposed else _dot
    j, i = pl.program_id(0), pl.program_id(1)
    nj = pl.num_programs(0)

    if side_chunks:
        side_in, side_out, side_sem = scratch[2 * n_w + 1:2 * n_w + 1 + 3]
        rows_c = side_in.shape[1]
        g = j * n_mt + i
        last_step = n_tiles * n_mt - 1
        chunk_rows = lambda c: pl.ds(pl.multiple_of(c * rows_c, rows_c), rows_c)
        fetch = lambda c: pltpu.make_async_copy(side_src.at[chunk_rows(c)], side_in.at[c % 2], side_sem.at[0, c % 2])
        flush = lambda c: pltpu.make_async_copy(side_out.at[c % 2], side_dst.at[chunk_rows(c)], side_sem.at[1, c % 2])

        pl.when(g == 0)(lambda: fetch(0).start())
        pl.when(jnp.logical_and(g >= 1, g <= side_chunks))(lambda: flush(g - 1).wait())

        @pl.when(g < side_chunks)
        def _():
            fetch(g).wait()
            side_out[g % 2] = side_in[g % 2].astype(BF16)
            flush(g).start()

        pl.when(g + 1 < side_chunks)(lambda: fetch(g + 1).start())
        if side_chunks > last_step:
            pl.when(g == last_step)(lambda: flush(g).wait())

    def slice_copy(w, tile, s, slot, narrow):
        rows, cols = stage_refs[w].shape[1:]
        if w_transposed:
            src = w_refs[w].at[pl.ds(tile * (rows * n_mt) + s * rows, rows), pl.ds(row_blocks[w] * cols, cols)]
            dst = stage_refs[w].at[slot]
        else:
            width = last_cols if narrow else cols
            src = w_refs[w].at[pl.ds((row_blocks[w] * n_mt + s) * rows, rows), pl.ds(tile * cols, width)]
            dst = stage_refs[w].at[slot, :, pl.ds(0, width)]
        return pltpu.make_async_copy(src, dst, sem.at[w, slot])

    def with_tile(tile, fn):
        if last_cols == stage_refs[0].shape[2] or w_transposed:
            fn(False)
        elif isinstance(tile, int):
            fn(tile == n_tiles - 1)
        else:
            pl.when(tile == n_tiles - 1)(lambda: fn(True))
            pl.when(tile != n_tiles - 1)(lambda: fn(False))

    def cast_slice(w, half, s, slot):
        rows = stage_refs[w].shape[1]
        wb_refs[w][half, pl.ds(pl.multiple_of(s * rows, rows), rows), :] = stage_refs[w][slot].astype(BF16)

    @pl.when(jnp.logical_and(j == 0, i == 0))
    def _():
        if last_cols != stage_refs[0].shape[2]:
            for w in range(n_w):
                stage_refs[w][...] = jnp.zeros_like(stage_refs[w])
        for w in range(n_w):
            with_tile(0, lambda narrow, w=w: slice_copy(w, 0, 0, 0, narrow).start())
            for s in range(n_mt):
                if s + 1 < n_mt:
                    with_tile(0, lambda narrow, w=w, s=s: slice_copy(w, 0, s + 1, (s + 1) % 2, narrow).start())
                with_tile(0, lambda narrow, w=w, s=s: slice_copy(w, 0, s, s % 2, narrow).wait())
                cast_slice(w, 0, s, s % 2)
        if n_tiles > 1:
            for w in range(n_w):
                with_tile(1, lambda narrow, w=w: slice_copy(w, 1, 0, 0, narrow).start())

    @pl.when(j + 1 < nj)
    def _():
        slot = (j * n_mt + i) % 2
        for w in range(n_w):
            with_tile(j + 1, lambda narrow, w=w: slice_copy(w, j + 1, i, slot, narrow).wait())
            cast_slice(w, (j + 1) % 2, i, slot)
        last = i + 1 == n_mt
        next_tile = jnp.where(last, j + 2, j + 1)
        next_s = jnp.where(last, 0, i + 1)

        @pl.when(next_tile < nj)
        def _():
            for w in range(n_w):
                with_tile(next_tile, lambda narrow, w=w: slice_copy(w, next_tile, next_s, 1 - slot, narrow).start())

    half = j % 2
    bmp = op_ref.shape[0]

    def apply(width):
        cols = slice(None) if w_transposed else slice(0, width)
        rhs = [wb_ref[half, :, cols] for wb_ref in wb_refs]
        lhs = [_stack_rows(a_refs[2 * p], a_refs[2 * p + 1]) for p in range(n_a)]
        if swiglu:
            g = mm(lhs[0], rhs[0])
            o = g * jax.nn.sigmoid(g) * mm(lhs[0], rhs[1])
        else:
            o = mm(lhs[0], rhs[0])
            for l, r in zip(lhs[1:], rhs[1:]):
                o = o + mm(l, r)
        op_ref[:, :width] = o[:bmp].astype(op_ref.dtype)
        os_ref[:, :width] = o[bmp:].astype(os_ref.dtype)

    full_cols = op_ref.shape[1]
    if last_cols == full_cols:
        apply(full_cols)
    else:
        pl.when(j != n_tiles - 1)(lambda: apply(full_cols))
        pl.when(j == n_tiles - 1)(lambda: apply(last_cols))


def _proj(a_pairs, w_blocks, n, bn, out_dtype, n_mt, name, swiglu=False, w_transposed=False, side=None):
    tp, ts = a_pairs[0][0].shape[0], a_pairs[0][1].shape[0]
    bmp, bms = tp // n_mt, ts // n_mt
    in_specs = []
    for a_p, a_s in a_pairs:
        in_specs.append(pl.BlockSpec((bmp, a_p.shape[1]), lambda j, i: (i, 0)))
        in_specs.append(pl.BlockSpec((bms, a_s.shape[1]), lambda j, i: (i, 0)))
    in_specs += [pl.BlockSpec(memory_space=pl.ANY) for _ in w_blocks]
    n_tiles = pl.cdiv(n, bn)
    tile = lambda k: (bn, k) if w_transposed else (k, bn)
    stage = lambda k: (2, bn // n_mt, k) if w_transposed else (2, k // n_mt, bn)
    operands = [a for pair in a_pairs for a in pair] + [w for w, _, _ in w_blocks]
    out_specs = [pl.BlockSpec((bmp, bn), lambda j, i: (i, j)), pl.BlockSpec((bms, bn), lambda j, i: (i, j))]
    out_shape = [jax.ShapeDtypeStruct((tp, n), out_dtype), jax.ShapeDtypeStruct((ts, n), out_dtype)]
    scratch = ([pltpu.VMEM(stage(k), F32) for _, _, k in w_blocks]
               + [pltpu.VMEM((2,) + tile(k), BF16) for _, _, k in w_blocks]
               + [pltpu.SemaphoreType.DMA((len(w_blocks), 2))])
    side_chunks = 0
    if side is not None:
        rows_c = SIDE_CHUNK_ROWS
        side_chunks = side.shape[0] // rows_c
        assert side.shape[0] == side_chunks * rows_c and side_chunks <= n_tiles * n_mt
        operands.append(side)
        in_specs.append(pl.BlockSpec(memory_space=pl.ANY))
        out_specs.append(pl.BlockSpec(memory_space=pl.ANY))
        out_shape.append(jax.ShapeDtypeStruct(side.shape, BF16))
        scratch += [pltpu.VMEM((2, rows_c, side.shape[1]), F32), pltpu.VMEM((2, rows_c, side.shape[1]), BF16),
                    pltpu.SemaphoreType.DMA((2, 2))]
    return pl.pallas_call(
        functools.partial(_proj_kernel, n_a=len(a_pairs), n_w=len(w_blocks), swiglu=swiglu,
                          w_transposed=w_transposed, row_blocks=tuple(rb for _, rb, _ in w_blocks), n_mt=n_mt,
                          n_tiles=n_tiles, last_cols=n - (n_tiles - 1) * bn, side_chunks=side_chunks),
        grid=(n_tiles, n_mt),
        in_specs=in_specs,
        out_specs=out_specs,
        out_shape=out_shape,
        scratch_shapes=scratch,
        compiler_params=_cparams(2),
        name=name,
    )(*operands)


def _down_kernel(ap_ref, as_ref, b_ref, op_ref, os_ref, accp_ref, accs_ref, *, bn, k_total):
    k = pl.program_id(1)
    nk = pl.num_programs(1)
    bk = b_ref.shape[0]
    bmp = ap_ref.shape[0]
    last_rows = k_total - (k_total // bk) * bk

    @pl.when(k == 0)
    def _():
        accp_ref[...] = jnp.zeros_like(accp_ref)
        accs_ref[...] = jnp.zeros_like(accs_ref)

    def accumulate(kk):
        a = _stack_rows(ap_ref, as_ref, slice(0, kk))
        for n0 in range(0, accp_ref.shape[1], bn):
            d = _dot(a, b_ref[:kk, n0:n0 + bn])
            accp_ref[:, n0:n0 + bn] += d[:bmp]
            accs_ref[:, n0:n0 + bn] += d[bmp:]

    if last_rows == 0:
        accumulate(bk)
    else:
        pl.when(k < nk - 1)(lambda: accumulate(bk))
        pl.when(k == nk - 1)(lambda: accumulate(last_rows))

    @pl.when(k == nk - 1)
    def _():
        op_ref[...] = accp_ref[...].astype(op_ref.dtype)
        os_ref[...] = accs_ref[...].astype(os_ref.dtype)


def _down(a_p, a_s, b, bk, n_mt):
    (tp, k_total), ts = a_p.shape, a_s.shape[0]
    n = b.shape[1]
    bmp, bms = tp // n_mt, ts // n_mt
    return pl.pallas_call(
        functools.partial(_down_kernel, bn=min(n, 512), k_total=k_total),
        grid=(n_mt, pl.cdiv(k_total, bk)),
        in_specs=[pl.BlockSpec((bmp, bk), lambda i, k: (i, k)), pl.BlockSpec((bms, bk), lambda i, k: (i, k)),
                  pl.BlockSpec((bk, n), lambda i, k: (k, 0))],
        out_specs=[pl.BlockSpec((bmp, n), lambda i, k: (i, 0), pipeline_mode=pl.Buffered(1)),
                   pl.BlockSpec((bms, n), lambda i, k: (i, 0))],
        out_shape=[jax.ShapeDtypeStruct((tp, n), BF16), jax.ShapeDtypeStruct((ts, n), BF16)],
        scratch_shapes=[pltpu.VMEM((bmp, n), F32), pltpu.VMEM((bms, n), F32)],
        compiler_params=_cparams(2),
        name="ffn_down",
    )(a_p, a_s, b)


def _rel_bucket(d):
    n = jnp.maximum(d, 0)
    max_exact = REL_BUCKETS // 2
    nf = jnp.maximum(n, 1).astype(F32)
    large = max_exact + (jnp.log(nf / max_exact) / math.log(REL_MAX_DIST / max_exact)
                         * (REL_BUCKETS - max_exact)).astype(jnp.int32)
    large = jnp.minimum(large, REL_BUCKETS - 1)
    return jnp.where(n < max_exact, n, large)


def _bias_kernel(rel_ref, bucket_ref, allowed_ref, o_ref):
    h = pl.program_id(0)
    bucket = bucket_ref[...]
    acc = jnp.zeros(bucket.shape, F32)
    for n in range(REL_BUCKETS):
        acc = jnp.where(bucket == n, rel_ref[n, h], acc)
    o_ref[0] = jnp.where(allowed_ref[...] > 0, acc, NEG_INF)


def _bias_table(rel_bias, dist):
    tq, tk = dist.shape
    allowed = ((dist >= 0) & (dist <= WINDOW)).astype(jnp.int32)
    full = pl.BlockSpec((tq, tk), lambda h: (0, 0))
    return pl.pallas_call(
        _bias_kernel,
        grid=(ATTN_HEADS,),
        in_specs=[pl.BlockSpec(memory_space=pltpu.SMEM), full, full],
        out_specs=pl.BlockSpec((1, tq, tk), lambda h: (h, 0, 0)),
        out_shape=jax.ShapeDtypeStruct((ATTN_HEADS, tq, tk), F32),
        compiler_params=_cparams(1),
        name="rel_bias_table",
    )(rel_bias.astype(F32), _rel_bucket(dist), allowed)


def _softmax_pv(s, sink):
    m = jnp.maximum(jnp.max(s, axis=-1, keepdims=True), sink)
    p = jnp.exp(s - m)
    denom = jnp.sum(p, axis=-1, keepdims=True) + jnp.exp(sink - m)
    return p.astype(BF16), denom


def _attn_prompt_kernel(sink_ref, cur_ref, prev_ref, tables_ref, o_ref, *, nq):
    i = pl.program_id(0)
    c = WINDOW
    head_cols = lambda base, h: slice(base + h * HEAD_DIM, base + (h + 1) * HEAD_DIM)
    blk = lambda b: slice(b * c, (b + 1) * c)

    def keys(b, h, cur_base, prev_base):
        before = prev_ref[:, head_cols(prev_base, h)] if b == 0 else cur_ref[blk(b - 1), head_cols(cur_base, h)]
        return jnp.concatenate([before, cur_ref[blk(b), head_cols(cur_base, h)]], axis=0).astype(BF16)

    kk = {(b, h): keys(b, h, K_OFF, 0) for b in range(nq) for h in range(ATTN_KV_HEADS)}
    vv = {(b, h): keys(b, h, V_OFF, KV_WIDTH) for b in range(nq) for h in range(ATTN_KV_HEADS)}
    pairs = [(b, hd) for b in range(nq) for hd in range(ATTN_HEADS)]
    scale = HEAD_DIM ** -0.5
    table = lambda b: jnp.minimum(i, 1) if b == 0 else 1
    s = {p: _dot_nt((cur_ref[blk(p[0]), head_cols(Q_OFF, p[1])] * scale).astype(BF16), kk[p[0], p[1] // ATTN_GROUP])
         for p in pairs}
    pd = {p: _softmax_pv(s[p] + tables_ref[table(p[0]), p[1]], sink_ref[p[1]]) for p in pairs}
    o = {p: _dot(pd[p][0], vv[p[0], p[1] // ATTN_GROUP]) / pd[p][1] for p in pairs}
    for b in range(nq):
        o_ref[blk(b), :] = jnp.concatenate([o[b, hd] for hd in range(ATTN_HEADS)], axis=1).astype(o_ref.dtype)


def _attn_prompt(proj, bias, sinks, nq):
    t = proj.shape[0]
    c = WINDOW
    qkv = V_OFF + KV_WIDTH
    first = jnp.where(jnp.arange(2 * c) < c, NEG_INF, bias)
    tables = jnp.stack([first, bias])
    return pl.pallas_call(
        functools.partial(_attn_prompt_kernel, nq=nq),
        grid=(t // (c * nq),),
        in_specs=[pl.BlockSpec(memory_space=pltpu.SMEM),
                  pl.BlockSpec((c * nq, qkv), lambda i: (i, 0)),
                  pl.BlockSpec((c, 2 * KV_WIDTH), lambda i: (jnp.maximum(i * nq - 1, 0), K_OFF // (2 * KV_WIDTH))),
                  pl.BlockSpec(tables.shape, lambda i: (0, 0, 0, 0))],
        out_specs=pl.BlockSpec((c * nq, ATTN_WIDTH), lambda i: (i, 0)),
        out_shape=jax.ShapeDtypeStruct((t, ATTN_WIDTH), BF16),
        compiler_params=_cparams(1),
        name="attn_prompt",
    )(sinks.astype(F32), proj, proj, tables)


def _attn_sample_kernel(x_ref, ck_ref, cv_ref, bias_c_ref, bias_n_ref, sink_ref, o_ref, ok_ref, ov_ref,
                        q_s, kn_s, vn_s, *, t_new):
    gb, l, kvh, hd = ck_ref.shape
    kc = ck_ref[...].reshape(gb, l * kvh, hd)
    vc = cv_ref[...].reshape(gb, l * kvh, hd)
    for b in range(gb):
        rows = slice(b * t_new, (b + 1) * t_new)
        for h in range(kvh):
            kn_s[b, h * t_new:(h + 1) * t_new, :] = x_ref[rows, K_OFF + h * hd:K_OFF + (h + 1) * hd]
            vn_s[b, h * t_new:(h + 1) * t_new, :] = x_ref[rows, V_OFF + h * hd:V_OFF + (h + 1) * hd]
        for head in range(ATTN_HEADS):
            q_s[b, head * t_new:(head + 1) * t_new, :] = x_ref[rows, head * hd:(head + 1) * hd]

    q = q_s[...].astype(BF16)
    scale = hd ** -0.5
    s_c = jnp.einsum("bqd,bkd->bqk", q, kc.astype(BF16), preferred_element_type=F32) * scale + bias_c_ref[...]
    s_n = jnp.einsum("bqd,bkd->bqk", q, kn_s[...].astype(BF16), preferred_element_type=F32) * scale + bias_n_ref[...]
    sink = sink_ref[...]
    m = jnp.maximum(jnp.maximum(jnp.max(s_c, axis=-1, keepdims=True), jnp.max(s_n, axis=-1, keepdims=True)), sink)
    p_c = jnp.exp(s_c - m)
    p_n = jnp.exp(s_n - m)
    denom = jnp.sum(p_c, axis=-1, keepdims=True) + jnp.sum(p_n, axis=-1, keepdims=True) + jnp.exp(sink - m)
    o = (jnp.einsum("bqk,bkd->bqd", p_c.astype(BF16), vc.astype(BF16), preferred_element_type=F32)
         + jnp.einsum("bqk,bkd->bqd", p_n.astype(BF16), vn_s[...].astype(BF16), preferred_element_type=F32)) / denom
    for b in range(gb):
        for head in range(ATTN_HEADS):
            o_ref[b * t_new:(b + 1) * t_new, head * hd:(head + 1) * hd] = o[b, head * t_new:(head + 1) * t_new, :]

    shift = t_new * kvh
    for c_val, n_s, out_ref in ((kc, kn_s, ok_ref), (vc, vn_s, ov_ref)):
        new_rows = [n_s[:, h * t_new + t:h * t_new + t + 1, :] for t in range(t_new) for h in range(kvh)]
        win = jnp.concatenate([c_val[:, shift:, :]] + new_rows, axis=1)
        out_ref[...] = win.reshape(gb, l, kvh, hd)


def _attn_sample(proj, cache_k, cache_v, bias_c, bias_n, sink_rows, t_new, gb):
    b, l = cache_k.shape[:2]
    rows = ATTN_HEADS * t_new
    qkv = V_OFF + KV_WIDTH
    cache_spec = pl.BlockSpec((gb, l, ATTN_KV_HEADS, HEAD_DIM), lambda i: (i, 0, 0, 0))
    full = lambda a: pl.BlockSpec(a.shape, lambda i: (0,) * a.ndim)
    return pl.pallas_call(
        functools.partial(_attn_sample_kernel, t_new=t_new),
        grid=(b // gb,),
        in_specs=[pl.BlockSpec((gb * t_new, qkv), lambda i: (i, 0)), cache_spec, cache_spec,
                  full(bias_c), full(bias_n), full(sink_rows)],
        out_specs=[pl.BlockSpec((gb * t_new, ATTN_WIDTH), lambda i: (i, 0)), cache_spec, cache_spec],
        out_shape=[jax.ShapeDtypeStruct((b * t_new, ATTN_WIDTH), F32),
                   jax.ShapeDtypeStruct(cache_k.shape, F32),
                   jax.ShapeDtypeStruct(cache_v.shape, F32)],
        scratch_shapes=[pltpu.VMEM((gb, rows, HEAD_DIM), F32),
                        pltpu.VMEM((gb, ATTN_KV_HEADS * t_new, HEAD_DIM), F32),
                        pltpu.VMEM((gb, ATTN_KV_HEADS * t_new, HEAD_DIM), F32)],
        compiler_params=_cparams(1),
        name="attn_sample",
    )(proj, cache_k, cache_v, bias_c, bias_n, sink_rows)


def _split3(x):
    hi = x.astype(BF16)
    r = x - hi.astype(F32)
    mid = r.astype(BF16)
    lo = (r - mid.astype(F32)).astype(BF16)
    return hi, mid, lo


def _select_dot(sel, x):
    hi, mid, lo = _split3(x)
    return _dot(sel, hi) + _dot(sel, mid) + _dot(sel, lo)


def _dot_select(x, sel):
    hi, mid, lo = _split3(x)
    return _dot(hi, sel) + _dot(mid, sel) + _dot(lo, sel)


def _softplus(x):
    return jnp.maximum(x, 0.0) + jnp.log1p(jnp.exp(-jnp.abs(x)))


def _conv_silu(pad_ref, w, r0, n):
    top = SUBLANES + r0
    y = pad_ref[top:top + n, :] * w[GDN_CONV - 1:GDN_CONV]
    for s in range(1, GDN_CONV):
        y = y + pad_ref[top - s:top - s + n, :] * w[GDN_CONV - 1 - s:GDN_CONV - s]
    return y * jax.nn.sigmoid(y)


def _l2norm_heads(x, scale):
    outs = []
    for hh in range(GDN_HB):
        xh = x[:, hh * HEAD_DIM:(hh + 1) * HEAD_DIM]
        outs.append(xh * (lax.rsqrt(jnp.sum(xh * xh, axis=-1, keepdims=True) + NORM_EPS) * scale))
    return jnp.concatenate(outs, axis=1)


GDN_N_PADS = 4
STAGE_ROWS = 32
PROBE_BOUND = 1e30
GDN_N_STAGED = 10


def _gdn_kernel(*refs, rows, seq, carry, live_from):
    kw = dict(rows=rows, seq=seq, carry=carry, live_from=live_from)
    if not carry:
        io, pads, staged = refs[:-GDN_N_PADS - GDN_N_STAGED], refs[-GDN_N_PADS - GDN_N_STAGED:-GDN_N_STAGED], \
            refs[-GDN_N_STAGED:]
        _gdn_body(io, pads, staged, staged, **kw)
        return
    n_scratch = GDN_N_PADS + 2 * GDN_N_STAGED
    io, pads = refs[:-n_scratch], refs[-n_scratch:-2 * GDN_N_STAGED]
    set_a, set_b = refs[-2 * GDN_N_STAGED:-GDN_N_STAGED], refs[-GDN_N_STAGED:]
    s_ref = io[-1]
    step = pl.program_id(1)

    @pl.when(step == 0)
    def _():
        s_ref[...] = jnp.zeros_like(s_ref)
        for ref in set_b:
            ref[...] = jnp.zeros_like(ref)

    pl.when(step % 2 == 0)(lambda: _gdn_body(io, pads, set_a, set_b, **kw))
    pl.when(step % 2 == 1)(lambda: _gdn_body(io, pads, set_b, set_a, **kw))


def _gdn_body(refs, pads, staged_w, staged_r, *, rows, seq, carry, live_from):
    c = GDN_CHUNK
    nc = rows // c
    nseq = c // seq
    n_live = seq - live_from
    lw = GDN_HB * HEAD_DIM
    if carry:
        (xq_ref, xk_ref, xv_ref, tq_ref, tk_ref, tv_ref, z_ref, wq_ref, wk_ref, wv_ref, gn_ref,
         ab_ref, pc_ref, og_ref, s_ref) = refs
        s0_ref = None
    else:
        (xq_ref, xk_ref, xv_ref, cq_ref, ck_ref, cv_ref, z_in, wq_ref, wk_ref, wv_ref, gn_ref,
         ab_in, pc_ref, s0_ref, og_ref, s_ref, z_ref, ab_ref) = refs
    padq_s, padk_s, padv_s, eb_s = pads
    kn_w, kb_w, qn_w, vb_w, kbe_w, qe_w, kd_w, egc_w, egl_w, gcr_w = staged_w
    kn_s, kb_s, qn_s, vb_s, kbe_s, qe_s, kd_s, egc_s, egl_s, gcr_s = staged_r

    group = pl.program_id(0)
    step = pl.program_id(1)
    row = lax.broadcasted_iota(jnp.int32, (rows, 1), 0)
    live = (row % seq) >= live_from
    keep_live = (lambda x: jnp.where(live, x, 0.0)) if live_from else (lambda x: x)

    if carry:
        for pad_ref, x_ref, t_ref in ((padq_s, xq_ref, tq_ref), (padk_s, xk_ref, tk_ref), (padv_s, xv_ref, tv_ref)):
            pad_ref[0:SUBLANES, :] = jnp.where(step == 0, 0.0, t_ref[...])
            pad_ref[SUBLANES:, :] = x_ref[...]
    else:
        def gather(dst_ref, x_ref, hist_ref, top):
            dst_ref[...] = jnp.zeros_like(dst_ref)
            for b in range(rows // seq):
                r0 = top + b * seq + live_from
                if hist_ref is not None:
                    dst_ref[r0 - (GDN_CONV - 1):r0, :] = hist_ref[:, b, :]
                dst_ref[r0:r0 + n_live, :] = x_ref[b * n_live:(b + 1) * n_live, :]

        gather(padq_s, xq_ref, cq_ref, SUBLANES)
        gather(padk_s, xk_ref, ck_ref, SUBLANES)
        gather(padv_s, xv_ref, cv_ref, SUBLANES)
        gather(z_ref, z_in, None, 0)
        gather(ab_ref, ab_in, None, 0)

    def stage_decays():
        ab = ab_ref[...]
        g_col = keep_live(-jnp.exp(pc_ref[0:1, :]) * _softplus(ab + pc_ref[1:2, :]))
        beta_col = keep_live(jax.nn.sigmoid(ab))
        ri = lax.broadcasted_iota(jnp.int32, (rows, rows), 0)
        ci = lax.broadcasted_iota(jnp.int32, (rows, rows), 1)
        same = (ri // seq) == (ci // seq)
        cum_sel = jnp.where(jnp.logical_and(same, ci <= ri), 1.0, 0.0).astype(BF16)
        tot_sel = jnp.where(same, 1.0, 0.0).astype(BF16)
        gc_col = _select_dot(cum_sel, g_col)
        gl_col = _select_dot(tot_sel, g_col)
        first = group * GDN_HB
        li = lax.broadcasted_iota(jnp.int32, (LANES, lw), 0)
        ni = lax.broadcasted_iota(jnp.int32, (LANES, lw), 1)
        spread_g = jnp.where(li == ni // HEAD_DIM + first, 1.0, 0.0).astype(BF16)
        spread_b = jnp.where(li == ni // HEAD_DIM + first + GDN_HEADS, 1.0, 0.0).astype(BF16)
        egc_w[...] = _dot_select(gc_col, spread_g)
        egl_w[...] = _dot_select(gl_col, spread_g)
        eb_s[...] = _dot_select(beta_col, spread_b)
        pi = lax.broadcasted_iota(jnp.int32, (LANES, LANES), 0)
        pj = lax.broadcasted_iota(jnp.int32, (LANES, LANES), 1)
        pick = jnp.where(jnp.logical_and(pi == pj + first, pj < GDN_HB), 1.0, 0.0).astype(BF16)
        gc_heads = _dot_select(gc_col, pick)
        if rows % LANES:
            gc_heads = jnp.concatenate([gc_heads, jnp.zeros((LANES - rows % LANES, LANES), F32)], axis=0)
        gcr_w[...] = gc_heads.T[:SUBLANES, :rows]

    def stage_rows(r0, n, after=None):
        sl = slice(r0, r0 + n)
        keep = (lambda x: jnp.where(live[sl], x, 0.0)) if live_from else (lambda x: x)
        gate = 1.0 if after is None else jnp.where(after > PROBE_BOUND, 0.0, 1.0)
        conv = lambda pad_ref, w_ref: keep(_conv_silu(pad_ref, w_ref[...] * gate, r0, n))
        qn = _l2norm_heads(conv(padq_s, wq_ref), HEAD_DIM ** -0.5)
        kn = _l2norm_heads(conv(padk_s, wk_ref), 1.0)
        v = conv(padv_s, wv_ref)
        egc, egl, eb = egc_w[sl, :], egl_w[sl, :], eb_s[sl, :]
        decay_in = jnp.exp(egc)
        kb = kn * eb
        kn_w[sl, :] = kn
        kb_w[sl, :] = kb
        qn_w[sl, :] = qn
        vb_w[sl, :] = v * eb
        kbe_w[sl, :] = kb * decay_in
        qe_w[sl, :] = qn * decay_in
        kd_w[sl, :] = kn * jnp.exp(egl - egc)

    n_slices = rows // STAGE_ROWS if carry else 1
    pending = [functools.partial(stage_rows, r * (rows // n_slices), rows // n_slices) for r in range(n_slices)]
    if not carry:
        stage_decays()
        pending.pop(0)()

    cr = lax.broadcasted_iota(jnp.int32, (c, c), 0)
    cc = lax.broadcasted_iota(jnp.int32, (c, c), 1)
    same_c = (cr // seq) == (cc // seq)
    causal_bias = jnp.where(jnp.logical_and(same_c, cr >= cc), 0.0, NEG_INF)
    strict = jnp.where(jnp.logical_and(same_c, cr > cc), 1.0, 0.0)

    eye = jnp.where(cr == cc, 1.0, 0.0)
    seq_of_row = lax.broadcasted_iota(jnp.int32, (c, HEAD_DIM), 0) // seq

    def lower_left(s):
        return jnp.logical_and(jnp.logical_and(cr // (2 * s) == cc // (2 * s), (cr // s) % 2 == 1),
                               (cc // s) % 2 == 0)

    merge_sizes = [2 ** e for e in range(1, max(1, math.ceil(math.log2(n_live))))]
    pair_mask = jnp.where(lower_left(1), 1.0, 0.0)
    merge_masks = [jnp.where(lower_left(s), 1.0, 0.0) for s in merge_sizes]

    pairs = [(ch, hh) for ch in range(nc) for hh in range(GDN_HB)]
    rows_of = lambda ch: slice(ch * c, (ch + 1) * c)
    lanes_of = lambda hh: slice(hh * HEAD_DIM, (hh + 1) * HEAD_DIM)
    uw_all, qk_all, a_mat, t_mat, t_a = {}, {}, {}, {}, {}
    for ch, hh in pairs:
        rs, hs = rows_of(ch), lanes_of(hh)
        gcol = egc_s[rs, hh * HEAD_DIM:hh * HEAD_DIM + c]
        grow = gcr_s[hh:hh + 1, rs]
        decay = jnp.exp(gcol - grow + causal_bias)
        kq = jnp.concatenate([kb_s[rs, hs], qn_s[rs, hs]], axis=0).astype(BF16)
        kk = _dot_nt(kq, kn_s[rs, hs].astype(BF16)) * jnp.concatenate([decay, decay], axis=0)
        a_mat[ch, hh] = kk[:c] * strict
        qk_all[ch, hh] = kk[c:].astype(BF16)
        t_mat[ch, hh] = eye - a_mat[ch, hh] * pair_mask
    probes = [a_mat[pairs[-1]][0:1, 0:1]]
    if carry:
        stage_decays()
    for mask in merge_masks:
        for p in pairs:
            t_a[p] = _dot(t_mat[p].astype(BF16), (a_mat[p] * mask).astype(BF16))
        for p in pairs:
            t_mat[p] = t_mat[p] - _dot(t_a[p].astype(BF16), t_mat[p].astype(BF16))
        probes.append(t_mat[pairs[-1]][0:1, 0:1])
    for ch, hh in pairs:
        rs, hs = rows_of(ch), lanes_of(hh)
        uw_all[ch, hh] = _dot(t_mat[ch, hh].astype(BF16),
                              jnp.concatenate([vb_s[rs, hs], kbe_s[rs, hs]], axis=1).astype(BF16))

    if carry:
        state = [s_ref[hh] for hh in range(GDN_HB)]
    new_states = {}
    heads = range(GDN_HB)
    tn_dot = lambda a, b: lax.dot_general(a, b, (((0,), (0,)), ((), ())), preferred_element_type=F32)
    for ch in range(nc):
        rs = rows_of(ch)
        w_s, q_s, v_new_b, o = {}, {}, {}, {}
        for hh in heads:
            w = uw_all[ch, hh][:, HEAD_DIM:]
            q_e = qe_s[rs, lanes_of(hh)]
            if carry:
                wq = _dot(jnp.concatenate([w, q_e], axis=0).astype(BF16), state[hh].astype(BF16))
                w_s[hh], q_s[hh] = wq[:c], wq[c:]
            else:
                parts = []
                for b in range(nseq):
                    bs = slice(b * seq, (b + 1) * seq)
                    parts.append(_dot(jnp.concatenate([w[bs], q_e[bs]], axis=0).astype(BF16),
                                      s0_ref[ch * nseq + b, hh].astype(BF16)))
                w_s[hh] = jnp.concatenate([p[:seq] for p in parts], axis=0)
                q_s[hh] = jnp.concatenate([p[seq:] for p in parts], axis=0)
        for hh in heads:
            v_new_b[hh] = (uw_all[ch, hh][:, :HEAD_DIM] - w_s[hh]).astype(BF16)
            o[hh] = q_s[hh] + _dot(qk_all[ch, hh], v_new_b[hh])
        for hh in heads:
            hs = lanes_of(hh)
            k_d = kd_s[rs, hs]
            if carry:
                state[hh] = (state[hh] * jnp.exp(egl_s[ch * c:ch * c + 1, hs])
                             + tn_dot(k_d.astype(BF16), v_new_b[hh]))
            else:
                for b in range(nseq):
                    k_db = jnp.where(seq_of_row == b, k_d, 0.0).astype(BF16)
                    r1 = ch * c + b * seq
                    new_states[ch * nseq + b, hh] = (
                        s0_ref[ch * nseq + b, hh] * jnp.exp(egl_s[r1:r1 + 1, hs]) + tn_dot(k_db, v_new_b[hh]))
        if carry:
            probes.append(_dot(qk_all[ch, 0], jnp.ones((c, HEAD_DIM), BF16))[0:1, 0:1])
        outs = []
        for hh in heads:
            z = z_ref[rs, lanes_of(hh)]
            outs.append(_rms(o[hh], gn_ref[...]) * (z * jax.nn.sigmoid(z)))
        og = jnp.concatenate(outs, axis=1)
        if carry:
            og_ref[rs, :] = og.astype(og_ref.dtype)
        else:
            for b in range(nseq):
                r0 = b * seq + live_from
                og_ref[(ch * nseq + b) * n_live:(ch * nseq + b + 1) * n_live, :] = (
                    og[r0:r0 + n_live].astype(og_ref.dtype))
    for n, stage in enumerate(pending):
        stage(after=probes[min(n, len(probes) - 1)])
    if carry:
        s_ref[...] = jnp.stack(state, axis=0)
    else:
        for (b, hh), s_new in new_states.items():
            s_ref[b, hh] = s_new


def _gdn_params(a_log, dt_bias):
    rows = jnp.stack([a_log.astype(F32), dt_bias.astype(F32)], axis=0)
    return jnp.pad(rows, ((0, SUBLANES - 2), (0, LANES - GDN_HEADS)))


def _gdn_scratch(rows, n_sets):
    lw = GDN_HB * HEAD_DIM
    staged = [(rows, lw)] * (GDN_N_STAGED - 1) + [(SUBLANES, rows)]
    return ([pltpu.VMEM((SUBLANES + rows, lw), F32) for _ in range(GDN_N_PADS - 1)] + [pltpu.VMEM((rows, lw), F32)]
            + [pltpu.VMEM(shape, F32) for _ in range(n_sets) for shape in staged])


def _gdn_prompt(proj, ab, conv_w, gnorm_w, params, rows):
    t = proj.shape[0]
    nb = t // rows
    lw = GDN_HB * HEAD_DIM
    head = lambda i: jnp.minimum(i, nb - 1)
    done = lambda i: jnp.maximum(i - 1, 0)
    x_spec = lambda off: pl.BlockSpec((rows, lw), lambda g, i: (head(i), off // lw + g))
    tail_spec = lambda off: pl.BlockSpec(
        (SUBLANES, lw), lambda g, i: (jnp.maximum(head(i) * (rows // SUBLANES) - 1, 0), off // lw + g))
    w_spec = lambda off: pl.BlockSpec((GDN_CONV, lw), lambda g, i: (0, (off - GQ_OFF) // lw + g))
    return pl.pallas_call(
        functools.partial(_gdn_kernel, rows=rows, seq=GDN_CHUNK, carry=True, live_from=0),
        grid=(GDN_NG, nb + 1),
        in_specs=[x_spec(GQ_OFF), x_spec(GK_OFF), x_spec(GV_OFF),
                  tail_spec(GQ_OFF), tail_spec(GK_OFF), tail_spec(GV_OFF),
                  pl.BlockSpec((rows, lw), lambda g, i: (done(i), Z_OFF // lw + g)),
                  w_spec(GQ_OFF), w_spec(GK_OFF), w_spec(GV_OFF),
                  pl.BlockSpec((1, HEAD_DIM), lambda g, i: (0, 0)),
                  pl.BlockSpec((rows, LANES), lambda g, i: (head(i), 0)),
                  pl.BlockSpec((SUBLANES, LANES), lambda g, i: (0, 0))],
        out_specs=[pl.BlockSpec((rows, lw), lambda g, i: (done(i), g)),
                   pl.BlockSpec((GDN_HB, HEAD_DIM, HEAD_DIM), lambda g, i: (g, 0, 0))],
        out_shape=[jax.ShapeDtypeStruct((t, GDN_WIDTH), BF16),
                   jax.ShapeDtypeStruct((GDN_HEADS, HEAD_DIM, HEAD_DIM), F32)],
        scratch_shapes=_gdn_scratch(rows, 2),
        compiler_params=_cparams(2),
        name="gdn_prompt",
    )(proj, proj, proj, proj, proj, proj, proj, conv_w, conv_w, conv_w,
      gnorm_w.reshape(1, HEAD_DIM).astype(F32), ab, params)


def _gdn_sample(proj, ab, conv_state, state, conv_w, gnorm_w, params, n_live, rows):
    lw = GDN_HB * HEAD_DIM
    nseq = rows // SAMPLE_PAD
    n_in = nseq * n_live
    x_spec = lambda off: pl.BlockSpec((n_in, lw), lambda g, i: (i, off // lw + g))
    hist_spec = lambda off: pl.BlockSpec((GDN_CONV - 1, nseq, lw), lambda g, i: (0, i, (off - GQ_OFF) // lw + g))
    w_spec = lambda off: pl.BlockSpec((GDN_CONV, lw), lambda g, i: (0, (off - GQ_OFF) // lw + g))
    s_spec = pl.BlockSpec((nseq, GDN_HB, HEAD_DIM, HEAD_DIM), lambda g, i: (i, g, 0, 0))
    return pl.pallas_call(
        functools.partial(_gdn_kernel, rows=rows, seq=SAMPLE_PAD, carry=False, live_from=SAMPLE_PAD - n_live),
        grid=(GDN_NG, proj.shape[0] // n_in),
        in_specs=[x_spec(GQ_OFF), x_spec(GK_OFF), x_spec(GV_OFF),
                  hist_spec(GQ_OFF), hist_spec(GK_OFF), hist_spec(GV_OFF),
                  x_spec(Z_OFF), w_spec(GQ_OFF), w_spec(GK_OFF), w_spec(GV_OFF),
                  pl.BlockSpec((1, HEAD_DIM), lambda g, i: (0, 0)),
                  pl.BlockSpec((n_in, LANES), lambda g, i: (i, 0)),
                  pl.BlockSpec((SUBLANES, LANES), lambda g, i: (0, 0)),
                  s_spec],
        out_specs=[pl.BlockSpec((n_in, lw), lambda g, i: (i, g)), s_spec],
        out_shape=[jax.ShapeDtypeStruct((proj.shape[0], GDN_WIDTH), F32),
                   jax.ShapeDtypeStruct(state.shape, F32)],
        scratch_shapes=[pltpu.VMEM((rows, lw), F32), pltpu.VMEM((rows, LANES), F32)] + _gdn_scratch(rows, 1),
        compiler_params=_cparams(2),
        name="gdn_sample",
    )(proj, proj, proj, conv_state, conv_state, conv_state, proj, conv_w, conv_w, conv_w,
      gnorm_w.reshape(1, HEAD_DIM).astype(F32), ab, params, state)


def _layer(x_prompt, x_sample, win_k, win_v, conv_state, gdn_state, rel_bias,
           w_in, sinks, conv_w, a_log, dt_bias, gnorm_w, w_out,
           n_pre_mix, n_post_mix, n_pre_ffn, n_post_ffn, w_gate, w_up, w_down, tiles):
    _, t, d = x_prompt.shape
    nb, ts, _ = x_sample.shape
    ms = nb * ts
    l = win_k.shape[1]
    ff = w_gate.shape[1]
    n_mt = tiles["n_mt"]
    xp = x_prompt.reshape(t, d)
    sds = jax.ShapeDtypeStruct
    norm_w = lambda w: w.reshape(1, d).astype(F32)

    w_in_t = w_in.T
    w_tail = jnp.pad(w_in_t[PROJ_MAIN:], ((0, LANES - AB_WIDTH), (0, 0))).astype(BF16)
    (hp, ab_p), (hs, ab_s) = _rowwise(
        _rmsnorm_tail_body, [(xp, x_sample)], [norm_w(n_pre_mix), w_tail],
        [(sds((t, d), BF16), sds((ms, d), BF16)), (sds((t, LANES), F32), sds((ms, LANES), F32))],
        tiles["norm_rows"], "rmsnorm")
    proj_p, proj_s = _proj([(hp, hs)], [(w_in_t, 0, d)], PROJ_MAIN, tiles["in_bn"], F32, n_mt, "in_proj",
                           w_transposed=True)

    qi = jnp.arange(WINDOW)[:, None]
    kj = jnp.arange(2 * WINDOW)[None, :]
    dist_p = WINDOW + qi - kj
    dist_s = l + jnp.arange(SUBLANES)[:, None] - kj
    dist_s = jnp.where(kj < l + ts, dist_s, -1)
    bias_all = _bias_table(rel_bias, jnp.concatenate([dist_p, dist_s], axis=0))
    attn_p = _attn_prompt(proj_p, bias_all[:, :WINDOW], sinks, tiles["attn_nq"])

    b_s = bias_all[:, WINDOW:WINDOW + ts]
    own_kv = (jnp.arange(ATTN_HEADS)[:, None] // ATTN_GROUP) == jnp.arange(ATTN_KV_HEADS)[None, :]
    bias_c = jnp.where(own_kv[:, None, None, :], b_s[:, :, :l, None], NEG_INF)
    bias_c = bias_c.reshape(ATTN_HEADS * ts, l * ATTN_KV_HEADS)
    bias_n = jnp.where(own_kv[:, None, :, None], b_s[:, :, None, l:l + ts], NEG_INF)
    bias_n = bias_n.reshape(ATTN_HEADS * ts, ATTN_KV_HEADS * ts)
    sink_rows = jnp.repeat(sinks.astype(F32), ts).reshape(ATTN_HEADS * ts, 1)
    attn_s, new_k, new_v = _attn_sample(proj_s, win_k, win_v, bias_c, bias_n, sink_rows, ts, tiles["attn_gb"])

    params = _gdn_params(a_log, dt_bias)
    og_p, s_p = _gdn_prompt(proj_p, ab_p, conv_w, gnorm_w, params, tiles["gdn_rows"])
    og_s, s_s = _gdn_sample(proj_s, ab_s, jnp.swapaxes(conv_state, 0, 1).astype(F32), gdn_state, conv_w, gnorm_w, params, ts,
                            tiles["gdn_sample_rows"])

    mix_p, mix_s = _proj([(attn_p, attn_s), (og_p, og_s)], [(w_out, 0, ATTN_WIDTH), (w_out, 1, GDN_WIDTH)],
                         d, tiles["out_bn"], BF16, n_mt, "out_proj")
    (h2_p,), (h2_s,) = _rowwise(
        _post_mix_body, [(xp, x_sample), (mix_p, mix_s)], [norm_w(n_post_mix), norm_w(n_pre_ffn)],
        [(sds((t, d), BF16), sds((ms, d), BF16))], tiles["mix_rows"], "post_mix")
    act_p, act_s, w_down_b = _proj([(h2_p, h2_s)], [(w_gate, 0, d), (w_up, 0, d)], ff, tiles["ff_bn"], BF16, n_mt,
                                   "ffn_gate_up", swiglu=True, side=w_down)
    ffn_p, ffn_s = _down(act_p, act_s, w_down_b, tiles["ff_bk"], n_mt)
    (y_p,), (y_s,) = _rowwise(
        _residual_body, [(xp, x_sample), (mix_p, mix_s), (ffn_p, ffn_s)], [norm_w(n_post_mix), norm_w(n_post_ffn)],
        [(sds((t, d), F32), sds(x_sample.shape, F32))], tiles["post_rows"], "residual_norm")

    lw = min(WINDOW, t)
    prompt_k = proj_p[t - lw:, K_OFF:K_OFF + KV_WIDTH].reshape(1, lw, ATTN_KV_HEADS, HEAD_DIM)
    prompt_v = proj_p[t - lw:, V_OFF:V_OFF + KV_WIDTH].reshape(1, lw, ATTN_KV_HEADS, HEAD_DIM)
    prompt_conv = proj_p[t - (GDN_CONV - 1):, GQ_OFF:GQ_OFF + GDN_CONV_CH][None]
    sample_conv = proj_s[:, GQ_OFF:GQ_OFF + GDN_CONV_CH].reshape(nb, ts, GDN_CONV_CH)[:, ts - (GDN_CONV - 1):]
    return (y_p.reshape(1, t, d), y_s,
            prompt_k, prompt_v, prompt_conv, s_p[None],
            new_k.reshape(nb, l, ATTN_KV_HEADS, HEAD_DIM), new_v.reshape(nb, l, ATTN_KV_HEADS, HEAD_DIM),
            sample_conv, s_s)


def _tiles(t, ms):
    n_mt = 8 if (t % (8 * 16) == 0 and ms % (8 * 16) == 0) else 1
    return dict(n_mt=n_mt, in_bn=1024, out_bn=1024, ff_bn=512, ff_bk=1024, norm_rows=512, mix_rows=512, post_rows=256,
                gdn_rows=256, gdn_sample_rows=2 * GDN_CHUNK, attn_gb=8, attn_nq=2)


def kernel(x_prompt, x_sample, cache_win_k, cache_win_v, state_conv, state_gdn, rel_bias, w_in, attn_sinks,
           gdn_conv_w, gdn_a_log, gdn_dt_bias, gdn_norm_w, w_out, norm_pre_mix, norm_post_mix, norm_pre_ffn,
           norm_post_ffn, w_gate, w_up, w_down):
    depth = w_in.shape[0]
    assert depth == 1 and x_prompt.shape[0] == 1
    tiles = _tiles(x_prompt.shape[1], x_sample.shape[0] * x_sample.shape[1])
    outs = _layer(x_prompt, x_sample, cache_win_k[0], cache_win_v[0], state_conv[0], state_gdn[0], rel_bias,
                  w_in[0], attn_sinks[0], gdn_conv_w[0], gdn_a_log[0], gdn_dt_bias[0], gdn_norm_w[0], w_out[0],
                  norm_pre_mix[0], norm_post_mix[0], norm_pre_ffn[0], norm_post_ffn[0],
                  w_gate[0], w_up[0], w_down[0], tiles)
    yp, ys, pk, pv, pc, ps, sk, sv, sc, ss = outs
    return (yp, ys, pk[None], pv[None], pc[None], ps[None], sk[None], sv[None], sc[None], ss[None])
```
